```python
import jax, jax.numpy as jnp
from jax import lax
import numpy as np

D_MODEL = 1024
BATCH = 8
SEQ = 2048
DEPTH = 1
DEC_BATCH = 128
DEC_SEQ = 8
PAST_LEN = 16384
PAGE_SIZE = 128

D_CONV = D_MODEL // 2
CONV_WIDTH = 31
DN_HEADS = 4
DN_HEAD_DIM = 128
DN_WIDTH = DN_HEADS * DN_HEAD_DIM
SHORT_WIDTH = 4
DELTA_CHUNK = 64
N_EXPERTS = 32
TOP_K = 4
D_FF = D_MODEL
SWIGLU_LIMIT = 7.0
SWIGLU_ALPHA = 1.702
MOE_BLOCK = 128
EPS = 1e-6
D_IN = 2 * D_CONV + 4 * DN_WIDTH + 2 * DN_HEADS + 2 * D_MODEL

kernel_name = "hybrid_conformer_conv_gated_deltanet_moe_step"


def rms_norm(x, g):
    xf = x.astype(jnp.float32)
    y = xf * lax.rsqrt(jnp.mean(xf * xf, axis=-1, keepdims=True) + EPS)
    return (y * g.astype(jnp.float32)).astype(x.dtype)


def layer_norm(x, g, b):
    xf = x.astype(jnp.float32)
    mu = jnp.mean(xf, axis=-1, keepdims=True)
    xc = xf - mu
    var = jnp.mean(xc * xc, axis=-1, keepdims=True)
    y = xc * lax.rsqrt(var + EPS) * g.astype(jnp.float32) + b.astype(jnp.float32)
    return y.astype(x.dtype)


def l2_norm(x):
    return x * lax.rsqrt(jnp.sum(x * x, axis=-1, keepdims=True) + EPS)


def causal_dw_conv(x_ext, w):
    return lax.conv_general_dilated(
        x_ext, w[:, None, :], window_strides=(1,), padding='VALID',
        dimension_numbers=('NWC', 'WIO', 'NWC'), feature_group_count=x_ext.shape[-1])


def gated_delta_rule(q, k, v, g, beta, S0):
    bsz, T, H, _ = q.shape
    DV = v.shape[-1]
    C = min(DELTA_CHUNK, T)
    n = -(-T // C)
    pad = n * C - T

    def chunks(t):
        t = jnp.pad(t, [(0, 0), (0, pad)] + [(0, 0)] * (t.ndim - 2))
        t = t.reshape((bsz, n, C) + t.shape[2:])
        return jnp.moveaxis(t, (1, 2), (0, 3))

    qc, kc, vc, gc, bc = chunks(q), chunks(k), chunks(v), chunks(g), chunks(beta)
    incl = jnp.tril(jnp.ones((C, C), bool))
    strict = jnp.tril(jnp.ones((C, C), bool), -1)
    eye = jnp.eye(C, dtype=jnp.float32)

    def step(S, inp):
        qi, ki, vi, gi, bi = inp
        gcum = jnp.cumsum(gi, axis=-1)
        decay = jnp.exp(jnp.where(incl, gcum[..., :, None] - gcum[..., None, :], -jnp.inf))
        kb = ki * bi[..., None]
        a = jnp.where(strict, jnp.einsum('bhik,bhjk->bhij', kb, ki) * decay, 0.0)
        rhs = jnp.concatenate([vi * bi[..., None], kb * jnp.exp(gcum)[..., None]], axis=-1)
        sol = lax.linalg.triangular_solve(eye + a, rhs, left_side=True, lower=True,
                                          unit_diagonal=True)
        value, kcum = sol[..., :DV], sol[..., DV:]
        v_new = value - jnp.einsum('bhck,bhkv->bhcv', kcum, S)
        scores = jnp.einsum('bhik,bhjk->bhij', qi, ki) * decay
        o = (jnp.einsum('bhck,bhkv->bhcv', qi * jnp.exp(gcum)[..., None], S)
             + jnp.einsum('bhij,bhjv->bhiv', scores, v_new))
        g_last = gcum[..., -1:]
        S = (S * jnp.exp(g_last)[..., None]
             + jnp.einsum('bhck,bhcv->bhkv', ki * jnp.exp(g_last - gcum)[..., None], v_new))
        return S, o

    S, o = lax.scan(step, S0, (qc, kc, vc, gc, bc))
    o = jnp.moveaxis(o, (0, 3), (1, 2)).reshape(bsz, n * C, H, DV)[:, :T]
    return o, S


def conformer_conv_branch(u_val, u_gate, buf, dw_w, dw_b, ln_g, ln_b, w_out):
    glu = u_val * jax.nn.sigmoid(u_gate)
    ext = jnp.concatenate([buf.astype(glu.dtype), glu], axis=1)
    c = causal_dw_conv(ext, dw_w) + dw_b
    c = jax.nn.silu(layer_norm(c, ln_g, ln_b))
    return c @ w_out, ext[:, -(CONV_WIDTH - 1):]


def gated_deltanet_branch(qkv_pre, z, a_raw, b_raw, buf, S0, conv_w, a_log, dt_bias, norm_g, w_out):
    bsz, T, _ = qkv_pre.shape
    ext = jnp.concatenate([buf.astype(qkv_pre.dtype), qkv_pre], axis=1)
    qkv = jax.nn.silu(causal_dw_conv(ext, conv_w)).astype(jnp.float32)
    new_buf = ext[:, -(SHORT_WIDTH - 1):]
    q, k, v = jnp.split(qkv, 3, axis=-1)
    q = l2_norm(q.reshape(bsz, T, DN_HEADS, DN_HEAD_DIM)) * (DN_HEAD_DIM ** -0.5)
    k = l2_norm(k.reshape(bsz, T, DN_HEADS, DN_HEAD_DIM))
    v = v.reshape(bsz, T, DN_HEADS, DN_HEAD_DIM)
    beta = jax.nn.sigmoid(b_raw.astype(jnp.float32))
    g = -jnp.exp(a_log.astype(jnp.float32)) * jax.nn.softplus(
        a_raw.astype(jnp.float32) + dt_bias.astype(jnp.float32))
    o, S = gated_delta_rule(q, k, v, g, beta, S0.astype(jnp.float32))
    zf = z.astype(jnp.float32).reshape(bsz, T, DN_HEADS, DN_HEAD_DIM)
    o = rms_norm(o, norm_g) * jax.nn.silu(zf)
    o = o.reshape(bsz, T, DN_WIDTH).astype(qkv_pre.dtype)
    return o @ w_out, new_buf, S


def moe_ffn(h, router_w, router_b, w_gate_up, b_gate_up, w_down, b_down):
    N, D = h.shape
    NK = N * TOP_K
    logits = (h @ router_w).astype(jnp.float32) + router_b.astype(jnp.float32)
    top_logit, top_idx = lax.top_k(logits, TOP_K)
    gate = jax.nn.softmax(top_logit, axis=-1)
    flat_e = top_idx.reshape(-1)
    flat_tok = jnp.arange(NK, dtype=jnp.int32) // TOP_K
    order = jnp.argsort(flat_e)
    sorted_e = flat_e[order]
    counts = jnp.bincount(flat_e, length=N_EXPERTS)
    padded = (counts + MOE_BLOCK - 1) // MOE_BLOCK * MOE_BLOCK
    start = jnp.cumsum(counts) - counts
    pad_end = jnp.cumsum(padded)
    pad_start = pad_end - padded
    dest = pad_start[sorted_e] + jnp.arange(NK, dtype=jnp.int32) - start[sorted_e]
    n_blocks = -(-NK // MOE_BLOCK) + N_EXPERTS
    rows = n_blocks * MOE_BLOCK
    row_tok = jnp.full((rows,), N, jnp.int32).at[dest].set(flat_tok[order])
    block_e = jnp.minimum(
        jnp.searchsorted(pad_end, jnp.arange(n_blocks, dtype=jnp.int32) * MOE_BLOCK, side='right'),
        N_EXPERTS - 1)
    h_pad = jnp.concatenate([h, jnp.zeros((1, D), h.dtype)], axis=0)
    xb = h_pad[row_tok].reshape(n_blocks, MOE_BLOCK, D)

    def expert_block(args):
        xe, e = args
        gu = xe @ w_gate_up[e] + b_gate_up[e]
        gt, up = jnp.split(gu, 2, axis=-1)
        gt = jnp.minimum(gt, SWIGLU_LIMIT)
        up = jnp.clip(up, -SWIGLU_LIMIT, SWIGLU_LIMIT)
        act = (up + 1.0) * (gt * jax.nn.sigmoid(SWIGLU_ALPHA * gt))
        return act @ w_down[e] + b_down[e]

    yb = lax.map(expert_block, (xb, block_e)).reshape(rows, D)
    slot = jnp.zeros((NK,), jnp.int32).at[order].set(dest)
    y = yb[slot].reshape(N, TOP_K, D)
    return jnp.einsum('nk,nkd->nd', gate.astype(y.dtype), y)


def decoder_layer(x, conv_buf, short_buf, S0, norm1_g, w_in, conv_dw_w, conv_dw_b, conv_ln_g,
                  conv_ln_b, w_conv_out, short_conv_w, a_log, dt_bias, delta_norm_g, w_delta_out,
                  w_merge_out, norm2_g, router_w, router_b, w_gate_up, b_gate_up, w_down, b_down):
    bsz, T, D = x.shape
    h = rms_norm(x, norm1_g)
    u = h @ w_in
    sizes = [D_CONV, D_CONV, 3 * DN_WIDTH, DN_WIDTH, DN_HEADS, DN_HEADS, D_MODEL, D_MODEL]
    cuts = []
    acc = 0
    for s in sizes[:-1]:
        acc += s
        cuts.append(acc)
    u_val, u_gate, qkv_pre, z, a_raw, b_raw, gate_a, gate_b = jnp.split(u, cuts, axis=-1)
    y_a, new_conv = conformer_conv_branch(u_val, u_gate, conv_buf, conv_dw_w, conv_dw_b,
                                          conv_ln_g, conv_ln_b, w_conv_out)
    y_b, new_short, S = gated_deltanet_branch(qkv_pre, z, a_raw, b_raw, short_buf, S0, short_conv_w,
                                              a_log, dt_bias, delta_norm_g, w_delta_out)
    mixed = jax.nn.sigmoid(gate_a) * y_a + jax.nn.sigmoid(gate_b) * y_b
    x = x + mixed @ w_merge_out
    f = moe_ffn(rms_norm(x, norm2_g).reshape(bsz * T, D), router_w, router_b,
                w_gate_up, b_gate_up, w_down, b_down)
    x = x + f.reshape(bsz, T, D)
    return x, new_conv, new_short, S


def setup_inputs(seed: int = 0) -> dict:
    key = jax.random.key(seed)
    ks = jax.random.split(key, 32)
    f32 = jnp.float32

    def nrm(k, shape, scale):
        return jax.random.normal(k, shape, f32) * scale

    dt = jnp.exp(jax.random.uniform(ks[12], (DEPTH, DN_HEADS), f32)
                 * (np.log(0.1) - np.log(0.001)) + np.log(0.001))
    dt_bias = dt + jnp.log(-jnp.expm1(-dt))
    return {
        'x_prompt': nrm(ks[0], (BATCH, SEQ, D_MODEL), 1.0),
        'x_sample': nrm(ks[1], (DEC_BATCH, DEC_SEQ, D_MODEL), 1.0),
        'state_conv': nrm(ks[2], (DEPTH, DEC_BATCH, CONV_WIDTH - 1, D_CONV), 0.5),
        'state_short_conv': nrm(ks[3], (DEPTH, DEC_BATCH, SHORT_WIDTH - 1, 3 * DN_WIDTH), 1.0),
        'state_delta': nrm(ks[4], (DEPTH, DEC_BATCH, DN_HEADS, DN_HEAD_DIM, DN_HEAD_DIM), 0.05),
        'norm1_g': 1.0 + nrm(ks[5], (DEPTH, D_MODEL), 0.02),
        'w_in': nrm(ks[6], (DEPTH, D_MODEL, D_IN), D_MODEL ** -0.5),
        'conv_dw_w': nrm(ks[7], (DEPTH, CONV_WIDTH, D_CONV), CONV_WIDTH ** -0.5),
        'conv_dw_b': nrm(ks[8], (DEPTH, D_CONV), 0.02),
        'conv_ln_g': 1.0 + nrm(ks[9], (DEPTH, D_CONV), 0.02),
        'conv_ln_b': nrm(ks[10], (DEPTH, D_CONV), 0.02),
        'w_conv_out': nrm(ks[11], (DEPTH, D_CONV, D_MODEL), D_CONV ** -0.5),
        'short_conv_w': nrm(ks[13], (DEPTH, SHORT_WIDTH, 3 * DN_WIDTH), SHORT_WIDTH ** -0.5),
        'a_log': jnp.log(jax.random.uniform(ks[14], (DEPTH, DN_HEADS), f32, 1.0, 16.0)),
        'dt_bias': dt_bias,
        'delta_norm_g': 1.0 + nrm(ks[15], (DEPTH, DN_HEAD_DIM), 0.02),
        'w_delta_out': nrm(ks[16], (DEPTH, DN_WIDTH, D_MODEL), DN_WIDTH ** -0.5),
        'w_merge_out': nrm(ks[17], (DEPTH, D_MODEL, D_MODEL), D_MODEL ** -0.5),
        'norm2_g': 1.0 + nrm(ks[18], (DEPTH, D_MODEL), 0.02),
        'router_w': nrm(ks[19], (DEPTH, D_MODEL, N_EXPERTS), D_MODEL ** -0.5),
        'router_b': nrm(ks[20], (DEPTH, N_EXPERTS), 0.01),
        'w_gate_up': nrm(ks[21], (DEPTH, N_EXPERTS, D_MODEL, 2 * D_FF), D_MODEL ** -0.5),
        'b_gate_up': nrm(ks[22], (DEPTH, N_EXPERTS, 2 * D_FF), 0.01),
        'w_down': nrm(ks[23], (DEPTH, N_EXPERTS, D_FF, D_MODEL), D_FF ** -0.5),
        'b_down': nrm(ks[24], (DEPTH, N_EXPERTS, D_MODEL), 0.01),
        'final_norm_g': 1.0 + nrm(ks[25], (D_MODEL,), 0.02),
    }


def reference(x_prompt, x_sample, state_conv, state_short_conv, state_delta, norm1_g, w_in,
              conv_dw_w, conv_dw_b, conv_ln_g, conv_ln_b, w_conv_out, short_conv_w, a_log, dt_bias,
              delta_norm_g, w_delta_out, w_merge_out, norm2_g, router_w, router_b, w_gate_up,
              b_gate_up, w_down, b_down, final_norm_g):
    hp, hs = x_prompt, x_sample
    bp = x_prompt.shape[0]
    conv_p, short_p, delta_p, conv_s, short_s, delta_s = [], [], [], [], [], []
    for l in range(DEPTH):
        lw = (norm1_g[l], w_in[l], conv_dw_w[l], conv_dw_b[l], conv_ln_g[l], conv_ln_b[l],
              w_conv_out[l], short_conv_w[l], a_log[l], dt_bias[l], delta_norm_g[l], w_delta_out[l],
              w_merge_out[l], norm2_g[l], router_w[l], router_b[l], w_gate_up[l], b_gate_up[l],
              w_down[l], b_down[l])
        hp, c_p, s_p, S_p = decoder_layer(
            hp, jnp.zeros((bp, CONV_WIDTH - 1, D_CONV), x_prompt.dtype),
            jnp.zeros((bp, SHORT_WIDTH - 1, 3 * DN_WIDTH), x_prompt.dtype),
            jnp.zeros((bp, DN_HEADS, DN_HEAD_DIM, DN_HEAD_DIM), jnp.float32), *lw)
        hs, c_s, s_s, S_s = decoder_layer(hs, state_conv[l], state_short_conv[l], state_delta[l], *lw)
        conv_p.append(c_p.astype(state_conv.dtype))
        short_p.append(s_p.astype(state_short_conv.dtype))
        delta_p.append(S_p.astype(state_delta.dtype))
        conv_s.append(c_s.astype(state_conv.dtype))
        short_s.append(s_s.astype(state_short_conv.dtype))
        delta_s.append(S_s.astype(state_delta.dtype))
    y_prompt = rms_norm(hp, final_norm_g)
    y_sample = rms_norm(hs, final_norm_g)
    state_conv_prompt = jnp.stack(conv_p)
    state_short_conv_prompt = jnp.stack(short_p)
    state_delta_prompt = jnp.stack(delta_p)
    state_conv_sample = jnp.stack(conv_s)
    state_short_conv_sample = jnp.stack(short_s)
    state_delta_sample = jnp.stack(delta_s)
    return (y_prompt, y_sample, state_conv_prompt, state_short_conv_prompt, state_delta_prompt,
            state_conv_sample, state_short_conv_sample, state_delta_sample)
```

```python
import functools

import jax
import jax.numpy as jnp
from jax import lax
from jax.experimental import pallas as pl
from jax.experimental.pallas import tpu as pltpu

F32 = jnp.float32
BF16 = jnp.bfloat16
EPS = 1e-6

LANES = 128
SUBLANES = 8
VMEM_LIMIT_BYTES = 56 * 1024 * 1024

D_CONV = 512
CONV_WIDTH = 31
DN_HEADS = 4
DN_HEAD_DIM = 128
DN_WIDTH = DN_HEADS * DN_HEAD_DIM
SHORT_WIDTH = 4
N_EXPERTS = 32
TOP_K = 4
SWIGLU_LIMIT = 7.0
SWIGLU_ALPHA = 1.702

CHUNK = 128
CONV_HALO = 32
SHORT_HALO = 8
HIGHEST = lax.Precision.HIGHEST


def _sigmoid(x):
    return 1.0 / (1.0 + jnp.exp(-x))


def _silu(x):
    return x * _sigmoid(x)


def _dot_bf16(a, b):
    return jnp.dot(a.astype(BF16), b.astype(BF16), preferred_element_type=F32)


def _dot_f32(a, b, dims=(((1,), (0,)), ((), ()))):
    return lax.dot_general(a, b, dims, precision=HIGHEST, preferred_element_type=F32)


def _inproj_kernel(x_ref, g_ref, w_ref, wab_ref, alog_ref, dtb_ref,
                   glu_ref, qkv_ref, z_ref, gb_ref, sa_ref, sb_ref):
    x = x_ref[...]
    h = x * lax.rsqrt(jnp.mean(x * x, axis=-1, keepdims=True) + EPS) * g_ref[...]
    hb = h.astype(BF16)

    def mm(lo, hi):
        return jnp.dot(hb, w_ref[:, lo:hi], preferred_element_type=F32)

    o_gate, o_qkv, o_z = D_CONV, 2 * D_CONV, 2 * D_CONV + 3 * DN_WIDTH
    o_ga = o_z + DN_WIDTH
    d = x.shape[-1]
    glu_ref[...] = mm(0, o_gate) * _sigmoid(mm(o_gate, o_qkv))
    qkv_ref[...] = mm(o_qkv, o_z)
    z_ref[...] = mm(o_z, o_ga)
    sa_ref[...] = _sigmoid(mm(o_ga, o_ga + d))
    sb_ref[...] = _sigmoid(mm(o_ga + d, o_ga + 2 * d))
    ab = _dot_f32(h, wab_ref[...])
    xa = ab + dtb_ref[...]
    softplus = jnp.maximum(xa, 0.0) + jnp.log(1.0 + jnp.exp(-jnp.abs(xa)))
    g = -jnp.exp(alog_ref[...]) * softplus
    lane = lax.broadcasted_iota(jnp.int32, ab.shape, 1)
    gb_ref[...] = jnp.where(lane < DN_HEADS, g, _sigmoid(ab))


def _inproj(x, norm_g, w_main, w_ab, alog, dtb, tm):
    n, d = x.shape
    assert n % tm == 0
    wcols = w_main.shape[1]
    row = lambda i: (i, 0)
    const = lambda i: (0, 0)
    outs = [(D_CONV, F32), (3 * DN_WIDTH, F32), (DN_WIDTH, F32), (LANES, F32), (d, F32), (d, F32)]
    return pl.pallas_call(
        _inproj_kernel,
        grid=(n // tm,),
        in_specs=[
            pl.BlockSpec((tm, d), row),
            pl.BlockSpec((1, d), const),
            pl.BlockSpec((d, wcols), const),
            pl.BlockSpec((d, LANES), const),
            pl.BlockSpec((1, LANES), const),
            pl.BlockSpec((1, LANES), const),
        ],
        out_specs=[pl.BlockSpec((tm, c), row) for c, _ in outs],
        out_shape=[jax.ShapeDtypeStruct((n, c), dt) for c, dt in outs],
        compiler_params=pltpu.CompilerParams(
            dimension_semantics=("arbitrary",), vmem_limit_bytes=VMEM_LIMIT_BYTES),
    )(x, norm_g, w_main, w_ab, alog, dtb)


def _conv_kernel(glu_ref, st_ref, w_ref, b_ref, lg_ref, lb_ref, out_ref, nst_ref, e_ref, *, bb, tt, rows):
    t = pl.program_id(1)

    @pl.when(t == 0)
    def _():
        e_ref[:, 0:CONV_HALO, :] = st_ref[...]

    e_ref[:, CONV_HALO:CONV_HALO + tt, :] = glu_ref[...]
    off = CONV_HALO - (CONV_WIDTH - 1)
    for b in range(bb):
        for c in range(tt // rows):
            r0 = c * rows
            acc = jnp.zeros((rows, D_CONV), F32) + b_ref[...]
            for j in range(CONV_WIDTH):
                acc = acc + w_ref[j:j + 1, :] * e_ref[b, r0 + j + off:r0 + j + off + rows, :]
            mu = jnp.mean(acc, axis=-1, keepdims=True)
            xc = acc - mu
            var = jnp.mean(xc * xc, axis=-1, keepdims=True)
            y = xc * lax.rsqrt(var + EPS) * lg_ref[...] + lb_ref[...]
            out_ref[b, r0:r0 + rows, :] = _silu(y).astype(out_ref.dtype)
    tail = e_ref[:, tt:tt + CONV_HALO, :]
    nst_ref[...] = tail
    e_ref[:, 0:CONV_HALO, :] = tail


def _conv_branch(glu, state32, dw_w, dw_b, ln_g, ln_b, bb, tt):
    bsz, t_len, c = glu.shape
    assert bsz % bb == 0 and t_len % tt == 0
    rows = min(tt, 32)
    kern = functools.partial(_conv_kernel, bb=bb, tt=tt, rows=rows)
    const = lambda b, t: (0, 0)
    return pl.pallas_call(
        kern,
        grid=(bsz // bb, t_len // tt),
        in_specs=[
            pl.BlockSpec((bb, tt, c), lambda b, t: (b, t, 0)),
            pl.BlockSpec((bb, CONV_HALO, c), lambda b, t: (b, 0, 0)),
            pl.BlockSpec((CONV_WIDTH, c), const),
            pl.BlockSpec((1, c), const),
            pl.BlockSpec((1, c), const),
            pl.BlockSpec((1, c), const),
        ],
        out_specs=[
            pl.BlockSpec((bb, tt, c), lambda b, t: (b, t, 0)),
            pl.BlockSpec((bb, CONV_HALO, c), lambda b, t: (b, 0, 0)),
        ],
        out_shape=[
            jax.ShapeDtypeStruct((bsz, t_len, c), BF16),
            jax.ShapeDtypeStruct((bsz, CONV_HALO, c), F32),
        ],
        scratch_shapes=[pltpu.VMEM((bb, CONV_HALO + tt, c), F32)],
        compiler_params=pltpu.CompilerParams(
            dimension_semantics=("arbitrary", "arbitrary"), vmem_limit_bytes=VMEM_LIMIT_BYTES),
    )(glu, state32, dw_w, dw_b, ln_g, ln_b)


def _chunk_masks(seq_len):
    i = lax.broadcasted_iota(jnp.int32, (CHUNK, CHUNK), 0)
    j = lax.broadcasted_iota(jnp.int32, (CHUNK, CHUNK), 1)
    same = (i // seq_len) == (j // seq_len)
    incl = same & (i >= j)
    upper = same & (i <= j)
    strict = same & (i > j)
    last = j == (i // seq_len) * seq_len + (seq_len - 1)
    levels = []
    blk = 1
    while blk < seq_len:
        levels.append(((i // (2 * blk)) == (j // (2 * blk))) & (((i // blk) % 2) == 1) & (((j // blk) % 2) == 0))
        blk *= 2
    eye = i == j
    return incl, upper, strict, last, levels, eye


def _unit_lower_inverse(a, levels, eye):
    x = jnp.where(eye, 1.0, 0.0) - jnp.where(levels[0], a, 0.0)
    for m in levels[1:]:
        am = jnp.where(m, a, 0.0)
        x = x - _dot_f32(_dot_f32(x, am), x)
    return x


def _lane_col(x, lane):
    return jnp.broadcast_to(x[:, lane:lane + 1], (x.shape[0], LANES))


def _l2norm(x):
    return x * lax.rsqrt(jnp.sum(x * x, axis=-1, keepdims=True) + EPS)


def _chunk_prepare(qkv, gbt, masks):
    incl, upper, strict, last, levels, eye = masks
    gc = _dot_f32(jnp.where(incl, 1.0, 0.0), gbt)
    gct = _dot_f32(gbt, jnp.where(upper, 1.0, 0.0), (((0,), (0,)), ((), ())))
    glast = _dot_f32(jnp.where(last, 1.0, 0.0), gc)
    out = []
    for h in range(DN_HEADS):
        q = _l2norm(qkv[:, h * DN_HEAD_DIM:(h + 1) * DN_HEAD_DIM]) * (DN_HEAD_DIM ** -0.5)
        k = _l2norm(qkv[:, DN_WIDTH + h * DN_HEAD_DIM:DN_WIDTH + (h + 1) * DN_HEAD_DIM])
        v = qkv[:, 2 * DN_WIDTH + h * DN_HEAD_DIM:2 * DN_WIDTH + (h + 1) * DN_HEAD_DIM]
        gcol = _lane_col(gc, h)
        grow = jnp.broadcast_to(gct[h:h + 1, :], (CHUNK, CHUNK))
        beta = _lane_col(gbt, DN_HEADS + h)
        gl = _lane_col(glast, h)
        decay = jnp.exp(jnp.where(incl, gcol - grow, -jnp.inf))
        kb = k * beta
        nt = (((1,), (1,)), ((), ()))
        a = jnp.where(strict, _dot_f32(kb, k, nt) * decay, 0.0)
        tinv = _unit_lower_inverse(a, levels, eye)
        egc = jnp.exp(gcol)
        sol = _dot_f32(tinv, jnp.concatenate([v * beta, kb * egc], axis=1))
        value, kcum = sol[:, :DN_HEAD_DIM], sol[:, DN_HEAD_DIM:]
        scores = _dot_f32(q, k, nt) * decay
        out.append((value, kcum, scores, q * egc, k * jnp.exp(gl - gcol), jnp.exp(gl)))
    return out


def _gated_out_norm(o, z, ng):
    y = o * lax.rsqrt(jnp.mean(o * o, axis=-1, keepdims=True) + EPS) * ng
    return y * _silu(z)


def _short_conv(e_ref, w_ref, tt):
    off = SHORT_HALO - (SHORT_WIDTH - 1)
    acc = w_ref[0:1, :] * e_ref[off:off + tt, :]
    for j in range(1, SHORT_WIDTH):
        acc = acc + w_ref[j:j + 1, :] * e_ref[off + j:off + j + tt, :]
    return _silu(acc)


def _delta_prompt_kernel(qkv_ref, z_ref, gb_ref, st_ref, s0_ref, w_ref, ng_ref,
                         o_ref, nst_ref, sout_ref, e_ref, s_ref, *, tt):
    t = pl.program_id(1)

    @pl.when(t == 0)
    def _():
        e_ref[0:SHORT_HALO, :] = st_ref[0]
        s_ref[...] = s0_ref[0]

    e_ref[SHORT_HALO:SHORT_HALO + tt, :] = qkv_ref[0]
    qkv = _short_conv(e_ref, w_ref, tt)
    tail = e_ref[tt:tt + SHORT_HALO, :]
    nst_ref[0] = tail
    e_ref[0:SHORT_HALO, :] = tail

    masks = _chunk_masks(CHUNK)
    tn = (((0,), (0,)), ((), ()))
    for c in range(tt // CHUNK):
        r0 = c * CHUNK
        prep = _chunk_prepare(qkv[r0:r0 + CHUNK, :], gb_ref[0, r0:r0 + CHUNK, :], masks)
        for h in range(DN_HEADS):
            value, kcum, scores, qexp, kdec, egl = prep[h]
            s = s_ref[h]
            v_new = value - _dot_f32(kcum, s)
            o = _dot_f32(qexp, s) + _dot_f32(scores, v_new)
            s_ref[h] = s * egl[0:1, :] + _dot_f32(kdec, v_new, tn)
            zh = z_ref[0, r0:r0 + CHUNK, h * DN_HEAD_DIM:(h + 1) * DN_HEAD_DIM]
            o_ref[0, r0:r0 + CHUNK, h * DN_HEAD_DIM:(h + 1) * DN_HEAD_DIM] = (
                _gated_out_norm(o, zh, ng_ref[...]).astype(o_ref.dtype))
    sout_ref[0] = s_ref[...]


def _delta_prompt(qkv_pre, z, gb, state8, s0, conv_w, norm_g, tt):
    bsz, t_len, _ = qkv_pre.shape
    assert t_len % tt == 0 and tt % CHUNK == 0
    kern = functools.partial(_delta_prompt_kernel, tt=tt)
    seq = lambda b, t: (b, t, 0)
    per_b = lambda b, t: (b, 0, 0)
    return pl.pallas_call(
        kern,
        grid=(bsz, t_len // tt),
        in_specs=[
            pl.BlockSpec((1, tt, 3 * DN_WIDTH), seq),
            pl.BlockSpec((1, tt, DN_WIDTH), seq),
            pl.BlockSpec((1, tt, LANES), seq),
            pl.BlockSpec((1, SHORT_HALO, 3 * DN_WIDTH), per_b),
            pl.BlockSpec((1, DN_HEADS, DN_HEAD_DIM, DN_HEAD_DIM), lambda b, t: (b, 0, 0, 0)),
            pl.BlockSpec((SHORT_WIDTH, 3 * DN_WIDTH), lambda b, t: (0, 0)),
            pl.BlockSpec((1, DN_HEAD_DIM), lambda b, t: (0, 0)),
        ],
        out_specs=[
            pl.BlockSpec((1, tt, DN_WIDTH), seq),
            pl.BlockSpec((1, SHORT_HALO, 3 * DN_WIDTH), per_b),
            pl.BlockSpec((1, DN_HEADS, DN_HEAD_DIM, DN_HEAD_DIM), lambda b, t: (b, 0, 0, 0)),
        ],
        out_shape=[
            jax.ShapeDtypeStruct((bsz, t_len, DN_WIDTH), BF16),
            jax.ShapeDtypeStruct((bsz, SHORT_HALO, 3 * DN_WIDTH), F32),
            jax.ShapeDtypeStruct((bsz, DN_HEADS, DN_HEAD_DIM, DN_HEAD_DIM), F32),
        ],
        scratch_shapes=[
            pltpu.VMEM((SHORT_HALO + tt, 3 * DN_WIDTH), F32),
            pltpu.VMEM((DN_HEADS, DN_HEAD_DIM, DN_HEAD_DIM), F32),
        ],
        compiler_params=pltpu.CompilerParams(
            dimension_semantics=("arbitrary", "arbitrary"), vmem_limit_bytes=VMEM_LIMIT_BYTES),
    )(qkv_pre, z, gb, state8, s0, conv_w, norm_g)


def _delta_sample_kernel(qkv_ref, z_ref, gb_ref, st_ref, s0_ref, w_ref, ng_ref,
                         o_ref, nst_ref, sout_ref, e_ref, *, nseq, seq_len):
    qkv_rows = []
    for b in range(nseq):
        e_ref[0:SHORT_HALO, :] = st_ref[b]
        e_ref[SHORT_HALO:SHORT_HALO + seq_len, :] = qkv_ref[b]
        qkv_rows.append(_short_conv(e_ref, w_ref, seq_len))
        nst_ref[b] = e_ref[seq_len:seq_len + SHORT_HALO, :]
    qkv = jnp.concatenate(qkv_rows, axis=0)
    gbt = jnp.concatenate([gb_ref[b] for b in range(nseq)], axis=0)
    zt = jnp.concatenate([z_ref[b] for b in range(nseq)], axis=0)
    masks = _chunk_masks(seq_len)
    prep = _chunk_prepare(qkv, gbt, masks)
    tn = (((0,), (0,)), ((), ()))
    for h in range(DN_HEADS):
        value, kcum, scores, qexp, kdec, egl = prep[h]
        v_rows, o_rows = [], []
        for b in range(nseq):
            r = slice(b * seq_len, (b + 1) * seq_len)
            s = s0_ref[b, h]
            both = _dot_f32(jnp.concatenate([kcum[r], qexp[r]], axis=0), s)
            v_new = value[r] - both[:seq_len]
            v_rows.append(v_new)
            o_rows.append(both[seq_len:])
            sout_ref[b, h] = s * egl[b * seq_len:b * seq_len + 1, :] + _dot_f32(kdec[r], v_new, tn)
        o = jnp.concatenate(o_rows, axis=0) + _dot_f32(scores, jnp.concatenate(v_rows, axis=0))
        y = _gated_out_norm(o, zt[:, h * DN_HEAD_DIM:(h + 1) * DN_HEAD_DIM], ng_ref[...]).astype(o_ref.dtype)
        for b in range(nseq):
            o_ref[b, :, h * DN_HEAD_DIM:(h + 1) * DN_HEAD_DIM] = y[b * seq_len:(b + 1) * seq_len]


def _delta_sample(qkv_pre, z, gb, state8, s0, conv_w, norm_g):
    bsz, seq_len, _ = qkv_pre.shape
    assert CHUNK % seq_len == 0
    nseq = CHUNK // seq_len
    assert bsz % nseq == 0
    kern = functools.partial(_delta_sample_kernel, nseq=nseq, seq_len=seq_len)
    blk3 = lambda i: (i, 0, 0)
    blk4 = lambda i: (i, 0, 0, 0)
    return pl.pallas_call(
        kern,
        grid=(bsz // nseq,),
        in_specs=[
            pl.BlockSpec((nseq, seq_len, 3 * DN_WIDTH), blk3),
            pl.BlockSpec((nseq, seq_len, DN_WIDTH), blk3),
            pl.BlockSpec((nseq, seq_len, LANES), blk3),
            pl.BlockSpec((nseq, SHORT_HALO, 3 * DN_WIDTH), blk3),
            pl.BlockSpec((nseq, DN_HEADS, DN_HEAD_DIM, DN_HEAD_DIM), blk4),
            pl.BlockSpec((SHORT_WIDTH, 3 * DN_WIDTH), lambda i: (0, 0)),
            pl.BlockSpec((1, DN_HEAD_DIM), lambda i: (0, 0)),
        ],
        out_specs=[
            pl.BlockSpec((nseq, seq_len, DN_WIDTH), blk3),
            pl.BlockSpec((nseq, SHORT_HALO, 3 * DN_WIDTH), blk3),
            pl.BlockSpec((nseq, DN_HEADS, DN_HEAD_DIM, DN_HEAD_DIM), blk4),
        ],
        out_shape=[
            jax.ShapeDtypeStruct((bsz, seq_len, DN_WIDTH), BF16),
            jax.ShapeDtypeStruct((bsz, SHORT_HALO, 3 * DN_WIDTH), F32),
            jax.ShapeDtypeStruct((bsz, DN_HEADS, DN_HEAD_DIM, DN_HEAD_DIM), F32),
        ],
        scratch_shapes=[pltpu.VMEM((SHORT_HALO + seq_len, 3 * DN_WIDTH), F32)],
        compiler_params=pltpu.CompilerParams(
            dimension_semantics=("arbitrary",), vmem_limit_bytes=VMEM_LIMIT_BYTES),
    )(qkv_pre, z, gb, state8, s0, conv_w, norm_g)


def _mix_kernel(x_ref, ca_ref, oa_ref, sa_ref, sb_ref, wc_ref, wd_ref, wm_ref, g2_ref, rw_ref, rb_ref,
                x2_ref, h2_ref, ti_ref, tg_ref):
    ya = jnp.dot(ca_ref[...], wc_ref[...], preferred_element_type=F32)
    yb = jnp.dot(oa_ref[...], wd_ref[...], preferred_element_type=F32)
    mixed = sa_ref[...] * ya + sb_ref[...] * yb
    x2 = x_ref[...] + jnp.dot(mixed.astype(BF16), wm_ref[...], preferred_element_type=F32)
    x2_ref[...] = x2
    h2 = x2 * lax.rsqrt(jnp.mean(x2 * x2, axis=-1, keepdims=True) + EPS) * g2_ref[...]
    h2_ref[...] = h2
    logits = _dot_f32(h2, rw_ref[...]) + rb_ref[...]
    lane = lax.broadcasted_iota(jnp.int32, logits.shape, 1)
    lane_f = lane.astype(F32)
    logits = jnp.where(lane < N_EXPERTS, logits, -jnp.inf)
    top_vals = []
    ti = jnp.zeros(logits.shape, jnp.int32)
    for k in range(TOP_K):
        m = jnp.max(logits, axis=-1, keepdims=True)
        idx = jnp.min(jnp.where(logits == m, lane_f, float(LANES)), axis=-1, keepdims=True).astype(jnp.int32)
        ti = jnp.where(lane == k, idx, ti)
        top_vals.append(m)
        logits = jnp.where(lane == idx, -jnp.inf, logits)
    exps = [jnp.exp(v - top_vals[0]) for v in top_vals]
    den = exps[0] + exps[1] + exps[2] + exps[3]
    tg = jnp.zeros(logits.shape, F32)
    for k in range(TOP_K):
        tg = jnp.where(lane == k, exps[k] / den, tg)
    ti_ref[...] = ti
    tg_ref[...] = tg


def _mix(x, cact, oact, siga, sigb, w_conv_out, w_delta_out, w_merge_out, norm2_g, router_w, router_b, tm):
    n, d = x.shape
    assert n % tm == 0
    row = lambda i: (i, 0)
    const = lambda i: (0, 0)
    return pl.pallas_call(
        _mix_kernel,
        grid=(n // tm,),
        in_specs=[
            pl.BlockSpec((tm, d), row),
            pl.BlockSpec((tm, D_CONV), row),
            pl.BlockSpec((tm, DN_WIDTH), row),
            pl.BlockSpec((tm, d), row),
            pl.BlockSpec((tm, d), row),
            pl.BlockSpec((D_CONV, d), const),
            pl.BlockSpec((DN_WIDTH, d), const),
            pl.BlockSpec((d, d), const),
            pl.BlockSpec((1, d), const),
            pl.BlockSpec((d, LANES), const),
            pl.BlockSpec((1, LANES), const),
        ],
        out_specs=[
            pl.BlockSpec((tm, d), row),
            pl.BlockSpec((tm, d), row),
            pl.BlockSpec((tm, LANES), row),
            pl.BlockSpec((tm, LANES), row),
        ],
        out_shape=[
            jax.ShapeDtypeStruct((n, d), F32),
            jax.ShapeDtypeStruct((n, d), F32),
            jax.ShapeDtypeStruct((n, LANES), jnp.int32),
            jax.ShapeDtypeStruct((n, LANES), F32),
        ],
        compiler_params=pltpu.CompilerParams(
            dimension_semantics=("arbitrary",), vmem_limit_bytes=VMEM_LIMIT_BYTES),
    )(x, cact, oact, siga, sigb, w_conv_out, w_delta_out, w_merge_out, norm2_g, router_w, router_b)


GATHER_ROWS = 128


def _gather_rows_kernel(idx_ref, src_ref, out_ref, sem):
    i = pl.program_id(0)
    for r in range(GATHER_ROWS):
        pltpu.make_async_copy(src_ref.at[pl.ds(idx_ref[i, r], 1)], out_ref.at[pl.ds(r, 1)], sem).start()
    pltpu.make_async_copy(src_ref.at[pl.ds(0, GATHER_ROWS)], out_ref, sem).wait()


def _gather_rows(src, idx2d):
    nblk = idx2d.shape[0]
    d = src.shape[1]
    return pl.pallas_call(
        _gather_rows_kernel,
        grid_spec=pltpu.PrefetchScalarGridSpec(
            num_scalar_prefetch=1,
            grid=(nblk,),
            in_specs=[pl.BlockSpec(memory_space=pl.ANY)],
            out_specs=pl.BlockSpec((GATHER_ROWS, d), lambda i, idx: (i, 0)),
            scratch_shapes=[pltpu.SemaphoreType.DMA(())],
        ),
        out_shape=jax.ShapeDtypeStruct((nblk * GATHER_ROWS, d), src.dtype),
        compiler_params=pltpu.CompilerParams(dimension_semantics=("arbitrary",)),
    )(idx2d, src)


def _moe_ffn_kernel(be_ref, nv_ref, xs_ref, wgu_ref, bgu_ref, wd_ref, bd_ref, out_ref, wgu_bf, wd_bf):
    i = pl.program_id(0)
    prev = be_ref[jnp.maximum(i - 1, 0)]
    changed = jnp.logical_or(i == 0, be_ref[i] != prev)

    @pl.when(changed)
    def _():
        wgu_bf[...] = wgu_ref[0].astype(BF16)
        wd_bf[...] = wd_ref[0].astype(BF16)

    @pl.when(i < nv_ref[0])
    def _():
        f = wd_bf.shape[0]
        gu = jnp.dot(xs_ref[...].astype(BF16), wgu_bf[...], preferred_element_type=F32) + bgu_ref[0]
        gt = jnp.minimum(gu[:, :f], SWIGLU_LIMIT)
        up = jnp.clip(gu[:, f:], -SWIGLU_LIMIT, SWIGLU_LIMIT)
        act = (up + 1.0) * (gt * _sigmoid(SWIGLU_ALPHA * gt))
        out_ref[...] = jnp.dot(act.astype(BF16), wd_bf[...], preferred_element_type=F32) + bd_ref[0]

    @pl.when(i >= nv_ref[0])
    def _():
        out_ref[...] = jnp.zeros(out_ref.shape, out_ref.dtype)


def _moe_ffn(xs, block_e, nvalid, w_gate_up, b_gate_up, w_down, b_down, tm):
    rows, d = xs.shape
    ne, _, f2 = w_gate_up.shape
    f = f2 // 2
    nblk = rows // tm
    return pl.pallas_call(
        _moe_ffn_kernel,
        grid_spec=pltpu.PrefetchScalarGridSpec(
            num_scalar_prefetch=2,
            grid=(nblk,),
            in_specs=[
                pl.BlockSpec((tm, d), lambda i, be, nv: (i, 0)),
                pl.BlockSpec((1, d, f2), lambda i, be, nv: (be[i], 0, 0)),
                pl.BlockSpec((1, 1, f2), lambda i, be, nv: (be[i], 0, 0)),
                pl.BlockSpec((1, f, d), lambda i, be, nv: (be[i], 0, 0)),
                pl.BlockSpec((1, 1, d), lambda i, be, nv: (be[i], 0, 0)),
            ],
            out_specs=pl.BlockSpec((tm, d), lambda i, be, nv: (i, 0)),
            scratch_shapes=[pltpu.VMEM((d, f2), BF16), pltpu.VMEM((f, d), BF16)],
        ),
        out_shape=jax.ShapeDtypeStruct((rows, d), F32),
        compiler_params=pltpu.CompilerParams(
            dimension_semantics=("arbitrary",), vmem_limit_bytes=VMEM_LIMIT_BYTES),
    )(block_e, nvalid, xs, w_gate_up, b_gate_up.reshape(ne, 1, f2), w_down, b_down.reshape(ne, 1, d))


def _combine_kernel(slot_ref, yb_ref, x2_ref, tg_ref, fg_ref, out_ref, buf, sem, *, tc):
    i = pl.program_id(0)
    per_row = GATHER_ROWS // TOP_K
    for t in range(tc):
        for k in range(TOP_K):
            s = slot_ref[i * (tc // per_row) + t // per_row, (t % per_row) * TOP_K + k]
            pltpu.make_async_copy(yb_ref.at[pl.ds(s, 1)], buf.at[k, pl.ds(t, 1)], sem).start()
    for k in range(TOP_K):
        pltpu.make_async_copy(yb_ref.at[pl.ds(0, tc)], buf.at[k], sem).wait()
    tg = tg_ref[...]
    y = x2_ref[...]
    for k in range(TOP_K):
        y = y + tg[:, k:k + 1] * buf[k]
    out_ref[...] = y * lax.rsqrt(jnp.mean(y * y, axis=-1, keepdims=True) + EPS) * fg_ref[...]


def _combine(slot2d, yb, x2, tg, final_g, tc):
    n, d = x2.shape
    assert n % tc == 0 and tc % (GATHER_ROWS // TOP_K) == 0
    kern = functools.partial(_combine_kernel, tc=tc)
    return pl.pallas_call(
        kern,
        grid_spec=pltpu.PrefetchScalarGridSpec(
            num_scalar_prefetch=1,
            grid=(n // tc,),
            in_specs=[
                pl.BlockSpec(memory_space=pl.ANY),
                pl.BlockSpec((tc, d), lambda i, s: (i, 0)),
                pl.BlockSpec((tc, LANES), lambda i, s: (i, 0)),
                pl.BlockSpec((1, d), lambda i, s: (0, 0)),
            ],
            out_specs=pl.BlockSpec((tc, d), lambda i, s: (i, 0)),
            scratch_shapes=[pltpu.VMEM((TOP_K, tc, d), F32), pltpu.SemaphoreType.DMA(())],
        ),
        out_shape=jax.ShapeDtypeStruct((n, d), F32),
        compiler_params=pltpu.CompilerParams(
            dimension_semantics=("arbitrary",), vmem_limit_bytes=VMEM_LIMIT_BYTES),
    )(slot2d, yb, x2, tg, final_g)


def _route(top_idx, tm, n_blocks):
    n = top_idx.shape[0]
    nk = n * TOP_K
    flat_e = top_idx.reshape(-1)
    onehot = (flat_e[:, None] == jnp.arange(N_EXPERTS, dtype=jnp.int32)[None, :]).astype(jnp.int32)
    csum = jnp.cumsum(onehot, axis=0)
    rank = jnp.sum(csum * onehot, axis=1) - 1
    counts = csum[-1]
    padded = (counts + tm - 1) // tm * tm
    pad_end = jnp.cumsum(padded)
    pad_start = pad_end - padded
    dest = pad_start[flat_e] + rank
    row_tok = jnp.zeros((n_blocks * tm,), jnp.int32).at[dest].set(jnp.arange(nk, dtype=jnp.int32) // TOP_K)
    nvalid = (pad_end[-1] // tm).astype(jnp.int32)
    blk = jnp.arange(n_blocks, dtype=jnp.int32)
    block_e = jnp.minimum(jnp.searchsorted(pad_end, blk * tm, side='right'), N_EXPERTS - 1).astype(jnp.int32)
    block_e = jnp.where(blk < nvalid, block_e, block_e[jnp.maximum(nvalid - 1, 0)])
    return row_tok, dest, block_e, nvalid.reshape(1)


TOKEN_TILE = 256
MOE_TILE = 256
COMBINE_TILE = 128


def _pad_lanes(v, width=LANES):
    v = v.reshape(1, -1)
    return jnp.pad(v, ((0, 0), (0, width - v.shape[1])))


def kernel(x_prompt, x_sample, state_conv, state_short_conv, state_delta, norm1_g, w_in, conv_dw_w,
           conv_dw_b, conv_ln_g, conv_ln_b, w_conv_out, short_conv_w, a_log, dt_bias, delta_norm_g,
           w_delta_out, w_merge_out, norm2_g, router_w, router_b, w_gate_up, b_gate_up, w_down, b_down,
           final_norm_g):
    depth = w_in.shape[0]
    assert depth == 1
    bp, tp, d = x_prompt.shape
    bs, ts, _ = x_sample.shape
    n_p, n_s = bp * tp, bs * ts
    n = n_p + n_s
    l = 0

    x_all = jnp.concatenate([x_prompt.reshape(n_p, d), x_sample.reshape(n_s, d)], axis=0)

    o_ab = 2 * D_CONV + 4 * DN_WIDTH
    w = w_in[l]
    w_main = jnp.concatenate([w[:, :o_ab], w[:, o_ab + 2 * DN_HEADS:]], axis=1).astype(BF16)
    w_ab = jnp.pad(w[:, o_ab:o_ab + 2 * DN_HEADS], ((0, 0), (0, LANES - 2 * DN_HEADS)))

    glu, qkv_pre, z, gb, siga, sigb = _inproj(
        x_all, norm1_g[l].reshape(1, d), w_main, w_ab, _pad_lanes(a_log[l]), _pad_lanes(dt_bias[l]), TOKEN_TILE)

    def split(a):
        c = a.shape[1]
        return a[:n_p].reshape(bp, tp, c), a[n_p:].reshape(bs, ts, c)

    glu_p, glu_s = split(glu)
    qkv_p, qkv_s = split(qkv_pre)
    z_p, z_s = split(z)
    gb_p, gb_s = split(gb)

    dw = (conv_dw_w[l], conv_dw_b[l].reshape(1, -1), conv_ln_g[l].reshape(1, -1), conv_ln_b[l].reshape(1, -1))
    pad_c = CONV_HALO - (CONV_WIDTH - 1)
    st_c_p = jnp.zeros((bp, CONV_HALO, D_CONV), F32)
    st_c_s = jnp.pad(state_conv[l], ((0, 0), (pad_c, 0), (0, 0)))
    cact_p, nconv_p = _conv_branch(glu_p, st_c_p, *dw, bb=1, tt=256)
    cact_s, nconv_s = _conv_branch(glu_s, st_c_s, *dw, bb=8, tt=ts)

    pad_s = SHORT_HALO - (SHORT_WIDTH - 1)
    st_s_p = jnp.zeros((bp, SHORT_HALO, 3 * DN_WIDTH), F32)
    st_s_s = jnp.pad(state_short_conv[l], ((0, 0), (pad_s, 0), (0, 0)))
    s0_p = jnp.zeros((bp, DN_HEADS, DN_HEAD_DIM, DN_HEAD_DIM), F32)
    ng = delta_norm_g[l].reshape(1, -1)
    oact_p, nshort_p, s_p = _delta_prompt(qkv_p, z_p, gb_p, st_s_p, s0_p, short_conv_w[l], ng, tt=256)
    oact_s, nshort_s, s_s = _delta_sample(qkv_s, z_s, gb_s, st_s_s, state_delta[l], short_conv_w[l], ng)

    cact = jnp.concatenate([cact_p.reshape(n_p, -1), cact_s.reshape(n_s, -1)], axis=0)
    oact = jnp.concatenate([oact_p.reshape(n_p, -1), oact_s.reshape(n_s, -1)], axis=0)

    rw = jnp.pad(router_w[l], ((0, 0), (0, LANES - N_EXPERTS)))
    x2, h2, ti, tg = _mix(x_all, cact, oact, siga, sigb, w_conv_out[l].astype(BF16), w_delta_out[l].astype(BF16),
                          w_merge_out[l].astype(BF16), norm2_g[l].reshape(1, d), rw, _pad_lanes(router_b[l]),
                          TOKEN_TILE)

    n_blocks = -(-(n * TOP_K) // MOE_TILE) + N_EXPERTS
    row_tok, dest, block_e, nvalid = _route(ti[:, :TOP_K], MOE_TILE, n_blocks)
    xs = _gather_rows(h2, row_tok.reshape(-1, GATHER_ROWS))
    yb = _moe_ffn(xs, block_e, nvalid, w_gate_up[l], b_gate_up[l], w_down[l], b_down[l], MOE_TILE)
    y = _combine(dest.reshape(-1, GATHER_ROWS), yb, x2, tg, final_norm_g.reshape(1, d), COMBINE_TILE)

    y_prompt = y[:n_p].reshape(bp, tp, d)
    y_sample = y[n_p:].reshape(bs, ts, d)
    conv_p = nconv_p[:, pad_c:, :][None]
    conv_s = nconv_s[:, pad_c:, :][None]
    short_p = nshort_p[:, pad_s:, :][None]
    short_s = nshort_s[:, pad_s:, :][None]
    return (y_prompt, y_sample, conv_p, short_p, s_p[None], conv_s, short_s, s_s[None])
```

```python
import functools

import jax
import jax.numpy as jnp
from jax import lax
from jax.experimental import pallas as pl
from jax.experimental.pallas import tpu as pltpu

F32 = jnp.float32
BF16 = jnp.bfloat16
EPS = 1e-6

LANES = 128
SUBLANES = 8
VMEM_LIMIT_BYTES = 56 * 1024 * 1024

D_CONV = 512
CONV_WIDTH = 31
DN_HEADS = 4
DN_HEAD_DIM = 128
DN_WIDTH = DN_HEADS * DN_HEAD_DIM
SHORT_WIDTH = 4
N_EXPERTS = 32
TOP_K = 4
SWIGLU_LIMIT = 7.0
SWIGLU_ALPHA = 1.702

CHUNK = 128
CONV_HALO = 32
SHORT_HALO = 8
HIGHEST = lax.Precision.HIGHEST


def _sigmoid(x):
    return 1.0 / (1.0 + jnp.exp(-x))


def _silu(x):
    return x * _sigmoid(x)


def _dot_bf16(a, b):
    return jnp.dot(a.astype(BF16), b.astype(BF16), preferred_element_type=F32)


def _dot_f32(a, b, dims=(((1,), (0,)), ((), ()))):
    return lax.dot_general(a, b, dims, precision=HIGHEST, preferred_element_type=F32)


def _dot_delta(a, b, dims=(((1,), (0,)), ((), ()))):
    return lax.dot_general(a.astype(BF16), b.astype(BF16), dims, preferred_element_type=F32)


def _inproj_kernel(x_ref, g_ref, w_ref, wab_ref, alog_ref, dtb_ref,
                   glu_ref, qkv_ref, z_ref, gb_ref, sa_ref, sb_ref):
    x = x_ref[...]
    h = x * lax.rsqrt(jnp.mean(x * x, axis=-1, keepdims=True) + EPS) * g_ref[...]
    hb = h.astype(BF16)

    def mm(lo, hi):
        return jnp.dot(hb, w_ref[:, lo:hi], preferred_element_type=F32)

    o_gate, o_qkv, o_z = D_CONV, 2 * D_CONV, 2 * D_CONV + 3 * DN_WIDTH
    o_ga = o_z + DN_WIDTH
    d = x.shape[-1]
    glu_ref[...] = mm(0, o_gate) * _sigmoid(mm(o_gate, o_qkv))
    qkv_ref[...] = mm(o_qkv, o_z)
    z_ref[...] = mm(o_z, o_ga)
    sa_ref[...] = _sigmoid(mm(o_ga, o_ga + d))
    sb_ref[...] = _sigmoid(mm(o_ga + d, o_ga + 2 * d))
    ab = _dot_f32(h, wab_ref[...])
    xa = ab + dtb_ref[...]
    softplus = jnp.maximum(xa, 0.0) + jnp.log(1.0 + jnp.exp(-jnp.abs(xa)))
    g = -jnp.exp(alog_ref[...]) * softplus
    lane = lax.broadcasted_iota(jnp.int32, ab.shape, 1)
    gb_ref[...] = jnp.where(lane < DN_HEADS, g, _sigmoid(ab))


def _inproj(x, norm_g, w_main, w_ab, alog, dtb, tm):
    n, d = x.shape
    assert n % tm == 0
    wcols = w_main.shape[1]
    row = lambda i: (i, 0)
    const = lambda i: (0, 0)
    outs = [(D_CONV, F32), (3 * DN_WIDTH, F32), (DN_WIDTH, F32), (LANES, F32), (d, F32), (d, F32)]
    return pl.pallas_call(
        _inproj_kernel,
        grid=(n // tm,),
        in_specs=[
            pl.BlockSpec((tm, d), row),
            pl.BlockSpec((1, d), const),
            pl.BlockSpec((d, wcols), const),
            pl.BlockSpec((d, LANES), const),
            pl.BlockSpec((1, LANES), const),
            pl.BlockSpec((1, LANES), const),
        ],
        out_specs=[pl.BlockSpec((tm, c), row) for c, _ in outs],
        out_shape=[jax.ShapeDtypeStruct((n, c), dt) for c, dt in outs],
        compiler_params=pltpu.CompilerParams(
            dimension_semantics=("arbitrary",), vmem_limit_bytes=VMEM_LIMIT_BYTES),
    )(x, norm_g, w_main, w_ab, alog, dtb)


def _conv_kernel(glu_ref, st_ref, w_ref, b_ref, lg_ref, lb_ref, out_ref, nst_ref, e_ref, *, bb, tt, rows):
    t = pl.program_id(1)

    @pl.when(t == 0)
    def _():
        e_ref[:, 0:CONV_HALO, :] = st_ref[...]

    e_ref[:, CONV_HALO:CONV_HALO + tt, :] = glu_ref[...]
    off = CONV_HALO - (CONV_WIDTH - 1)
    for b in range(bb):
        for c in range(tt // rows):
            r0 = c * rows
            acc = jnp.zeros((rows, D_CONV), F32) + b_ref[...]
            for j in range(CONV_WIDTH):
                acc = acc + w_ref[j:j + 1, :] * e_ref[b, r0 + j + off:r0 + j + off + rows, :]
            mu = jnp.mean(acc, axis=-1, keepdims=True)
            xc = acc - mu
            var = jnp.mean(xc * xc, axis=-1, keepdims=True)
            y = xc * lax.rsqrt(var + EPS) * lg_ref[...] + lb_ref[...]
            out_ref[b, r0:r0 + rows, :] = _silu(y).astype(out_ref.dtype)
    tail = e_ref[:, tt:tt + CONV_HALO, :]
    nst_ref[...] = tail
    e_ref[:, 0:CONV_HALO, :] = tail


def _conv_branch(glu, state32, dw_w, dw_b, ln_g, ln_b, bb, tt):
    bsz, t_len, c = glu.shape
    assert bsz % bb == 0 and t_len % tt == 0
    rows = min(tt, 32)
    kern = functools.partial(_conv_kernel, bb=bb, tt=tt, rows=rows)
    const = lambda b, t: (0, 0)
    return pl.pallas_call(
        kern,
        grid=(bsz // bb, t_len // tt),
        in_specs=[
            pl.BlockSpec((bb, tt, c), lambda b, t: (b, t, 0)),
            pl.BlockSpec((bb, CONV_HALO, c), lambda b, t: (b, 0, 0)),
            pl.BlockSpec((CONV_WIDTH, c), const),
            pl.BlockSpec((1, c), const),
            pl.BlockSpec((1, c), const),
            pl.BlockSpec((1, c), const),
        ],
        out_specs=[
            pl.BlockSpec((bb, tt, c), lambda b, t: (b, t, 0)),
            pl.BlockSpec((bb, CONV_HALO, c), lambda b, t: (b, 0, 0)),
        ],
        out_shape=[
            jax.ShapeDtypeStruct((bsz, t_len, c), BF16),
            jax.ShapeDtypeStruct((bsz, CONV_HALO, c), F32),
        ],
        scratch_shapes=[pltpu.VMEM((bb, CONV_HALO + tt, c), F32)],
        compiler_params=pltpu.CompilerParams(
            dimension_semantics=("arbitrary", "arbitrary"), vmem_limit_bytes=VMEM_LIMIT_BYTES),
    )(glu, state32, dw_w, dw_b, ln_g, ln_b)


def _chunk_masks(seq_len):
    i = lax.broadcasted_iota(jnp.int32, (CHUNK, CHUNK), 0)
    j = lax.broadcasted_iota(jnp.int32, (CHUNK, CHUNK), 1)
    same = (i // seq_len) == (j // seq_len)
    incl = same & (i >= j)
    upper = same & (i <= j)
    strict = same & (i > j)
    last = j == (i // seq_len) * seq_len + (seq_len - 1)
    levels = []
    blk = 1
    while blk < seq_len:
        levels.append(((i // (2 * blk)) == (j // (2 * blk))) & (((i // blk) % 2) == 1) & (((j // blk) % 2) == 0))
        blk *= 2
    eye = i == j
    return incl, upper, strict, last, levels, eye


def _unit_lower_inverse(a, levels, eye):
    x = jnp.where(eye, 1.0, 0.0) - jnp.where(levels[0], a, 0.0)
    for m in levels[1:]:
        am = jnp.where(m, a, 0.0)
        x = x - _dot_delta(_dot_delta(x, am), x)
    return x


def _lane_col(x, lane):
    return jnp.broadcast_to(x[:, lane:lane + 1], (x.shape[0], LANES))


def _l2norm(x):
    return x * lax.rsqrt(jnp.sum(x * x, axis=-1, keepdims=True) + EPS)


def _chunk_prepare(qkv, gbt, masks):
    incl, upper, strict, last, levels, eye = masks
    gc = _dot_f32(jnp.where(incl, 1.0, 0.0), gbt)
    gct = _dot_f32(gbt, jnp.where(upper, 1.0, 0.0), (((0,), (0,)), ((), ())))
    glast = _dot_f32(jnp.where(last, 1.0, 0.0), gc)
    out = []
    for h in range(DN_HEADS):
        q = _l2norm(qkv[:, h * DN_HEAD_DIM:(h + 1) * DN_HEAD_DIM]) * (DN_HEAD_DIM ** -0.5)
        k = _l2norm(qkv[:, DN_WIDTH + h * DN_HEAD_DIM:DN_WIDTH + (h + 1) * DN_HEAD_DIM])
        v = qkv[:, 2 * DN_WIDTH + h * DN_HEAD_DIM:2 * DN_WIDTH + (h + 1) * DN_HEAD_DIM]
        gcol = _lane_col(gc, h)
        grow = jnp.broadcast_to(gct[h:h + 1, :], (CHUNK, CHUNK))
        beta = _lane_col(gbt, DN_HEADS + h)
        gl = _lane_col(glast, h)
        decay = jnp.exp(jnp.where(incl, gcol - grow, -jnp.inf))
        kb = k * beta
        nt = (((1,), (1,)), ((), ()))
        a = jnp.where(strict, _dot_delta(kb, k, nt) * decay, 0.0)
        tinv = _unit_lower_inverse(a, levels, eye)
        egc = jnp.exp(gcol)
        sol = _dot_delta(tinv, jnp.concatenate([v * beta, kb * egc], axis=1))
        value, kcum = sol[:, :DN_HEAD_DIM], sol[:, DN_HEAD_DIM:]
        scores = _dot_delta(q, k, nt) * decay
        out.append((value, kcum, scores, q * egc, k * jnp.exp(gl - gcol), jnp.exp(gl)))
    return out


def _gated_out_norm(o, z, ng):
    y = o * lax.rsqrt(jnp.mean(o * o, axis=-1, keepdims=True) + EPS) * ng
    return y * _silu(z)


def _short_conv(e_ref, w_ref, tt):
    off = SHORT_HALO - (SHORT_WIDTH - 1)
    acc = w_ref[0:1, :] * e_ref[off:off + tt, :]
    for j in range(1, SHORT_WIDTH):
        acc = acc + w_ref[j:j + 1, :] * e_ref[off + j:off + j + tt, :]
    return _silu(acc)


def _delta_prompt_kernel(qkv_ref, z_ref, gb_ref, st_ref, s0_ref, w_ref, ng_ref,
                         o_ref, nst_ref, sout_ref, e_ref, s_ref, *, tt):
    t = pl.program_id(1)

    @pl.when(t == 0)
    def _():
        e_ref[0:SHORT_HALO, :] = st_ref[0]
        s_ref[...] = s0_ref[0]

    e_ref[SHORT_HALO:SHORT_HALO + tt, :] = qkv_ref[0]
    qkv = _short_conv(e_ref, w_ref, tt)
    tail = e_ref[tt:tt + SHORT_HALO, :]
    nst_ref[0] = tail
    e_ref[0:SHORT_HALO, :] = tail

    masks = _chunk_masks(CHUNK)
    tn = (((0,), (0,)), ((), ()))
    for c in range(tt // CHUNK):
        r0 = c * CHUNK
        prep = _chunk_prepare(qkv[r0:r0 + CHUNK, :], gb_ref[0, r0:r0 + CHUNK, :], masks)
        for h in range(DN_HEADS):
            value, kcum, scores, qexp, kdec, egl = prep[h]
            s = s_ref[h]
            v_new = value - _dot_delta(kcum, s)
            o = _dot_delta(qexp, s) + _dot_delta(scores, v_new)
            s_ref[h] = s * egl[0:1, :] + _dot_delta(kdec, v_new, tn)
            zh = z_ref[0, r0:r0 + CHUNK, h * DN_HEAD_DIM:(h + 1) * DN_HEAD_DIM]
            o_ref[0, r0:r0 + CHUNK, h * DN_HEAD_DIM:(h + 1) * DN_HEAD_DIM] = (
                _gated_out_norm(o, zh, ng_ref[...]).astype(o_ref.dtype))
    sout_ref[0] = s_ref[...]


def _delta_prompt(qkv_pre, z, gb, state8, s0, conv_w, norm_g, tt):
    bsz, t_len, _ = qkv_pre.shape
    assert t_len % tt == 0 and tt % CHUNK == 0
    kern = functools.partial(_delta_prompt_kernel, tt=tt)
    seq = lambda b, t: (b, t, 0)
    per_b = lambda b, t: (b, 0, 0)
    return pl.pallas_call(
        kern,
        grid=(bsz, t_len // tt),
        in_specs=[
            pl.BlockSpec((1, tt, 3 * DN_WIDTH), seq),
            pl.BlockSpec((1, tt, DN_WIDTH), seq),
            pl.BlockSpec((1, tt, LANES), seq),
            pl.BlockSpec((1, SHORT_HALO, 3 * DN_WIDTH), per_b),
            pl.BlockSpec((1, DN_HEADS, DN_HEAD_DIM, DN_HEAD_DIM), lambda b, t: (b, 0, 0, 0)),
            pl.BlockSpec((SHORT_WIDTH, 3 * DN_WIDTH), lambda b, t: (0, 0)),
            pl.BlockSpec((1, DN_HEAD_DIM), lambda b, t: (0, 0)),
        ],
        out_specs=[
            pl.BlockSpec((1, tt, DN_WIDTH), seq),
            pl.BlockSpec((1, SHORT_HALO, 3 * DN_WIDTH), per_b),
            pl.BlockSpec((1, DN_HEADS, DN_HEAD_DIM, DN_HEAD_DIM), lambda b, t: (b, 0, 0, 0)),
        ],
        out_shape=[
            jax.ShapeDtypeStruct((bsz, t_len, DN_WIDTH), BF16),
            jax.ShapeDtypeStruct((bsz, SHORT_HALO, 3 * DN_WIDTH), F32),
            jax.ShapeDtypeStruct((bsz, DN_HEADS, DN_HEAD_DIM, DN_HEAD_DIM), F32),
        ],
        scratch_shapes=[
            pltpu.VMEM((SHORT_HALO + tt, 3 * DN_WIDTH), F32),
            pltpu.VMEM((DN_HEADS, DN_HEAD_DIM, DN_HEAD_DIM), F32),
        ],
        compiler_params=pltpu.CompilerParams(
            dimension_semantics=("arbitrary", "arbitrary"), vmem_limit_bytes=VMEM_LIMIT_BYTES),
    )(qkv_pre, z, gb, state8, s0, conv_w, norm_g)


def _delta_sample_kernel(qkv_ref, z_ref, gb_ref, st_ref, s0_ref, w_ref, ng_ref,
                         o_ref, nst_ref, sout_ref, e_ref, *, nseq, seq_len):
    qkv_rows = []
    for b in range(nseq):
        e_ref[0:SHORT_HALO, :] = st_ref[b]
        e_ref[SHORT_HALO:SHORT_HALO + seq_len, :] = qkv_ref[b]
        qkv_rows.append(_short_conv(e_ref, w_ref, seq_len))
        nst_ref[b] = e_ref[seq_len:seq_len + SHORT_HALO, :]
    qkv = jnp.concatenate(qkv_rows, axis=0)
    gbt = jnp.concatenate([gb_ref[b] for b in range(nseq)], axis=0)
    zt = jnp.concatenate([z_ref[b] for b in range(nseq)], axis=0)
    masks = _chunk_masks(seq_len)
    prep = _chunk_prepare(qkv, gbt, masks)
    tn = (((0,), (0,)), ((), ()))
    for h in range(DN_HEADS):
        value, kcum, scores, qexp, kdec, egl = prep[h]
        v_rows, o_rows = [], []
        for b in range(nseq):
            r = slice(b * seq_len, (b + 1) * seq_len)
            s = s0_ref[b, h]
            both = _dot_delta(jnp.concatenate([kcum[r], qexp[r]], axis=0), s)
            v_new = value[r] - both[:seq_len]
            v_rows.append(v_new)
            o_rows.append(both[seq_len:])
            sout_ref[b, h] = s * egl[b * seq_len:b * seq_len + 1, :] + _dot_delta(kdec[r], v_new, tn)
        o = jnp.concatenate(o_rows, axis=0) + _dot_delta(scores, jnp.concatenate(v_rows, axis=0))
        y = _gated_out_norm(o, zt[:, h * DN_HEAD_DIM:(h + 1) * DN_HEAD_DIM], ng_ref[...]).astype(o_ref.dtype)
        for b in range(nseq):
            o_ref[b, :, h * DN_HEAD_DIM:(h + 1) * DN_HEAD_DIM] = y[b * seq_len:(b + 1) * seq_len]


def _delta_sample(qkv_pre, z, gb, state8, s0, conv_w, norm_g):
    bsz, seq_len, _ = qkv_pre.shape
    assert CHUNK % seq_len == 0
    nseq = CHUNK // seq_len
    assert bsz % nseq == 0
    kern = functools.partial(_delta_sample_kernel, nseq=nseq, seq_len=seq_len)
    blk3 = lambda i: (i, 0, 0)
    blk4 = lambda i: (i, 0, 0, 0)
    return pl.pallas_call(
        kern,
        grid=(bsz // nseq,),
        in_specs=[
            pl.BlockSpec((nseq, seq_len, 3 * DN_WIDTH), blk3),
            pl.BlockSpec((nseq, seq_len, DN_WIDTH), blk3),
            pl.BlockSpec((nseq, seq_len, LANES), blk3),
            pl.BlockSpec((nseq, SHORT_HALO, 3 * DN_WIDTH), blk3),
            pl.BlockSpec((nseq, DN_HEADS, DN_HEAD_DIM, DN_HEAD_DIM), blk4),
            pl.BlockSpec((SHORT_WIDTH, 3 * DN_WIDTH), lambda i: (0, 0)),
            pl.BlockSpec((1, DN_HEAD_DIM), lambda i: (0, 0)),
        ],
        out_specs=[
            pl.BlockSpec((nseq, seq_len, DN_WIDTH), blk3),
            pl.BlockSpec((nseq, SHORT_HALO, 3 * DN_WIDTH), blk3),
            pl.BlockSpec((nseq, DN_HEADS, DN_HEAD_DIM, DN_HEAD_DIM), blk4),
        ],
        out_shape=[
            jax.ShapeDtypeStruct((bsz, seq_len, DN_WIDTH), BF16),
            jax.ShapeDtypeStruct((bsz, SHORT_HALO, 3 * DN_WIDTH), F32),
            jax.ShapeDtypeStruct((bsz, DN_HEADS, DN_HEAD_DIM, DN_HEAD_DIM), F32),
        ],
        scratch_shapes=[pltpu.VMEM((SHORT_HALO + seq_len, 3 * DN_WIDTH), F32)],
        compiler_params=pltpu.CompilerParams(
            dimension_semantics=("arbitrary",), vmem_limit_bytes=VMEM_LIMIT_BYTES),
    )(qkv_pre, z, gb, state8, s0, conv_w, norm_g)


def _mix_kernel(x_ref, ca_ref, oa_ref, sa_ref, sb_ref, wc_ref, wd_ref, wm_ref, g2_ref, rw_ref, rb_ref,
                x2_ref, h2_ref, ti_ref, tg_ref):
    ya = jnp.dot(ca_ref[...], wc_ref[...], preferred_element_type=F32)
    yb = jnp.dot(oa_ref[...], wd_ref[...], preferred_element_type=F32)
    mixed = sa_ref[...] * ya + sb_ref[...] * yb
    x2 = x_ref[...] + jnp.dot(mixed.astype(BF16), wm_ref[...], preferred_element_type=F32)
    x2_ref[...] = x2
    h2 = x2 * lax.rsqrt(jnp.mean(x2 * x2, axis=-1, keepdims=True) + EPS) * g2_ref[...]
    h2_ref[...] = h2
    logits = _dot_f32(h2, rw_ref[...]) + rb_ref[...]
    lane = lax.broadcasted_iota(jnp.int32, logits.shape, 1)
    lane_f = lane.astype(F32)
    logits = jnp.where(lane < N_EXPERTS, logits, -jnp.inf)
    top_vals = []
    ti = jnp.zeros(logits.shape, jnp.int32)
    for k in range(TOP_K):
        m = jnp.max(logits, axis=-1, keepdims=True)
        idx = jnp.min(jnp.where(logits == m, lane_f, float(LANES)), axis=-1, keepdims=True).astype(jnp.int32)
        ti = jnp.where(lane == k, idx, ti)
        top_vals.append(m)
        logits = jnp.where(lane == idx, -jnp.inf, logits)
    exps = [jnp.exp(v - top_vals[0]) for v in top_vals]
    den = exps[0] + exps[1] + exps[2] + exps[3]
    tg = jnp.zeros(logits.shape, F32)
    for k in range(TOP_K):
        tg = jnp.where(lane == k, exps[k] / den, tg)
    ti_ref[...] = ti
    tg_ref[...] = tg


def _mix(x, cact, oact, siga, sigb, w_conv_out, w_delta_out, w_merge_out, norm2_g, router_w, router_b, tm):
    n, d = x.shape
    assert n % tm == 0
    row = lambda i: (i, 0)
    const = lambda i: (0, 0)
    return pl.pallas_call(
        _mix_kernel,
        grid=(n // tm,),
        in_specs=[
            pl.BlockSpec((tm, d), row),
            pl.BlockSpec((tm, D_CONV), row),
            pl.BlockSpec((tm, DN_WIDTH), row),
            pl.BlockSpec((tm, d), row),
            pl.BlockSpec((tm, d), row),
            pl.BlockSpec((D_CONV, d), const),
            pl.BlockSpec((DN_WIDTH, d), const),
            pl.BlockSpec((d, d), const),
            pl.BlockSpec((1, d), const),
            pl.BlockSpec((d, LANES), const),
            pl.BlockSpec((1, LANES), const),
        ],
        out_specs=[
            pl.BlockSpec((tm, d), row),
            pl.BlockSpec((tm, d), row),
            pl.BlockSpec((tm, LANES), row),
            pl.BlockSpec((tm, LANES), row),
        ],
        out_shape=[
            jax.ShapeDtypeStruct((n, d), F32),
            jax.ShapeDtypeStruct((n, d), F32),
            jax.ShapeDtypeStruct((n, LANES), jnp.int32),
            jax.ShapeDtypeStruct((n, LANES), F32),
        ],
        compiler_params=pltpu.CompilerParams(
            dimension_semantics=("arbitrary",), vmem_limit_bytes=VMEM_LIMIT_BYTES),
    )(x, cact, oact, siga, sigb, w_conv_out, w_delta_out, w_merge_out, norm2_g, router_w, router_b)


GATHER_ROWS = 128


def _gather_rows_kernel(idx_ref, src_ref, out_ref, sem):
    i = pl.program_id(0)
    for r in range(GATHER_ROWS):
        pltpu.make_async_copy(src_ref.at[pl.ds(idx_ref[i, r], 1)], out_ref.at[pl.ds(r, 1)], sem).start()
    pltpu.make_async_copy(src_ref.at[pl.ds(0, GATHER_ROWS)], out_ref, sem).wait()


def _gather_rows(src, idx2d):
    nblk = idx2d.shape[0]
    d = src.shape[1]
    return pl.pallas_call(
        _gather_rows_kernel,
        grid_spec=pltpu.PrefetchScalarGridSpec(
            num_scalar_prefetch=1,
            grid=(nblk,),
            in_specs=[pl.BlockSpec(memory_space=pl.ANY)],
            out_specs=pl.BlockSpec((GATHER_ROWS, d), lambda i, idx: (i, 0)),
            scratch_shapes=[pltpu.SemaphoreType.DMA(())],
        ),
        out_shape=jax.ShapeDtypeStruct((nblk * GATHER_ROWS, d), src.dtype),
        compiler_params=pltpu.CompilerParams(dimension_semantics=("arbitrary",)),
    )(idx2d, src)


def _moe_ffn_kernel(be_ref, nv_ref, xs_ref, wgu_ref, bgu_ref, wd_ref, bd_ref, out_ref, wgu_bf, wd_bf):
    i = pl.program_id(0)
    prev = be_ref[jnp.maximum(i - 1, 0)]
    changed = jnp.logical_or(i == 0, be_ref[i] != prev)

    @pl.when(changed)
    def _():
        wgu_bf[...] = wgu_ref[0].astype(BF16)
        wd_bf[...] = wd_ref[0].astype(BF16)

    @pl.when(i < nv_ref[0])
    def _():
        f = wd_bf.shape[0]
        gu = jnp.dot(xs_ref[...].astype(BF16), wgu_bf[...], preferred_element_type=F32) + bgu_ref[0]
        gt = jnp.minimum(gu[:, :f], SWIGLU_LIMIT)
        up = jnp.clip(gu[:, f:], -SWIGLU_LIMIT, SWIGLU_LIMIT)
        act = (up + 1.0) * (gt * _sigmoid(SWIGLU_ALPHA * gt))
        out_ref[...] = jnp.dot(act.astype(BF16), wd_bf[...], preferred_element_type=F32) + bd_ref[0]

    @pl.when(i >= nv_ref[0])
    def _():
        out_ref[...] = jnp.zeros(out_ref.shape, out_ref.dtype)


def _moe_ffn(xs, block_e, nvalid, w_gate_up, b_gate_up, w_down, b_down, tm):
    rows, d = xs.shape
    ne, _, f2 = w_gate_up.shape
    f = f2 // 2
    nblk = rows // tm
    return pl.pallas_call(
        _moe_ffn_kernel,
        grid_spec=pltpu.PrefetchScalarGridSpec(
            num_scalar_prefetch=2,
            grid=(nblk,),
            in_specs=[
                pl.BlockSpec((tm, d), lambda i, be, nv: (i, 0)),
                pl.BlockSpec((1, d, f2), lambda i, be, nv: (be[i], 0, 0)),
                pl.BlockSpec((1, 1, f2), lambda i, be, nv: (be[i], 0, 0)),
                pl.BlockSpec((1, f, d), lambda i, be, nv: (be[i], 0, 0)),
                pl.BlockSpec((1, 1, d), lambda i, be, nv: (be[i], 0, 0)),
            ],
            out_specs=pl.BlockSpec((tm, d), lambda i, be, nv: (i, 0)),
            scratch_shapes=[pltpu.VMEM((d, f2), BF16), pltpu.VMEM((f, d), BF16)],
        ),
        out_shape=jax.ShapeDtypeStruct((rows, d), F32),
        compiler_params=pltpu.CompilerParams(
            dimension_semantics=("arbitrary",), vmem_limit_bytes=VMEM_LIMIT_BYTES),
    )(block_e, nvalid, xs, w_gate_up, b_gate_up.reshape(ne, 1, f2), w_down, b_down.reshape(ne, 1, d))


def _combine_kernel(slot_ref, yb_ref, x2_ref, tg_ref, fg_ref, out_ref, buf, sem, *, tc):
    i = pl.program_id(0)
    per_row = GATHER_ROWS // TOP_K
    for t in range(tc):
        for k in range(TOP_K):
            s = slot_ref[i * (tc // per_row) + t // per_row, (t % per_row) * TOP_K + k]
            pltpu.make_async_copy(yb_ref.at[pl.ds(s, 1)], buf.at[k, pl.ds(t, 1)], sem).start()
    for k in range(TOP_K):
        pltpu.make_async_copy(yb_ref.at[pl.ds(0, tc)], buf.at[k], sem).wait()
    tg = tg_ref[...]
    y = x2_ref[...]
    for k in range(TOP_K):
        y = y + tg[:, k:k + 1] * buf[k]
    out_ref[...] = y * lax.rsqrt(jnp.mean(y * y, axis=-1, keepdims=True) + EPS) * fg_ref[...]


def _combine(slot2d, yb, x2, tg, final_g, tc):
    n, d = x2.shape
    assert n % tc == 0 and tc % (GATHER_ROWS // TOP_K) == 0
    kern = functools.partial(_combine_kernel, tc=tc)
    return pl.pallas_call(
        kern,
        grid_spec=pltpu.PrefetchScalarGridSpec(
            num_scalar_prefetch=1,
            grid=(n // tc,),
            in_specs=[
                pl.BlockSpec(memory_space=pl.ANY),
                pl.BlockSpec((tc, d), lambda i, s: (i, 0)),
                pl.BlockSpec((tc, LANES), lambda i, s: (i, 0)),
                pl.BlockSpec((1, d), lambda i, s: (0, 0)),
            ],
            out_specs=pl.BlockSpec((tc, d), lambda i, s: (i, 0)),
            scratch_shapes=[pltpu.VMEM((TOP_K, tc, d), F32), pltpu.SemaphoreType.DMA(())],
        ),
        out_shape=jax.ShapeDtypeStruct((n, d), F32),
        compiler_params=pltpu.CompilerParams(
            dimension_semantics=("arbitrary",), vmem_limit_bytes=VMEM_LIMIT_BYTES),
    )(slot2d, yb, x2, tg, final_g)


def _route(top_idx, tm, n_blocks):
    n = top_idx.shape[0]
    nk = n * TOP_K
    flat_e = top_idx.reshape(-1)
    onehot = (flat_e[:, None] == jnp.arange(N_EXPERTS, dtype=jnp.int32)[None, :]).astype(jnp.int32)
    csum = jnp.cumsum(onehot, axis=0)
    rank = jnp.sum(csum * onehot, axis=1) - 1
    counts = csum[-1]
    padded = (counts + tm - 1) // tm * tm
    pad_end = jnp.cumsum(padded)
    pad_start = pad_end - padded
    dest = pad_start[flat_e] + rank
    row_tok = jnp.zeros((n_blocks * tm,), jnp.int32).at[dest].set(jnp.arange(nk, dtype=jnp.int32) // TOP_K)
    nvalid = (pad_end[-1] // tm).astype(jnp.int32)
    blk = jnp.arange(n_blocks, dtype=jnp.int32)
    owner = jnp.sum((pad_end[None, :] <= (blk * tm)[:, None]).astype(jnp.int32), axis=1)
    block_e = jnp.minimum(owner, N_EXPERTS - 1)
    block_e = jnp.where(blk < nvalid, block_e, block_e[jnp.maximum(nvalid - 1, 0)])
    return row_tok, dest, block_e, nvalid.reshape(1)


TOKEN_TILE = 256
MOE_TILE = 256
COMBINE_TILE = 128


def _pad_lanes(v, width=LANES):
    v = v.reshape(1, -1)
    return jnp.pad(v, ((0, 0), (0, width - v.shape[1])))


def kernel(x_prompt, x_sample, state_conv, state_short_conv, state_delta, norm1_g, w_in, conv_dw_w,
           conv_dw_b, conv_ln_g, conv_ln_b, w_conv_out, short_conv_w, a_log, dt_bias, delta_norm_g,
           w_delta_out, w_merge_out, norm2_g, router_w, router_b, w_gate_up, b_gate_up, w_down, b_down,
           final_norm_g):
    depth = w_in.shape[0]
    assert depth == 1
    bp, tp, d = x_prompt.shape
    bs, ts, _ = x_sample.shape
    n_p, n_s = bp * tp, bs * ts
    n = n_p + n_s
    l = 0

    x_all = jnp.concatenate([x_prompt.reshape(n_p, d), x_sample.reshape(n_s, d)], axis=0)

    o_ab = 2 * D_CONV + 4 * DN_WIDTH
    w = w_in[l]
    w_main = jnp.concatenate([w[:, :o_ab], w[:, o_ab + 2 * DN_HEADS:]], axis=1).astype(BF16)
    w_ab = jnp.pad(w[:, o_ab:o_ab + 2 * DN_HEADS], ((0, 0), (0, LANES - 2 * DN_HEADS)))

    glu, qkv_pre, z, gb, siga, sigb = _inproj(
        x_all, norm1_g[l].reshape(1, d), w_main, w_ab, _pad_lanes(a_log[l]), _pad_lanes(dt_bias[l]), TOKEN_TILE)

    def split(a):
        c = a.shape[1]
        return a[:n_p].reshape(bp, tp, c), a[n_p:].reshape(bs, ts, c)

    glu_p, glu_s = split(glu)
    qkv_p, qkv_s = split(qkv_pre)
    z_p, z_s = split(z)
    gb_p, gb_s = split(gb)

    dw = (conv_dw_w[l], conv_dw_b[l].reshape(1, -1), conv_ln_g[l].reshape(1, -1), conv_ln_b[l].reshape(1, -1))
    pad_c = CONV_HALO - (CONV_WIDTH - 1)
    st_c_p = jnp.zeros((bp, CONV_HALO, D_CONV), F32)
    st_c_s = jnp.pad(state_conv[l], ((0, 0), (pad_c, 0), (0, 0)))
    cact_p, nconv_p = _conv_branch(glu_p, st_c_p, *dw, bb=1, tt=256)
    cact_s, nconv_s = _conv_branch(glu_s, st_c_s, *dw, bb=8, tt=ts)

    pad_s = SHORT_HALO - (SHORT_WIDTH - 1)
    st_s_p = jnp.zeros((bp, SHORT_HALO, 3 * DN_WIDTH), F32)
    st_s_s = jnp.pad(state_short_conv[l], ((0, 0), (pad_s, 0), (0, 0)))
    s0_p = jnp.zeros((bp, DN_HEADS, DN_HEAD_DIM, DN_HEAD_DIM), F32)
    ng = delta_norm_g[l].reshape(1, -1)
    oact_p, nshort_p, s_p = _delta_prompt(qkv_p, z_p, gb_p, st_s_p, s0_p, short_conv_w[l], ng, tt=256)
    oact_s, nshort_s, s_s = _delta_sample(qkv_s, z_s, gb_s, st_s_s, state_delta[l], short_conv_w[l], ng)

    cact = jnp.concatenate([cact_p.reshape(n_p, -1), cact_s.reshape(n_s, -1)], axis=0)
    oact = jnp.concatenate([oact_p.reshape(n_p, -1), oact_s.reshape(n_s, -1)], axis=0)

    rw = jnp.pad(router_w[l], ((0, 0), (0, LANES - N_EXPERTS)))
    x2, h2, ti, tg = _mix(x_all, cact, oact, siga, sigb, w_conv_out[l].astype(BF16), w_delta_out[l].astype(BF16),
                          w_merge_out[l].astype(BF16), norm2_g[l].reshape(1, d), rw, _pad_lanes(router_b[l]),
                          TOKEN_TILE)

    n_blocks = -(-(n * TOP_K) // MOE_TILE) + N_EXPERTS
    row_tok, dest, block_e, nvalid = _route(ti[:, :TOP_K], MOE_TILE, n_blocks)
    xs = _gather_rows(h2, row_tok.reshape(-1, GATHER_ROWS))
    yb = _moe_ffn(xs, block_e, nvalid, w_gate_up[l], b_gate_up[l], w_down[l], b_down[l], MOE_TILE)
    y = _combine(dest.reshape(-1, GATHER_ROWS), yb, x2, tg, final_norm_g.reshape(1, d), COMBINE_TILE)

    y_prompt = y[:n_p].reshape(bp, tp, d)
    y_sample = y[n_p:].reshape(bs, ts, d)
    conv_p = nconv_p[:, pad_c:, :][None]
    conv_s = nconv_s[:, pad_c:, :][None]
    short_p = nshort_p[:, pad_s:, :][None]
    short_s = nshort_s[:, pad_s:, :][None]
    return (y_prompt, y_sample, conv_p, short_p, s_p[None], conv_s, short_s, s_s[None])
```

```python
import functools

import jax
import jax.numpy as jnp
from jax import lax
from jax.experimental import pallas as pl
from jax.experimental.pallas import tpu as pltpu

F32 = jnp.float32
BF16 = jnp.bfloat16
EPS = 1e-6

LANES = 128
SUBLANES = 8
VMEM_LIMIT_BYTES = 56 * 1024 * 1024

D_CONV = 512
CONV_WIDTH = 31
DN_HEADS = 4
DN_HEAD_DIM = 128
DN_WIDTH = DN_HEADS * DN_HEAD_DIM
SHORT_WIDTH = 4
N_EXPERTS = 32
TOP_K = 4
SWIGLU_LIMIT = 7.0
SWIGLU_ALPHA = 1.702

CHUNK = 128
CONV_HALO = 32
SHORT_HALO = 8
HIGHEST = lax.Precision.HIGHEST

TOKEN_TILE = 256
SEQ_TILE = 256
MOE_TILE = 256


def _sigmoid(x):
    return 1.0 / (1.0 + jnp.exp(-x))


def _silu(x):
    return x * _sigmoid(x)


def _dot_f32(a, b, dims=(((1,), (0,)), ((), ()))):
    return lax.dot_general(a, b, dims, precision=HIGHEST, preferred_element_type=F32)


def _dot_delta(a, b, dims=(((1,), (0,)), ((), ()))):
    return lax.dot_general(a.astype(BF16), b.astype(BF16), dims, preferred_element_type=F32)


def _params(n_axes):
    return pltpu.CompilerParams(dimension_semantics=("arbitrary",) * n_axes, vmem_limit_bytes=VMEM_LIMIT_BYTES)


def _two_source_specs(tm, d, n_first_tiles):
    first = pl.BlockSpec((tm, d), lambda i, *_: (jnp.minimum(i, n_first_tiles - 1), 0))
    second = pl.BlockSpec((tm, d), lambda i, *_: (jnp.maximum(i - n_first_tiles, 0), 0))
    return first, second


def _inproj_kernel(xp_ref, xs_ref, g_ref, w_ref, wab_ref, alog_ref, dtb_ref,
                   glu_ref, qkv_ref, z_ref, gb_ref, sa_ref, sb_ref, *, n_prompt_tiles):
    x = jnp.where(pl.program_id(0) < n_prompt_tiles, xp_ref[...], xs_ref[...])
    h = x * lax.rsqrt(jnp.mean(x * x, axis=-1, keepdims=True) + EPS) * g_ref[...]
    hb = h.astype(BF16)

    def mm(lo, hi):
        return jnp.dot(hb, w_ref[:, lo:hi], preferred_element_type=F32)

    o_gate, o_qkv, o_z = D_CONV, 2 * D_CONV, 2 * D_CONV + 3 * DN_WIDTH
    o_ga = o_z + DN_WIDTH
    d = x.shape[-1]
    glu_ref[...] = mm(0, o_gate) * _sigmoid(mm(o_gate, o_qkv))
    qkv_ref[...] = mm(o_qkv, o_z)
    z_ref[...] = mm(o_z, o_ga)
    sa_ref[...] = _sigmoid(mm(o_ga, o_ga + d))
    sb_ref[...] = _sigmoid(mm(o_ga + d, o_ga + 2 * d))
    ab = _dot_f32(h, wab_ref[...])
    xa = ab + dtb_ref[...]
    softplus = jnp.maximum(xa, 0.0) + jnp.log(1.0 + jnp.exp(-jnp.abs(xa)))
    g = -jnp.exp(alog_ref[...]) * softplus
    lane = lax.broadcasted_iota(jnp.int32, ab.shape, 1)
    gb_ref[...] = jnp.where(lane < DN_HEADS, g, _sigmoid(ab))


def _inproj(x_p, x_s, norm_g, w_main, w_ab, alog, dtb, tm):
    (n_p, d), n_s = x_p.shape, x_s.shape[0]
    assert n_p % tm == 0 and n_s % tm == 0
    n = n_p + n_s
    wcols = w_main.shape[1]
    row = lambda i: (i, 0)
    const = lambda i: (0, 0)
    outs = [(D_CONV, F32), (3 * DN_WIDTH, F32), (DN_WIDTH, F32), (LANES, F32), (d, F32), (d, F32)]
    return pl.pallas_call(
        functools.partial(_inproj_kernel, n_prompt_tiles=n_p // tm),
        grid=(n // tm,),
        in_specs=[
            *_two_source_specs(tm, d, n_p // tm),
            pl.BlockSpec((1, d), const),
            pl.BlockSpec((d, wcols), const),
            pl.BlockSpec((d, LANES), const),
            pl.BlockSpec((1, LANES), const),
            pl.BlockSpec((1, LANES), const),
        ],
        out_specs=[pl.BlockSpec((tm, c), row) for c, _ in outs],
        out_shape=[jax.ShapeDtypeStruct((n, c), dt) for c, dt in outs],
        compiler_params=_params(1),
    )(x_p, x_s, norm_g, w_main, w_ab, alog, dtb)


def _conv_kernel(glu_ref, st_ref, w_ref, b_ref, lg_ref, lb_ref, out_ref, nst_ref, e_ref, *, bb, tt, rows):
    t = pl.program_id(1)

    @pl.when(t == 0)
    def _():
        e_ref[:, 0:CONV_HALO, :] = st_ref[...]

    for b in range(bb):
        e_ref[b, CONV_HALO:CONV_HALO + tt, :] = glu_ref[b * tt:(b + 1) * tt, :]
    off = CONV_HALO - (CONV_WIDTH - 1)
    for b in range(bb):
        for c in range(tt // rows):
            r0 = c * rows
            acc = jnp.zeros((rows, D_CONV), F32) + b_ref[...]
            for j in range(CONV_WIDTH):
                acc = acc + w_ref[j:j + 1, :] * e_ref[b, r0 + j + off:r0 + j + off + rows, :]
            mu = jnp.mean(acc, axis=-1, keepdims=True)
            xc = acc - mu
            var = jnp.mean(xc * xc, axis=-1, keepdims=True)
            y = xc * lax.rsqrt(var + EPS) * lg_ref[...] + lb_ref[...]
            out_ref[b * tt + r0:b * tt + r0 + rows, :] = _silu(y).astype(out_ref.dtype)
    tail = e_ref[:, tt:tt + CONV_HALO, :]
    nst_ref[...] = tail
    e_ref[:, 0:CONV_HALO, :] = tail


def _conv_branch(glu, state32, dw_w, dw_b, ln_g, ln_b, *, row0, bsz, t_len, bb, tt):
    c = glu.shape[1]
    assert bsz % bb == 0 and t_len % tt == 0 and row0 % (bb * tt) == 0
    nt = t_len // tt
    blk0 = row0 // (bb * tt)
    rows = min(tt, 32)
    kern = functools.partial(_conv_kernel, bb=bb, tt=tt, rows=rows)
    const = lambda b, t: (0, 0)
    return pl.pallas_call(
        kern,
        grid=(bsz // bb, nt),
        in_specs=[
            pl.BlockSpec((bb * tt, c), lambda b, t: (blk0 + b * nt + t, 0)),
            pl.BlockSpec((bb, CONV_HALO, c), lambda b, t: (b, 0, 0)),
            pl.BlockSpec((CONV_WIDTH, c), const),
            pl.BlockSpec((1, c), const),
            pl.BlockSpec((1, c), const),
            pl.BlockSpec((1, c), const),
        ],
        out_specs=[
            pl.BlockSpec((bb * tt, c), lambda b, t: (b * nt + t, 0)),
            pl.BlockSpec((bb, CONV_HALO, c), lambda b, t: (b, 0, 0)),
        ],
        out_shape=[
            jax.ShapeDtypeStruct((bsz * t_len, c), BF16),
            jax.ShapeDtypeStruct((bsz, CONV_HALO, c), F32),
        ],
        scratch_shapes=[pltpu.VMEM((bb, CONV_HALO + tt, c), F32)],
        compiler_params=_params(2),
    )(glu, state32, dw_w, dw_b, ln_g, ln_b)


def _chunk_masks(seq_len):
    i = lax.broadcasted_iota(jnp.int32, (CHUNK, CHUNK), 0)
    j = lax.broadcasted_iota(jnp.int32, (CHUNK, CHUNK), 1)
    same = (i // seq_len) == (j // seq_len)
    incl = same & (i >= j)
    upper = same & (i <= j)
    strict = same & (i > j)
    last = j == (i // seq_len) * seq_len + (seq_len - 1)
    levels = []
    blk = 1
    while blk < seq_len:
        levels.append(((i // (2 * blk)) == (j // (2 * blk))) & (((i // blk) % 2) == 1) & (((j // blk) % 2) == 0))
        blk *= 2
    eye = i == j
    return incl, upper, strict, last, levels, eye


def _unit_lower_inverse(a, levels, eye):
    x = jnp.where(eye, 1.0, 0.0) - jnp.where(levels[0], a, 0.0)
    for m in levels[1:]:
        am = jnp.where(m, a, 0.0)
        x = x - _dot_delta(_dot_delta(x, am), x)
    return x


def _lane_col(x, lane):
    return jnp.broadcast_to(x[:, lane:lane + 1], (x.shape[0], LANES))


def _l2norm(x):
    return x * lax.rsqrt(jnp.sum(x * x, axis=-1, keepdims=True) + EPS)


def _chunk_prepare(qkv, gbt, masks):
    incl, upper, strict, last, levels, eye = masks
    gc = _dot_f32(jnp.where(incl, 1.0, 0.0), gbt)
    gct = _dot_f32(gbt, jnp.where(upper, 1.0, 0.0), (((0,), (0,)), ((), ())))
    glast = _dot_f32(jnp.where(last, 1.0, 0.0), gc)
    out = []
    for h in range(DN_HEADS):
        q = _l2norm(qkv[:, h * DN_HEAD_DIM:(h + 1) * DN_HEAD_DIM]) * (DN_HEAD_DIM ** -0.5)
        k = _l2norm(qkv[:, DN_WIDTH + h * DN_HEAD_DIM:DN_WIDTH + (h + 1) * DN_HEAD_DIM])
        v = qkv[:, 2 * DN_WIDTH + h * DN_HEAD_DIM:2 * DN_WIDTH + (h + 1) * DN_HEAD_DIM]
        gcol = _lane_col(gc, h)
        grow = jnp.broadcast_to(gct[h:h + 1, :], (CHUNK, CHUNK))
        beta = _lane_col(gbt, DN_HEADS + h)
        gl = _lane_col(glast, h)
        decay = jnp.exp(jnp.where(incl, gcol - grow, -jnp.inf))
        kb = k * beta
        nt = (((1,), (1,)), ((), ()))
        a = jnp.where(strict, _dot_delta(kb, k, nt) * decay, 0.0)
        tinv = _unit_lower_inverse(a, levels, eye)
        egc = jnp.exp(gcol)
        sol = _dot_delta(tinv, jnp.concatenate([v * beta, kb * egc], axis=1))
        value, kcum = sol[:, :DN_HEAD_DIM], sol[:, DN_HEAD_DIM:]
        scores = _dot_delta(q, k, nt) * decay
        out.append((value, kcum, scores, q * egc, k * jnp.exp(gl - gcol), jnp.exp(gl)))
    return out


def _gated_out_norm(o, z, ng):
    y = o * lax.rsqrt(jnp.mean(o * o, axis=-1, keepdims=True) + EPS) * ng
    return y * _silu(z)


def _short_conv(e_ref, w_ref, tt):
    off = SHORT_HALO - (SHORT_WIDTH - 1)
    acc = w_ref[0:1, :] * e_ref[off:off + tt, :]
    for j in range(1, SHORT_WIDTH):
        acc = acc + w_ref[j:j + 1, :] * e_ref[off + j:off + j + tt, :]
    return _silu(acc)


def _delta_prompt_kernel(qkv_ref, z_ref, gb_ref, st_ref, s0_ref, w_ref, ng_ref,
                         o_ref, nst_ref, sout_ref, e_ref, s_ref, *, tt):
    t = pl.program_id(1)

    @pl.when(t == 0)
    def _():
        e_ref[0:SHORT_HALO, :] = st_ref[0]
        s_ref[...] = s0_ref[0]

    e_ref[SHORT_HALO:SHORT_HALO + tt, :] = qkv_ref[...]
    qkv = _short_conv(e_ref, w_ref, tt)
    tail = e_ref[tt:tt + SHORT_HALO, :]
    nst_ref[0] = tail
    e_ref[0:SHORT_HALO, :] = tail

    masks = _chunk_masks(CHUNK)
    tn = (((0,), (0,)), ((), ()))
    for c in range(tt // CHUNK):
        r0 = c * CHUNK
        prep = _chunk_prepare(qkv[r0:r0 + CHUNK, :], gb_ref[r0:r0 + CHUNK, :], masks)
        for h in range(DN_HEADS):
            value, kcum, scores, qexp, kdec, egl = prep[h]
            s = s_ref[h]
            v_new = value - _dot_delta(kcum, s)
            o = _dot_delta(qexp, s) + _dot_delta(scores, v_new)
            s_ref[h] = s * egl[0:1, :] + _dot_delta(kdec, v_new, tn)
            zh = z_ref[r0:r0 + CHUNK, h * DN_HEAD_DIM:(h + 1) * DN_HEAD_DIM]
            o_ref[r0:r0 + CHUNK, h * DN_HEAD_DIM:(h + 1) * DN_HEAD_DIM] = (
                _gated_out_norm(o, zh, ng_ref[...]).astype(o_ref.dtype))
    sout_ref[0] = s_ref[...]


def _delta_prompt(qkv_pre, z, gb, state8, s0, conv_w, norm_g, *, bsz, t_len, tt):
    n = bsz * t_len
    assert t_len % tt == 0 and tt % CHUNK == 0
    nt = t_len // tt
    kern = functools.partial(_delta_prompt_kernel, tt=tt)
    tile = lambda b, t: (b * nt + t, 0)
    per_b = lambda b, t: (b, 0, 0)
    per_b4 = lambda b, t: (b, 0, 0, 0)
    return pl.pallas_call(
        kern,
        grid=(bsz, nt),
        in_specs=[
            pl.BlockSpec((tt, 3 * DN_WIDTH), tile),
            pl.BlockSpec((tt, DN_WIDTH), tile),
            pl.BlockSpec((tt, LANES), tile),
            pl.BlockSpec((1, SHORT_HALO, 3 * DN_WIDTH), per_b),
            pl.BlockSpec((1, DN_HEADS, DN_HEAD_DIM, DN_HEAD_DIM), per_b4),
            pl.BlockSpec((SHORT_WIDTH, 3 * DN_WIDTH), lambda b, t: (0, 0)),
            pl.BlockSpec((1, DN_HEAD_DIM), lambda b, t: (0, 0)),
        ],
        out_specs=[
            pl.BlockSpec((tt, DN_WIDTH), tile),
            pl.BlockSpec((1, SHORT_HALO, 3 * DN_WIDTH), per_b),
            pl.BlockSpec((1, DN_HEADS, DN_HEAD_DIM, DN_HEAD_DIM), per_b4),
        ],
        out_shape=[
            jax.ShapeDtypeStruct((n, DN_WIDTH), BF16),
            jax.ShapeDtypeStruct((bsz, SHORT_HALO, 3 * DN_WIDTH), F32),
            jax.ShapeDtypeStruct((bsz, DN_HEADS, DN_HEAD_DIM, DN_HEAD_DIM), F32),
        ],
        scratch_shapes=[
            pltpu.VMEM((SHORT_HALO + tt, 3 * DN_WIDTH), F32),
            pltpu.VMEM((DN_HEADS, DN_HEAD_DIM, DN_HEAD_DIM), F32),
        ],
        compiler_params=_params(2),
    )(qkv_pre, z, gb, state8, s0, conv_w, norm_g)


def _delta_sample_kernel(qkv_ref, z_ref, gb_ref, st_ref, s0_ref, w_ref, ng_ref, o_ref, nst_ref, sout_ref,
                         e_ref, *, nseq, seq_len):
    qkv_rows = []
    for b in range(nseq):
        e_ref[0:SHORT_HALO, :] = st_ref[b]
        e_ref[SHORT_HALO:SHORT_HALO + seq_len, :] = qkv_ref[b * seq_len:(b + 1) * seq_len, :]
        qkv_rows.append(_short_conv(e_ref, w_ref, seq_len))
        nst_ref[b] = e_ref[seq_len:seq_len + SHORT_HALO, :]
    qkv = jnp.concatenate(qkv_rows, axis=0)
    masks = _chunk_masks(seq_len)
    prep = _chunk_prepare(qkv, gb_ref[...], masks)
    tn = (((0,), (0,)), ((), ()))
    for h in range(DN_HEADS):
        value, kcum, scores, qexp, kdec, egl = prep[h]
        v_rows, o_rows = [], []
        for b in range(nseq):
            r = slice(b * seq_len, (b + 1) * seq_len)
            s = s0_ref[b, h]
            both = _dot_delta(jnp.concatenate([kcum[r], qexp[r]], axis=0), s)
            v_new = value[r] - both[:seq_len]
            v_rows.append(v_new)
            o_rows.append(both[seq_len:])
            sout_ref[b, h] = s * egl[b * seq_len:b * seq_len + 1, :] + _dot_delta(kdec[r], v_new, tn)
        o = jnp.concatenate(o_rows, axis=0) + _dot_delta(scores, jnp.concatenate(v_rows, axis=0))
        lanes = slice(h * DN_HEAD_DIM, (h + 1) * DN_HEAD_DIM)
        o_ref[:, lanes] = _gated_out_norm(o, z_ref[:, lanes], ng_ref[...]).astype(o_ref.dtype)


def _delta_sample(qkv_pre, z, gb, state8, s0, conv_w, norm_g, *, row0, bsz, seq_len):
    n = bsz * seq_len
    assert CHUNK % seq_len == 0
    nseq = CHUNK // seq_len
    assert bsz % nseq == 0 and row0 % CHUNK == 0
    blk0 = row0 // CHUNK
    kern = functools.partial(_delta_sample_kernel, nseq=nseq, seq_len=seq_len)
    tile = lambda i: (blk0 + i, 0)
    blk3 = lambda i: (i, 0, 0)
    blk4 = lambda i: (i, 0, 0, 0)
    return pl.pallas_call(
        kern,
        grid=(bsz // nseq,),
        in_specs=[
            pl.BlockSpec((CHUNK, 3 * DN_WIDTH), tile),
            pl.BlockSpec((CHUNK, DN_WIDTH), tile),
            pl.BlockSpec((CHUNK, LANES), tile),
            pl.BlockSpec((nseq, SHORT_HALO, 3 * DN_WIDTH), blk3),
            pl.BlockSpec((nseq, DN_HEADS, DN_HEAD_DIM, DN_HEAD_DIM), blk4),
            pl.BlockSpec((SHORT_WIDTH, 3 * DN_WIDTH), lambda i: (0, 0)),
            pl.BlockSpec((1, DN_HEAD_DIM), lambda i: (0, 0)),
        ],
        out_specs=[
            pl.BlockSpec((CHUNK, DN_WIDTH), lambda i: (i, 0)),
            pl.BlockSpec((nseq, SHORT_HALO, 3 * DN_WIDTH), blk3),
            pl.BlockSpec((nseq, DN_HEADS, DN_HEAD_DIM, DN_HEAD_DIM), blk4),
        ],
        out_shape=[
            jax.ShapeDtypeStruct((n, DN_WIDTH), BF16),
            jax.ShapeDtypeStruct((bsz, SHORT_HALO, 3 * DN_WIDTH), F32),
            jax.ShapeDtypeStruct((bsz, DN_HEADS, DN_HEAD_DIM, DN_HEAD_DIM), F32),
        ],
        scratch_shapes=[pltpu.VMEM((SHORT_HALO + seq_len, 3 * DN_WIDTH), F32)],
        compiler_params=_params(1),
    )(qkv_pre, z, gb, state8, s0, conv_w, norm_g)


def _mix_kernel(xp_ref, xs_ref, cap_ref, cas_ref, oap_ref, oas_ref, sa_ref, sb_ref, wc_ref, wd_ref, wm_ref,
                g2_ref, rw_ref, rb_ref, x2_ref, h2_ref, ti_ref, tg_ref, *, n_prompt_tiles):
    is_prompt = pl.program_id(0) < n_prompt_tiles
    x = jnp.where(is_prompt, xp_ref[...], xs_ref[...])
    ca = jnp.where(is_prompt, cap_ref[...], cas_ref[...])
    oa = jnp.where(is_prompt, oap_ref[...], oas_ref[...])
    ya = jnp.dot(ca, wc_ref[...], preferred_element_type=F32)
    yb = jnp.dot(oa, wd_ref[...], preferred_element_type=F32)
    mixed = sa_ref[...] * ya + sb_ref[...] * yb
    x2 = x + jnp.dot(mixed.astype(BF16), wm_ref[...], preferred_element_type=F32)
    x2_ref[...] = x2
    h2 = x2 * lax.rsqrt(jnp.mean(x2 * x2, axis=-1, keepdims=True) + EPS) * g2_ref[...]
    h2_ref[...] = h2
    logits = _dot_f32(h2, rw_ref[...]) + rb_ref[...]
    lane = lax.broadcasted_iota(jnp.int32, logits.shape, 1)
    lane_f = lane.astype(F32)
    logits = jnp.where(lane < N_EXPERTS, logits, -jnp.inf)
    top_vals = []
    ti = jnp.zeros(logits.shape, jnp.int32)
    for k in range(TOP_K):
        m = jnp.max(logits, axis=-1, keepdims=True)
        idx = jnp.min(jnp.where(logits == m, lane_f, float(LANES)), axis=-1, keepdims=True).astype(jnp.int32)
        ti = jnp.where(lane == k, idx, ti)
        top_vals.append(m)
        logits = jnp.where(lane == idx, -jnp.inf, logits)
    exps = [jnp.exp(v - top_vals[0]) for v in top_vals]
    den = exps[0] + exps[1] + exps[2] + exps[3]
    tg = jnp.zeros(logits.shape, F32)
    for k in range(TOP_K):
        tg = jnp.where(lane == k, exps[k] / den, tg)
    ti_ref[...] = ti
    tg_ref[...] = tg


def _mix(x_p, x_s, cact_p, cact_s, oact_p, oact_s, siga, sigb, w_conv_out, w_delta_out, w_merge_out, norm2_g,
         router_w, router_b, tm):
    (n_p, d), n_s = x_p.shape, x_s.shape[0]
    n = n_p + n_s
    row = lambda i: (i, 0)
    const = lambda i: (0, 0)
    return pl.pallas_call(
        functools.partial(_mix_kernel, n_prompt_tiles=n_p // tm),
        grid=(n // tm,),
        in_specs=[
            *_two_source_specs(tm, d, n_p // tm),
            *_two_source_specs(tm, D_CONV, n_p // tm),
            *_two_source_specs(tm, DN_WIDTH, n_p // tm),
            pl.BlockSpec((tm, d), row),
            pl.BlockSpec((tm, d), row),
            pl.BlockSpec((D_CONV, d), const),
            pl.BlockSpec((DN_WIDTH, d), const),
            pl.BlockSpec((d, d), const),
            pl.BlockSpec((1, d), const),
            pl.BlockSpec((d, LANES), const),
            pl.BlockSpec((1, LANES), const),
        ],
        out_specs=[
            pl.BlockSpec((tm, d), row),
            pl.BlockSpec((tm, d), row),
            pl.BlockSpec((tm, LANES), row),
            pl.BlockSpec((tm, LANES), row),
        ],
        out_shape=[
            jax.ShapeDtypeStruct((n, d), F32),
            jax.ShapeDtypeStruct((n, d), F32),
            jax.ShapeDtypeStruct((n, LANES), jnp.int32),
            jax.ShapeDtypeStruct((n, LANES), F32),
        ],
        compiler_params=_params(1),
    )(x_p, x_s, cact_p, cact_s, oact_p, oact_s, siga, sigb, w_conv_out, w_delta_out, w_merge_out, norm2_g,
      router_w, router_b)


def _fill_rows_per_step(n_fill, n_steps):
    per_step = SUBLANES
    while per_step * n_steps < n_fill:
        per_step *= 2
    assert n_fill % per_step == 0
    return per_step


def _route(top_idx, tm, n_blocks, n_steps):
    n = top_idx.shape[0]
    n_fill = n_blocks * tm - n * TOP_K
    assert n_fill == N_EXPERTS * tm
    flat_e = top_idx.reshape(-1)
    experts = jnp.arange(N_EXPERTS, dtype=jnp.int32)
    onehot = (flat_e[:, None] == experts[None, :]).astype(jnp.int32)
    csum = jnp.cumsum(onehot, axis=0)
    rank = jnp.sum(csum * onehot, axis=1) - 1
    counts = csum[-1]
    padded = (counts + tm - 1) // tm * tm
    pad_end = jnp.cumsum(padded)
    pad_start = pad_end - padded
    dest = jnp.sum(onehot * pad_start[None, :], axis=1) + rank
    nvalid = (pad_end[-1] // tm).astype(jnp.int32)
    blk = jnp.arange(n_blocks, dtype=jnp.int32)
    owner = jnp.sum((pad_end[None, :] <= (blk * tm)[:, None]).astype(jnp.int32), axis=1)
    block_e = jnp.minimum(owner, N_EXPERTS - 1)
    block_e = jnp.where(blk < nvalid, block_e, jnp.sum(jnp.where(blk == nvalid - 1, block_e, 0)))
    n_pad = padded - counts
    spill = tm - n_pad
    spill_start = pad_end[-1] + jnp.cumsum(spill) - spill
    j = jnp.arange(tm, dtype=jnp.int32)[None, :]
    fill = jnp.where(j < n_pad[:, None], (pad_start + counts)[:, None] + j, (spill_start - n_pad)[:, None] + j)
    fill_step = _fill_rows_per_step(n_fill, n_steps)
    fill = jnp.pad(fill.reshape(-1, fill_step), ((0, n_steps - n_fill // fill_step), (0, 0)))
    table = jnp.concatenate([dest.reshape(n_steps, -1), fill], axis=1)
    return dest.reshape(n, TOP_K), table, block_e, nvalid.reshape(1)


def _dispatch_kernel(tab_ref, h2_ref, xs_ref, sem, *, tokens, fill_step, n_fill_steps):
    i = pl.program_id(0)

    def wait_rows(count):
        while count > 0:
            rows = min(count, tokens)
            pltpu.make_async_copy(h2_ref.at[pl.ds(0, rows)], xs_ref.at[pl.ds(0, rows)], sem).wait()
            count -= rows

    for t in range(tokens):
        for k in range(TOP_K):
            pltpu.make_async_copy(h2_ref.at[pl.ds(t, 1)], xs_ref.at[pl.ds(tab_ref[i, t * TOP_K + k], 1)], sem).start()

    @pl.when(i < n_fill_steps)
    def _():
        for p in range(fill_step):
            pltpu.make_async_copy(
                h2_ref.at[pl.ds(0, 1)], xs_ref.at[pl.ds(tab_ref[i, tokens * TOP_K + p], 1)], sem).start()
        wait_rows(fill_step)

    wait_rows(tokens * TOP_K)


def _dispatch(h2, table, n_rows, tokens):
    n, d = h2.shape
    fill_step = table.shape[1] - tokens * TOP_K
    n_fill_steps = (n_rows - n * TOP_K) // fill_step
    return pl.pallas_call(
        functools.partial(_dispatch_kernel, tokens=tokens, fill_step=fill_step, n_fill_steps=n_fill_steps),
        grid_spec=pltpu.PrefetchScalarGridSpec(
            num_scalar_prefetch=1,
            grid=(n // tokens,),
            in_specs=[pl.BlockSpec((tokens, d), lambda i, tab: (i, 0))],
            out_specs=pl.BlockSpec(memory_space=pl.ANY),
            scratch_shapes=[pltpu.SemaphoreType.DMA(())],
        ),
        out_shape=jax.ShapeDtypeStruct((n_rows, d), h2.dtype),
        compiler_params=_params(1),
    )(table, h2)


def _moe_ffn_kernel(be_ref, nv_ref, xs_ref, wgu_ref, bgu_ref, wd_ref, bd_ref, out_ref, wgu_bf, wd_bf):
    i = pl.program_id(0)
    prev = be_ref[jnp.maximum(i - 1, 0)]
    changed = jnp.logical_or(i == 0, be_ref[i] != prev)

    @pl.when(changed)
    def _():
        wgu_bf[...] = wgu_ref[0].astype(BF16)
        wd_bf[...] = wd_ref[0].astype(BF16)

    @pl.when(i < nv_ref[0])
    def _():
        f = wd_bf.shape[0]
        gu = jnp.dot(xs_ref[...].astype(BF16), wgu_bf[...], preferred_element_type=F32) + bgu_ref[0]
        gt = jnp.minimum(gu[:, :f], SWIGLU_LIMIT)
        up = jnp.clip(gu[:, f:], -SWIGLU_LIMIT, SWIGLU_LIMIT)
        act = (up + 1.0) * (gt * _sigmoid(SWIGLU_ALPHA * gt))
        out_ref[...] = jnp.dot(act.astype(BF16), wd_bf[...], preferred_element_type=F32) + bd_ref[0]

    @pl.when(i >= nv_ref[0])
    def _():
        out_ref[...] = jnp.zeros(out_ref.shape, out_ref.dtype)


def _moe_ffn(xs, block_e, nvalid, w_gate_up, b_gate_up, w_down, b_down, tm, n_blocks):
    d = xs.shape[1]
    ne, _, f2 = w_gate_up.shape
    f = f2 // 2
    used = lambda i, be, nv: (jnp.minimum(i, nv[0] - 1), 0)
    return pl.pallas_call(
        _moe_ffn_kernel,
        grid_spec=pltpu.PrefetchScalarGridSpec(
            num_scalar_prefetch=2,
            grid=(n_blocks,),
            in_specs=[
                pl.BlockSpec((tm, d), used),
                pl.BlockSpec((1, d, f2), lambda i, be, nv: (be[i], 0, 0)),
                pl.BlockSpec((1, 1, f2), lambda i, be, nv: (be[i], 0, 0)),
                pl.BlockSpec((1, f, d), lambda i, be, nv: (be[i], 0, 0)),
                pl.BlockSpec((1, 1, d), lambda i, be, nv: (be[i], 0, 0)),
            ],
            out_specs=pl.BlockSpec((tm, d), lambda i, be, nv: (i, 0)),
            scratch_shapes=[pltpu.VMEM((d, f2), BF16), pltpu.VMEM((f, d), BF16)],
        ),
        out_shape=jax.ShapeDtypeStruct((n_blocks * tm, d), F32),
        compiler_params=_params(1),
    )(block_e, nvalid, xs, w_gate_up, b_gate_up.reshape(ne, 1, f2), w_down, b_down.reshape(ne, 1, d))


def _combine_kernel(slot_ref, yb_ref, x2_ref, tg_ref, fg_ref, yp_ref, ys_ref, buf, sems, *, tc, n_prompt_tiles):
    i = pl.program_id(0)
    n_steps = pl.num_programs(0)

    def fetch(step, slot):
        for t in range(tc):
            for k in range(TOP_K):
                pltpu.make_async_copy(
                    yb_ref.at[pl.ds(slot_ref[step, t * TOP_K + k], 1)], buf.at[slot, k, pl.ds(t, 1)],
                    sems.at[slot]).start()

    @pl.when(i == 0)
    def _():
        fetch(0, 0)

    @pl.when(i + 1 < n_steps)
    def _():
        fetch(i + 1, (i + 1) % 2)

    slot = i % 2
    for k in range(TOP_K):
        pltpu.make_async_copy(yb_ref.at[pl.ds(0, tc)], buf.at[slot, k], sems.at[slot]).wait()
    tg = tg_ref[...]
    y = x2_ref[...]
    for k in range(TOP_K):
        y = y + tg[:, k:k + 1] * buf[slot, k]
    out = y * lax.rsqrt(jnp.mean(y * y, axis=-1, keepdims=True) + EPS) * fg_ref[...]

    @pl.when(i < n_prompt_tiles)
    def _():
        yp_ref[...] = out

    @pl.when(i >= n_prompt_tiles)
    def _():
        ys_ref[...] = out


def _combine(slot2d, yb, x2, tg, final_g, n_p, tc):
    n, d = x2.shape
    n_s = n - n_p
    assert n_p % tc == 0 and n_s % tc == 0
    npt = n_p // tc
    kern = functools.partial(_combine_kernel, tc=tc, n_prompt_tiles=npt)
    out_p, out_s = _two_source_specs(tc, d, npt)
    return pl.pallas_call(
        kern,
        grid_spec=pltpu.PrefetchScalarGridSpec(
            num_scalar_prefetch=1,
            grid=(n // tc,),
            in_specs=[
                pl.BlockSpec(memory_space=pl.ANY),
                pl.BlockSpec((tc, d), lambda i, s: (i, 0)),
                pl.BlockSpec((tc, LANES), lambda i, s: (i, 0)),
                pl.BlockSpec((1, d), lambda i, s: (0, 0)),
            ],
            out_specs=[out_p, out_s],
            scratch_shapes=[pltpu.VMEM((2, TOP_K, tc, d), F32), pltpu.SemaphoreType.DMA((2,))],
        ),
        out_shape=[jax.ShapeDtypeStruct((n_p, d), F32), jax.ShapeDtypeStruct((n_s, d), F32)],
        compiler_params=_params(1),
    )(slot2d, yb, x2, tg, final_g)


def _pad_lanes(v, width=LANES):
    v = v.reshape(1, -1)
    return jnp.pad(v, ((0, 0), (0, width - v.shape[1])))


def kernel(x_prompt, x_sample, state_conv, state_short_conv, state_delta, norm1_g, w_in, conv_dw_w,
           conv_dw_b, conv_ln_g, conv_ln_b, w_conv_out, short_conv_w, a_log, dt_bias, delta_norm_g,
           w_delta_out, w_merge_out, norm2_g, router_w, router_b, w_gate_up, b_gate_up, w_down, b_down,
           final_norm_g):
    depth = w_in.shape[0]
    assert depth == 1
    bp, tp, d = x_prompt.shape
    bs, ts, _ = x_sample.shape
    n_p, n_s = bp * tp, bs * ts
    n = n_p + n_s
    l = 0
    x_p = x_prompt.reshape(n_p, d)
    x_s = x_sample.reshape(n_s, d)

    o_ab = 2 * D_CONV + 4 * DN_WIDTH
    w = w_in[l]
    w_main = jnp.concatenate([w[:, :o_ab], w[:, o_ab + 2 * DN_HEADS:]], axis=1).astype(BF16)
    w_ab = jnp.pad(w[:, o_ab:o_ab + 2 * DN_HEADS], ((0, 0), (0, LANES - 2 * DN_HEADS)))

    glu, qkv_pre, z, gb, siga, sigb = _inproj(
        x_p, x_s, norm1_g[l].reshape(1, d), w_main, w_ab, _pad_lanes(a_log[l]), _pad_lanes(dt_bias[l]), TOKEN_TILE)

    dw = (conv_dw_w[l], conv_dw_b[l].reshape(1, -1), conv_ln_g[l].reshape(1, -1), conv_ln_b[l].reshape(1, -1))
    pad_c = CONV_HALO - (CONV_WIDTH - 1)
    st_c_p = jnp.zeros((bp, CONV_HALO, D_CONV), F32)
    st_c_s = jnp.pad(state_conv[l], ((0, 0), (pad_c, 0), (0, 0)))
    cact_p, nconv_p = _conv_branch(glu, st_c_p, *dw, row0=0, bsz=bp, t_len=tp, bb=1, tt=SEQ_TILE)
    cact_s, nconv_s = _conv_branch(glu, st_c_s, *dw, row0=n_p, bsz=bs, t_len=ts, bb=8, tt=ts)

    pad_s = SHORT_HALO - (SHORT_WIDTH - 1)
    st_s_p = jnp.zeros((bp, SHORT_HALO, 3 * DN_WIDTH), F32)
    st_s_s = jnp.pad(state_short_conv[l], ((0, 0), (pad_s, 0), (0, 0)))
    s0_p = jnp.zeros((bp, DN_HEADS, DN_HEAD_DIM, DN_HEAD_DIM), F32)
    ng = delta_norm_g[l].reshape(1, -1)
    oact_p, nshort_p, s_p = _delta_prompt(qkv_pre, z, gb, st_s_p, s0_p, short_conv_w[l], ng,
                                          bsz=bp, t_len=tp, tt=SEQ_TILE)
    oact_s, nshort_s, s_s = _delta_sample(qkv_pre, z, gb, st_s_s, state_delta[l], short_conv_w[l], ng,
                                          row0=n_p, bsz=bs, seq_len=ts)

    rw = jnp.pad(router_w[l], ((0, 0), (0, LANES - N_EXPERTS)))
    x2, h2, ti, tg = _mix(x_p, x_s, cact_p, cact_s, oact_p, oact_s, siga, sigb, w_conv_out[l].astype(BF16),
                          w_delta_out[l].astype(BF16), w_merge_out[l].astype(BF16), norm2_g[l].reshape(1, d),
                          rw, _pad_lanes(router_b[l]), TOKEN_TILE)

    n_blocks = -(-(n * TOP_K) // MOE_TILE) + N_EXPERTS
    n_steps = n // TOKEN_TILE
    dest, table, block_e, nvalid = _route(ti[:, :TOP_K], MOE_TILE, n_blocks, n_steps)
    xs = _dispatch(h2, table, n_blocks * MOE_TILE, TOKEN_TILE)
    yb = _moe_ffn(xs, block_e, nvalid, w_gate_up[l], b_gate_up[l], w_down[l], b_down[l], MOE_TILE, n_blocks)
    y_p, y_s = _combine(dest.reshape(n_steps, -1), yb, x2, tg, final_norm_g.reshape(1, d), n_p, TOKEN_TILE)

    conv_p = nconv_p[:, pad_c:, :][None]
    conv_s = nconv_s[:, pad_c:, :][None]
    short_p = nshort_p[:, pad_s:, :][None]
    short_s = nshort_s[:, pad_s:, :][None]
    return (y_p.reshape(bp, tp, d), y_s.reshape(bs, ts, d), conv_p, short_p, s_p[None],
            conv_s, short_s, s_s[None])
```

```python
import functools

import jax
import jax.numpy as jnp
from jax import lax
from jax.experimental import pallas as pl
from jax.experimental.pallas import tpu as pltpu

F32 = jnp.float32
BF16 = jnp.bfloat16
EPS = 1e-6

LANES = 128
SUBLANES = 8
VMEM_LIMIT_BYTES = 56 * 1024 * 1024
DMA_PRIORITIES = 2

D_CONV = 512
CONV_WIDTH = 31
DN_HEADS = 4
DN_HEAD_DIM = 128
DN_WIDTH = DN_HEADS * DN_HEAD_DIM
SHORT_WIDTH = 4
N_EXPERTS = 32
TOP_K = 4
SWIGLU_LIMIT = 7.0
SWIGLU_ALPHA = 1.702

CHUNK = 128
CONV_HALO = 32
SHORT_HALO = 8

TOKEN_TILE = 256
SEQ_TILE = 256
MOE_TILE = 256


def _sigmoid(x):
    return 1.0 / (1.0 + jnp.exp(-x))


def _silu(x):
    return x * _sigmoid(x)


def _split_bf16(w):
    hi = w.astype(BF16)
    lo = (w - hi.astype(F32)).astype(BF16)
    return jnp.concatenate([hi, lo], axis=-1)


def _dot_split(x, w_split):
    n = w_split.shape[-1] // 2
    x_hi = x.astype(BF16)
    x_lo = (x - x_hi.astype(F32)).astype(BF16)
    r = jnp.dot(x_hi, w_split, preferred_element_type=F32)
    return r[:, :n] + r[:, n:] + jnp.dot(x_lo, w_split[:, :n], preferred_element_type=F32)


def _dot_delta(a, b, dims=(((1,), (0,)), ((), ()))):
    return lax.dot_general(a.astype(BF16), b.astype(BF16), dims, preferred_element_type=F32)


def _params(n_axes):
    return pltpu.CompilerParams(dimension_semantics=("arbitrary",) * n_axes, vmem_limit_bytes=VMEM_LIMIT_BYTES)


def _two_source_specs(tm, d, n_first_tiles):
    first = pl.BlockSpec((tm, d), lambda i, *_: (jnp.minimum(i, n_first_tiles - 1), 0))
    second = pl.BlockSpec((tm, d), lambda i, *_: (jnp.maximum(i - n_first_tiles, 0), 0))
    return first, second


def _inproj_kernel(xp_ref, xs_ref, g_ref, w_ref, wab_ref, alog_ref, dtb_ref,
                   glu_ref, qkv_ref, z_ref, gb_ref, sa_ref, sb_ref, *, n_prompt_tiles):
    x = jnp.where(pl.program_id(0) < n_prompt_tiles, xp_ref[...], xs_ref[...])
    h = x * lax.rsqrt(jnp.mean(x * x, axis=-1, keepdims=True) + EPS) * g_ref[...]
    hb = h.astype(BF16)

    def mm(lo, hi):
        return jnp.dot(hb, w_ref[:, lo:hi], preferred_element_type=F32)

    o_gate, o_qkv, o_z = D_CONV, 2 * D_CONV, 2 * D_CONV + 3 * DN_WIDTH
    o_ga = o_z + DN_WIDTH
    d = x.shape[-1]
    glu_ref[...] = mm(0, o_gate) * _sigmoid(mm(o_gate, o_qkv))
    qkv_ref[...] = mm(o_qkv, o_z)
    z_ref[...] = mm(o_z, o_ga)
    sa_ref[...] = _sigmoid(mm(o_ga, o_ga + d))
    sb_ref[...] = _sigmoid(mm(o_ga + d, o_ga + 2 * d))
    ab = _dot_split(h, wab_ref[...])
    xa = ab + dtb_ref[...]
    softplus = jnp.maximum(xa, 0.0) + jnp.log(1.0 + jnp.exp(-jnp.abs(xa)))
    g = -jnp.exp(alog_ref[...]) * softplus
    lane = lax.broadcasted_iota(jnp.int32, ab.shape, 1)
    gb_ref[...] = jnp.where(lane < DN_HEADS, g, _sigmoid(ab))


def _inproj(x_p, x_s, norm_g, w_main, w_ab, alog, dtb, tm):
    (n_p, d), n_s = x_p.shape, x_s.shape[0]
    assert n_p % tm == 0 and n_s % tm == 0
    n = n_p + n_s
    wcols = w_main.shape[1]
    row = lambda i: (i, 0)
    const = lambda i: (0, 0)
    outs = [(D_CONV, F32), (3 * DN_WIDTH, F32), (DN_WIDTH, F32), (LANES, F32), (d, F32), (d, F32)]
    return pl.pallas_call(
        functools.partial(_inproj_kernel, n_prompt_tiles=n_p // tm),
        grid=(n // tm,),
        in_specs=[
            *_two_source_specs(tm, d, n_p // tm),
            pl.BlockSpec((1, d), const),
            pl.BlockSpec((d, wcols), const),
            pl.BlockSpec((d, 2 * LANES), const),
            pl.BlockSpec((1, LANES), const),
            pl.BlockSpec((1, LANES), const),
        ],
        out_specs=[pl.BlockSpec((tm, c), row) for c, _ in outs],
        out_shape=[jax.ShapeDtypeStruct((n, c), dt) for c, dt in outs],
        compiler_params=_params(1),
    )(x_p, x_s, norm_g, w_main, w_ab, alog, dtb)


def _conv_kernel(glu_ref, st_ref, w_ref, b_ref, lg_ref, lb_ref, out_ref, nst_ref, e_ref, sh_ref, *, bb, tt, rows):
    t = pl.program_id(1)

    @pl.when(t == 0)
    def _():
        e_ref[:, 0:CONV_HALO, :] = st_ref[...]

    for b in range(bb):
        e_ref[b, CONV_HALO:CONV_HALO + tt, :] = glu_ref[b * tt:(b + 1) * tt, :]
    off = CONV_HALO - (CONV_WIDTH - 1)
    span = sh_ref.shape[1]
    for b in range(bb):
        for s in range(1, SUBLANES):
            sh_ref[s - 1] = e_ref[b, s:s + span, :]
        for c in range(tt // rows):
            r0 = c * rows
            acc = jnp.zeros((rows, D_CONV), F32) + b_ref[...]
            for j in range(CONV_WIDTH):
                q, s = divmod(j + off, SUBLANES)
                lo = r0 + q * SUBLANES
                src = e_ref[b, lo:lo + rows, :] if s == 0 else sh_ref[s - 1, lo:lo + rows, :]
                acc = acc + w_ref[j:j + 1, :] * src
            mu = jnp.mean(acc, axis=-1, keepdims=True)
            xc = acc - mu
            var = jnp.mean(xc * xc, axis=-1, keepdims=True)
            y = xc * lax.rsqrt(var + EPS) * lg_ref[...] + lb_ref[...]
            out_ref[b * tt + r0:b * tt + r0 + rows, :] = _silu(y).astype(out_ref.dtype)
    tail = e_ref[:, tt:tt + CONV_HALO, :]
    nst_ref[...] = tail
    e_ref[:, 0:CONV_HALO, :] = tail


def _conv_branch(glu, state32, dw_w, dw_b, ln_g, ln_b, *, row0, bsz, t_len, bb, tt):
    c = glu.shape[1]
    assert bsz % bb == 0 and t_len % tt == 0 and row0 % (bb * tt) == 0
    nt = t_len // tt
    blk0 = row0 // (bb * tt)
    rows = min(tt, 32)
    kern = functools.partial(_conv_kernel, bb=bb, tt=tt, rows=rows)
    const = lambda b, t: (0, 0)
    return pl.pallas_call(
        kern,
        grid=(bsz // bb, nt),
        in_specs=[
            pl.BlockSpec((bb * tt, c), lambda b, t: (blk0 + b * nt + t, 0)),
            pl.BlockSpec((bb, CONV_HALO, c), lambda b, t: (b, 0, 0)),
            pl.BlockSpec((CONV_WIDTH, c), const),
            pl.BlockSpec((1, c), const),
            pl.BlockSpec((1, c), const),
            pl.BlockSpec((1, c), const),
        ],
        out_specs=[
            pl.BlockSpec((bb * tt, c), lambda b, t: (b * nt + t, 0)),
            pl.BlockSpec((bb, CONV_HALO, c), lambda b, t: (b, 0, 0)),
        ],
        out_shape=[
            jax.ShapeDtypeStruct((bsz * t_len, c), BF16),
            jax.ShapeDtypeStruct((bsz, CONV_HALO, c), F32),
        ],
        scratch_shapes=[pltpu.VMEM((bb, CONV_HALO + tt, c), F32),
                        pltpu.VMEM((SUBLANES - 1, tt + CONV_HALO - SUBLANES, c), F32)],
        compiler_params=_params(2),
    )(glu, state32, dw_w, dw_b, ln_g, ln_b)


def _chunk_masks(seq_len):
    i = lax.broadcasted_iota(jnp.int32, (CHUNK, CHUNK), 0)
    j = lax.broadcasted_iota(jnp.int32, (CHUNK, CHUNK), 1)
    same = (i // seq_len) == (j // seq_len)
    incl = same & (i >= j)
    strict = same & (i > j)
    last = j == (i // seq_len) * seq_len + (seq_len - 1)
    levels = []
    blk = 1
    while blk < seq_len:
        levels.append(((i // (2 * blk)) == (j // (2 * blk))) & (((i // blk) % 2) == 1) & (((j // blk) % 2) == 0))
        blk *= 2
    eye = i == j
    return incl, strict, last, levels, eye


def _lane_col(x, lane):
    return jnp.broadcast_to(x[:, lane:lane + 1], (x.shape[0], LANES))


def _l2norm(x):
    return x * lax.rsqrt(jnp.sum(x * x, axis=-1, keepdims=True) + EPS)


def _select_sum(mask01, x):
    hi = x.astype(BF16)
    r1 = x - hi.astype(F32)
    mid = r1.astype(BF16)
    lo = (r1 - mid.astype(F32)).astype(BF16)
    w = x.shape[1]
    parts = jnp.dot(mask01, jnp.concatenate([hi, mid, lo], axis=1), preferred_element_type=F32)
    return parts[:, :w] + parts[:, w:2 * w] + parts[:, 2 * w:]


def _chunks_prepare(qkvs, gbts, masks, seq_len):
    incl, strict, last, levels, eye = masks
    nt = (((1,), (1,)), ((), ()))
    lower01 = jnp.where(incl, 1.0, 0.0).astype(BF16)
    probs = []
    for qkv, gbt in zip(qkvs, gbts):
        gc = _select_sum(lower01, gbt)
        gct = gc.T
        if seq_len == CHUNK:
            glast = jnp.broadcast_to(gc[CHUNK - 1:CHUNK, :], gc.shape)
        else:
            glast = _select_sum(jnp.where(last, 1.0, 0.0).astype(BF16), gc)
        for h in range(DN_HEADS):
            q = _l2norm(qkv[:, h * DN_HEAD_DIM:(h + 1) * DN_HEAD_DIM]) * (DN_HEAD_DIM ** -0.5)
            k = _l2norm(qkv[:, DN_WIDTH + h * DN_HEAD_DIM:DN_WIDTH + (h + 1) * DN_HEAD_DIM])
            v = qkv[:, 2 * DN_WIDTH + h * DN_HEAD_DIM:2 * DN_WIDTH + (h + 1) * DN_HEAD_DIM]
            gcol = _lane_col(gc, h)
            grow = jnp.broadcast_to(gct[h:h + 1, :], (CHUNK, CHUNK))
            beta = _lane_col(gbt, DN_HEADS + h)
            gl = _lane_col(glast, h)
            decay = jnp.exp(jnp.where(incl, gcol - grow, -jnp.inf))
            egc = jnp.exp(gcol)
            kb = k * beta
            probs.append(dict(q=q, k=k, kb=kb, decay=decay, rhs=jnp.concatenate([v * beta, kb * egc], axis=1),
                              qexp=q * egc, kdec=k * jnp.exp(gl - gcol), egl=jnp.exp(gl)))
    for p in probs:
        p['a'] = jnp.where(strict, _dot_delta(p['kb'], p['k'], nt) * p['decay'], 0.0)
        p['scores'] = _dot_delta(p['q'], p['k'], nt) * p['decay']
    for p in probs:
        p['x'] = jnp.where(eye, 1.0, 0.0) - jnp.where(levels[0], p['a'], 0.0)
    for m in levels[1:]:
        for p in probs:
            p['xa'] = _dot_delta(p['x'], jnp.where(m, p['a'], 0.0))
        for p in probs:
            p['x'] = p['x'] - _dot_delta(p['xa'], p['x'])
    out = []
    for c in range(len(qkvs)):
        heads = []
        for h in range(DN_HEADS):
            p = probs[c * DN_HEADS + h]
            sol = _dot_delta(p['x'], p['rhs'])
            heads.append((sol[:, :DN_HEAD_DIM], sol[:, DN_HEAD_DIM:], p['scores'], p['qexp'], p['kdec'], p['egl']))
        out.append(heads)
    return out


def _gated_out_norm(o, z, ng):
    y = o * lax.rsqrt(jnp.mean(o * o, axis=-1, keepdims=True) + EPS) * ng
    return y * _silu(z)


def _short_conv(e_ref, w_ref, tt):
    off = SHORT_HALO - (SHORT_WIDTH - 1)
    acc = w_ref[0:1, :] * e_ref[off:off + tt, :]
    for j in range(1, SHORT_WIDTH):
        acc = acc + w_ref[j:j + 1, :] * e_ref[off + j:off + j + tt, :]
    return _silu(acc)


def _delta_prompt_kernel(qkv_ref, z_ref, gb_ref, st_ref, s0_ref, w_ref, ng_ref,
                         o_ref, nst_ref, sout_ref, e_ref, s_ref, *, tt):
    t = pl.program_id(1)

    @pl.when(t == 0)
    def _():
        e_ref[0:SHORT_HALO, :] = st_ref[0]
        s_ref[...] = s0_ref[0]

    e_ref[SHORT_HALO:SHORT_HALO + tt, :] = qkv_ref[...]
    qkv = _short_conv(e_ref, w_ref, tt)
    tail = e_ref[tt:tt + SHORT_HALO, :]
    nst_ref[0] = tail
    e_ref[0:SHORT_HALO, :] = tail

    masks = _chunk_masks(CHUNK)
    tn = (((0,), (0,)), ((), ()))
    n_chunks = tt // CHUNK
    prep = _chunks_prepare([qkv[c * CHUNK:(c + 1) * CHUNK, :] for c in range(n_chunks)],
                           [gb_ref[c * CHUNK:(c + 1) * CHUNK, :] for c in range(n_chunks)], masks, CHUNK)
    heads = range(DN_HEADS)
    s = [s_ref[h] for h in heads]
    for c in range(n_chunks):
        r0 = c * CHUNK
        value, kcum, scores, qexp, kdec, egl = zip(*prep[c])
        both = [_dot_delta(jnp.concatenate([kcum[h], qexp[h]], axis=0), s[h]) for h in heads]
        v_new = [value[h] - both[h][:CHUNK] for h in heads]
        o = [both[h][CHUNK:] + _dot_delta(scores[h], v_new[h]) for h in heads]
        s = [s[h] * egl[h][0:1, :] + _dot_delta(kdec[h], v_new[h], tn) for h in heads]
        for h in heads:
            lanes = slice(h * DN_HEAD_DIM, (h + 1) * DN_HEAD_DIM)
            o_ref[r0:r0 + CHUNK, lanes] = _gated_out_norm(
                o[h], z_ref[r0:r0 + CHUNK, lanes], ng_ref[...]).astype(o_ref.dtype)
    for h in heads:
        s_ref[h] = s[h]
        sout_ref[0, h] = s[h]


def _delta_prompt(qkv_pre, z, gb, state8, s0, conv_w, norm_g, *, bsz, t_len, tt):
    n = bsz * t_len
    assert t_len % tt == 0 and tt % CHUNK == 0
    nt = t_len // tt
    kern = functools.partial(_delta_prompt_kernel, tt=tt)
    tile = lambda b, t: (b * nt + t, 0)
    per_b = lambda b, t: (b, 0, 0)
    per_b4 = lambda b, t: (b, 0, 0, 0)
    return pl.pallas_call(
        kern,
        grid=(bsz, nt),
        in_specs=[
            pl.BlockSpec((tt, 3 * DN_WIDTH), tile),
            pl.BlockSpec((tt, DN_WIDTH), tile),
            pl.BlockSpec((tt, LANES), tile),
            pl.BlockSpec((1, SHORT_HALO, 3 * DN_WIDTH), per_b),
            pl.BlockSpec((1, DN_HEADS, DN_HEAD_DIM, DN_HEAD_DIM), per_b4),
            pl.BlockSpec((SHORT_WIDTH, 3 * DN_WIDTH), lambda b, t: (0, 0)),
            pl.BlockSpec((1, DN_HEAD_DIM), lambda b, t: (0, 0)),
        ],
        out_specs=[
            pl.BlockSpec((tt, DN_WIDTH), tile),
            pl.BlockSpec((1, SHORT_HALO, 3 * DN_WIDTH), per_b),
            pl.BlockSpec((1, DN_HEADS, DN_HEAD_DIM, DN_HEAD_DIM), per_b4),
        ],
        out_shape=[
            jax.ShapeDtypeStruct((n, DN_WIDTH), BF16),
            jax.ShapeDtypeStruct((bsz, SHORT_HALO, 3 * DN_WIDTH), F32),
            jax.ShapeDtypeStruct((bsz, DN_HEADS, DN_HEAD_DIM, DN_HEAD_DIM), F32),
        ],
        scratch_shapes=[
            pltpu.VMEM((SHORT_HALO + tt, 3 * DN_WIDTH), F32),
            pltpu.VMEM((DN_HEADS, DN_HEAD_DIM, DN_HEAD_DIM), F32),
        ],
        compiler_params=_params(2),
    )(qkv_pre, z, gb, state8, s0, conv_w, norm_g)


def _delta_sample_kernel(qkv_ref, z_ref, gb_ref, st_ref, s0_ref, w_ref, ng_ref, o_ref, nst_ref, sout_ref,
                         e_ref, *, nseq, seq_len):
    qkv_rows = []
    for b in range(nseq):
        e_ref[0:SHORT_HALO, :] = st_ref[b]
        e_ref[SHORT_HALO:SHORT_HALO + seq_len, :] = qkv_ref[b * seq_len:(b + 1) * seq_len, :]
        qkv_rows.append(_short_conv(e_ref, w_ref, seq_len))
        nst_ref[b] = e_ref[seq_len:seq_len + SHORT_HALO, :]
    qkv = jnp.concatenate(qkv_rows, axis=0)
    masks = _chunk_masks(seq_len)
    prep = _chunks_prepare([qkv], [gb_ref[...]], masks, seq_len)[0]
    tn = (((0,), (0,)), ((), ()))
    rows = [slice(b * seq_len, (b + 1) * seq_len) for b in range(nseq)]
    both = [[_dot_delta(jnp.concatenate([prep[h][1][r], prep[h][3][r]], axis=0), s0_ref[b, h])
             for b, r in enumerate(rows)] for h in range(DN_HEADS)]
    v_new = [[prep[h][0][r] - both[h][b][:seq_len] for b, r in enumerate(rows)] for h in range(DN_HEADS)]
    for h in range(DN_HEADS):
        kdec, egl = prep[h][4], prep[h][5]
        for b, r in enumerate(rows):
            sout_ref[b, h] = (s0_ref[b, h] * egl[b * seq_len:b * seq_len + 1, :]
                              + _dot_delta(kdec[r], v_new[h][b], tn))
    for h in range(DN_HEADS):
        o = (jnp.concatenate([both[h][b][seq_len:] for b in range(nseq)], axis=0)
             + _dot_delta(prep[h][2], jnp.concatenate(v_new[h], axis=0)))
        lanes = slice(h * DN_HEAD_DIM, (h + 1) * DN_HEAD_DIM)
        o_ref[:, lanes] = _gated_out_norm(o, z_ref[:, lanes], ng_ref[...]).astype(o_ref.dtype)


def _delta_sample(qkv_pre, z, gb, state8, s0, conv_w, norm_g, *, row0, bsz, seq_len):
    n = bsz * seq_len
    assert CHUNK % seq_len == 0
    nseq = CHUNK // seq_len
    assert bsz % nseq == 0 and row0 % CHUNK == 0
    blk0 = row0 // CHUNK
    kern = functools.partial(_delta_sample_kernel, nseq=nseq, seq_len=seq_len)
    tile = lambda i: (blk0 + i, 0)
    blk3 = lambda i: (i, 0, 0)
    blk4 = lambda i: (i, 0, 0, 0)
    return pl.pallas_call(
        kern,
        grid=(bsz // nseq,),
        in_specs=[
            pl.BlockSpec((CHUNK, 3 * DN_WIDTH), tile),
            pl.BlockSpec((CHUNK, DN_WIDTH), tile),
            pl.BlockSpec((CHUNK, LANES), tile),
            pl.BlockSpec((nseq, SHORT_HALO, 3 * DN_WIDTH), blk3),
            pl.BlockSpec((nseq, DN_HEADS, DN_HEAD_DIM, DN_HEAD_DIM), blk4),
            pl.BlockSpec((SHORT_WIDTH, 3 * DN_WIDTH), lambda i: (0, 0)),
            pl.BlockSpec((1, DN_HEAD_DIM), lambda i: (0, 0)),
        ],
        out_specs=[
            pl.BlockSpec((CHUNK, DN_WIDTH), lambda i: (i, 0)),
            pl.BlockSpec((nseq, SHORT_HALO, 3 * DN_WIDTH), blk3),
            pl.BlockSpec((nseq, DN_HEADS, DN_HEAD_DIM, DN_HEAD_DIM), blk4),
        ],
        out_shape=[
            jax.ShapeDtypeStruct((n, DN_WIDTH), BF16),
            jax.ShapeDtypeStruct((bsz, SHORT_HALO, 3 * DN_WIDTH), F32),
            jax.ShapeDtypeStruct((bsz, DN_HEADS, DN_HEAD_DIM, DN_HEAD_DIM), F32),
        ],
        scratch_shapes=[pltpu.VMEM((SHORT_HALO + seq_len, 3 * DN_WIDTH), F32)],
        compiler_params=_params(1),
    )(qkv_pre, z, gb, state8, s0, conv_w, norm_g)


def _mix_kernel(xp_ref, xs_ref, cap_ref, cas_ref, oap_ref, oas_ref, sa_ref, sb_ref, wc_ref, wd_ref, wm_ref,
                g2_ref, rw_ref, rb_ref, x2_ref, h2_ref, ti_ref, tg_ref, *, n_prompt_tiles):
    is_prompt = pl.program_id(0) < n_prompt_tiles
    x = jnp.where(is_prompt, xp_ref[...], xs_ref[...])
    ca = jnp.where(is_prompt, cap_ref[...], cas_ref[...])
    oa = jnp.where(is_prompt, oap_ref[...], oas_ref[...])
    ya = jnp.dot(ca, wc_ref[...], preferred_element_type=F32)
    yb = jnp.dot(oa, wd_ref[...], preferred_element_type=F32)
    mixed = sa_ref[...] * ya + sb_ref[...] * yb
    x2 = x + jnp.dot(mixed.astype(BF16), wm_ref[...], preferred_element_type=F32)
    x2_ref[...] = x2
    h2 = x2 * lax.rsqrt(jnp.mean(x2 * x2, axis=-1, keepdims=True) + EPS) * g2_ref[...]
    h2_ref[...] = h2
    logits = _dot_split(h2, rw_ref[...]) + rb_ref[...]
    lane = lax.broadcasted_iota(jnp.int32, logits.shape, 1)
    lane_f = lane.astype(F32)
    logits = jnp.where(lane < N_EXPERTS, logits, -jnp.inf)
    top_vals = []
    ti = jnp.zeros(logits.shape, jnp.int32)
    for k in range(TOP_K):
        m = jnp.max(logits, axis=-1, keepdims=True)
        idx = jnp.min(jnp.where(logits == m, lane_f, float(LANES)), axis=-1, keepdims=True).astype(jnp.int32)
        ti = jnp.where(lane == k, idx, ti)
        top_vals.append(m)
        logits = jnp.where(lane == idx, -jnp.inf, logits)
    exps = [jnp.exp(v - top_vals[0]) for v in top_vals]
    den = exps[0] + exps[1] + exps[2] + exps[3]
    tg = jnp.zeros(logits.shape, F32)
    for k in range(TOP_K):
        tg = jnp.where(lane == k, exps[k] / den, tg)
    ti_ref[...] = ti
    tg_ref[...] = tg


def _mix(x_p, x_s, cact_p, cact_s, oact_p, oact_s, siga, sigb, w_conv_out, w_delta_out, w_merge_out, norm2_g,
         router_w, router_b, tm):
    (n_p, d), n_s = x_p.shape, x_s.shape[0]
    n = n_p + n_s
    row = lambda i: (i, 0)
    const = lambda i: (0, 0)
    return pl.pallas_call(
        functools.partial(_mix_kernel, n_prompt_tiles=n_p // tm),
        grid=(n // tm,),
        in_specs=[
            *_two_source_specs(tm, d, n_p // tm),
            *_two_source_specs(tm, D_CONV, n_p // tm),
            *_two_source_specs(tm, DN_WIDTH, n_p // tm),
            pl.BlockSpec((tm, d), row),
            pl.BlockSpec((tm, d), row),
            pl.BlockSpec((D_CONV, d), const),
            pl.BlockSpec((DN_WIDTH, d), const),
            pl.BlockSpec((d, d), const),
            pl.BlockSpec((1, d), const),
            pl.BlockSpec((d, 2 * LANES), const),
            pl.BlockSpec((1, LANES), const),
        ],
        out_specs=[
            pl.BlockSpec((tm, d), row),
            pl.BlockSpec((tm, d), row),
            pl.BlockSpec((tm, LANES), row),
            pl.BlockSpec((tm, LANES), row),
        ],
        out_shape=[
            jax.ShapeDtypeStruct((n, d), F32),
            jax.ShapeDtypeStruct((n, d), F32),
            jax.ShapeDtypeStruct((n, LANES), jnp.int32),
            jax.ShapeDtypeStruct((n, LANES), F32),
        ],
        compiler_params=_params(1),
    )(x_p, x_s, cact_p, cact_s, oact_p, oact_s, siga, sigb, w_conv_out, w_delta_out, w_merge_out, norm2_g,
      router_w, router_b)


def _fill_rows_per_step(n_fill, n_steps):
    per_step = SUBLANES
    while per_step * n_steps < n_fill:
        per_step *= 2
    assert n_fill % per_step == 0
    return per_step


def _route(top_idx, tm, n_blocks, n_steps):
    n = top_idx.shape[0]
    n_fill = n_blocks * tm - n * TOP_K
    assert n_fill == N_EXPERTS * tm
    flat_e = top_idx.reshape(-1)
    experts = jnp.arange(N_EXPERTS, dtype=jnp.int32)
    onehot = (flat_e[:, None] == experts[None, :]).astype(jnp.int32)
    csum = jnp.cumsum(onehot, axis=0)
    rank = jnp.sum(csum * onehot, axis=1) - 1
    counts = csum[-1]
    padded = (counts + tm - 1) // tm * tm
    pad_end = jnp.cumsum(padded)
    pad_start = pad_end - padded
    dest = jnp.sum(onehot * pad_start[None, :], axis=1) + rank
    nvalid = (pad_end[-1] // tm).astype(jnp.int32)
    blk = jnp.arange(n_blocks, dtype=jnp.int32)
    owner = jnp.sum((pad_end[None, :] <= (blk * tm)[:, None]).astype(jnp.int32), axis=1)
    block_e = jnp.minimum(owner, N_EXPERTS - 1)
    block_e = jnp.where(blk < nvalid, block_e, jnp.sum(jnp.where(blk == nvalid - 1, block_e, 0)))
    n_pad = padded - counts
    spill = tm - n_pad
    spill_start = pad_end[-1] + jnp.cumsum(spill) - spill
    j = jnp.arange(tm, dtype=jnp.int32)[None, :]
    fill = jnp.where(j < n_pad[:, None], (pad_start + counts)[:, None] + j, (spill_start - n_pad)[:, None] + j)
    fill_step = _fill_rows_per_step(n_fill, n_steps)
    fill = jnp.pad(fill.reshape(-1, fill_step), ((0, n_steps - n_fill // fill_step), (0, 0)))
    table = jnp.concatenate([dest.reshape(n_steps, -1), fill], axis=1)
    return dest.reshape(n, TOP_K), table, block_e, nvalid.reshape(1)


def _dispatch_kernel(tab_ref, h2_ref, xs_ref, sem, *, tokens, fill_step, n_fill_steps):
    i = pl.program_id(0)

    def wait_rows(count):
        while count > 0:
            rows = min(count, tokens)
            pltpu.make_async_copy(h2_ref.at[pl.ds(0, rows)], xs_ref.at[pl.ds(0, rows)], sem).wait()
            count -= rows

    for t in range(tokens):
        for k in range(TOP_K):
            pltpu.make_async_copy(h2_ref.at[pl.ds(t, 1)], xs_ref.at[pl.ds(tab_ref[i, t * TOP_K + k], 1)],
                                  sem).start(priority=k % DMA_PRIORITIES)

    @pl.when(i < n_fill_steps)
    def _():
        for p in range(fill_step):
            pltpu.make_async_copy(
                h2_ref.at[pl.ds(0, 1)], xs_ref.at[pl.ds(tab_ref[i, tokens * TOP_K + p], 1)],
                sem).start(priority=p % DMA_PRIORITIES)
        wait_rows(fill_step)

    wait_rows(tokens * TOP_K)


def _dispatch(h2, table, n_rows, tokens):
    n, d = h2.shape
    fill_step = table.shape[1] - tokens * TOP_K
    n_fill_steps = (n_rows - n * TOP_K) // fill_step
    return pl.pallas_call(
        functools.partial(_dispatch_kernel, tokens=tokens, fill_step=fill_step, n_fill_steps=n_fill_steps),
        grid_spec=pltpu.PrefetchScalarGridSpec(
            num_scalar_prefetch=1,
            grid=(n // tokens,),
            in_specs=[pl.BlockSpec((tokens, d), lambda i, tab: (i, 0))],
            out_specs=pl.BlockSpec(memory_space=pl.ANY),
            scratch_shapes=[pltpu.SemaphoreType.DMA(())],
        ),
        out_shape=jax.ShapeDtypeStruct((n_rows, d), h2.dtype),
        compiler_params=_params(1),
    )(table, h2)


def _moe_ffn_kernel(be_ref, nv_ref, xs_ref, wgu_ref, bgu_ref, wd_ref, bd_ref, out_ref, wgu_bf, wd_bf):
    i = pl.program_id(0)
    prev = be_ref[jnp.maximum(i - 1, 0)]
    changed = jnp.logical_or(i == 0, be_ref[i] != prev)

    @pl.when(changed)
    def _():
        wgu_bf[...] = wgu_ref[0].astype(BF16)
        wd_bf[...] = wd_ref[0].astype(BF16)

    @pl.when(i < nv_ref[0])
    def _():
        f = wd_bf.shape[0]
        gu = jnp.dot(xs_ref[...].astype(BF16), wgu_bf[...], preferred_element_type=F32) + bgu_ref[0]
        gt = jnp.minimum(gu[:, :f], SWIGLU_LIMIT)
        up = jnp.clip(gu[:, f:], -SWIGLU_LIMIT, SWIGLU_LIMIT)
        act = (up + 1.0) * (gt * _sigmoid(SWIGLU_ALPHA * gt))
        out_ref[...] = jnp.dot(act.astype(BF16), wd_bf[...], preferred_element_type=F32) + bd_ref[0]

    @pl.when(i >= nv_ref[0])
    def _():
        out_ref[...] = jnp.zeros(out_ref.shape, out_ref.dtype)


def _moe_ffn(xs, block_e, nvalid, w_gate_up, b_gate_up, w_down, b_down, tm, n_blocks):
    d = xs.shape[1]
    ne, _, f2 = w_gate_up.shape
    f = f2 // 2
    used = lambda i, be, nv: (jnp.minimum(i, nv[0] - 1), 0)
    return pl.pallas_call(
        _moe_ffn_kernel,
        grid_spec=pltpu.PrefetchScalarGridSpec(
            num_scalar_prefetch=2,
            grid=(n_blocks,),
            in_specs=[
                pl.BlockSpec((tm, d), used),
                pl.BlockSpec((1, d, f2), lambda i, be, nv: (be[i], 0, 0)),
                pl.BlockSpec((1, 1, f2), lambda i, be, nv: (be[i], 0, 0)),
                pl.BlockSpec((1, f, d), lambda i, be, nv: (be[i], 0, 0)),
                pl.BlockSpec((1, 1, d), lambda i, be, nv: (be[i], 0, 0)),
            ],
            out_specs=pl.BlockSpec((tm, d), lambda i, be, nv: (i, 0)),
            scratch_shapes=[pltpu.VMEM((d, f2), BF16), pltpu.VMEM((f, d), BF16)],
        ),
        out_shape=jax.ShapeDtypeStruct((n_blocks * tm, d), F32),
        compiler_params=_params(1),
    )(block_e, nvalid, xs, w_gate_up, b_gate_up.reshape(ne, 1, f2), w_down, b_down.reshape(ne, 1, d))


def _combine_kernel(slot_ref, yb_ref, x2_ref, tg_ref, fg_ref, yp_ref, ys_ref, buf, sems, *, tc, n_prompt_tiles):
    i = pl.program_id(0)
    n_steps = pl.num_programs(0)

    def fetch(step, slot):
        for t in range(tc):
            for k in range(TOP_K):
                pltpu.make_async_copy(
                    yb_ref.at[pl.ds(slot_ref[step, t * TOP_K + k], 1)], buf.at[slot, k, pl.ds(t, 1)],
                    sems.at[slot]).start(priority=k % DMA_PRIORITIES)

    @pl.when(i == 0)
    def _():
        fetch(0, 0)

    @pl.when(i + 1 < n_steps)
    def _():
        fetch(i + 1, (i + 1) % 2)

    slot = i % 2
    for k in range(TOP_K):
        pltpu.make_async_copy(yb_ref.at[pl.ds(0, tc)], buf.at[slot, k], sems.at[slot]).wait()
    tg = tg_ref[...]
    y = x2_ref[...]
    for k in range(TOP_K):
        y = y + tg[:, k:k + 1] * buf[slot, k]
    out = y * lax.rsqrt(jnp.mean(y * y, axis=-1, keepdims=True) + EPS) * fg_ref[...]

    @pl.when(i < n_prompt_tiles)
    def _():
        yp_ref[...] = out

    @pl.when(i >= n_prompt_tiles)
    def _():
        ys_ref[...] = out


def _combine(slot2d, yb, x2, tg, final_g, n_p, tc):
    n, d = x2.shape
    n_s = n - n_p
    assert n_p % tc == 0 and n_s % tc == 0
    npt = n_p // tc
    kern = functools.partial(_combine_kernel, tc=tc, n_prompt_tiles=npt)
    out_p, out_s = _two_source_specs(tc, d, npt)
    return pl.pallas_call(
        kern,
        grid_spec=pltpu.PrefetchScalarGridSpec(
            num_scalar_prefetch=1,
            grid=(n // tc,),
            in_specs=[
                pl.BlockSpec(memory_space=pl.ANY),
                pl.BlockSpec((tc, d), lambda i, s: (i, 0)),
                pl.BlockSpec((tc, LANES), lambda i, s: (i, 0)),
                pl.BlockSpec((1, d), lambda i, s: (0, 0)),
            ],
            out_specs=[out_p, out_s],
            scratch_shapes=[pltpu.VMEM((2, TOP_K, tc, d), F32), pltpu.SemaphoreType.DMA((2,))],
        ),
        out_shape=[jax.ShapeDtypeStruct((n_p, d), F32), jax.ShapeDtypeStruct((n_s, d), F32)],
        compiler_params=_params(1),
    )(slot2d, yb, x2, tg, final_g)


def _pad_lanes(v, width=LANES):
    v = v.reshape(1, -1)
    return jnp.pad(v, ((0, 0), (0, width - v.shape[1])))


def kernel(x_prompt, x_sample, state_conv, state_short_conv, state_delta, norm1_g, w_in, conv_dw_w,
           conv_dw_b, conv_ln_g, conv_ln_b, w_conv_out, short_conv_w, a_log, dt_bias, delta_norm_g,
           w_delta_out, w_merge_out, norm2_g, router_w, router_b, w_gate_up, b_gate_up, w_down, b_down,
           final_norm_g):
    depth = w_in.shape[0]
    assert depth == 1
    bp, tp, d = x_prompt.shape
    bs, ts, _ = x_sample.shape
    n_p, n_s = bp * tp, bs * ts
    n = n_p + n_s
    l = 0
    x_p = x_prompt.reshape(n_p, d)
    x_s = x_sample.reshape(n_s, d)

    o_ab = 2 * D_CONV + 4 * DN_WIDTH
    w = w_in[l]
    w_main = jnp.concatenate([w[:, :o_ab], w[:, o_ab + 2 * DN_HEADS:]], axis=1).astype(BF16)
    w_ab = _split_bf16(jnp.pad(w[:, o_ab:o_ab + 2 * DN_HEADS], ((0, 0), (0, LANES - 2 * DN_HEADS))))

    glu, qkv_pre, z, gb, siga, sigb = _inproj(
        x_p, x_s, norm1_g[l].reshape(1, d), w_main, w_ab, _pad_lanes(a_log[l]), _pad_lanes(dt_bias[l]), TOKEN_TILE)

    dw = (conv_dw_w[l], conv_dw_b[l].reshape(1, -1), conv_ln_g[l].reshape(1, -1), conv_ln_b[l].reshape(1, -1))
    pad_c = CONV_HALO - (CONV_WIDTH - 1)
    st_c_p = jnp.zeros((bp, CONV_HALO, D_CONV), F32)
    st_c_s = jnp.pad(state_conv[l], ((0, 0), (pad_c, 0), (0, 0)))
    cact_p, nconv_p = _conv_branch(glu, st_c_p, *dw, row0=0, bsz=bp, t_len=tp, bb=1, tt=SEQ_TILE)
    cact_s, nconv_s = _conv_branch(glu, st_c_s, *dw, row0=n_p, bsz=bs, t_len=ts, bb=8, tt=ts)

    pad_s = SHORT_HALO - (SHORT_WIDTH - 1)
    st_s_p = jnp.zeros((bp, SHORT_HALO, 3 * DN_WIDTH), F32)
    st_s_s = jnp.pad(state_short_conv[l], ((0, 0), (pad_s, 0), (0, 0)))
    s0_p = jnp.zeros((bp, DN_HEADS, DN_HEAD_DIM, DN_HEAD_DIM), F32)
    ng = delta_norm_g[l].reshape(1, -1)
    oact_p, nshort_p, s_p = _delta_prompt(qkv_pre, z, gb, st_s_p, s0_p, short_conv_w[l], ng,
                                          bsz=bp, t_len=tp, tt=SEQ_TILE)
    oact_s, nshort_s, s_s = _delta_sample(qkv_pre, z, gb, st_s_s, state_delta[l], short_conv_w[l], ng,
                                          row0=n_p, bsz=bs, seq_len=ts)

    rw = _split_bf16(jnp.pad(router_w[l], ((0, 0), (0, LANES - N_EXPERTS))))
    x2, h2, ti, tg = _mix(x_p, x_s, cact_p, cact_s, oact_p, oact_s, siga, sigb, w_conv_out[l].astype(BF16),
                          w_delta_out[l].astype(BF16), w_merge_out[l].astype(BF16), norm2_g[l].reshape(1, d),
                          rw, _pad_lanes(router_b[l]), TOKEN_TILE)

    n_blocks = -(-(n * TOP_K) // MOE_TILE) + N_EXPERTS
    n_steps = n // TOKEN_TILE
    dest, table, block_e, nvalid = _route(ti[:, :TOP_K], MOE_TILE, n_blocks, n_steps)
    xs = _dispatch(h2, table, n_blocks * MOE_TILE, TOKEN_TILE)
    yb = _moe_ffn(xs, block_e, nvalid, w_gate_up[l], b_gate_up[l], w_down[l], b_down[l], MOE_TILE, n_blocks)
    y_p, y_s = _combine(dest.reshape(n_steps, -1), yb, x2, tg, final_norm_g.reshape(1, d), n_p, TOKEN_TILE)

    conv_p = nconv_p[:, pad_c:, :][None]
    conv_s = nconv_s[:, pad_c:, :][None]
    short_p = nshort_p[:, pad_s:, :][None]
    short_s = nshort_s[:, pad_s:, :][None]
    return (y_p.reshape(bp, tp, d), y_s.reshape(bs, ts, d), conv_p, short_p, s_p[None],
            conv_s, short_s, s_s[None])
```

```python
import functools

import jax
import jax.numpy as jnp
from jax import lax
from jax.experimental import pallas as pl
from jax.experimental.pallas import tpu as pltpu

F32 = jnp.float32
BF16 = jnp.bfloat16
EPS = 1e-6

LANES = 128
SUBLANES = 8
VMEM_LIMIT_BYTES = 56 * 1024 * 1024
DMA_PRIORITIES = 2

D_CONV = 512
CONV_WIDTH = 31
DN_HEADS = 4
DN_HEAD_DIM = 128
DN_WIDTH = DN_HEADS * DN_HEAD_DIM
SHORT_WIDTH = 4
N_EXPERTS = 32
TOP_K = 4
SWIGLU_LIMIT = 7.0
SWIGLU_ALPHA = 1.702

CHUNK = 128
CONV_HALO = 32
SHORT_HALO = 8

TOKEN_TILE = 256
SEQ_TILE = 256
MOE_TILE = 256


def _sigmoid(x):
    return 1.0 / (1.0 + jnp.exp(-x))


def _silu(x):
    return x * _sigmoid(x)


def _split_bf16(w):
    hi = w.astype(BF16)
    lo = (w - hi.astype(F32)).astype(BF16)
    return jnp.concatenate([hi, lo], axis=-1)


def _dot_split(x, w_split):
    n = w_split.shape[-1] // 2
    x_hi = x.astype(BF16)
    x_lo = (x - x_hi.astype(F32)).astype(BF16)
    r = jnp.dot(x_hi, w_split, preferred_element_type=F32)
    return r[:, :n] + r[:, n:] + jnp.dot(x_lo, w_split[:, :n], preferred_element_type=F32)


def _dot_delta(a, b, dims=(((1,), (0,)), ((), ()))):
    return lax.dot_general(a.astype(BF16), b.astype(BF16), dims, preferred_element_type=F32)


def _params(n_axes):
    return pltpu.CompilerParams(dimension_semantics=("arbitrary",) * n_axes, vmem_limit_bytes=VMEM_LIMIT_BYTES)


def _two_source_specs(tm, d, n_first_tiles):
    first = pl.BlockSpec((tm, d), lambda i, *_: (jnp.minimum(i, n_first_tiles - 1), 0))
    second = pl.BlockSpec((tm, d), lambda i, *_: (jnp.maximum(i - n_first_tiles, 0), 0))
    return first, second


def _inproj_kernel(xp_ref, xs_ref, g_ref, w_ref, wab_ref, alog_ref, dtb_ref,
                   glu_ref, qkv_ref, z_ref, gb_ref, sa_ref, sb_ref, *, n_prompt_tiles):
    x = jnp.where(pl.program_id(0) < n_prompt_tiles, xp_ref[...], xs_ref[...])
    h = x * lax.rsqrt(jnp.mean(x * x, axis=-1, keepdims=True) + EPS) * g_ref[...]
    hb = h.astype(BF16)

    def mm(lo, hi):
        return jnp.dot(hb, w_ref[:, lo:hi], preferred_element_type=F32)

    o_gate, o_qkv, o_z = D_CONV, 2 * D_CONV, 2 * D_CONV + 3 * DN_WIDTH
    o_ga = o_z + DN_WIDTH
    d = x.shape[-1]
    glu_ref[...] = mm(0, o_gate) * _sigmoid(mm(o_gate, o_qkv))
    qkv_ref[...] = mm(o_qkv, o_z)
    z_ref[...] = mm(o_z, o_ga)
    sa_ref[...] = _sigmoid(mm(o_ga, o_ga + d))
    sb_ref[...] = _sigmoid(mm(o_ga + d, o_ga + 2 * d))
    ab = _dot_split(h, wab_ref[...])
    xa = ab + dtb_ref[...]
    softplus = jnp.maximum(xa, 0.0) + jnp.log(1.0 + jnp.exp(-jnp.abs(xa)))
    g = -jnp.exp(alog_ref[...]) * softplus
    lane = lax.broadcasted_iota(jnp.int32, ab.shape, 1)
    gb_ref[...] = jnp.where(lane < DN_HEADS, g, _sigmoid(ab))


def _inproj(x_p, x_s, norm_g, w_main, w_ab, alog, dtb, tm):
    (n_p, d), n_s = x_p.shape, x_s.shape[0]
    assert n_p % tm == 0 and n_s % tm == 0
    n = n_p + n_s
    wcols = w_main.shape[1]
    row = lambda i: (i, 0)
    const = lambda i: (0, 0)
    outs = [(D_CONV, F32), (3 * DN_WIDTH, F32), (DN_WIDTH, F32), (LANES, F32), (d, F32), (d, F32)]
    return pl.pallas_call(
        functools.partial(_inproj_kernel, n_prompt_tiles=n_p // tm),
        grid=(n // tm,),
        in_specs=[
            *_two_source_specs(tm, d, n_p // tm),
            pl.BlockSpec((1, d), const),
            pl.BlockSpec((d, wcols), const),
            pl.BlockSpec((d, 2 * LANES), const),
            pl.BlockSpec((1, LANES), const),
            pl.BlockSpec((1, LANES), const),
        ],
        out_specs=[pl.BlockSpec((tm, c), row) for c, _ in outs],
        out_shape=[jax.ShapeDtypeStruct((n, c), dt) for c, dt in outs],
        compiler_params=_params(1),
    )(x_p, x_s, norm_g, w_main, w_ab, alog, dtb)


def _conv_kernel(glu_ref, st_ref, w_ref, b_ref, lg_ref, lb_ref, out_ref, nst_ref, e_ref, sh_ref, *, bb, tt, rows):
    t = pl.program_id(1)

    @pl.when(t == 0)
    def _():
        e_ref[:, 0:CONV_HALO, :] = st_ref[...]

    for b in range(bb):
        e_ref[b, CONV_HALO:CONV_HALO + tt, :] = glu_ref[b * tt:(b + 1) * tt, :]
    off = CONV_HALO - (CONV_WIDTH - 1)
    span = sh_ref.shape[1]
    for b in range(bb):
        for s in range(1, SUBLANES):
            sh_ref[s - 1] = e_ref[b, s:s + span, :]
        for c in range(tt // rows):
            r0 = c * rows
            acc = jnp.zeros((rows, D_CONV), F32) + b_ref[...]
            for j in range(CONV_WIDTH):
                q, s = divmod(j + off, SUBLANES)
                lo = r0 + q * SUBLANES
                src = e_ref[b, lo:lo + rows, :] if s == 0 else sh_ref[s - 1, lo:lo + rows, :]
                acc = acc + w_ref[j:j + 1, :] * src
            mu = jnp.mean(acc, axis=-1, keepdims=True)
            xc = acc - mu
            var = jnp.mean(xc * xc, axis=-1, keepdims=True)
            y = xc * lax.rsqrt(var + EPS) * lg_ref[...] + lb_ref[...]
            out_ref[b * tt + r0:b * tt + r0 + rows, :] = _silu(y).astype(out_ref.dtype)
    tail = e_ref[:, tt:tt + CONV_HALO, :]
    nst_ref[...] = tail
    e_ref[:, 0:CONV_HALO, :] = tail


def _conv_branch(glu, state32, dw_w, dw_b, ln_g, ln_b, *, row0, bsz, t_len, bb, tt):
    c = glu.shape[1]
    assert bsz % bb == 0 and t_len % tt == 0 and row0 % (bb * tt) == 0
    nt = t_len // tt
    blk0 = row0 // (bb * tt)
    rows = min(tt, 32)
    kern = functools.partial(_conv_kernel, bb=bb, tt=tt, rows=rows)
    const = lambda b, t: (0, 0)
    return pl.pallas_call(
        kern,
        grid=(bsz // bb, nt),
        in_specs=[
            pl.BlockSpec((bb * tt, c), lambda b, t: (blk0 + b * nt + t, 0)),
            pl.BlockSpec((bb, CONV_HALO, c), lambda b, t: (b, 0, 0)),
            pl.BlockSpec((CONV_WIDTH, c), const),
            pl.BlockSpec((1, c), const),
            pl.BlockSpec((1, c), const),
            pl.BlockSpec((1, c), const),
        ],
        out_specs=[
            pl.BlockSpec((bb * tt, c), lambda b, t: (b * nt + t, 0)),
            pl.BlockSpec((bb, CONV_HALO, c), lambda b, t: (b, 0, 0)),
        ],
        out_shape=[
            jax.ShapeDtypeStruct((bsz * t_len, c), BF16),
            jax.ShapeDtypeStruct((bsz, CONV_HALO, c), F32),
        ],
        scratch_shapes=[pltpu.VMEM((bb, CONV_HALO + tt, c), F32),
                        pltpu.VMEM((SUBLANES - 1, tt + CONV_HALO - SUBLANES, c), F32)],
        compiler_params=_params(2),
    )(glu, state32, dw_w, dw_b, ln_g, ln_b)


def _chunk_masks(seq_len):
    i = lax.broadcasted_iota(jnp.int32, (CHUNK, CHUNK), 0)
    j = lax.broadcasted_iota(jnp.int32, (CHUNK, CHUNK), 1)
    same = (i // seq_len) == (j // seq_len)
    incl = same & (i >= j)
    strict = same & (i > j)
    last = j == (i // seq_len) * seq_len + (seq_len - 1)
    levels = []
    blk = 1
    while blk < seq_len:
        levels.append(((i // (2 * blk)) == (j // (2 * blk))) & (((i // blk) % 2) == 1) & (((j // blk) % 2) == 0))
        blk *= 2
    eye = i == j
    return incl, strict, last, levels, eye


def _lane_col(x, lane):
    return jnp.broadcast_to(x[:, lane:lane + 1], (x.shape[0], LANES))


def _l2norm(x):
    return x * lax.rsqrt(jnp.sum(x * x, axis=-1, keepdims=True) + EPS)


def _select_sum(mask01, x):
    hi = x.astype(BF16)
    r1 = x - hi.astype(F32)
    mid = r1.astype(BF16)
    lo = (r1 - mid.astype(F32)).astype(BF16)
    w = x.shape[1]
    parts = jnp.dot(mask01, jnp.concatenate([hi, mid, lo], axis=1), preferred_element_type=F32)
    return parts[:, :w] + parts[:, w:2 * w] + parts[:, 2 * w:]


def _chunks_prepare(qkvs, gbts, masks, seq_len):
    incl, strict, last, levels, eye = masks
    nt = (((1,), (1,)), ((), ()))
    lower01 = jnp.where(incl, 1.0, 0.0).astype(BF16)
    probs = []
    for qkv, gbt in zip(qkvs, gbts):
        gc = _select_sum(lower01, gbt)
        gct = gc.T
        if seq_len == CHUNK:
            glast = jnp.broadcast_to(gc[CHUNK - 1:CHUNK, :], gc.shape)
        else:
            glast = _select_sum(jnp.where(last, 1.0, 0.0).astype(BF16), gc)
        for h in range(DN_HEADS):
            q = _l2norm(qkv[:, h * DN_HEAD_DIM:(h + 1) * DN_HEAD_DIM]) * (DN_HEAD_DIM ** -0.5)
            k = _l2norm(qkv[:, DN_WIDTH + h * DN_HEAD_DIM:DN_WIDTH + (h + 1) * DN_HEAD_DIM])
            v = qkv[:, 2 * DN_WIDTH + h * DN_HEAD_DIM:2 * DN_WIDTH + (h + 1) * DN_HEAD_DIM]
            gcol = _lane_col(gc, h)
            grow = jnp.broadcast_to(gct[h:h + 1, :], (CHUNK, CHUNK))
            beta = _lane_col(gbt, DN_HEADS + h)
            gl = _lane_col(glast, h)
            decay = jnp.exp(jnp.where(incl, gcol - grow, -jnp.inf))
            egc = jnp.exp(gcol)
            kb = k * beta
            probs.append(dict(q=q, k=k, kb=kb, decay=decay, rhs=jnp.concatenate([v * beta, kb * egc], axis=1),
                              qexp=q * egc, kdec=k * jnp.exp(gl - gcol), egl=jnp.exp(gl)))
    for p in probs:
        p['a'] = jnp.where(strict, _dot_delta(p['kb'], p['k'], nt) * p['decay'], 0.0)
        p['scores'] = _dot_delta(p['q'], p['k'], nt) * p['decay']
    for p in probs:
        p['x'] = jnp.where(eye, 1.0, 0.0) - jnp.where(levels[0], p['a'], 0.0)
    for m in levels[1:]:
        for p in probs:
            p['xa'] = _dot_delta(p['x'], jnp.where(m, p['a'], 0.0))
        for p in probs:
            p['x'] = p['x'] - _dot_delta(p['xa'], p['x'])
    out = []
    for c in range(len(qkvs)):
        heads = []
        for h in range(DN_HEADS):
            p = probs[c * DN_HEADS + h]
            sol = _dot_delta(p['x'], p['rhs'])
            heads.append((sol[:, :DN_HEAD_DIM], sol[:, DN_HEAD_DIM:], p['scores'], p['qexp'], p['kdec'], p['egl']))
        out.append(heads)
    return out


def _gated_out_norm(o, z, ng):
    y = o * lax.rsqrt(jnp.mean(o * o, axis=-1, keepdims=True) + EPS) * ng
    return y * _silu(z)


def _short_conv(e_ref, w_ref, tt):
    off = SHORT_HALO - (SHORT_WIDTH - 1)
    acc = w_ref[0:1, :] * e_ref[off:off + tt, :]
    for j in range(1, SHORT_WIDTH):
        acc = acc + w_ref[j:j + 1, :] * e_ref[off + j:off + j + tt, :]
    return _silu(acc)


def _delta_prompt_kernel(qkv_ref, z_ref, gb_ref, st_ref, s0_ref, w_ref, ng_ref,
                         o_ref, nst_ref, sout_ref, e_ref, s_ref, *, tt):
    t = pl.program_id(1)

    @pl.when(t == 0)
    def _():
        e_ref[0:SHORT_HALO, :] = st_ref[0]
        s_ref[...] = s0_ref[0]

    e_ref[SHORT_HALO:SHORT_HALO + tt, :] = qkv_ref[...]
    qkv = _short_conv(e_ref, w_ref, tt)
    tail = e_ref[tt:tt + SHORT_HALO, :]
    nst_ref[0] = tail
    e_ref[0:SHORT_HALO, :] = tail

    masks = _chunk_masks(CHUNK)
    tn = (((0,), (0,)), ((), ()))
    n_chunks = tt // CHUNK
    prep = _chunks_prepare([qkv[c * CHUNK:(c + 1) * CHUNK, :] for c in range(n_chunks)],
                           [gb_ref[c * CHUNK:(c + 1) * CHUNK, :] for c in range(n_chunks)], masks, CHUNK)
    heads = range(DN_HEADS)
    s = [s_ref[h] for h in heads]
    for c in range(n_chunks):
        r0 = c * CHUNK
        value, kcum, scores, qexp, kdec, egl = zip(*prep[c])
        both = [_dot_delta(jnp.concatenate([kcum[h], qexp[h]], axis=0), s[h]) for h in heads]
        v_new = [value[h] - both[h][:CHUNK] for h in heads]
        o = [both[h][CHUNK:] + _dot_delta(scores[h], v_new[h]) for h in heads]
        s = [s[h] * egl[h][0:1, :] + _dot_delta(kdec[h], v_new[h], tn) for h in heads]
        for h in heads:
            lanes = slice(h * DN_HEAD_DIM, (h + 1) * DN_HEAD_DIM)
            o_ref[r0:r0 + CHUNK, lanes] = _gated_out_norm(
                o[h], z_ref[r0:r0 + CHUNK, lanes], ng_ref[...]).astype(o_ref.dtype)
    for h in heads:
        s_ref[h] = s[h]
        sout_ref[0, h] = s[h]


def _delta_prompt(qkv_pre, z, gb, state8, s0, conv_w, norm_g, *, bsz, t_len, tt):
    n = bsz * t_len
    assert t_len % tt == 0 and tt % CHUNK == 0
    nt = t_len // tt
    kern = functools.partial(_delta_prompt_kernel, tt=tt)
    tile = lambda b, t: (b * nt + t, 0)
    per_b = lambda b, t: (b, 0, 0)
    per_b4 = lambda b, t: (b, 0, 0, 0)
    return pl.pallas_call(
        kern,
        grid=(bsz, nt),
        in_specs=[
            pl.BlockSpec((tt, 3 * DN_WIDTH), tile),
            pl.BlockSpec((tt, DN_WIDTH), tile),
            pl.BlockSpec((tt, LANES), tile),
            pl.BlockSpec((1, SHORT_HALO, 3 * DN_WIDTH), per_b),
            pl.BlockSpec((1, DN_HEADS, DN_HEAD_DIM, DN_HEAD_DIM), per_b4),
            pl.BlockSpec((SHORT_WIDTH, 3 * DN_WIDTH), lambda b, t: (0, 0)),
            pl.BlockSpec((1, DN_HEAD_DIM), lambda b, t: (0, 0)),
        ],
        out_specs=[
            pl.BlockSpec((tt, DN_WIDTH), tile),
            pl.BlockSpec((1, SHORT_HALO, 3 * DN_WIDTH), per_b),
            pl.BlockSpec((1, DN_HEADS, DN_HEAD_DIM, DN_HEAD_DIM), per_b4),
        ],
        out_shape=[
            jax.ShapeDtypeStruct((n, DN_WIDTH), BF16),
            jax.ShapeDtypeStruct((bsz, SHORT_HALO, 3 * DN_WIDTH), F32),
            jax.ShapeDtypeStruct((bsz, DN_HEADS, DN_HEAD_DIM, DN_HEAD_DIM), F32),
        ],
        scratch_shapes=[
            pltpu.VMEM((SHORT_HALO + tt, 3 * DN_WIDTH), F32),
            pltpu.VMEM((DN_HEADS, DN_HEAD_DIM, DN_HEAD_DIM), F32),
        ],
        compiler_params=_params(2),
    )(qkv_pre, z, gb, state8, s0, conv_w, norm_g)


def _delta_sample_kernel(qkv_ref, z_ref, gb_ref, st_ref, s0_ref, w_ref, ng_ref, o_ref, nst_ref, sout_ref,
                         e_ref, *, nseq, seq_len):
    qkv_rows = []
    for b in range(nseq):
        e_ref[0:SHORT_HALO, :] = st_ref[b]
        e_ref[SHORT_HALO:SHORT_HALO + seq_len, :] = qkv_ref[b * seq_len:(b + 1) * seq_len, :]
        qkv_rows.append(_short_conv(e_ref, w_ref, seq_len))
        nst_ref[b] = e_ref[seq_len:seq_len + SHORT_HALO, :]
    qkv = jnp.concatenate(qkv_rows, axis=0)
    masks = _chunk_masks(seq_len)
    prep = _chunks_prepare([qkv], [gb_ref[...]], masks, seq_len)[0]
    tn = (((0,), (0,)), ((), ()))
    rows = [slice(b * seq_len, (b + 1) * seq_len) for b in range(nseq)]
    both = [[_dot_delta(jnp.concatenate([prep[h][1][r], prep[h][3][r]], axis=0), s0_ref[b, h])
             for b, r in enumerate(rows)] for h in range(DN_HEADS)]
    v_new = [[prep[h][0][r] - both[h][b][:seq_len] for b, r in enumerate(rows)] for h in range(DN_HEADS)]
    for h in range(DN_HEADS):
        kdec, egl = prep[h][4], prep[h][5]
        for b, r in enumerate(rows):
            sout_ref[b, h] = (s0_ref[b, h] * egl[b * seq_len:b * seq_len + 1, :]
                              + _dot_delta(kdec[r], v_new[h][b], tn))
    for h in range(DN_HEADS):
        o = (jnp.concatenate([both[h][b][seq_len:] for b in range(nseq)], axis=0)
             + _dot_delta(prep[h][2], jnp.concatenate(v_new[h], axis=0)))
        lanes = slice(h * DN_HEAD_DIM, (h + 1) * DN_HEAD_DIM)
        o_ref[:, lanes] = _gated_out_norm(o, z_ref[:, lanes], ng_ref[...]).astype(o_ref.dtype)


def _delta_sample(qkv_pre, z, gb, state8, s0, conv_w, norm_g, *, row0, bsz, seq_len):
    n = bsz * seq_len
    assert CHUNK % seq_len == 0
    nseq = CHUNK // seq_len
    assert bsz % nseq == 0 and row0 % CHUNK == 0
    blk0 = row0 // CHUNK
    kern = functools.partial(_delta_sample_kernel, nseq=nseq, seq_len=seq_len)
    tile = lambda i: (blk0 + i, 0)
    blk3 = lambda i: (i, 0, 0)
    blk4 = lambda i: (i, 0, 0, 0)
    return pl.pallas_call(
        kern,
        grid=(bsz // nseq,),
        in_specs=[
            pl.BlockSpec((CHUNK, 3 * DN_WIDTH), tile),
            pl.BlockSpec((CHUNK, DN_WIDTH), tile),
            pl.BlockSpec((CHUNK, LANES), tile),
            pl.BlockSpec((nseq, SHORT_HALO, 3 * DN_WIDTH), blk3),
            pl.BlockSpec((nseq, DN_HEADS, DN_HEAD_DIM, DN_HEAD_DIM), blk4),
            pl.BlockSpec((SHORT_WIDTH, 3 * DN_WIDTH), lambda i: (0, 0)),
            pl.BlockSpec((1, DN_HEAD_DIM), lambda i: (0, 0)),
        ],
        out_specs=[
            pl.BlockSpec((CHUNK, DN_WIDTH), lambda i: (i, 0)),
            pl.BlockSpec((nseq, SHORT_HALO, 3 * DN_WIDTH), blk3),
            pl.BlockSpec((nseq, DN_HEADS, DN_HEAD_DIM, DN_HEAD_DIM), blk4),
        ],
        out_shape=[
            jax.ShapeDtypeStruct((n, DN_WIDTH), BF16),
            jax.ShapeDtypeStruct((bsz, SHORT_HALO, 3 * DN_WIDTH), F32),
            jax.ShapeDtypeStruct((bsz, DN_HEADS, DN_HEAD_DIM, DN_HEAD_DIM), F32),
        ],
        scratch_shapes=[pltpu.VMEM((SHORT_HALO + seq_len, 3 * DN_WIDTH), F32)],
        compiler_params=_params(1),
    )(qkv_pre, z, gb, state8, s0, conv_w, norm_g)


def _mix_kernel(xp_ref, xs_ref, cap_ref, cas_ref, oap_ref, oas_ref, sa_ref, sb_ref, wc_ref, wd_ref, wm_ref,
                g2_ref, rw_ref, rb_ref, x2_ref, ti_ref, tg_ref, *, n_prompt_tiles):
    is_prompt = pl.program_id(0) < n_prompt_tiles
    x = jnp.where(is_prompt, xp_ref[...], xs_ref[...])
    ca = jnp.where(is_prompt, cap_ref[...], cas_ref[...])
    oa = jnp.where(is_prompt, oap_ref[...], oas_ref[...])
    ya = jnp.dot(ca, wc_ref[...], preferred_element_type=F32)
    yb = jnp.dot(oa, wd_ref[...], preferred_element_type=F32)
    mixed = sa_ref[...] * ya + sb_ref[...] * yb
    x2 = x + jnp.dot(mixed.astype(BF16), wm_ref[...], preferred_element_type=F32)
    x2_ref[...] = x2
    h2 = x2 * lax.rsqrt(jnp.mean(x2 * x2, axis=-1, keepdims=True) + EPS) * g2_ref[...]
    logits = _dot_split(h2, rw_ref[...]) + rb_ref[...]
    lane = lax.broadcasted_iota(jnp.int32, logits.shape, 1)
    lane_f = lane.astype(F32)
    logits = jnp.where(lane < N_EXPERTS, logits, -jnp.inf)
    top_vals = []
    ti = jnp.zeros(logits.shape, jnp.int32)
    for k in range(TOP_K):
        m = jnp.max(logits, axis=-1, keepdims=True)
        idx = jnp.min(jnp.where(logits == m, lane_f, float(LANES)), axis=-1, keepdims=True).astype(jnp.int32)
        ti = jnp.where(lane == k, idx, ti)
        top_vals.append(m)
        logits = jnp.where(lane == idx, -jnp.inf, logits)
    exps = [jnp.exp(v - top_vals[0]) for v in top_vals]
    den = exps[0] + exps[1] + exps[2] + exps[3]
    tg = jnp.zeros(logits.shape, F32)
    for k in range(TOP_K):
        tg = jnp.where(lane == k, exps[k] / den, tg)
    ti_ref[...] = ti
    tg_ref[...] = tg


def _mix(x_p, x_s, cact_p, cact_s, oact_p, oact_s, siga, sigb, w_conv_out, w_delta_out, w_merge_out, norm2_g,
         router_w, router_b, tm):
    (n_p, d), n_s = x_p.shape, x_s.shape[0]
    n = n_p + n_s
    row = lambda i: (i, 0)
    const = lambda i: (0, 0)
    return pl.pallas_call(
        functools.partial(_mix_kernel, n_prompt_tiles=n_p // tm),
        grid=(n // tm,),
        in_specs=[
            *_two_source_specs(tm, d, n_p // tm),
            *_two_source_specs(tm, D_CONV, n_p // tm),
            *_two_source_specs(tm, DN_WIDTH, n_p // tm),
            pl.BlockSpec((tm, d), row),
            pl.BlockSpec((tm, d), row),
            pl.BlockSpec((D_CONV, d), const),
            pl.BlockSpec((DN_WIDTH, d), const),
            pl.BlockSpec((d, d), const),
            pl.BlockSpec((1, d), const),
            pl.BlockSpec((d, 2 * LANES), const),
            pl.BlockSpec((1, LANES), const),
        ],
        out_specs=[
            pl.BlockSpec((tm, d), row),
            pl.BlockSpec((tm, LANES), row),
            pl.BlockSpec((tm, LANES), row),
        ],
        out_shape=[
            jax.ShapeDtypeStruct((n, d), F32),
            jax.ShapeDtypeStruct((n, LANES), jnp.int32),
            jax.ShapeDtypeStruct((n, LANES), F32),
        ],
        compiler_params=_params(1),
    )(x_p, x_s, cact_p, cact_s, oact_p, oact_s, siga, sigb, w_conv_out, w_delta_out, w_merge_out, norm2_g,
      router_w, router_b)


def _fill_rows_per_step(n_fill, n_steps):
    per_step = SUBLANES
    while per_step * n_steps < n_fill:
        per_step *= 2
    assert n_fill % per_step == 0
    return per_step


def _route(top_idx, tm, n_blocks, n_steps):
    n = top_idx.shape[0]
    n_fill = n_blocks * tm - n * TOP_K
    assert n_fill == N_EXPERTS * tm
    flat_e = top_idx.reshape(-1)
    experts = jnp.arange(N_EXPERTS, dtype=jnp.int32)
    onehot = (flat_e[:, None] == experts[None, :]).astype(jnp.int32)
    csum = jnp.cumsum(onehot, axis=0)
    rank = jnp.sum(csum * onehot, axis=1) - 1
    counts = csum[-1]
    padded = (counts + tm - 1) // tm * tm
    pad_end = jnp.cumsum(padded)
    pad_start = pad_end - padded
    dest = jnp.sum(onehot * pad_start[None, :], axis=1) + rank
    nvalid = (pad_end[-1] // tm).astype(jnp.int32)
    blk = jnp.arange(n_blocks, dtype=jnp.int32)
    owner = jnp.sum((pad_end[None, :] <= (blk * tm)[:, None]).astype(jnp.int32), axis=1)
    block_e = jnp.minimum(owner, N_EXPERTS - 1)
    block_e = jnp.where(blk < nvalid, block_e, jnp.sum(jnp.where(blk == nvalid - 1, block_e, 0)))
    n_pad = padded - counts
    spill = tm - n_pad
    spill_start = pad_end[-1] + jnp.cumsum(spill) - spill
    j = jnp.arange(tm, dtype=jnp.int32)[None, :]
    fill = jnp.where(j < n_pad[:, None], (pad_start + counts)[:, None] + j, (spill_start - n_pad)[:, None] + j)
    fill_step = _fill_rows_per_step(n_fill, n_steps)
    fill = jnp.pad(fill.reshape(-1, fill_step), ((0, n_steps - n_fill // fill_step), (0, 0)))
    table = jnp.concatenate([dest.reshape(n_steps, -1), fill], axis=1)
    return dest.reshape(n, TOP_K), table, block_e, nvalid.reshape(1)


def _to_token_tiles(x, ref, row0=0):
    t = x.shape[0]
    for s in range(SUBLANES):
        ref[pl.ds(row0 + s, t, stride=SUBLANES), :] = x[:, s * LANES:(s + 1) * LANES]


def _from_token_tiles(ref, t, row0=0):
    return jnp.concatenate([ref[pl.ds(row0 + s, t, stride=SUBLANES), :] for s in range(SUBLANES)], axis=1)


def _tile(ref, row):
    return ref.at[pl.ds(pl.multiple_of(row * SUBLANES, SUBLANES), SUBLANES)]


def _dispatch_kernel(tab_ref, x2_ref, g2_ref, xs_ref, buf, sems, *, tokens, fill_step, n_fill_steps):
    i = pl.program_id(0)
    last = pl.num_programs(0) - 1
    slot = i % 2
    base = slot * tokens
    fill_sem = 2

    x2 = x2_ref[...]
    h2 = x2 * lax.rsqrt(jnp.mean(x2 * x2, axis=-1, keepdims=True) + EPS) * g2_ref[...]
    _to_token_tiles(h2, buf, pl.multiple_of(base * SUBLANES, SUBLANES))

    def wait_tiles(sem_idx, count):
        while count > 0:
            rows = min(count, tokens) * SUBLANES
            pltpu.make_async_copy(buf.at[pl.ds(0, rows)], xs_ref.at[pl.ds(0, rows)], sems.at[sem_idx]).wait()
            count -= min(count, tokens)

    for t in range(tokens):
        for k in range(TOP_K):
            pltpu.make_async_copy(_tile(buf, base + t), _tile(xs_ref, tab_ref[i, t * TOP_K + k]),
                                  sems.at[slot]).start(priority=k % DMA_PRIORITIES)

    @pl.when(i < n_fill_steps)
    def _():
        for p in range(fill_step):
            pltpu.make_async_copy(_tile(buf, base), _tile(xs_ref, tab_ref[i, tokens * TOP_K + p]),
                                  sems.at[fill_sem]).start(priority=p % DMA_PRIORITIES)
        wait_tiles(fill_sem, fill_step)

    @pl.when(i > 0)
    def _():
        wait_tiles(1 - slot, tokens * TOP_K)

    @pl.when(i == last)
    def _():
        wait_tiles(slot, tokens * TOP_K)


def _dispatch(x2, norm_g, table, n_rows, tokens):
    n, d = x2.shape
    assert d == SUBLANES * LANES
    fill_step = table.shape[1] - tokens * TOP_K
    n_fill_steps = (n_rows - n * TOP_K) // fill_step
    return pl.pallas_call(
        functools.partial(_dispatch_kernel, tokens=tokens, fill_step=fill_step, n_fill_steps=n_fill_steps),
        grid_spec=pltpu.PrefetchScalarGridSpec(
            num_scalar_prefetch=1,
            grid=(n // tokens,),
            in_specs=[pl.BlockSpec((tokens, d), lambda i, tab: (i, 0)),
                      pl.BlockSpec((1, d), lambda i, tab: (0, 0))],
            out_specs=pl.BlockSpec(memory_space=pl.ANY),
            scratch_shapes=[pltpu.VMEM((2 * tokens * SUBLANES, LANES), F32), pltpu.SemaphoreType.DMA((3,))],
        ),
        out_shape=jax.ShapeDtypeStruct((n_rows * SUBLANES, LANES), F32),
        compiler_params=_params(1),
    )(table, x2, norm_g)


def _moe_ffn_kernel(be_ref, nv_ref, xs_ref, wgu_ref, bgu_ref, wd_ref, bd_ref, out_ref, wgu_bf, wd_bf):
    i = pl.program_id(0)
    prev = be_ref[jnp.maximum(i - 1, 0)]
    changed = jnp.logical_or(i == 0, be_ref[i] != prev)

    @pl.when(changed)
    def _():
        wgu_bf[...] = wgu_ref[0].astype(BF16)
        wd_bf[...] = wd_ref[0].astype(BF16)

    @pl.when(i < nv_ref[0])
    def _():
        f = wd_bf.shape[0]
        tm = xs_ref.shape[0] // SUBLANES
        x = _from_token_tiles(xs_ref, tm).astype(BF16)
        gu = jnp.dot(x, wgu_bf[...], preferred_element_type=F32) + bgu_ref[0]
        gt = jnp.minimum(gu[:, :f], SWIGLU_LIMIT)
        up = jnp.clip(gu[:, f:], -SWIGLU_LIMIT, SWIGLU_LIMIT)
        act = (up + 1.0) * (gt * _sigmoid(SWIGLU_ALPHA * gt))
        y = jnp.dot(act.astype(BF16), wd_bf[...], preferred_element_type=F32) + bd_ref[0]
        _to_token_tiles(y, out_ref)

    @pl.when(i >= nv_ref[0])
    def _():
        out_ref[...] = jnp.zeros(out_ref.shape, out_ref.dtype)


def _moe_ffn(xs, block_e, nvalid, w_gate_up, b_gate_up, w_down, b_down, tm, n_blocks):
    ne, d, f2 = w_gate_up.shape
    f = f2 // 2
    used = lambda i, be, nv: (jnp.minimum(i, nv[0] - 1), 0)
    return pl.pallas_call(
        _moe_ffn_kernel,
        grid_spec=pltpu.PrefetchScalarGridSpec(
            num_scalar_prefetch=2,
            grid=(n_blocks,),
            in_specs=[
                pl.BlockSpec((tm * SUBLANES, LANES), used),
                pl.BlockSpec((1, d, f2), lambda i, be, nv: (be[i], 0, 0)),
                pl.BlockSpec((1, 1, f2), lambda i, be, nv: (be[i], 0, 0)),
                pl.BlockSpec((1, f, d), lambda i, be, nv: (be[i], 0, 0)),
                pl.BlockSpec((1, 1, d), lambda i, be, nv: (be[i], 0, 0)),
            ],
            out_specs=pl.BlockSpec((tm * SUBLANES, LANES), lambda i, be, nv: (i, 0)),
            scratch_shapes=[pltpu.VMEM((d, f2), BF16), pltpu.VMEM((f, d), BF16)],
        ),
        out_shape=jax.ShapeDtypeStruct((n_blocks * tm * SUBLANES, LANES), F32),
        compiler_params=_params(1),
    )(block_e, nvalid, xs, w_gate_up, b_gate_up.reshape(ne, 1, f2), w_down, b_down.reshape(ne, 1, d))


def _combine_kernel(slot_ref, yb_ref, x2_ref, tg_ref, fg_ref, yp_ref, ys_ref, buf, sems, *, tc, n_prompt_tiles):
    i = pl.program_id(0)
    n_steps = pl.num_programs(0)

    def region(slot, k):
        return (slot * TOP_K + k) * tc

    def fetch(step, slot):
        for t in range(tc):
            for k in range(TOP_K):
                pltpu.make_async_copy(
                    _tile(yb_ref, slot_ref[step, t * TOP_K + k]), _tile(buf, region(slot, k) + t),
                    sems.at[slot]).start(priority=k % DMA_PRIORITIES)

    @pl.when(i == 0)
    def _():
        fetch(0, 0)

    @pl.when(i + 1 < n_steps)
    def _():
        fetch(i + 1, (i + 1) % 2)

    slot = i % 2
    for k in range(TOP_K):
        pltpu.make_async_copy(yb_ref.at[pl.ds(0, tc * SUBLANES)], buf.at[pl.ds(0, tc * SUBLANES)],
                              sems.at[slot]).wait()
    tg = tg_ref[...]
    y = x2_ref[...]
    for k in range(TOP_K):
        rows = _from_token_tiles(buf, tc, pl.multiple_of(region(slot, k) * SUBLANES, SUBLANES))
        y = y + tg[:, k:k + 1] * rows
    out = y * lax.rsqrt(jnp.mean(y * y, axis=-1, keepdims=True) + EPS) * fg_ref[...]

    @pl.when(i < n_prompt_tiles)
    def _():
        yp_ref[...] = out

    @pl.when(i >= n_prompt_tiles)
    def _():
        ys_ref[...] = out


def _combine(slot2d, yb, x2, tg, final_g, n_p, tc):
    n, d = x2.shape
    n_s = n - n_p
    assert n_p % tc == 0 and n_s % tc == 0
    npt = n_p // tc
    kern = functools.partial(_combine_kernel, tc=tc, n_prompt_tiles=npt)
    out_p, out_s = _two_source_specs(tc, d, npt)
    return pl.pallas_call(
        kern,
        grid_spec=pltpu.PrefetchScalarGridSpec(
            num_scalar_prefetch=1,
            grid=(n // tc,),
            in_specs=[
                pl.BlockSpec(memory_space=pl.ANY),
                pl.BlockSpec((tc, d), lambda i, s: (i, 0)),
                pl.BlockSpec((tc, LANES), lambda i, s: (i, 0)),
                pl.BlockSpec((1, d), lambda i, s: (0, 0)),
            ],
            out_specs=[out_p, out_s],
            scratch_shapes=[pltpu.VMEM((2 * TOP_K * tc * SUBLANES, LANES), F32), pltpu.SemaphoreType.DMA((2,))],
        ),
        out_shape=[jax.ShapeDtypeStruct((n_p, d), F32), jax.ShapeDtypeStruct((n_s, d), F32)],
        compiler_params=_params(1),
    )(slot2d, yb, x2, tg, final_g)


def _pad_lanes(v, width=LANES):
    v = v.reshape(1, -1)
    return jnp.pad(v, ((0, 0), (0, width - v.shape[1])))


def kernel(x_prompt, x_sample, state_conv, state_short_conv, state_delta, norm1_g, w_in, conv_dw_w,
           conv_dw_b, conv_ln_g, conv_ln_b, w_conv_out, short_conv_w, a_log, dt_bias, delta_norm_g,
           w_delta_out, w_merge_out, norm2_g, router_w, router_b, w_gate_up, b_gate_up, w_down, b_down,
           final_norm_g):
    depth = w_in.shape[0]
    assert depth == 1
    bp, tp, d = x_prompt.shape
    bs, ts, _ = x_sample.shape
    n_p, n_s = bp * tp, bs * ts
    n = n_p + n_s
    l = 0
    x_p = x_prompt.reshape(n_p, d)
    x_s = x_sample.reshape(n_s, d)

    o_ab = 2 * D_CONV + 4 * DN_WIDTH
    w = w_in[l]
    w_main = jnp.concatenate([w[:, :o_ab], w[:, o_ab + 2 * DN_HEADS:]], axis=1).astype(BF16)
    w_ab = _split_bf16(jnp.pad(w[:, o_ab:o_ab + 2 * DN_HEADS], ((0, 0), (0, LANES - 2 * DN_HEADS))))

    glu, qkv_pre, z, gb, siga, sigb = _inproj(
        x_p, x_s, norm1_g[l].reshape(1, d), w_main, w_ab, _pad_lanes(a_log[l]), _pad_lanes(dt_bias[l]), TOKEN_TILE)

    dw = (conv_dw_w[l], conv_dw_b[l].reshape(1, -1), conv_ln_g[l].reshape(1, -1), conv_ln_b[l].reshape(1, -1))
    pad_c = CONV_HALO - (CONV_WIDTH - 1)
    st_c_p = jnp.zeros((bp, CONV_HALO, D_CONV), F32)
    st_c_s = jnp.pad(state_conv[l], ((0, 0), (pad_c, 0), (0, 0)))
    cact_p, nconv_p = _conv_branch(glu, st_c_p, *dw, row0=0, bsz=bp, t_len=tp, bb=1, tt=SEQ_TILE)
    cact_s, nconv_s = _conv_branch(glu, st_c_s, *dw, row0=n_p, bsz=bs, t_len=ts, bb=8, tt=ts)

    pad_s = SHORT_HALO - (SHORT_WIDTH - 1)
    st_s_p = jnp.zeros((bp, SHORT_HALO, 3 * DN_WIDTH), F32)
    st_s_s = jnp.pad(state_short_conv[l], ((0, 0), (pad_s, 0), (0, 0)))
    s0_p = jnp.zeros((bp, DN_HEADS, DN_HEAD_DIM, DN_HEAD_DIM), F32)
    ng = delta_norm_g[l].reshape(1, -1)
    oact_p, nshort_p, s_p = _delta_prompt(qkv_pre, z, gb, st_s_p, s0_p, short_conv_w[l], ng,
                                          bsz=bp, t_len=tp, tt=SEQ_TILE)
    oact_s, nshort_s, s_s = _delta_sample(qkv_pre, z, gb, st_s_s, state_delta[l], short_conv_w[l], ng,
                                          row0=n_p, bsz=bs, seq_len=ts)

    rw = _split_bf16(jnp.pad(router_w[l], ((0, 0), (0, LANES - N_EXPERTS))))
    x2, ti, tg = _mix(x_p, x_s, cact_p, cact_s, oact_p, oact_s, siga, sigb, w_conv_out[l].astype(BF16),
                          w_delta_out[l].astype(BF16), w_merge_out[l].astype(BF16), norm2_g[l].reshape(1, d),
                          rw, _pad_lanes(router_b[l]), TOKEN_TILE)

    n_blocks = -(-(n * TOP_K) // MOE_TILE) + N_EXPERTS
    n_steps = n // TOKEN_TILE
    dest, table, block_e, nvalid = _route(ti[:, :TOP_K], MOE_TILE, n_blocks, n_steps)
    xs = _dispatch(x2, norm2_g[l].reshape(1, d), table, n_blocks * MOE_TILE, TOKEN_TILE)
    yb = _moe_ffn(xs, block_e, nvalid, w_gate_up[l], b_gate_up[l], w_down[l], b_down[l], MOE_TILE, n_blocks)
    y_p, y_s = _combine(dest.reshape(n_steps, -1), yb, x2, tg, final_norm_g.reshape(1, d), n_p, TOKEN_TILE)

    conv_p = nconv_p[:, pad_c:, :][None]
    conv_s = nconv_s[:, pad_c:, :][None]
    short_p = nshort_p[:, pad_s:, :][None]
    short_s = nshort_s[:, pad_s:, :][None]
    return (y_p.reshape(bp, tp, d), y_s.reshape(bs, ts, d), conv_p, short_p, s_p[None],
            conv_s, short_s, s_s[None])
```

```python
import functools

import jax
import jax.numpy as jnp
from jax import lax
from jax.experimental import pallas as pl
from jax.experimental.pallas import tpu as pltpu

F32 = jnp.float32
BF16 = jnp.bfloat16
EPS = 1e-6

LANES = 128
SUBLANES = 8
VMEM_LIMIT_BYTES = 56 * 1024 * 1024
DMA_PRIORITIES = 2

D_CONV = 512
CONV_WIDTH = 31
DN_HEADS = 4
DN_HEAD_DIM = 128
DN_WIDTH = DN_HEADS * DN_HEAD_DIM
SHORT_WIDTH = 4
N_EXPERTS = 32
TOP_K = 4
SWIGLU_LIMIT = 7.0
SWIGLU_ALPHA = 1.702

CHUNK = 128
CONV_HALO = 32
SHORT_HALO = 8

TOKEN_TILE = 256
SEQ_TILE = 256
MOE_TILE = 256


def _sigmoid(x):
    return 1.0 / (1.0 + jnp.exp(-x))


def _silu(x):
    return x * _sigmoid(x)


def _split_bf16(w):
    hi = w.astype(BF16)
    lo = (w - hi.astype(F32)).astype(BF16)
    return jnp.concatenate([hi, lo], axis=-1)


def _dot_split(x, w_split):
    n = w_split.shape[-1] // 2
    x_hi = x.astype(BF16)
    x_lo = (x - x_hi.astype(F32)).astype(BF16)
    r = jnp.dot(x_hi, w_split, preferred_element_type=F32)
    return r[:, :n] + r[:, n:] + jnp.dot(x_lo, w_split[:, :n], preferred_element_type=F32)


def _dot_delta(a, b, dims=(((1,), (0,)), ((), ()))):
    return lax.dot_general(a.astype(BF16), b.astype(BF16), dims, preferred_element_type=F32)


def _params(n_axes):
    return pltpu.CompilerParams(dimension_semantics=("arbitrary",) * n_axes, vmem_limit_bytes=VMEM_LIMIT_BYTES)


def _two_source_specs(tm, d, n_first_tiles):
    first = pl.BlockSpec((tm, d), lambda i, *_: (jnp.minimum(i, n_first_tiles - 1), 0))
    second = pl.BlockSpec((tm, d), lambda i, *_: (jnp.maximum(i - n_first_tiles, 0), 0))
    return first, second


def _inproj_kernel(xp_ref, xs_ref, g_ref, w_ref, wab_ref, alog_ref, dtb_ref,
                   glu_ref, qkv_ref, z_ref, gb_ref, sa_ref, sb_ref, *, n_prompt_tiles):
    x = jnp.where(pl.program_id(0) < n_prompt_tiles, xp_ref[...], xs_ref[...])
    h = x * lax.rsqrt(jnp.mean(x * x, axis=-1, keepdims=True) + EPS) * g_ref[...]
    hb = h.astype(BF16)

    def mm(lo, hi):
        return jnp.dot(hb, w_ref[:, lo:hi], preferred_element_type=F32)

    o_gate, o_qkv, o_z = D_CONV, 2 * D_CONV, 2 * D_CONV + 3 * DN_WIDTH
    o_ga = o_z + DN_WIDTH
    d = x.shape[-1]
    glu_ref[...] = mm(0, o_gate) * _sigmoid(mm(o_gate, o_qkv))
    qkv_ref[...] = mm(o_qkv, o_z)
    z_ref[...] = mm(o_z, o_ga)
    sa_ref[...] = _sigmoid(mm(o_ga, o_ga + d))
    sb_ref[...] = _sigmoid(mm(o_ga + d, o_ga + 2 * d))
    ab = _dot_split(h, wab_ref[...])
    xa = ab + dtb_ref[...]
    softplus = jnp.maximum(xa, 0.0) + jnp.log(1.0 + jnp.exp(-jnp.abs(xa)))
    g = -jnp.exp(alog_ref[...]) * softplus
    lane = lax.broadcasted_iota(jnp.int32, ab.shape, 1)
    gb_ref[...] = jnp.where(lane < DN_HEADS, g, _sigmoid(ab))


def _inproj(x_p, x_s, norm_g, w_main, w_ab, alog, dtb, tm):
    (n_p, d), n_s = x_p.shape, x_s.shape[0]
    assert n_p % tm == 0 and n_s % tm == 0
    n = n_p + n_s
    wcols = w_main.shape[1]
    row = lambda i: (i, 0)
    const = lambda i: (0, 0)
    outs = [(D_CONV, F32), (3 * DN_WIDTH, F32), (DN_WIDTH, F32), (LANES, F32), (d, F32), (d, F32)]
    return pl.pallas_call(
        functools.partial(_inproj_kernel, n_prompt_tiles=n_p // tm),
        grid=(n // tm,),
        in_specs=[
            *_two_source_specs(tm, d, n_p // tm),
            pl.BlockSpec((1, d), const),
            pl.BlockSpec((d, wcols), const),
            pl.BlockSpec((d, 2 * LANES), const),
            pl.BlockSpec((1, LANES), const),
            pl.BlockSpec((1, LANES), const),
        ],
        out_specs=[pl.BlockSpec((tm, c), row) for c, _ in outs],
        out_shape=[jax.ShapeDtypeStruct((n, c), dt) for c, dt in outs],
        compiler_params=_params(1),
    )(x_p, x_s, norm_g, w_main, w_ab, alog, dtb)


def _conv_kernel(glu_ref, st_ref, w_ref, b_ref, lg_ref, lb_ref, out_ref, nst_ref, e_ref, sh_ref, *, bb, tt, rows):
    t = pl.program_id(1)

    @pl.when(t == 0)
    def _():
        e_ref[:, 0:CONV_HALO, :] = st_ref[...]

    for b in range(bb):
        e_ref[b, CONV_HALO:CONV_HALO + tt, :] = glu_ref[b * tt:(b + 1) * tt, :]
    off = CONV_HALO - (CONV_WIDTH - 1)
    span = sh_ref.shape[1]
    for b in range(bb):
        for s in range(1, SUBLANES):
            sh_ref[s - 1] = e_ref[b, s:s + span, :]
        for c in range(tt // rows):
            r0 = c * rows
            acc = jnp.zeros((rows, D_CONV), F32) + b_ref[...]
            for j in range(CONV_WIDTH):
                q, s = divmod(j + off, SUBLANES)
                lo = r0 + q * SUBLANES
                src = e_ref[b, lo:lo + rows, :] if s == 0 else sh_ref[s - 1, lo:lo + rows, :]
                acc = acc + w_ref[j:j + 1, :] * src
            mu = jnp.mean(acc, axis=-1, keepdims=True)
            xc = acc - mu
            var = jnp.mean(xc * xc, axis=-1, keepdims=True)
            y = xc * lax.rsqrt(var + EPS) * lg_ref[...] + lb_ref[...]
            out_ref[b * tt + r0:b * tt + r0 + rows, :] = _silu(y).astype(out_ref.dtype)
    tail = e_ref[:, tt:tt + CONV_HALO, :]
    nst_ref[...] = tail
    e_ref[:, 0:CONV_HALO, :] = tail


def _conv_branch(glu, state32, dw_w, dw_b, ln_g, ln_b, *, row0, bsz, t_len, bb, tt):
    c = glu.shape[1]
    assert bsz % bb == 0 and t_len % tt == 0 and row0 % (bb * tt) == 0
    nt = t_len // tt
    blk0 = row0 // (bb * tt)
    rows = min(tt, 32)
    kern = functools.partial(_conv_kernel, bb=bb, tt=tt, rows=rows)
    const = lambda b, t: (0, 0)
    return pl.pallas_call(
        kern,
        grid=(bsz // bb, nt),
        in_specs=[
            pl.BlockSpec((bb * tt, c), lambda b, t: (blk0 + b * nt + t, 0)),
            pl.BlockSpec((bb, CONV_HALO, c), lambda b, t: (b, 0, 0)),
            pl.BlockSpec((CONV_WIDTH, c), const),
            pl.BlockSpec((1, c), const),
            pl.BlockSpec((1, c), const),
            pl.BlockSpec((1, c), const),
        ],
        out_specs=[
            pl.BlockSpec((bb * tt, c), lambda b, t: (b * nt + t, 0)),
            pl.BlockSpec((bb, CONV_HALO, c), lambda b, t: (b, 0, 0)),
        ],
        out_shape=[
            jax.ShapeDtypeStruct((bsz * t_len, c), BF16),
            jax.ShapeDtypeStruct((bsz, CONV_HALO, c), F32),
        ],
        scratch_shapes=[pltpu.VMEM((bb, CONV_HALO + tt, c), F32),
                        pltpu.VMEM((SUBLANES - 1, tt + CONV_HALO - SUBLANES, c), F32)],
        compiler_params=_params(2),
    )(glu, state32, dw_w, dw_b, ln_g, ln_b)


def _chunk_masks(seq_len):
    i = lax.broadcasted_iota(jnp.int32, (CHUNK, CHUNK), 0)
    j = lax.broadcasted_iota(jnp.int32, (CHUNK, CHUNK), 1)
    same = (i // seq_len) == (j // seq_len)
    incl = same & (i >= j)
    strict = same & (i > j)
    last = j == (i // seq_len) * seq_len + (seq_len - 1)
    levels = []
    blk = 1
    while blk < seq_len:
        levels.append(((i // (2 * blk)) == (j // (2 * blk))) & (((i // blk) % 2) == 1) & (((j // blk) % 2) == 0))
        blk *= 2
    eye = i == j
    return incl, strict, last, levels, eye


def _lane_col(x, lane):
    return jnp.broadcast_to(x[:, lane:lane + 1], (x.shape[0], LANES))


def _l2norm(x):
    return x * lax.rsqrt(jnp.sum(x * x, axis=-1, keepdims=True) + EPS)


def _select_sum(mask01, x):
    hi = x.astype(BF16)
    r1 = x - hi.astype(F32)
    mid = r1.astype(BF16)
    lo = (r1 - mid.astype(F32)).astype(BF16)
    w = x.shape[1]
    parts = jnp.dot(mask01, jnp.concatenate([hi, mid, lo], axis=1), preferred_element_type=F32)
    return parts[:, :w] + parts[:, w:2 * w] + parts[:, 2 * w:]


def _chunks_prepare(qkvs, gbts, masks, seq_len):
    incl, strict, last, levels, eye = masks
    nt = (((1,), (1,)), ((), ()))
    lower01 = jnp.where(incl, 1.0, 0.0).astype(BF16)
    probs = []
    for qkv, gbt in zip(qkvs, gbts):
        gc = _select_sum(lower01, gbt)
        gct = gc.T
        if seq_len == CHUNK:
            glast = jnp.broadcast_to(gc[CHUNK - 1:CHUNK, :], gc.shape)
        else:
            glast = _select_sum(jnp.where(last, 1.0, 0.0).astype(BF16), gc)
        for h in range(DN_HEADS):
            q = _l2norm(qkv[:, h * DN_HEAD_DIM:(h + 1) * DN_HEAD_DIM]) * (DN_HEAD_DIM ** -0.5)
            k = _l2norm(qkv[:, DN_WIDTH + h * DN_HEAD_DIM:DN_WIDTH + (h + 1) * DN_HEAD_DIM])
            v = qkv[:, 2 * DN_WIDTH + h * DN_HEAD_DIM:2 * DN_WIDTH + (h + 1) * DN_HEAD_DIM]
            gcol = _lane_col(gc, h)
            grow = jnp.broadcast_to(gct[h:h + 1, :], (CHUNK, CHUNK))
            beta = _lane_col(gbt, DN_HEADS + h)
            gl = _lane_col(glast, h)
            decay = jnp.exp(jnp.where(incl, gcol - grow, -jnp.inf))
            egc = jnp.exp(gcol)
            kb = k * beta
            probs.append(dict(q=q, k=k, kb=kb, decay=decay, rhs=jnp.concatenate([v * beta, kb * egc], axis=1),
                              qexp=q * egc, kdec=k * jnp.exp(gl - gcol), egl=jnp.exp(gl)))
    for p in probs:
        p['a'] = jnp.where(strict, _dot_delta(p['kb'], p['k'], nt) * p['decay'], 0.0)
        p['scores'] = _dot_delta(p['q'], p['k'], nt) * p['decay']
    for p in probs:
        p['x'] = jnp.where(eye, 1.0, 0.0) - jnp.where(levels[0], p['a'], 0.0)
    for m in levels[1:]:
        for p in probs:
            p['xa'] = _dot_delta(p['x'], jnp.where(m, p['a'], 0.0))
        for p in probs:
            p['x'] = p['x'] - _dot_delta(p['xa'], p['x'])
    out = []
    for c in range(len(qkvs)):
        heads = []
        for h in range(DN_HEADS):
            p = probs[c * DN_HEADS + h]
            sol = _dot_delta(p['x'], p['rhs'])
            heads.append((sol[:, :DN_HEAD_DIM], sol[:, DN_HEAD_DIM:], p['scores'], p['qexp'], p['kdec'], p['egl']))
        out.append(heads)
    return out


def _gated_out_norm(o, z, ng):
    y = o * lax.rsqrt(jnp.mean(o * o, axis=-1, keepdims=True) + EPS) * ng
    return y * _silu(z)


def _short_conv(e_ref, w_ref, tt):
    off = SHORT_HALO - (SHORT_WIDTH - 1)
    acc = w_ref[0:1, :] * e_ref[off:off + tt, :]
    for j in range(1, SHORT_WIDTH):
        acc = acc + w_ref[j:j + 1, :] * e_ref[off + j:off + j + tt, :]
    return _silu(acc)


def _delta_prompt_kernel(qkv_ref, z_ref, gb_ref, st_ref, s0_ref, w_ref, ng_ref,
                         o_ref, nst_ref, sout_ref, e_ref, s_ref, *, tt):
    t = pl.program_id(1)

    @pl.when(t == 0)
    def _():
        e_ref[0:SHORT_HALO, :] = st_ref[0]
        s_ref[...] = s0_ref[0]

    e_ref[SHORT_HALO:SHORT_HALO + tt, :] = qkv_ref[...]
    qkv = _short_conv(e_ref, w_ref, tt)
    tail = e_ref[tt:tt + SHORT_HALO, :]
    nst_ref[0] = tail
    e_ref[0:SHORT_HALO, :] = tail

    masks = _chunk_masks(CHUNK)
    tn = (((0,), (0,)), ((), ()))
    n_chunks = tt // CHUNK
    prep = _chunks_prepare([qkv[c * CHUNK:(c + 1) * CHUNK, :] for c in range(n_chunks)],
                           [gb_ref[c * CHUNK:(c + 1) * CHUNK, :] for c in range(n_chunks)], masks, CHUNK)
    heads = range(DN_HEADS)
    s = [s_ref[h] for h in heads]
    for c in range(n_chunks):
        r0 = c * CHUNK
        value, kcum, scores, qexp, kdec, egl = zip(*prep[c])
        both = [_dot_delta(jnp.concatenate([kcum[h], qexp[h]], axis=0), s[h]) for h in heads]
        v_new = [value[h] - both[h][:CHUNK] for h in heads]
        o = [both[h][CHUNK:] + _dot_delta(scores[h], v_new[h]) for h in heads]
        s = [s[h] * egl[h][0:1, :] + _dot_delta(kdec[h], v_new[h], tn) for h in heads]
        for h in heads:
            lanes = slice(h * DN_HEAD_DIM, (h + 1) * DN_HEAD_DIM)
            o_ref[r0:r0 + CHUNK, lanes] = _gated_out_norm(
                o[h], z_ref[r0:r0 + CHUNK, lanes], ng_ref[...]).astype(o_ref.dtype)
    for h in heads:
        s_ref[h] = s[h]
        sout_ref[0, h] = s[h]


def _delta_prompt(qkv_pre, z, gb, state8, s0, conv_w, norm_g, *, bsz, t_len, tt):
    n = bsz * t_len
    assert t_len % tt == 0 and tt % CHUNK == 0
    nt = t_len // tt
    kern = functools.partial(_delta_prompt_kernel, tt=tt)
    tile = lambda b, t: (b * nt + t, 0)
    per_b = lambda b, t: (b, 0, 0)
    per_b4 = lambda b, t: (b, 0, 0, 0)
    return pl.pallas_call(
        kern,
        grid=(bsz, nt),
        in_specs=[
            pl.BlockSpec((tt, 3 * DN_WIDTH), tile),
            pl.BlockSpec((tt, DN_WIDTH), tile),
            pl.BlockSpec((tt, LANES), tile),
            pl.BlockSpec((1, SHORT_HALO, 3 * DN_WIDTH), per_b),
            pl.BlockSpec((1, DN_HEADS, DN_HEAD_DIM, DN_HEAD_DIM), per_b4),
            pl.BlockSpec((SHORT_WIDTH, 3 * DN_WIDTH), lambda b, t: (0, 0)),
            pl.BlockSpec((1, DN_HEAD_DIM), lambda b, t: (0, 0)),
        ],
        out_specs=[
            pl.BlockSpec((tt, DN_WIDTH), tile),
            pl.BlockSpec((1, SHORT_HALO, 3 * DN_WIDTH), per_b),
            pl.BlockSpec((1, DN_HEADS, DN_HEAD_DIM, DN_HEAD_DIM), per_b4),
        ],
        out_shape=[
            jax.ShapeDtypeStruct((n, DN_WIDTH), BF16),
            jax.ShapeDtypeStruct((bsz, SHORT_HALO, 3 * DN_WIDTH), F32),
            jax.ShapeDtypeStruct((bsz, DN_HEADS, DN_HEAD_DIM, DN_HEAD_DIM), F32),
        ],
        scratch_shapes=[
            pltpu.VMEM((SHORT_HALO + tt, 3 * DN_WIDTH), F32),
            pltpu.VMEM((DN_HEADS, DN_HEAD_DIM, DN_HEAD_DIM), F32),
        ],
        compiler_params=_params(2),
    )(qkv_pre, z, gb, state8, s0, conv_w, norm_g)


def _delta_sample_kernel(qkv_ref, z_ref, gb_ref, st_ref, s0_ref, w_ref, ng_ref, o_ref, nst_ref, sout_ref,
                         e_ref, *, nseq, seq_len):
    qkv_rows = []
    for b in range(nseq):
        e_ref[0:SHORT_HALO, :] = st_ref[b]
        e_ref[SHORT_HALO:SHORT_HALO + seq_len, :] = qkv_ref[b * seq_len:(b + 1) * seq_len, :]
        qkv_rows.append(_short_conv(e_ref, w_ref, seq_len))
        nst_ref[b] = e_ref[seq_len:seq_len + SHORT_HALO, :]
    qkv = jnp.concatenate(qkv_rows, axis=0)
    masks = _chunk_masks(seq_len)
    prep = _chunks_prepare([qkv], [gb_ref[...]], masks, seq_len)[0]
    tn = (((0,), (0,)), ((), ()))
    rows = [slice(b * seq_len, (b + 1) * seq_len) for b in range(nseq)]
    both = [[_dot_delta(jnp.concatenate([prep[h][1][r], prep[h][3][r]], axis=0), s0_ref[b, h])
             for b, r in enumerate(rows)] for h in range(DN_HEADS)]
    v_new = [[prep[h][0][r] - both[h][b][:seq_len] for b, r in enumerate(rows)] for h in range(DN_HEADS)]
    for h in range(DN_HEADS):
        kdec, egl = prep[h][4], prep[h][5]
        for b, r in enumerate(rows):
            sout_ref[b, h] = (s0_ref[b, h] * egl[b * seq_len:b * seq_len + 1, :]
                              + _dot_delta(kdec[r], v_new[h][b], tn))
    for h in range(DN_HEADS):
        o = (jnp.concatenate([both[h][b][seq_len:] for b in range(nseq)], axis=0)
             + _dot_delta(prep[h][2], jnp.concatenate(v_new[h], axis=0)))
        lanes = slice(h * DN_HEAD_DIM, (h + 1) * DN_HEAD_DIM)
        o_ref[:, lanes] = _gated_out_norm(o, z_ref[:, lanes], ng_ref[...]).astype(o_ref.dtype)


def _delta_sample(qkv_pre, z, gb, state8, s0, conv_w, norm_g, *, row0, bsz, seq_len):
    n = bsz * seq_len
    assert CHUNK % seq_len == 0
    nseq = CHUNK // seq_len
    assert bsz % nseq == 0 and row0 % CHUNK == 0
    blk0 = row0 // CHUNK
    kern = functools.partial(_delta_sample_kernel, nseq=nseq, seq_len=seq_len)
    tile = lambda i: (blk0 + i, 0)
    blk3 = lambda i: (i, 0, 0)
    blk4 = lambda i: (i, 0, 0, 0)
    return pl.pallas_call(
        kern,
        grid=(bsz // nseq,),
        in_specs=[
            pl.BlockSpec((CHUNK, 3 * DN_WIDTH), tile),
            pl.BlockSpec((CHUNK, DN_WIDTH), tile),
            pl.BlockSpec((CHUNK, LANES), tile),
            pl.BlockSpec((nseq, SHORT_HALO, 3 * DN_WIDTH), blk3),
            pl.BlockSpec((nseq, DN_HEADS, DN_HEAD_DIM, DN_HEAD_DIM), blk4),
            pl.BlockSpec((SHORT_WIDTH, 3 * DN_WIDTH), lambda i: (0, 0)),
            pl.BlockSpec((1, DN_HEAD_DIM), lambda i: (0, 0)),
        ],
        out_specs=[
            pl.BlockSpec((CHUNK, DN_WIDTH), lambda i: (i, 0)),
            pl.BlockSpec((nseq, SHORT_HALO, 3 * DN_WIDTH), blk3),
            pl.BlockSpec((nseq, DN_HEADS, DN_HEAD_DIM, DN_HEAD_DIM), blk4),
        ],
        out_shape=[
            jax.ShapeDtypeStruct((n, DN_WIDTH), BF16),
            jax.ShapeDtypeStruct((bsz, SHORT_HALO, 3 * DN_WIDTH), F32),
            jax.ShapeDtypeStruct((bsz, DN_HEADS, DN_HEAD_DIM, DN_HEAD_DIM), F32),
        ],
        scratch_shapes=[pltpu.VMEM((SHORT_HALO + seq_len, 3 * DN_WIDTH), F32)],
        compiler_params=_params(1),
    )(qkv_pre, z, gb, state8, s0, conv_w, norm_g)


def _mix_kernel(xp_ref, xs_ref, cap_ref, cas_ref, oap_ref, oas_ref, sa_ref, sb_ref, wc_ref, wd_ref, wm_ref,
                g2_ref, rw_ref, rb_ref, x2_ref, tr_ref, *, n_prompt_tiles):
    is_prompt = pl.program_id(0) < n_prompt_tiles
    x = jnp.where(is_prompt, xp_ref[...], xs_ref[...])
    ca = jnp.where(is_prompt, cap_ref[...], cas_ref[...])
    oa = jnp.where(is_prompt, oap_ref[...], oas_ref[...])
    ya = jnp.dot(ca, wc_ref[...], preferred_element_type=F32)
    yb = jnp.dot(oa, wd_ref[...], preferred_element_type=F32)
    mixed = sa_ref[...] * ya + sb_ref[...] * yb
    x2 = x + jnp.dot(mixed.astype(BF16), wm_ref[...], preferred_element_type=F32)
    x2_ref[...] = x2
    h2 = x2 * lax.rsqrt(jnp.mean(x2 * x2, axis=-1, keepdims=True) + EPS) * g2_ref[...]
    logits = _dot_split(h2, rw_ref[...]) + rb_ref[...]
    lt = logits.T[:N_EXPERTS, :]
    tokens = lt.shape[1]
    row = lax.broadcasted_iota(jnp.int32, lt.shape, 0).astype(F32)
    top_vals, top_idx = [], []
    for k in range(TOP_K):
        m = jnp.max(lt, axis=0, keepdims=True)
        idx = jnp.min(jnp.where(lt == m, row, float(N_EXPERTS)), axis=0, keepdims=True)
        top_vals.append(m)
        top_idx.append(idx)
        lt = jnp.where(row == idx, -jnp.inf, lt)
    exps = [jnp.exp(v - top_vals[0]) for v in top_vals]
    den = exps[0] + exps[1] + exps[2] + exps[3]
    slot = lax.broadcasted_iota(jnp.int32, (2 * TOP_K, tokens), 0)
    packed = jnp.zeros((2 * TOP_K, tokens), F32)
    for k in range(TOP_K):
        packed = jnp.where(slot == k, top_idx[k], packed)
        packed = jnp.where(slot == TOP_K + k, exps[k] / den, packed)
    packed = jnp.concatenate([packed, jnp.zeros((LANES - 2 * TOP_K, tokens), F32)], axis=0)
    tr_ref[...] = packed.T


def _mix(x_p, x_s, cact_p, cact_s, oact_p, oact_s, siga, sigb, w_conv_out, w_delta_out, w_merge_out, norm2_g,
         router_w, router_b, tm):
    (n_p, d), n_s = x_p.shape, x_s.shape[0]
    n = n_p + n_s
    row = lambda i: (i, 0)
    const = lambda i: (0, 0)
    return pl.pallas_call(
        functools.partial(_mix_kernel, n_prompt_tiles=n_p // tm),
        grid=(n // tm,),
        in_specs=[
            *_two_source_specs(tm, d, n_p // tm),
            *_two_source_specs(tm, D_CONV, n_p // tm),
            *_two_source_specs(tm, DN_WIDTH, n_p // tm),
            pl.BlockSpec((tm, d), row),
            pl.BlockSpec((tm, d), row),
            pl.BlockSpec((D_CONV, d), const),
            pl.BlockSpec((DN_WIDTH, d), const),
            pl.BlockSpec((d, d), const),
            pl.BlockSpec((1, d), const),
            pl.BlockSpec((d, 2 * LANES), const),
            pl.BlockSpec((1, LANES), const),
        ],
        out_specs=[
            pl.BlockSpec((tm, d), row),
            pl.BlockSpec((tm, LANES), row),
        ],
        out_shape=[
            jax.ShapeDtypeStruct((n, d), F32),
            jax.ShapeDtypeStruct((n, LANES), F32),
        ],
        compiler_params=_params(1),
    )(x_p, x_s, cact_p, cact_s, oact_p, oact_s, siga, sigb, w_conv_out, w_delta_out, w_merge_out, norm2_g,
      router_w, router_b)


def _fill_rows_per_step(n_fill, n_steps):
    per_step = SUBLANES
    while per_step * n_steps < n_fill:
        per_step *= 2
    assert n_fill % per_step == 0
    return per_step


def _route(top_idx, tm, n_blocks, n_steps):
    n = top_idx.shape[0]
    n_fill = n_blocks * tm - n * TOP_K
    assert n_fill == N_EXPERTS * tm
    flat_e = top_idx.reshape(-1)
    experts = jnp.arange(N_EXPERTS, dtype=jnp.int32)
    onehot = (flat_e[:, None] == experts[None, :]).astype(jnp.int32)
    csum = jnp.cumsum(onehot, axis=0)
    rank = jnp.sum(csum * onehot, axis=1) - 1
    counts = csum[-1]
    padded = (counts + tm - 1) // tm * tm
    pad_end = jnp.cumsum(padded)
    pad_start = pad_end - padded
    dest = jnp.sum(onehot * pad_start[None, :], axis=1) + rank
    nvalid = (pad_end[-1] // tm).astype(jnp.int32)
    blk = jnp.arange(n_blocks, dtype=jnp.int32)
    owner = jnp.sum((pad_end[None, :] <= (blk * tm)[:, None]).astype(jnp.int32), axis=1)
    block_e = jnp.minimum(owner, N_EXPERTS - 1)
    block_e = jnp.where(blk < nvalid, block_e, jnp.sum(jnp.where(blk == nvalid - 1, block_e, 0)))
    present = counts > 0
    later = present[None, :] & (experts[None, :] > experts[:, None])
    next_present = jnp.min(jnp.where(later, experts[None, :], N_EXPERTS), axis=1)
    next_present = jnp.where(next_present == N_EXPERTS, -1, next_present)
    parity = (jnp.cumsum(present.astype(jnp.int32)) - 1) % 2
    of_block = (block_e[:, None] == experts[None, :]).astype(jnp.int32)
    sched = jnp.stack([block_e, jnp.sum(of_block * next_present[None, :], axis=1),
                       jnp.sum(of_block * parity[None, :], axis=1)])
    n_pad = padded - counts
    spill = tm - n_pad
    spill_start = pad_end[-1] + jnp.cumsum(spill) - spill
    j = jnp.arange(tm, dtype=jnp.int32)[None, :]
    fill = jnp.where(j < n_pad[:, None], (pad_start + counts)[:, None] + j, (spill_start - n_pad)[:, None] + j)
    fill_step = _fill_rows_per_step(n_fill, n_steps)
    fill = jnp.pad(fill.reshape(-1, fill_step), ((0, n_steps - n_fill // fill_step), (0, 0)))
    table = jnp.concatenate([dest.reshape(n_steps, -1), fill], axis=1)
    return dest.reshape(n, TOP_K), table, sched, nvalid.reshape(1)


def _to_token_tiles(x, ref, row0=0):
    t = x.shape[0]
    for s in range(SUBLANES):
        ref[pl.ds(row0 + s, t, stride=SUBLANES), :] = x[:, s * LANES:(s + 1) * LANES]


def _from_token_tiles(ref, t, row0=0):
    return jnp.concatenate([ref[pl.ds(row0 + s, t, stride=SUBLANES), :] for s in range(SUBLANES)], axis=1)


def _tile(ref, row):
    return ref.at[pl.ds(pl.multiple_of(row * SUBLANES, SUBLANES), SUBLANES)]


def _dispatch_kernel(tab_ref, x2_ref, g2_ref, xs_ref, buf, sems, *, tokens, fill_step, n_fill_steps):
    i = pl.program_id(0)
    last = pl.num_programs(0) - 1
    slot = i % 2
    base = slot * tokens
    fill_sem = 2

    x2 = x2_ref[...]
    h2 = x2 * lax.rsqrt(jnp.mean(x2 * x2, axis=-1, keepdims=True) + EPS) * g2_ref[...]
    _to_token_tiles(h2, buf, pl.multiple_of(base * SUBLANES, SUBLANES))

    def wait_tiles(sem_idx, count):
        while count > 0:
            rows = min(count, tokens) * SUBLANES
            pltpu.make_async_copy(buf.at[pl.ds(0, rows)], xs_ref.at[pl.ds(0, rows)], sems.at[sem_idx]).wait()
            count -= min(count, tokens)

    for t in range(tokens):
        for k in range(TOP_K):
            pltpu.make_async_copy(_tile(buf, base + t), _tile(xs_ref, tab_ref[i, t * TOP_K + k]),
                                  sems.at[slot]).start(priority=k % DMA_PRIORITIES)

    @pl.when(i < n_fill_steps)
    def _():
        for p in range(fill_step):
            pltpu.make_async_copy(_tile(buf, base), _tile(xs_ref, tab_ref[i, tokens * TOP_K + p]),
                                  sems.at[fill_sem]).start(priority=p % DMA_PRIORITIES)
        wait_tiles(fill_sem, fill_step)

    @pl.when(i > 0)
    def _():
        wait_tiles(1 - slot, tokens * TOP_K)

    @pl.when(i == last)
    def _():
        wait_tiles(slot, tokens * TOP_K)


def _dispatch(x2, norm_g, table, n_rows, tokens):
    n, d = x2.shape
    assert d == SUBLANES * LANES
    fill_step = table.shape[1] - tokens * TOP_K
    n_fill_steps = (n_rows - n * TOP_K) // fill_step
    return pl.pallas_call(
        functools.partial(_dispatch_kernel, tokens=tokens, fill_step=fill_step, n_fill_steps=n_fill_steps),
        grid_spec=pltpu.PrefetchScalarGridSpec(
            num_scalar_prefetch=1,
            grid=(n // tokens,),
            in_specs=[pl.BlockSpec((tokens, d), lambda i, tab: (i, 0)),
                      pl.BlockSpec((1, d), lambda i, tab: (0, 0))],
            out_specs=pl.BlockSpec(memory_space=pl.ANY),
            scratch_shapes=[pltpu.VMEM((2 * tokens * SUBLANES, LANES), F32), pltpu.SemaphoreType.DMA((3,))],
        ),
        out_shape=jax.ShapeDtypeStruct((n_rows * SUBLANES, LANES), F32),
        compiler_params=_params(1),
    )(table, x2, norm_g)


def _moe_ffn_kernel(sched_ref, nv_ref, xs_ref, wgu_hbm, bgu_ref, wd_hbm, bd_ref, out_ref,
                    wgu_f32, wd_f32, wgu_bf, wd_bf, sems):
    i = pl.program_id(0)
    expert, next_expert, half = sched_ref[0, i], sched_ref[1, i], sched_ref[2, i]

    def weight_copies(e, h):
        return (pltpu.make_async_copy(wgu_hbm.at[e], wgu_f32.at[h], sems.at[h]),
                pltpu.make_async_copy(wd_hbm.at[e], wd_f32.at[h], sems.at[h]))

    @pl.when(i == 0)
    def _():
        for c in weight_copies(expert, half):
            c.start()

    @pl.when(jnp.logical_or(i == 0, expert != sched_ref[0, jnp.maximum(i - 1, 0)]))
    def _():
        for c in weight_copies(expert, half):
            c.wait()
        wgu_bf[...] = wgu_f32[half].astype(BF16)
        wd_bf[...] = wd_f32[half].astype(BF16)

        @pl.when(next_expert >= 0)
        def _():
            for c in weight_copies(next_expert, 1 - half):
                c.start()

    @pl.when(i < nv_ref[0])
    def _():
        f = wd_bf.shape[0]
        tm = xs_ref.shape[0] // SUBLANES
        x = _from_token_tiles(xs_ref, tm).astype(BF16)
        gu = jnp.dot(x, wgu_bf[...], preferred_element_type=F32) + bgu_ref[0]
        gt = jnp.minimum(gu[:, :f], SWIGLU_LIMIT)
        up = jnp.clip(gu[:, f:], -SWIGLU_LIMIT, SWIGLU_LIMIT)
        act = (up + 1.0) * (gt * _sigmoid(SWIGLU_ALPHA * gt))
        y = jnp.dot(act.astype(BF16), wd_bf[...], preferred_element_type=F32) + bd_ref[0]
        _to_token_tiles(y, out_ref)

    @pl.when(i >= nv_ref[0])
    def _():
        out_ref[...] = jnp.zeros(out_ref.shape, out_ref.dtype)


def _moe_ffn(xs, sched, nvalid, w_gate_up, b_gate_up, w_down, b_down, tm, n_blocks):
    ne, d, f2 = w_gate_up.shape
    f = f2 // 2
    used = lambda i, sc, nv: (jnp.minimum(i, nv[0] - 1), 0)
    return pl.pallas_call(
        _moe_ffn_kernel,
        grid_spec=pltpu.PrefetchScalarGridSpec(
            num_scalar_prefetch=2,
            grid=(n_blocks,),
            in_specs=[
                pl.BlockSpec((tm * SUBLANES, LANES), used),
                pl.BlockSpec(memory_space=pl.ANY),
                pl.BlockSpec((1, 1, f2), lambda i, sc, nv: (sc[0, i], 0, 0)),
                pl.BlockSpec(memory_space=pl.ANY),
                pl.BlockSpec((1, 1, d), lambda i, sc, nv: (sc[0, i], 0, 0)),
            ],
            out_specs=pl.BlockSpec((tm * SUBLANES, LANES), lambda i, sc, nv: (i, 0)),
            scratch_shapes=[pltpu.VMEM((2, d, f2), F32), pltpu.VMEM((2, f, d), F32),
                            pltpu.VMEM((d, f2), BF16), pltpu.VMEM((f, d), BF16), pltpu.SemaphoreType.DMA((2,))],
        ),
        out_shape=jax.ShapeDtypeStruct((n_blocks * tm * SUBLANES, LANES), F32),
        compiler_params=_params(1),
    )(sched, nvalid, xs, w_gate_up, b_gate_up.reshape(ne, 1, f2), w_down, b_down.reshape(ne, 1, d))


def _combine_kernel(slot_ref, yb_ref, x2_ref, tg_ref, fg_ref, yp_ref, ys_ref, buf, sems, *, tc, n_prompt_tiles):
    i = pl.program_id(0)
    n_steps = pl.num_programs(0)

    def region(slot, k):
        return (slot * TOP_K + k) * tc

    def fetch(step, slot):
        for t in range(tc):
            for k in range(TOP_K):
                pltpu.make_async_copy(
                    _tile(yb_ref, slot_ref[step, t * TOP_K + k]), _tile(buf, region(slot, k) + t),
                    sems.at[slot]).start(priority=k % DMA_PRIORITIES)

    @pl.when(i == 0)
    def _():
        fetch(0, 0)

    @pl.when(i + 1 < n_steps)
    def _():
        fetch(i + 1, (i + 1) % 2)

    slot = i % 2
    for k in range(TOP_K):
        pltpu.make_async_copy(yb_ref.at[pl.ds(0, tc * SUBLANES)], buf.at[pl.ds(0, tc * SUBLANES)],
                              sems.at[slot]).wait()
    tg = tg_ref[...]
    y = x2_ref[...]
    for k in range(TOP_K):
        rows = _from_token_tiles(buf, tc, pl.multiple_of(region(slot, k) * SUBLANES, SUBLANES))
        y = y + tg[:, TOP_K + k:TOP_K + k + 1] * rows
    out = y * lax.rsqrt(jnp.mean(y * y, axis=-1, keepdims=True) + EPS) * fg_ref[...]

    @pl.when(i < n_prompt_tiles)
    def _():
        yp_ref[...] = out

    @pl.when(i >= n_prompt_tiles)
    def _():
        ys_ref[...] = out


def _combine(slot2d, yb, x2, tg, final_g, n_p, tc):
    n, d = x2.shape
    n_s = n - n_p
    assert n_p % tc == 0 and n_s % tc == 0
    npt = n_p // tc
    kern = functools.partial(_combine_kernel, tc=tc, n_prompt_tiles=npt)
    out_p, out_s = _two_source_specs(tc, d, npt)
    return pl.pallas_call(
        kern,
        grid_spec=pltpu.PrefetchScalarGridSpec(
            num_scalar_prefetch=1,
            grid=(n // tc,),
            in_specs=[
                pl.BlockSpec(memory_space=pl.ANY),
                pl.BlockSpec((tc, d), lambda i, s: (i, 0)),
                pl.BlockSpec((tc, LANES), lambda i, s: (i, 0)),
                pl.BlockSpec((1, d), lambda i, s: (0, 0)),
            ],
            out_specs=[out_p, out_s],
            scratch_shapes=[pltpu.VMEM((2 * TOP_K * tc * SUBLANES, LANES), F32), pltpu.SemaphoreType.DMA((2,))],
        ),
        out_shape=[jax.ShapeDtypeStruct((n_p, d), F32), jax.ShapeDtypeStruct((n_s, d), F32)],
        compiler_params=_params(1),
    )(slot2d, yb, x2, tg, final_g)


def _pad_lanes(v, width=LANES):
    v = v.reshape(1, -1)
    return jnp.pad(v, ((0, 0), (0, width - v.shape[1])))


def kernel(x_prompt, x_sample, state_conv, state_short_conv, state_delta, norm1_g, w_in, conv_dw_w,
           conv_dw_b, conv_ln_g, conv_ln_b, w_conv_out, short_conv_w, a_log, dt_bias, delta_norm_g,
           w_delta_out, w_merge_out, norm2_g, router_w, router_b, w_gate_up, b_gate_up, w_down, b_down,
           final_norm_g):
    depth = w_in.shape[0]
    assert depth == 1
    bp, tp, d = x_prompt.shape
    bs, ts, _ = x_sample.shape
    n_p, n_s = bp * tp, bs * ts
    n = n_p + n_s
    l = 0
    x_p = x_prompt.reshape(n_p, d)
    x_s = x_sample.reshape(n_s, d)

    o_ab = 2 * D_CONV + 4 * DN_WIDTH
    w = w_in[l]
    w_main = jnp.concatenate([w[:, :o_ab], w[:, o_ab + 2 * DN_HEADS:]], axis=1).astype(BF16)
    w_ab = _split_bf16(jnp.pad(w[:, o_ab:o_ab + 2 * DN_HEADS], ((0, 0), (0, LANES - 2 * DN_HEADS))))

    glu, qkv_pre, z, gb, siga, sigb = _inproj(
        x_p, x_s, norm1_g[l].reshape(1, d), w_main, w_ab, _pad_lanes(a_log[l]), _pad_lanes(dt_bias[l]), TOKEN_TILE)

    dw = (conv_dw_w[l], conv_dw_b[l].reshape(1, -1), conv_ln_g[l].reshape(1, -1), conv_ln_b[l].reshape(1, -1))
    pad_c = CONV_HALO - (CONV_WIDTH - 1)
    st_c_p = jnp.zeros((bp, CONV_HALO, D_CONV), F32)
    st_c_s = jnp.pad(state_conv[l], ((0, 0), (pad_c, 0), (0, 0)))
    cact_p, nconv_p = _conv_branch(glu, st_c_p, *dw, row0=0, bsz=bp, t_len=tp, bb=1, tt=SEQ_TILE)
    cact_s, nconv_s = _conv_branch(glu, st_c_s, *dw, row0=n_p, bsz=bs, t_len=ts, bb=8, tt=ts)

    pad_s = SHORT_HALO - (SHORT_WIDTH - 1)
    st_s_p = jnp.zeros((bp, SHORT_HALO, 3 * DN_WIDTH), F32)
    st_s_s = jnp.pad(state_short_conv[l], ((0, 0), (pad_s, 0), (0, 0)))
    s0_p = jnp.zeros((bp, DN_HEADS, DN_HEAD_DIM, DN_HEAD_DIM), F32)
    ng = delta_norm_g[l].reshape(1, -1)
    oact_p, nshort_p, s_p = _delta_prompt(qkv_pre, z, gb, st_s_p, s0_p, short_conv_w[l], ng,
                                          bsz=bp, t_len=tp, tt=SEQ_TILE)
    oact_s, nshort_s, s_s = _delta_sample(qkv_pre, z, gb, st_s_s, state_delta[l], short_conv_w[l], ng,
                                          row0=n_p, bsz=bs, seq_len=ts)

    rw = _split_bf16(jnp.pad(router_w[l], ((0, 0), (0, LANES - N_EXPERTS))))
    x2, tg = _mix(x_p, x_s, cact_p, cact_s, oact_p, oact_s, siga, sigb, w_conv_out[l].astype(BF16),
                          w_delta_out[l].astype(BF16), w_merge_out[l].astype(BF16), norm2_g[l].reshape(1, d),
                          rw, _pad_lanes(router_b[l]), TOKEN_TILE)

    n_blocks = -(-(n * TOP_K) // MOE_TILE) + N_EXPERTS
    n_steps = n // TOKEN_TILE
    dest, table, sched, nvalid = _route(tg[:, :TOP_K].astype(jnp.int32), MOE_TILE, n_blocks, n_steps)
    xs = _dispatch(x2, norm2_g[l].reshape(1, d), table, n_blocks * MOE_TILE, TOKEN_TILE)
    yb = _moe_ffn(xs, sched, nvalid, w_gate_up[l], b_gate_up[l], w_down[l], b_down[l], MOE_TILE, n_blocks)
    y_p, y_s = _combine(dest.reshape(n_steps, -1), yb, x2, tg, final_norm_g.reshape(1, d), n_p, TOKEN_TILE)

    conv_p = nconv_p[:, pad_c:, :][None]
    conv_s = nconv_s[:, pad_c:, :][None]
    short_p = nshort_p[:, pad_s:, :][None]
    short_s = nshort_s[:, pad_s:, :][None]
    return (y_p.reshape(bp, tp, d), y_s.reshape(bs, ts, d), conv_p, short_p, s_p[None],
            conv_s, short_s, s_s[None])
```

```python
import functools

import jax
import jax.numpy as jnp
from jax import lax
from jax.experimental import pallas as pl
from jax.experimental.pallas import tpu as pltpu

F32 = jnp.float32
BF16 = jnp.bfloat16
EPS = 1e-6

LANES = 128
SUBLANES = 8
VMEM_LIMIT_BYTES = 56 * 1024 * 1024
DMA_PRIORITIES = 2

D_CONV = 512
CONV_WIDTH = 31
DN_HEADS = 4
DN_HEAD_DIM = 128
DN_WIDTH = DN_HEADS * DN_HEAD_DIM
SHORT_WIDTH = 4
N_EXPERTS = 32
TOP_K = 4
SWIGLU_LIMIT = 7.0
SWIGLU_ALPHA = 1.702

CHUNK = 128
CONV_HALO = 32
SHORT_HALO = 8

MIX_TILE = 512
TOKEN_TILE = 256
SEQ_TILE = 512
MOE_TILE = 256


def _sigmoid(x):
    return 1.0 / (1.0 + jnp.exp(-x))


def _silu(x):
    return x * _sigmoid(x)


def _split_bf16(w):
    hi = w.astype(BF16)
    lo = (w - hi.astype(F32)).astype(BF16)
    return jnp.concatenate([hi, lo], axis=-1)


def _dot_split(x, w_split):
    n = w_split.shape[-1] // 2
    x_hi = x.astype(BF16)
    x_lo = (x - x_hi.astype(F32)).astype(BF16)
    r = jnp.dot(x_hi, w_split, preferred_element_type=F32)
    return r[:, :n] + r[:, n:] + jnp.dot(x_lo, w_split[:, :n], preferred_element_type=F32)


def _dot_delta(a, b, dims=(((1,), (0,)), ((), ()))):
    return lax.dot_general(a.astype(BF16), b.astype(BF16), dims, preferred_element_type=F32)


def _params(n_axes):
    return pltpu.CompilerParams(dimension_semantics=("arbitrary",) * n_axes, vmem_limit_bytes=VMEM_LIMIT_BYTES)


def _two_source_specs(tm, d, n_first_tiles):
    first = pl.BlockSpec((tm, d), lambda i, *_: (jnp.minimum(i, n_first_tiles - 1), 0))
    second = pl.BlockSpec((tm, d), lambda i, *_: (jnp.maximum(i - n_first_tiles, 0), 0))
    return first, second


def _inproj_kernel(xp_ref, xs_ref, g_ref, w_ref, wab_ref, alog_ref, dtb_ref,
                   glu_ref, qkv_ref, z_ref, gb_ref, sa_ref, sb_ref, *, n_prompt_tiles):
    x = jnp.where(pl.program_id(0) < n_prompt_tiles, xp_ref[...], xs_ref[...])
    h = x * lax.rsqrt(jnp.mean(x * x, axis=-1, keepdims=True) + EPS) * g_ref[...]
    hb = h.astype(BF16)

    def mm(lo, hi):
        return jnp.dot(hb, w_ref[:, lo:hi], preferred_element_type=F32)

    o_gate, o_qkv, o_z = D_CONV, 2 * D_CONV, 2 * D_CONV + 3 * DN_WIDTH
    o_ga = o_z + DN_WIDTH
    d = x.shape[-1]
    glu_ref[...] = mm(0, o_gate) * _sigmoid(mm(o_gate, o_qkv))
    qkv_ref[...] = mm(o_qkv, o_z)
    z_ref[...] = mm(o_z, o_ga)
    sa_ref[...] = _sigmoid(mm(o_ga, o_ga + d))
    sb_ref[...] = _sigmoid(mm(o_ga + d, o_ga + 2 * d))
    ab = _dot_split(h, wab_ref[...])
    xa = ab + dtb_ref[...]
    softplus = jnp.maximum(xa, 0.0) + jnp.log(1.0 + jnp.exp(-jnp.abs(xa)))
    g = -jnp.exp(alog_ref[...]) * softplus
    lane = lax.broadcasted_iota(jnp.int32, ab.shape, 1)
    gb_ref[...] = jnp.where(lane < DN_HEADS, g, _sigmoid(ab))


def _inproj(x_p, x_s, norm_g, w_main, w_ab, alog, dtb, tm):
    (n_p, d), n_s = x_p.shape, x_s.shape[0]
    assert n_p % tm == 0 and n_s % tm == 0
    n = n_p + n_s
    wcols = w_main.shape[1]
    row = lambda i: (i, 0)
    const = lambda i: (0, 0)
    outs = [(D_CONV, F32), (3 * DN_WIDTH, F32), (DN_WIDTH, F32), (LANES, F32), (d, F32), (d, F32)]
    return pl.pallas_call(
        functools.partial(_inproj_kernel, n_prompt_tiles=n_p // tm),
        grid=(n // tm,),
        in_specs=[
            *_two_source_specs(tm, d, n_p // tm),
            pl.BlockSpec((1, d), const),
            pl.BlockSpec((d, wcols), const),
            pl.BlockSpec((d, 2 * LANES), const),
            pl.BlockSpec((1, LANES), const),
            pl.BlockSpec((1, LANES), const),
        ],
        out_specs=[pl.BlockSpec((tm, c), row) for c, _ in outs],
        out_shape=[jax.ShapeDtypeStruct((n, c), dt) for c, dt in outs],
        compiler_params=_params(1),
    )(x_p, x_s, norm_g, w_main, w_ab, alog, dtb)


def _conv_kernel(glu_ref, st_ref, w_ref, b_ref, lg_ref, lb_ref, out_ref, nst_ref, e_ref, sh_ref, *, bb, tt, rows):
    t = pl.program_id(1)

    hist = CONV_WIDTH - 1

    @pl.when(t == 0)
    def _():
        e_ref[:, 0:SUBLANES, :] = jnp.zeros((bb, SUBLANES, D_CONV), F32)
        e_ref[:, CONV_HALO - hist:CONV_HALO, :] = st_ref[...]

    for b in range(bb):
        e_ref[b, CONV_HALO:CONV_HALO + tt, :] = glu_ref[b * tt:(b + 1) * tt, :]
    off = CONV_HALO - (CONV_WIDTH - 1)
    span = sh_ref.shape[1]
    for b in range(bb):
        for s in range(1, SUBLANES):
            sh_ref[s - 1] = e_ref[b, s:s + span, :]
        for c in range(tt // rows):
            r0 = c * rows
            acc = jnp.zeros((rows, D_CONV), F32) + b_ref[...]
            for j in range(CONV_WIDTH):
                q, s = divmod(j + off, SUBLANES)
                lo = r0 + q * SUBLANES
                src = e_ref[b, lo:lo + rows, :] if s == 0 else sh_ref[s - 1, lo:lo + rows, :]
                acc = acc + w_ref[j:j + 1, :] * src
            mu = jnp.mean(acc, axis=-1, keepdims=True)
            xc = acc - mu
            var = jnp.mean(xc * xc, axis=-1, keepdims=True)
            y = xc * lax.rsqrt(var + EPS) * lg_ref[...] + lb_ref[...]
            out_ref[b * tt + r0:b * tt + r0 + rows, :] = _silu(y).astype(out_ref.dtype)
    nst_ref[...] = e_ref[:, tt + CONV_HALO - hist:tt + CONV_HALO, :]
    e_ref[:, 0:CONV_HALO, :] = e_ref[:, tt:tt + CONV_HALO, :]


def _conv_branch(glu, state32, dw_w, dw_b, ln_g, ln_b, *, row0, bsz, t_len, bb, tt):
    c = glu.shape[1]
    assert bsz % bb == 0 and t_len % tt == 0 and row0 % (bb * tt) == 0
    nt = t_len // tt
    blk0 = row0 // (bb * tt)
    rows = min(tt, 32)
    kern = functools.partial(_conv_kernel, bb=bb, tt=tt, rows=rows)
    const = lambda b, t: (0, 0)
    return pl.pallas_call(
        kern,
        grid=(bsz // bb, nt),
        in_specs=[
            pl.BlockSpec((bb * tt, c), lambda b, t: (blk0 + b * nt + t, 0)),
            pl.BlockSpec((bb, CONV_WIDTH - 1, c), lambda b, t: (b, 0, 0)),
            pl.BlockSpec((CONV_WIDTH, c), const),
            pl.BlockSpec((1, c), const),
            pl.BlockSpec((1, c), const),
            pl.BlockSpec((1, c), const),
        ],
        out_specs=[
            pl.BlockSpec((bb * tt, c), lambda b, t: (b * nt + t, 0)),
            pl.BlockSpec((bb, CONV_WIDTH - 1, c), lambda b, t: (b, 0, 0)),
        ],
        out_shape=[
            jax.ShapeDtypeStruct((bsz * t_len, c), BF16),
            jax.ShapeDtypeStruct((bsz, CONV_WIDTH - 1, c), F32),
        ],
        scratch_shapes=[pltpu.VMEM((bb, CONV_HALO + tt, c), F32),
                        pltpu.VMEM((SUBLANES - 1, tt + CONV_HALO - SUBLANES, c), F32)],
        compiler_params=_params(2),
    )(glu, state32, dw_w, dw_b, ln_g, ln_b)


def _chunk_masks(seq_len):
    i = lax.broadcasted_iota(jnp.int32, (CHUNK, CHUNK), 0)
    j = lax.broadcasted_iota(jnp.int32, (CHUNK, CHUNK), 1)
    same = (i // seq_len) == (j // seq_len)
    incl = same & (i >= j)
    strict = same & (i > j)
    last = j == (i // seq_len) * seq_len + (seq_len - 1)
    levels = []
    blk = 1
    while blk < seq_len:
        levels.append(((i // (2 * blk)) == (j // (2 * blk))) & (((i // blk) % 2) == 1) & (((j // blk) % 2) == 0))
        blk *= 2
    eye = i == j
    return incl, strict, last, levels, eye


def _lane_col(x, lane):
    return jnp.broadcast_to(x[:, lane:lane + 1], (x.shape[0], LANES))


def _l2norm(x):
    return x * lax.rsqrt(jnp.sum(x * x, axis=-1, keepdims=True) + EPS)


def _select_sum(mask01, x):
    hi = x.astype(BF16)
    r1 = x - hi.astype(F32)
    mid = r1.astype(BF16)
    lo = (r1 - mid.astype(F32)).astype(BF16)
    w = x.shape[1]
    parts = jnp.dot(mask01, jnp.concatenate([hi, mid, lo], axis=1), preferred_element_type=F32)
    return parts[:, :w] + parts[:, w:2 * w] + parts[:, 2 * w:]


def _chunks_prepare(qkvs, gbts, masks, seq_len):
    incl, strict, last, levels, eye = masks
    nt = (((1,), (1,)), ((), ()))
    lower01 = jnp.where(incl, 1.0, 0.0).astype(BF16)
    probs = []
    for qkv, gbt in zip(qkvs, gbts):
        gc = _select_sum(lower01, gbt)
        gct = gc.T
        if seq_len == CHUNK:
            glast = jnp.broadcast_to(gc[CHUNK - 1:CHUNK, :], gc.shape)
        else:
            glast = _select_sum(jnp.where(last, 1.0, 0.0).astype(BF16), gc)
        for h in range(DN_HEADS):
            q = _l2norm(qkv[:, h * DN_HEAD_DIM:(h + 1) * DN_HEAD_DIM]) * (DN_HEAD_DIM ** -0.5)
            k = _l2norm(qkv[:, DN_WIDTH + h * DN_HEAD_DIM:DN_WIDTH + (h + 1) * DN_HEAD_DIM])
            v = qkv[:, 2 * DN_WIDTH + h * DN_HEAD_DIM:2 * DN_WIDTH + (h + 1) * DN_HEAD_DIM]
            gcol = _lane_col(gc, h)
            grow = jnp.broadcast_to(gct[h:h + 1, :], (CHUNK, CHUNK))
            beta = _lane_col(gbt, DN_HEADS + h)
            gl = _lane_col(glast, h)
            decay = jnp.exp(jnp.where(incl, gcol - grow, -jnp.inf))
            egc = jnp.exp(gcol)
            kb = k * beta
            probs.append(dict(q=q, k=k, kb=kb, decay=decay, rhs=jnp.concatenate([v * beta, kb * egc], axis=1),
                              qexp=q * egc, kdec=k * jnp.exp(gl - gcol), egl=jnp.exp(gl)))
    for p in probs:
        p['a'] = jnp.where(strict, _dot_delta(p['kb'], p['k'], nt) * p['decay'], 0.0)
        p['scores'] = _dot_delta(p['q'], p['k'], nt) * p['decay']
    for p in probs:
        p['x'] = jnp.where(eye, 1.0, 0.0) - jnp.where(levels[0], p['a'], 0.0)
    for m in levels[1:]:
        for p in probs:
            p['xa'] = _dot_delta(p['x'], jnp.where(m, p['a'], 0.0))
        for p in probs:
            p['x'] = p['x'] - _dot_delta(p['xa'], p['x'])
    out = []
    for c in range(len(qkvs)):
        heads = []
        for h in range(DN_HEADS):
            p = probs[c * DN_HEADS + h]
            sol = _dot_delta(p['x'], p['rhs'])
            heads.append((sol[:, :DN_HEAD_DIM], sol[:, DN_HEAD_DIM:], p['scores'], p['qexp'], p['kdec'], p['egl']))
        out.append(heads)
    return out


def _gated_out_norm(o, z, ng):
    y = o * lax.rsqrt(jnp.mean(o * o, axis=-1, keepdims=True) + EPS) * ng
    return y * _silu(z)


def _short_conv(e_ref, w_ref, tt):
    off = SHORT_HALO - (SHORT_WIDTH - 1)
    acc = w_ref[0:1, :] * e_ref[off:off + tt, :]
    for j in range(1, SHORT_WIDTH):
        acc = acc + w_ref[j:j + 1, :] * e_ref[off + j:off + j + tt, :]
    return _silu(acc)


def _delta_prompt_kernel(qkv_ref, z_ref, gb_ref, st_ref, s0_ref, w_ref, ng_ref,
                         o_ref, nst_ref, sout_ref, e_ref, s_ref, *, tt):
    t = pl.program_id(1)

    hist = SHORT_WIDTH - 1

    @pl.when(t == 0)
    def _():
        e_ref[0:SUBLANES, :] = jnp.zeros((SUBLANES, 3 * DN_WIDTH), F32)
        e_ref[SHORT_HALO - hist:SHORT_HALO, :] = st_ref[0]
        s_ref[...] = s0_ref[0]

    e_ref[SHORT_HALO:SHORT_HALO + tt, :] = qkv_ref[...]
    qkv = _short_conv(e_ref, w_ref, tt)
    nst_ref[0] = e_ref[tt + SHORT_HALO - hist:tt + SHORT_HALO, :]
    e_ref[0:SHORT_HALO, :] = e_ref[tt:tt + SHORT_HALO, :]

    masks = _chunk_masks(CHUNK)
    tn = (((0,), (0,)), ((), ()))
    n_chunks = tt // CHUNK
    prep = _chunks_prepare([qkv[c * CHUNK:(c + 1) * CHUNK, :] for c in range(n_chunks)],
                           [gb_ref[c * CHUNK:(c + 1) * CHUNK, :] for c in range(n_chunks)], masks, CHUNK)
    heads = range(DN_HEADS)
    s = [s_ref[h] for h in heads]
    for c in range(n_chunks):
        r0 = c * CHUNK
        value, kcum, scores, qexp, kdec, egl = zip(*prep[c])
        both = [_dot_delta(jnp.concatenate([kcum[h], qexp[h]], axis=0), s[h]) for h in heads]
        v_new = [value[h] - both[h][:CHUNK] for h in heads]
        o = [both[h][CHUNK:] + _dot_delta(scores[h], v_new[h]) for h in heads]
        s = [s[h] * egl[h][0:1, :] + _dot_delta(kdec[h], v_new[h], tn) for h in heads]
        for h in heads:
            lanes = slice(h * DN_HEAD_DIM, (h + 1) * DN_HEAD_DIM)
            o_ref[r0:r0 + CHUNK, lanes] = _gated_out_norm(
                o[h], z_ref[r0:r0 + CHUNK, lanes], ng_ref[...]).astype(o_ref.dtype)
    for h in heads:
        s_ref[h] = s[h]
        sout_ref[0, h] = s[h]


def _delta_prompt(qkv_pre, z, gb, state8, s0, conv_w, norm_g, *, bsz, t_len, tt):
    n = bsz * t_len
    assert t_len % tt == 0 and tt % CHUNK == 0
    nt = t_len // tt
    kern = functools.partial(_delta_prompt_kernel, tt=tt)
    tile = lambda b, t: (b * nt + t, 0)
    per_b = lambda b, t: (b, 0, 0)
    per_b4 = lambda b, t: (b, 0, 0, 0)
    return pl.pallas_call(
        kern,
        grid=(bsz, nt),
        in_specs=[
            pl.BlockSpec((tt, 3 * DN_WIDTH), tile),
            pl.BlockSpec((tt, DN_WIDTH), tile),
            pl.BlockSpec((tt, LANES), tile),
            pl.BlockSpec((1, SHORT_WIDTH - 1, 3 * DN_WIDTH), per_b),
            pl.BlockSpec((1, DN_HEADS, DN_HEAD_DIM, DN_HEAD_DIM), per_b4),
            pl.BlockSpec((SHORT_WIDTH, 3 * DN_WIDTH), lambda b, t: (0, 0)),
            pl.BlockSpec((1, DN_HEAD_DIM), lambda b, t: (0, 0)),
        ],
        out_specs=[
            pl.BlockSpec((tt, DN_WIDTH), tile),
            pl.BlockSpec((1, SHORT_WIDTH - 1, 3 * DN_WIDTH), per_b),
            pl.BlockSpec((1, DN_HEADS, DN_HEAD_DIM, DN_HEAD_DIM), per_b4),
        ],
        out_shape=[
            jax.ShapeDtypeStruct((n, DN_WIDTH), BF16),
            jax.ShapeDtypeStruct((bsz, SHORT_WIDTH - 1, 3 * DN_WIDTH), F32),
            jax.ShapeDtypeStruct((bsz, DN_HEADS, DN_HEAD_DIM, DN_HEAD_DIM), F32),
        ],
        scratch_shapes=[
            pltpu.VMEM((SHORT_HALO + tt, 3 * DN_WIDTH), F32),
            pltpu.VMEM((DN_HEADS, DN_HEAD_DIM, DN_HEAD_DIM), F32),
        ],
        compiler_params=_params(2),
    )(qkv_pre, z, gb, state8, s0, conv_w, norm_g)


def _delta_sample_kernel(qkv_ref, z_ref, gb_ref, st_ref, s0_ref, w_ref, ng_ref, o_ref, nst_ref, sout_ref,
                         e_ref, *, nseq, seq_len):
    qkv_rows = []
    hist = SHORT_WIDTH - 1
    e_ref[0:SUBLANES, :] = jnp.zeros((SUBLANES, 3 * DN_WIDTH), F32)
    for b in range(nseq):
        e_ref[SHORT_HALO - hist:SHORT_HALO, :] = st_ref[b]
        e_ref[SHORT_HALO:SHORT_HALO + seq_len, :] = qkv_ref[b * seq_len:(b + 1) * seq_len, :]
        qkv_rows.append(_short_conv(e_ref, w_ref, seq_len))
        nst_ref[b] = e_ref[seq_len + SHORT_HALO - hist:seq_len + SHORT_HALO, :]
    qkv = jnp.concatenate(qkv_rows, axis=0)
    masks = _chunk_masks(seq_len)
    prep = _chunks_prepare([qkv], [gb_ref[...]], masks, seq_len)[0]
    tn = (((0,), (0,)), ((), ()))
    rows = [slice(b * seq_len, (b + 1) * seq_len) for b in range(nseq)]
    both = [[_dot_delta(jnp.concatenate([prep[h][1][r], prep[h][3][r]], axis=0), s0_ref[b, h])
             for b, r in enumerate(rows)] for h in range(DN_HEADS)]
    v_new = [[prep[h][0][r] - both[h][b][:seq_len] for b, r in enumerate(rows)] for h in range(DN_HEADS)]
    for h in range(DN_HEADS):
        kdec, egl = prep[h][4], prep[h][5]
        for b, r in enumerate(rows):
            sout_ref[b, h] = (s0_ref[b, h] * egl[b * seq_len:b * seq_len + 1, :]
                              + _dot_delta(kdec[r], v_new[h][b], tn))
    for h in range(DN_HEADS):
        o = (jnp.concatenate([both[h][b][seq_len:] for b in range(nseq)], axis=0)
             + _dot_delta(prep[h][2], jnp.concatenate(v_new[h], axis=0)))
        lanes = slice(h * DN_HEAD_DIM, (h + 1) * DN_HEAD_DIM)
        o_ref[:, lanes] = _gated_out_norm(o, z_ref[:, lanes], ng_ref[...]).astype(o_ref.dtype)


def _delta_sample(qkv_pre, z, gb, state8, s0, conv_w, norm_g, *, row0, bsz, seq_len):
    n = bsz * seq_len
    assert CHUNK % seq_len == 0
    nseq = CHUNK // seq_len
    assert bsz % nseq == 0 and row0 % CHUNK == 0
    blk0 = row0 // CHUNK
    kern = functools.partial(_delta_sample_kernel, nseq=nseq, seq_len=seq_len)
    tile = lambda i: (blk0 + i, 0)
    blk3 = lambda i: (i, 0, 0)
    blk4 = lambda i: (i, 0, 0, 0)
    return pl.pallas_call(
        kern,
        grid=(bsz // nseq,),
        in_specs=[
            pl.BlockSpec((CHUNK, 3 * DN_WIDTH), tile),
            pl.BlockSpec((CHUNK, DN_WIDTH), tile),
            pl.BlockSpec((CHUNK, LANES), tile),
            pl.BlockSpec((nseq, SHORT_WIDTH - 1, 3 * DN_WIDTH), blk3),
            pl.BlockSpec((nseq, DN_HEADS, DN_HEAD_DIM, DN_HEAD_DIM), blk4),
            pl.BlockSpec((SHORT_WIDTH, 3 * DN_WIDTH), lambda i: (0, 0)),
            pl.BlockSpec((1, DN_HEAD_DIM), lambda i: (0, 0)),
        ],
        out_specs=[
            pl.BlockSpec((CHUNK, DN_WIDTH), lambda i: (i, 0)),
            pl.BlockSpec((nseq, SHORT_WIDTH - 1, 3 * DN_WIDTH), blk3),
            pl.BlockSpec((nseq, DN_HEADS, DN_HEAD_DIM, DN_HEAD_DIM), blk4),
        ],
        out_shape=[
            jax.ShapeDtypeStruct((n, DN_WIDTH), BF16),
            jax.ShapeDtypeStruct((bsz, SHORT_WIDTH - 1, 3 * DN_WIDTH), F32),
            jax.ShapeDtypeStruct((bsz, DN_HEADS, DN_HEAD_DIM, DN_HEAD_DIM), F32),
        ],
        scratch_shapes=[pltpu.VMEM((SHORT_HALO + seq_len, 3 * DN_WIDTH), F32)],
        compiler_params=_params(1),
    )(qkv_pre, z, gb, state8, s0, conv_w, norm_g)


def _mix_kernel(xp_ref, xs_ref, cap_ref, cas_ref, oap_ref, oas_ref, sa_ref, sb_ref, wc_ref, wd_ref, wm_ref,
                g2_ref, rw_ref, rb_ref, x2_ref, tr_ref, *, n_prompt_tiles):
    is_prompt = pl.program_id(0) < n_prompt_tiles
    x = jnp.where(is_prompt, xp_ref[...], xs_ref[...])
    ca = jnp.where(is_prompt, cap_ref[...], cas_ref[...])
    oa = jnp.where(is_prompt, oap_ref[...], oas_ref[...])
    ya = jnp.dot(ca, wc_ref[...], preferred_element_type=F32)
    yb = jnp.dot(oa, wd_ref[...], preferred_element_type=F32)
    mixed = sa_ref[...] * ya + sb_ref[...] * yb
    x2 = x + jnp.dot(mixed.astype(BF16), wm_ref[...], preferred_element_type=F32)
    x2_ref[...] = x2
    h2 = x2 * lax.rsqrt(jnp.mean(x2 * x2, axis=-1, keepdims=True) + EPS) * g2_ref[...]
    logits = _dot_split(h2, rw_ref[...]) + rb_ref[...]
    lt = logits.T[:N_EXPERTS, :]
    tokens = lt.shape[1]
    row = lax.broadcasted_iota(jnp.int32, lt.shape, 0).astype(F32)
    top_vals, top_idx = [], []
    for k in range(TOP_K):
        m = jnp.max(lt, axis=0, keepdims=True)
        idx = jnp.min(jnp.where(lt == m, row, float(N_EXPERTS)), axis=0, keepdims=True)
        top_vals.append(m)
        top_idx.append(idx)
        lt = jnp.where(row == idx, -jnp.inf, lt)
    exps = [jnp.exp(v - top_vals[0]) for v in top_vals]
    den = exps[0] + exps[1] + exps[2] + exps[3]
    slot = lax.broadcasted_iota(jnp.int32, (2 * TOP_K, tokens), 0)
    packed = jnp.zeros((2 * TOP_K, tokens), F32)
    for k in range(TOP_K):
        packed = jnp.where(slot == k, top_idx[k], packed)
        packed = jnp.where(slot == TOP_K + k, exps[k] / den, packed)
    packed = jnp.concatenate([packed, jnp.zeros((LANES - 2 * TOP_K, tokens), F32)], axis=0)
    tr_ref[...] = packed.T


def _mix(x_p, x_s, cact_p, cact_s, oact_p, oact_s, siga, sigb, w_conv_out, w_delta_out, w_merge_out, norm2_g,
         router_w, router_b, tm):
    (n_p, d), n_s = x_p.shape, x_s.shape[0]
    n = n_p + n_s
    row = lambda i: (i, 0)
    const = lambda i: (0, 0)
    return pl.pallas_call(
        functools.partial(_mix_kernel, n_prompt_tiles=n_p // tm),
        grid=(n // tm,),
        in_specs=[
            *_two_source_specs(tm, d, n_p // tm),
            *_two_source_specs(tm, D_CONV, n_p // tm),
            *_two_source_specs(tm, DN_WIDTH, n_p // tm),
            pl.BlockSpec((tm, d), row),
            pl.BlockSpec((tm, d), row),
            pl.BlockSpec((D_CONV, d), const),
            pl.BlockSpec((DN_WIDTH, d), const),
            pl.BlockSpec((d, d), const),
            pl.BlockSpec((1, d), const),
            pl.BlockSpec((d, 2 * LANES), const),
            pl.BlockSpec((1, LANES), const),
        ],
        out_specs=[
            pl.BlockSpec((tm, d), row),
            pl.BlockSpec((tm, LANES), row),
        ],
        out_shape=[
            jax.ShapeDtypeStruct((n, d), F32),
            jax.ShapeDtypeStruct((n, LANES), F32),
        ],
        compiler_params=_params(1),
    )(x_p, x_s, cact_p, cact_s, oact_p, oact_s, siga, sigb, w_conv_out, w_delta_out, w_merge_out, norm2_g,
      router_w, router_b)


def _fill_rows_per_step(n_fill, n_steps):
    per_step = SUBLANES
    while per_step * n_steps < n_fill:
        per_step *= 2
    assert n_fill % per_step == 0
    return per_step


def _route(top_idx, tm, n_blocks, n_steps):
    n = top_idx.shape[0]
    n_fill = n_blocks * tm - n * TOP_K
    assert n_fill == N_EXPERTS * tm
    flat_e = top_idx.reshape(-1)
    experts = jnp.arange(N_EXPERTS, dtype=jnp.int32)
    onehot = (flat_e[:, None] == experts[None, :]).astype(jnp.int32)
    csum = jnp.cumsum(onehot, axis=0)
    rank = jnp.sum(csum * onehot, axis=1) - 1
    counts = csum[-1]
    padded = (counts + tm - 1) // tm * tm
    pad_end = jnp.cumsum(padded)
    pad_start = pad_end - padded
    dest = jnp.sum(onehot * pad_start[None, :], axis=1) + rank
    nvalid = (pad_end[-1] // tm).astype(jnp.int32)
    blk = jnp.arange(n_blocks, dtype=jnp.int32)
    owner = jnp.sum((pad_end[None, :] <= (blk * tm)[:, None]).astype(jnp.int32), axis=1)
    block_e = jnp.minimum(owner, N_EXPERTS - 1)
    block_e = jnp.where(blk < nvalid, block_e, jnp.sum(jnp.where(blk == nvalid - 1, block_e, 0)))
    present = counts > 0
    later = present[None, :] & (experts[None, :] > experts[:, None])
    next_present = jnp.min(jnp.where(later, experts[None, :], N_EXPERTS), axis=1)
    next_present = jnp.where(next_present == N_EXPERTS, -1, next_present)
    parity = (jnp.cumsum(present.astype(jnp.int32)) - 1) % 2
    of_block = (block_e[:, None] == experts[None, :]).astype(jnp.int32)
    sched = jnp.stack([block_e, jnp.sum(of_block * next_present[None, :], axis=1),
                       jnp.sum(of_block * parity[None, :], axis=1)])
    n_pad = padded - counts
    spill = tm - n_pad
    spill_start = pad_end[-1] + jnp.cumsum(spill) - spill
    j = jnp.arange(tm, dtype=jnp.int32)[None, :]
    fill = jnp.where(j < n_pad[:, None], (pad_start + counts)[:, None] + j, (spill_start - n_pad)[:, None] + j)
    fill_step = _fill_rows_per_step(n_fill, n_steps)
    fill = jnp.pad(fill.reshape(-1, fill_step), ((0, n_steps - n_fill // fill_step), (0, 0)))
    table = jnp.concatenate([dest.reshape(n_steps, -1), fill], axis=1)
    return dest.reshape(n, TOP_K), table, sched, nvalid.reshape(1)


def _to_token_tiles(x, ref, row0=0):
    t = x.shape[0]
    for s in range(SUBLANES):
        ref[pl.ds(row0 + s, t, stride=SUBLANES), :] = x[:, s * LANES:(s + 1) * LANES]


def _from_token_tiles(ref, t, row0=0):
    return jnp.concatenate([ref[pl.ds(row0 + s, t, stride=SUBLANES), :] for s in range(SUBLANES)], axis=1)


def _tile(ref, row):
    return ref.at[pl.ds(pl.multiple_of(row * SUBLANES, SUBLANES), SUBLANES)]


def _dispatch_kernel(tab_ref, x2_ref, g2_ref, xs_ref, buf, sems, *, tokens, fill_step, n_fill_steps):
    i = pl.program_id(0)
    last = pl.num_programs(0) - 1
    slot = i % 2
    base = slot * tokens
    fill_sem = 2

    x2 = x2_ref[...]
    h2 = x2 * lax.rsqrt(jnp.mean(x2 * x2, axis=-1, keepdims=True) + EPS) * g2_ref[...]
    _to_token_tiles(h2, buf, pl.multiple_of(base * SUBLANES, SUBLANES))

    def wait_tiles(sem_idx, count):
        while count > 0:
            rows = min(count, tokens) * SUBLANES
            pltpu.make_async_copy(buf.at[pl.ds(0, rows)], xs_ref.at[pl.ds(0, rows)], sems.at[sem_idx]).wait()
            count -= min(count, tokens)

    for t in range(tokens):
        for k in range(TOP_K):
            pltpu.make_async_copy(_tile(buf, base + t), _tile(xs_ref, tab_ref[i, t * TOP_K + k]),
                                  sems.at[slot]).start(priority=k % DMA_PRIORITIES)

    @pl.when(i < n_fill_steps)
    def _():
        for p in range(fill_step):
            pltpu.make_async_copy(_tile(buf, base), _tile(xs_ref, tab_ref[i, tokens * TOP_K + p]),
                                  sems.at[fill_sem]).start(priority=p % DMA_PRIORITIES)
        wait_tiles(fill_sem, fill_step)

    @pl.when(i > 0)
    def _():
        wait_tiles(1 - slot, tokens * TOP_K)

    @pl.when(i == last)
    def _():
        wait_tiles(slot, tokens * TOP_K)


def _dispatch(x2, norm_g, table, n_rows, tokens):
    n, d = x2.shape
    assert d == SUBLANES * LANES
    fill_step = table.shape[1] - tokens * TOP_K
    n_fill_steps = (n_rows - n * TOP_K) // fill_step
    return pl.pallas_call(
        functools.partial(_dispatch_kernel, tokens=tokens, fill_step=fill_step, n_fill_steps=n_fill_steps),
        grid_spec=pltpu.PrefetchScalarGridSpec(
            num_scalar_prefetch=1,
            grid=(n // tokens,),
            in_specs=[pl.BlockSpec((tokens, d), lambda i, tab: (i, 0)),
                      pl.BlockSpec((1, d), lambda i, tab: (0, 0))],
            out_specs=pl.BlockSpec(memory_space=pl.ANY),
            scratch_shapes=[pltpu.VMEM((2 * tokens * SUBLANES, LANES), F32), pltpu.SemaphoreType.DMA((3,))],
        ),
        out_shape=jax.ShapeDtypeStruct((n_rows * SUBLANES, LANES), F32),
        compiler_params=_params(1),
    )(table, x2, norm_g)


def _moe_ffn_kernel(sched_ref, nv_ref, xs_ref, wgu_hbm, bgu_ref, wd_hbm, bd_ref, out_ref,
                    wgu_f32, wd_f32, wgu_bf, wd_bf, sems):
    i = pl.program_id(0)
    expert, next_expert, half = sched_ref[0, i], sched_ref[1, i], sched_ref[2, i]

    def weight_copies(e, h):
        return (pltpu.make_async_copy(wgu_hbm.at[e], wgu_f32.at[h], sems.at[h]),
                pltpu.make_async_copy(wd_hbm.at[e], wd_f32.at[h], sems.at[h]))

    @pl.when(i == 0)
    def _():
        for c in weight_copies(expert, half):
            c.start()

    @pl.when(jnp.logical_or(i == 0, expert != sched_ref[0, jnp.maximum(i - 1, 0)]))
    def _():
        for c in weight_copies(expert, half):
            c.wait()
        wgu_bf[...] = wgu_f32[half].astype(BF16)
        wd_bf[...] = wd_f32[half].astype(BF16)

        @pl.when(next_expert >= 0)
        def _():
            for c in weight_copies(next_expert, 1 - half):
                c.start()

    @pl.when(i < nv_ref[0])
    def _():
        f = wd_bf.shape[0]
        tm = xs_ref.shape[0] // SUBLANES
        x = _from_token_tiles(xs_ref, tm).astype(BF16)
        gu = jnp.dot(x, wgu_bf[...], preferred_element_type=F32) + bgu_ref[0]
        gt = jnp.minimum(gu[:, :f], SWIGLU_LIMIT)
        up = jnp.clip(gu[:, f:], -SWIGLU_LIMIT, SWIGLU_LIMIT)
        act = (up + 1.0) * (gt * _sigmoid(SWIGLU_ALPHA * gt))
        y = jnp.dot(act.astype(BF16), wd_bf[...], preferred_element_type=F32) + bd_ref[0]
        _to_token_tiles(y, out_ref)

    @pl.when(i >= nv_ref[0])
    def _():
        out_ref[...] = jnp.zeros(out_ref.shape, out_ref.dtype)


def _moe_ffn(xs, sched, nvalid, w_gate_up, b_gate_up, w_down, b_down, tm, n_blocks):
    ne, d, f2 = w_gate_up.shape
    f = f2 // 2
    used = lambda i, sc, nv: (jnp.minimum(i, nv[0] - 1), 0)
    return pl.pallas_call(
        _moe_ffn_kernel,
        grid_spec=pltpu.PrefetchScalarGridSpec(
            num_scalar_prefetch=2,
            grid=(n_blocks,),
            in_specs=[
                pl.BlockSpec((tm * SUBLANES, LANES), used),
                pl.BlockSpec(memory_space=pl.ANY),
                pl.BlockSpec((1, 1, f2), lambda i, sc, nv: (sc[0, i], 0, 0)),
                pl.BlockSpec(memory_space=pl.ANY),
                pl.BlockSpec((1, 1, d), lambda i, sc, nv: (sc[0, i], 0, 0)),
            ],
            out_specs=pl.BlockSpec((tm * SUBLANES, LANES), lambda i, sc, nv: (i, 0)),
            scratch_shapes=[pltpu.VMEM((2, d, f2), F32), pltpu.VMEM((2, f, d), F32),
                            pltpu.VMEM((d, f2), BF16), pltpu.VMEM((f, d), BF16), pltpu.SemaphoreType.DMA((2,))],
        ),
        out_shape=jax.ShapeDtypeStruct((n_blocks * tm * SUBLANES, LANES), F32),
        compiler_params=_params(1),
    )(sched, nvalid, xs, w_gate_up, b_gate_up.reshape(ne, 1, f2), w_down, b_down.reshape(ne, 1, d))


def _combine_kernel(slot_ref, yb_ref, x2_ref, tg_ref, fg_ref, yp_ref, ys_ref, buf, sems, *, tc, n_prompt_tiles):
    i = pl.program_id(0)
    n_steps = pl.num_programs(0)

    def region(slot, k):
        return (slot * TOP_K + k) * tc

    def fetch(step, slot):
        for t in range(tc):
            for k in range(TOP_K):
                pltpu.make_async_copy(
                    _tile(yb_ref, slot_ref[step, t * TOP_K + k]), _tile(buf, region(slot, k) + t),
                    sems.at[slot]).start(priority=k % DMA_PRIORITIES)

    @pl.when(i == 0)
    def _():
        fetch(0, 0)

    @pl.when(i + 1 < n_steps)
    def _():
        fetch(i + 1, (i + 1) % 2)

    slot = i % 2
    for k in range(TOP_K):
        pltpu.make_async_copy(yb_ref.at[pl.ds(0, tc * SUBLANES)], buf.at[pl.ds(0, tc * SUBLANES)],
                              sems.at[slot]).wait()
    tg = tg_ref[...]
    y = x2_ref[...]
    for k in range(TOP_K):
        rows = _from_token_tiles(buf, tc, pl.multiple_of(region(slot, k) * SUBLANES, SUBLANES))
        y = y + tg[:, TOP_K + k:TOP_K + k + 1] * rows
    out = y * lax.rsqrt(jnp.mean(y * y, axis=-1, keepdims=True) + EPS) * fg_ref[...]

    @pl.when(i < n_prompt_tiles)
    def _():
        yp_ref[...] = out

    @pl.when(i >= n_prompt_tiles)
    def _():
        ys_ref[...] = out


def _combine(slot2d, yb, x2, tg, final_g, n_p, tc):
    n, d = x2.shape
    n_s = n - n_p
    assert n_p % tc == 0 and n_s % tc == 0
    npt = n_p // tc
    kern = functools.partial(_combine_kernel, tc=tc, n_prompt_tiles=npt)
    out_p, out_s = _two_source_specs(tc, d, npt)
    return pl.pallas_call(
        kern,
        grid_spec=pltpu.PrefetchScalarGridSpec(
            num_scalar_prefetch=1,
            grid=(n // tc,),
            in_specs=[
                pl.BlockSpec(memory_space=pl.ANY),
                pl.BlockSpec((tc, d), lambda i, s: (i, 0)),
                pl.BlockSpec((tc, LANES), lambda i, s: (i, 0)),
                pl.BlockSpec((1, d), lambda i, s: (0, 0)),
            ],
            out_specs=[out_p, out_s],
            scratch_shapes=[pltpu.VMEM((2 * TOP_K * tc * SUBLANES, LANES), F32), pltpu.SemaphoreType.DMA((2,))],
        ),
        out_shape=[jax.ShapeDtypeStruct((n_p, d), F32), jax.ShapeDtypeStruct((n_s, d), F32)],
        compiler_params=_params(1),
    )(slot2d, yb, x2, tg, final_g)


def _pad_lanes(v, width=LANES):
    v = v.reshape(1, -1)
    return jnp.pad(v, ((0, 0), (0, width - v.shape[1])))


def kernel(x_prompt, x_sample, state_conv, state_short_conv, state_delta, norm1_g, w_in, conv_dw_w,
           conv_dw_b, conv_ln_g, conv_ln_b, w_conv_out, short_conv_w, a_log, dt_bias, delta_norm_g,
           w_delta_out, w_merge_out, norm2_g, router_w, router_b, w_gate_up, b_gate_up, w_down, b_down,
           final_norm_g):
    depth = w_in.shape[0]
    assert depth == 1
    bp, tp, d = x_prompt.shape
    bs, ts, _ = x_sample.shape
    n_p, n_s = bp * tp, bs * ts
    n = n_p + n_s
    l = 0
    x_p = x_prompt.reshape(n_p, d)
    x_s = x_sample.reshape(n_s, d)

    o_ab = 2 * D_CONV + 4 * DN_WIDTH
    w = w_in[l]
    w_main = jnp.concatenate([w[:, :o_ab], w[:, o_ab + 2 * DN_HEADS:]], axis=1).astype(BF16)
    w_ab = _split_bf16(jnp.pad(w[:, o_ab:o_ab + 2 * DN_HEADS], ((0, 0), (0, LANES - 2 * DN_HEADS))))

    glu, qkv_pre, z, gb, siga, sigb = _inproj(
        x_p, x_s, norm1_g[l].reshape(1, d), w_main, w_ab, _pad_lanes(a_log[l]), _pad_lanes(dt_bias[l]), TOKEN_TILE)

    dw = (conv_dw_w[l], conv_dw_b[l].reshape(1, -1), conv_ln_g[l].reshape(1, -1), conv_ln_b[l].reshape(1, -1))
    st_c_p = jnp.zeros((bp, CONV_WIDTH - 1, D_CONV), F32)
    seq_tile = SEQ_TILE if tp % SEQ_TILE == 0 else TOKEN_TILE
    cact_p, conv_p = _conv_branch(glu, st_c_p, *dw, row0=0, bsz=bp, t_len=tp, bb=1, tt=seq_tile)
    cact_s, conv_s = _conv_branch(glu, state_conv[l], *dw, row0=n_p, bsz=bs, t_len=ts, bb=8, tt=ts)

    st_s_p = jnp.zeros((bp, SHORT_WIDTH - 1, 3 * DN_WIDTH), F32)
    s0_p = jnp.zeros((bp, DN_HEADS, DN_HEAD_DIM, DN_HEAD_DIM), F32)
    ng = delta_norm_g[l].reshape(1, -1)
    oact_p, short_p, s_p = _delta_prompt(qkv_pre, z, gb, st_s_p, s0_p, short_conv_w[l], ng,
                                         bsz=bp, t_len=tp, tt=seq_tile)
    oact_s, short_s, s_s = _delta_sample(qkv_pre, z, gb, state_short_conv[l], state_delta[l], short_conv_w[l], ng,
                                         row0=n_p, bsz=bs, seq_len=ts)

    rw = _split_bf16(jnp.pad(router_w[l], ((0, 0), (0, LANES - N_EXPERTS))))
    x2, tg = _mix(x_p, x_s, cact_p, cact_s, oact_p, oact_s, siga, sigb, w_conv_out[l].astype(BF16),
                  w_delta_out[l].astype(BF16), w_merge_out[l].astype(BF16), norm2_g[l].reshape(1, d),
                  rw, _pad_lanes(router_b[l]),
                  MIX_TILE if n_p % MIX_TILE == 0 and n_s % MIX_TILE == 0 else TOKEN_TILE)

    n_blocks = -(-(n * TOP_K) // MOE_TILE) + N_EXPERTS
    n_steps = n // TOKEN_TILE
    dest, table, sched, nvalid = _route(tg[:, :TOP_K].astype(jnp.int32), MOE_TILE, n_blocks, n_steps)
    xs = _dispatch(x2, norm2_g[l].reshape(1, d), table, n_blocks * MOE_TILE, TOKEN_TILE)
    yb = _moe_ffn(xs, sched, nvalid, w_gate_up[l], b_gate_up[l], w_down[l], b_down[l], MOE_TILE, n_blocks)
    y_p, y_s = _combine(dest.reshape(n_steps, -1), yb, x2, tg, final_norm_g.reshape(1, d), n_p, TOKEN_TILE)

    return (y_p.reshape(bp, tp, d), y_s.reshape(bs, ts, d), conv_p[None], short_p[None], s_p[None],
            conv_s[None], short_s[None], s_s[None])
```

```python
import functools

import jax
import jax.numpy as jnp
from jax import lax
from jax.experimental import pallas as pl
from jax.experimental.pallas import tpu as pltpu

F32 = jnp.float32
BF16 = jnp.bfloat16
EPS = 1e-6

LANES = 128
SUBLANES = 8
VMEM_LIMIT_BYTES = 56 * 1024 * 1024
DMA_PRIORITIES = 2

D_CONV = 512
CONV_WIDTH = 31
DN_HEADS = 4
DN_HEAD_DIM = 128
DN_WIDTH = DN_HEADS * DN_HEAD_DIM
SHORT_WIDTH = 4
N_EXPERTS = 32
TOP_K = 4
SWIGLU_LIMIT = 7.0
SWIGLU_ALPHA = 1.702

CHUNK = 128
CONV_HALO = 32
SHORT_HALO = 8

MIX_TILE = 512
MIX_SPLIT = 2
TOKEN_TILE = 256
SEQ_TILE = 512
MOE_TILE = 256
FFN_SPLIT = 2


def _sigmoid(x):
    return 1.0 / (1.0 + jnp.exp(-x))


def _silu(x):
    return x * _sigmoid(x)


def _split_bf16(w):
    hi = w.astype(BF16)
    lo = (w - hi.astype(F32)).astype(BF16)
    return jnp.concatenate([hi, lo], axis=-1)


def _dot_split(x, w_split):
    n = w_split.shape[-1] // 2
    x_hi = x.astype(BF16)
    x_lo = (x - x_hi.astype(F32)).astype(BF16)
    r = jnp.dot(x_hi, w_split, preferred_element_type=F32)
    return r[:, :n] + r[:, n:] + jnp.dot(x_lo, w_split[:, :n], preferred_element_type=F32)


def _dot_delta(a, b, dims=(((1,), (0,)), ((), ()))):
    return lax.dot_general(a.astype(BF16), b.astype(BF16), dims, preferred_element_type=F32)


def _params(n_axes):
    return pltpu.CompilerParams(dimension_semantics=("arbitrary",) * n_axes, vmem_limit_bytes=VMEM_LIMIT_BYTES)


def _two_source_specs(tm, d, n_first_tiles):
    first = pl.BlockSpec((tm, d), lambda i, *_: (jnp.minimum(i, n_first_tiles - 1), 0))
    second = pl.BlockSpec((tm, d), lambda i, *_: (jnp.maximum(i - n_first_tiles, 0), 0))
    return first, second


def _inproj_kernel(xp_ref, xs_ref, g_ref, w_ref, wab_ref, alog_ref, dtb_ref,
                   glu_ref, qkv_ref, z_ref, gb_ref, sa_ref, sb_ref, *, n_prompt_tiles):
    x = jnp.where(pl.program_id(0) < n_prompt_tiles, xp_ref[...], xs_ref[...])
    h = x * lax.rsqrt(jnp.mean(x * x, axis=-1, keepdims=True) + EPS) * g_ref[...]
    hb = h.astype(BF16)

    def mm(lo, hi):
        return jnp.dot(hb, w_ref[:, lo:hi], preferred_element_type=F32)

    o_gate, o_qkv, o_z = D_CONV, 2 * D_CONV, 2 * D_CONV + 3 * DN_WIDTH
    o_ga = o_z + DN_WIDTH
    d = x.shape[-1]
    glu_ref[...] = mm(0, o_gate) * _sigmoid(mm(o_gate, o_qkv))
    qkv_ref[...] = mm(o_qkv, o_z)
    z_ref[...] = mm(o_z, o_ga)
    sa_ref[...] = _sigmoid(mm(o_ga, o_ga + d))
    sb_ref[...] = _sigmoid(mm(o_ga + d, o_ga + 2 * d))
    ab = _dot_split(h, wab_ref[...])
    xa = ab + dtb_ref[...]
    softplus = jnp.maximum(xa, 0.0) + jnp.log(1.0 + jnp.exp(-jnp.abs(xa)))
    g = -jnp.exp(alog_ref[...]) * softplus
    lane = lax.broadcasted_iota(jnp.int32, ab.shape, 1)
    gb_ref[...] = jnp.where(lane < DN_HEADS, g, _sigmoid(ab))


def _inproj(x_p, x_s, norm_g, w_main, w_ab, alog, dtb, tm):
    (n_p, d), n_s = x_p.shape, x_s.shape[0]
    assert n_p % tm == 0 and n_s % tm == 0
    n = n_p + n_s
    wcols = w_main.shape[1]
    row = lambda i: (i, 0)
    const = lambda i: (0, 0)
    outs = [(D_CONV, F32), (3 * DN_WIDTH, F32), (DN_WIDTH, F32), (LANES, F32), (d, F32), (d, F32)]
    return pl.pallas_call(
        functools.partial(_inproj_kernel, n_prompt_tiles=n_p // tm),
        grid=(n // tm,),
        in_specs=[
            *_two_source_specs(tm, d, n_p // tm),
            pl.BlockSpec((1, d), const),
            pl.BlockSpec((d, wcols), const),
            pl.BlockSpec((d, 2 * LANES), const),
            pl.BlockSpec((1, LANES), const),
            pl.BlockSpec((1, LANES), const),
        ],
        out_specs=[pl.BlockSpec((tm, c), row) for c, _ in outs],
        out_shape=[jax.ShapeDtypeStruct((n, c), dt) for c, dt in outs],
        compiler_params=_params(1),
    )(x_p, x_s, norm_g, w_main, w_ab, alog, dtb)


def _conv_kernel(glu_ref, st_ref, w_ref, b_ref, lg_ref, lb_ref, out_ref, nst_ref, e_ref, sh_ref, *, bb, tt, rows):
    t = pl.program_id(1)

    hist = CONV_WIDTH - 1

    @pl.when(t == 0)
    def _():
        e_ref[:, 0:SUBLANES, :] = jnp.zeros((bb, SUBLANES, D_CONV), F32)
        e_ref[:, CONV_HALO - hist:CONV_HALO, :] = st_ref[...]

    for b in range(bb):
        e_ref[b, CONV_HALO:CONV_HALO + tt, :] = glu_ref[b * tt:(b + 1) * tt, :]
    off = CONV_HALO - (CONV_WIDTH - 1)
    span = sh_ref.shape[1]
    for b in range(bb):
        for s in range(1, SUBLANES):
            sh_ref[s - 1] = e_ref[b, s:s + span, :]
        for c in range(tt // rows):
            r0 = c * rows
            acc = jnp.zeros((rows, D_CONV), F32) + b_ref[...]
            for j in range(CONV_WIDTH):
                q, s = divmod(j + off, SUBLANES)
                lo = r0 + q * SUBLANES
                src = e_ref[b, lo:lo + rows, :] if s == 0 else sh_ref[s - 1, lo:lo + rows, :]
                acc = acc + w_ref[j:j + 1, :] * src
            mu = jnp.mean(acc, axis=-1, keepdims=True)
            xc = acc - mu
            var = jnp.mean(xc * xc, axis=-1, keepdims=True)
            y = xc * lax.rsqrt(var + EPS) * lg_ref[...] + lb_ref[...]
            out_ref[b * tt + r0:b * tt + r0 + rows, :] = _silu(y).astype(out_ref.dtype)
    nst_ref[...] = e_ref[:, tt + CONV_HALO - hist:tt + CONV_HALO, :]
    e_ref[:, 0:CONV_HALO, :] = e_ref[:, tt:tt + CONV_HALO, :]


def _conv_branch(glu, state32, dw_w, dw_b, ln_g, ln_b, *, row0, bsz, t_len, bb, tt):
    c = glu.shape[1]
    assert bsz % bb == 0 and t_len % tt == 0 and row0 % (bb * tt) == 0
    nt = t_len // tt
    blk0 = row0 // (bb * tt)
    rows = min(tt, 32)
    kern = functools.partial(_conv_kernel, bb=bb, tt=tt, rows=rows)
    const = lambda b, t: (0, 0)
    return pl.pallas_call(
        kern,
        grid=(bsz // bb, nt),
        in_specs=[
            pl.BlockSpec((bb * tt, c), lambda b, t: (blk0 + b * nt + t, 0)),
            pl.BlockSpec((bb, CONV_WIDTH - 1, c), lambda b, t: (b, 0, 0)),
            pl.BlockSpec((CONV_WIDTH, c), const),
            pl.BlockSpec((1, c), const),
            pl.BlockSpec((1, c), const),
            pl.BlockSpec((1, c), const),
        ],
        out_specs=[
            pl.BlockSpec((bb * tt, c), lambda b, t: (b * nt + t, 0)),
            pl.BlockSpec((bb, CONV_WIDTH - 1, c), lambda b, t: (b, 0, 0)),
        ],
        out_shape=[
            jax.ShapeDtypeStruct((bsz * t_len, c), BF16),
            jax.ShapeDtypeStruct((bsz, CONV_WIDTH - 1, c), F32),
        ],
        scratch_shapes=[pltpu.VMEM((bb, CONV_HALO + tt, c), F32),
                        pltpu.VMEM((SUBLANES - 1, tt + CONV_HALO - SUBLANES, c), F32)],
        compiler_params=_params(2),
    )(glu, state32, dw_w, dw_b, ln_g, ln_b)


def _chunk_masks(seq_len):
    i = lax.broadcasted_iota(jnp.int32, (CHUNK, CHUNK), 0)
    j = lax.broadcasted_iota(jnp.int32, (CHUNK, CHUNK), 1)
    same = (i // seq_len) == (j // seq_len)
    incl = same & (i >= j)
    strict = same & (i > j)
    last = j == (i // seq_len) * seq_len + (seq_len - 1)
    levels = []
    blk = 1
    while blk < seq_len:
        levels.append(((i // (2 * blk)) == (j // (2 * blk))) & (((i // blk) % 2) == 1) & (((j // blk) % 2) == 0))
        blk *= 2
    eye = i == j
    return incl, strict, last, levels, eye


def _lane_col(x, lane):
    return jnp.broadcast_to(x[:, lane:lane + 1], (x.shape[0], LANES))


def _l2norm(x):
    return x * lax.rsqrt(jnp.sum(x * x, axis=-1, keepdims=True) + EPS)


def _select_sum(mask01, x):
    hi = x.astype(BF16)
    r1 = x - hi.astype(F32)
    mid = r1.astype(BF16)
    lo = (r1 - mid.astype(F32)).astype(BF16)
    w = x.shape[1]
    parts = jnp.dot(mask01, jnp.concatenate([hi, mid, lo], axis=1), preferred_element_type=F32)
    return parts[:, :w] + parts[:, w:2 * w] + parts[:, 2 * w:]


def _chunks_prepare(qkvs, gbts, masks, seq_len):
    incl, strict, last, levels, eye = masks
    nt = (((1,), (1,)), ((), ()))
    lower01 = jnp.where(incl, 1.0, 0.0).astype(BF16)
    probs = []
    for qkv, gbt in zip(qkvs, gbts):
        gc = _select_sum(lower01, gbt)
        gct = gc.T
        if seq_len == CHUNK:
            glast = jnp.broadcast_to(gc[CHUNK - 1:CHUNK, :], gc.shape)
        else:
            glast = _select_sum(jnp.where(last, 1.0, 0.0).astype(BF16), gc)
        for h in range(DN_HEADS):
            q = _l2norm(qkv[:, h * DN_HEAD_DIM:(h + 1) * DN_HEAD_DIM]) * (DN_HEAD_DIM ** -0.5)
            k = _l2norm(qkv[:, DN_WIDTH + h * DN_HEAD_DIM:DN_WIDTH + (h + 1) * DN_HEAD_DIM])
            v = qkv[:, 2 * DN_WIDTH + h * DN_HEAD_DIM:2 * DN_WIDTH + (h + 1) * DN_HEAD_DIM]
            gcol = _lane_col(gc, h)
            grow = jnp.broadcast_to(gct[h:h + 1, :], (CHUNK, CHUNK))
            beta = _lane_col(gbt, DN_HEADS + h)
            gl = _lane_col(glast, h)
            decay = jnp.exp(jnp.where(incl, gcol - grow, -jnp.inf))
            egc = jnp.exp(gcol)
            kb = k * beta
            probs.append(dict(q=q, k=k, kb=kb, decay=decay, rhs=jnp.concatenate([v * beta, kb * egc], axis=1),
                              qexp=q * egc, kdec=k * jnp.exp(gl - gcol), egl=jnp.exp(gl)))
    for p in probs:
        p['a'] = jnp.where(strict, _dot_delta(p['kb'], p['k'], nt) * p['decay'], 0.0)
        p['scores'] = _dot_delta(p['q'], p['k'], nt) * p['decay']
    for p in probs:
        p['x'] = jnp.where(eye, 1.0, 0.0) - jnp.where(levels[0], p['a'], 0.0)
    for m in levels[1:]:
        for p in probs:
            p['xa'] = _dot_delta(p['x'], jnp.where(m, p['a'], 0.0))
        for p in probs:
            p['x'] = p['x'] - _dot_delta(p['xa'], p['x'])
    out = []
    for c in range(len(qkvs)):
        heads = []
        for h in range(DN_HEADS):
            p = probs[c * DN_HEADS + h]
            sol = _dot_delta(p['x'], p['rhs'])
            heads.append((sol[:, :DN_HEAD_DIM], sol[:, DN_HEAD_DIM:], p['scores'], p['qexp'], p['kdec'], p['egl']))
        out.append(heads)
    return out


def _gated_out_norm(o, z, ng):
    y = o * lax.rsqrt(jnp.mean(o * o, axis=-1, keepdims=True) + EPS) * ng
    return y * _silu(z)


def _short_conv(e_ref, w_ref, tt):
    off = SHORT_HALO - (SHORT_WIDTH - 1)
    acc = w_ref[0:1, :] * e_ref[off:off + tt, :]
    for j in range(1, SHORT_WIDTH):
        acc = acc + w_ref[j:j + 1, :] * e_ref[off + j:off + j + tt, :]
    return _silu(acc)


def _delta_prompt_kernel(qkv_ref, z_ref, gb_ref, st_ref, s0_ref, w_ref, ng_ref,
                         o_ref, nst_ref, sout_ref, e_ref, s_ref, *, tt):
    t = pl.program_id(1)

    hist = SHORT_WIDTH - 1

    @pl.when(t == 0)
    def _():
        e_ref[0:SUBLANES, :] = jnp.zeros((SUBLANES, 3 * DN_WIDTH), F32)
        e_ref[SHORT_HALO - hist:SHORT_HALO, :] = st_ref[0]
        s_ref[...] = s0_ref[0]

    e_ref[SHORT_HALO:SHORT_HALO + tt, :] = qkv_ref[...]
    qkv = _short_conv(e_ref, w_ref, tt)
    nst_ref[0] = e_ref[tt + SHORT_HALO - hist:tt + SHORT_HALO, :]
    e_ref[0:SHORT_HALO, :] = e_ref[tt:tt + SHORT_HALO, :]

    masks = _chunk_masks(CHUNK)
    tn = (((0,), (0,)), ((), ()))
    n_chunks = tt // CHUNK
    prep = _chunks_prepare([qkv[c * CHUNK:(c + 1) * CHUNK, :] for c in range(n_chunks)],
                           [gb_ref[c * CHUNK:(c + 1) * CHUNK, :] for c in range(n_chunks)], masks, CHUNK)
    heads = range(DN_HEADS)
    s = [s_ref[h] for h in heads]
    for c in range(n_chunks):
        r0 = c * CHUNK
        value, kcum, scores, qexp, kdec, egl = zip(*prep[c])
        both = [_dot_delta(jnp.concatenate([kcum[h], qexp[h]], axis=0), s[h]) for h in heads]
        v_new = [value[h] - both[h][:CHUNK] for h in heads]
        o = [both[h][CHUNK:] + _dot_delta(scores[h], v_new[h]) for h in heads]
        s = [s[h] * egl[h][0:1, :] + _dot_delta(kdec[h], v_new[h], tn) for h in heads]
        for h in heads:
            lanes = slice(h * DN_HEAD_DIM, (h + 1) * DN_HEAD_DIM)
            o_ref[r0:r0 + CHUNK, lanes] = _gated_out_norm(
                o[h], z_ref[r0:r0 + CHUNK, lanes], ng_ref[...]).astype(o_ref.dtype)
    for h in heads:
        s_ref[h] = s[h]
        sout_ref[0, h] = s[h]


def _delta_prompt(qkv_pre, z, gb, state8, s0, conv_w, norm_g, *, bsz, t_len, tt):
    n = bsz * t_len
    assert t_len % tt == 0 and tt % CHUNK == 0
    nt = t_len // tt
    kern = functools.partial(_delta_prompt_kernel, tt=tt)
    tile = lambda b, t: (b * nt + t, 0)
    per_b = lambda b, t: (b, 0, 0)
    per_b4 = lambda b, t: (b, 0, 0, 0)
    return pl.pallas_call(
        kern,
        grid=(bsz, nt),
        in_specs=[
            pl.BlockSpec((tt, 3 * DN_WIDTH), tile),
            pl.BlockSpec((tt, DN_WIDTH), tile),
            pl.BlockSpec((tt, LANES), tile),
            pl.BlockSpec((1, SHORT_WIDTH - 1, 3 * DN_WIDTH), per_b),
            pl.BlockSpec((1, DN_HEADS, DN_HEAD_DIM, DN_HEAD_DIM), per_b4),
            pl.BlockSpec((SHORT_WIDTH, 3 * DN_WIDTH), lambda b, t: (0, 0)),
            pl.BlockSpec((1, DN_HEAD_DIM), lambda b, t: (0, 0)),
        ],
        out_specs=[
            pl.BlockSpec((tt, DN_WIDTH), tile),
            pl.BlockSpec((1, SHORT_WIDTH - 1, 3 * DN_WIDTH), per_b),
            pl.BlockSpec((1, DN_HEADS, DN_HEAD_DIM, DN_HEAD_DIM), per_b4),
        ],
        out_shape=[
            jax.ShapeDtypeStruct((n, DN_WIDTH), BF16),
            jax.ShapeDtypeStruct((bsz, SHORT_WIDTH - 1, 3 * DN_WIDTH), F32),
            jax.ShapeDtypeStruct((bsz, DN_HEADS, DN_HEAD_DIM, DN_HEAD_DIM), F32),
        ],
        scratch_shapes=[
            pltpu.VMEM((SHORT_HALO + tt, 3 * DN_WIDTH), F32),
            pltpu.VMEM((DN_HEADS, DN_HEAD_DIM, DN_HEAD_DIM), F32),
        ],
        compiler_params=_params(2),
    )(qkv_pre, z, gb, state8, s0, conv_w, norm_g)


def _delta_sample_kernel(qkv_ref, z_ref, gb_ref, st_ref, s0_ref, w_ref, ng_ref, o_ref, nst_ref, sout_ref,
                         e_ref, *, nseq, seq_len):
    qkv_rows = []
    hist = SHORT_WIDTH - 1
    e_ref[0:SUBLANES, :] = jnp.zeros((SUBLANES, 3 * DN_WIDTH), F32)
    for b in range(nseq):
        e_ref[SHORT_HALO - hist:SHORT_HALO, :] = st_ref[b]
        e_ref[SHORT_HALO:SHORT_HALO + seq_len, :] = qkv_ref[b * seq_len:(b + 1) * seq_len, :]
        qkv_rows.append(_short_conv(e_ref, w_ref, seq_len))
        nst_ref[b] = e_ref[seq_len + SHORT_HALO - hist:seq_len + SHORT_HALO, :]
    qkv = jnp.concatenate(qkv_rows, axis=0)
    masks = _chunk_masks(seq_len)
    prep = _chunks_prepare([qkv], [gb_ref[...]], masks, seq_len)[0]
    tn = (((0,), (0,)), ((), ()))
    rows = [slice(b * seq_len, (b + 1) * seq_len) for b in range(nseq)]
    both = [[_dot_delta(jnp.concatenate([prep[h][1][r], prep[h][3][r]], axis=0), s0_ref[b, h])
             for b, r in enumerate(rows)] for h in range(DN_HEADS)]
    v_new = [[prep[h][0][r] - both[h][b][:seq_len] for b, r in enumerate(rows)] for h in range(DN_HEADS)]
    for h in range(DN_HEADS):
        kdec, egl = prep[h][4], prep[h][5]
        for b, r in enumerate(rows):
            sout_ref[b, h] = (s0_ref[b, h] * egl[b * seq_len:b * seq_len + 1, :]
                              + _dot_delta(kdec[r], v_new[h][b], tn))
    for h in range(DN_HEADS):
        o = (jnp.concatenate([both[h][b][seq_len:] for b in range(nseq)], axis=0)
             + _dot_delta(prep[h][2], jnp.concatenate(v_new[h], axis=0)))
        lanes = slice(h * DN_HEAD_DIM, (h + 1) * DN_HEAD_DIM)
        o_ref[:, lanes] = _gated_out_norm(o, z_ref[:, lanes], ng_ref[...]).astype(o_ref.dtype)


def _delta_sample(qkv_pre, z, gb, state8, s0, conv_w, norm_g, *, row0, bsz, seq_len):
    n = bsz * seq_len
    assert CHUNK % seq_len == 0
    nseq = CHUNK // seq_len
    assert bsz % nseq == 0 and row0 % CHUNK == 0
    blk0 = row0 // CHUNK
    kern = functools.partial(_delta_sample_kernel, nseq=nseq, seq_len=seq_len)
    tile = lambda i: (blk0 + i, 0)
    blk3 = lambda i: (i, 0, 0)
    blk4 = lambda i: (i, 0, 0, 0)
    return pl.pallas_call(
        kern,
        grid=(bsz // nseq,),
        in_specs=[
            pl.BlockSpec((CHUNK, 3 * DN_WIDTH), tile),
            pl.BlockSpec((CHUNK, DN_WIDTH), tile),
            pl.BlockSpec((CHUNK, LANES), tile),
            pl.BlockSpec((nseq, SHORT_WIDTH - 1, 3 * DN_WIDTH), blk3),
            pl.BlockSpec((nseq, DN_HEADS, DN_HEAD_DIM, DN_HEAD_DIM), blk4),
            pl.BlockSpec((SHORT_WIDTH, 3 * DN_WIDTH), lambda i: (0, 0)),
            pl.BlockSpec((1, DN_HEAD_DIM), lambda i: (0, 0)),
        ],
        out_specs=[
            pl.BlockSpec((CHUNK, DN_WIDTH), lambda i: (i, 0)),
            pl.BlockSpec((nseq, SHORT_WIDTH - 1, 3 * DN_WIDTH), blk3),
            pl.BlockSpec((nseq, DN_HEADS, DN_HEAD_DIM, DN_HEAD_DIM), blk4),
        ],
        out_shape=[
            jax.ShapeDtypeStruct((n, DN_WIDTH), BF16),
            jax.ShapeDtypeStruct((bsz, SHORT_WIDTH - 1, 3 * DN_WIDTH), F32),
            jax.ShapeDtypeStruct((bsz, DN_HEADS, DN_HEAD_DIM, DN_HEAD_DIM), F32),
        ],
        scratch_shapes=[pltpu.VMEM((SHORT_HALO + seq_len, 3 * DN_WIDTH), F32)],
        compiler_params=_params(1),
    )(qkv_pre, z, gb, state8, s0, conv_w, norm_g)


def _mix_kernel(xp_ref, xs_ref, cap_ref, cas_ref, oap_ref, oas_ref, sa_ref, sb_ref, wc_ref, wd_ref, wm_ref,
                g2_ref, rw_ref, rb_ref, x2_ref, tr_ref, *, n_prompt_tiles):
    is_prompt = pl.program_id(0) < n_prompt_tiles
    x = jnp.where(is_prompt, xp_ref[...], xs_ref[...])
    ca = jnp.where(is_prompt, cap_ref[...], cas_ref[...])
    oa = jnp.where(is_prompt, oap_ref[...], oas_ref[...])
    tokens = x.shape[0] // MIX_SPLIT
    parts = [slice(p * tokens, (p + 1) * tokens) for p in range(MIX_SPLIT)]
    ya = [jnp.dot(ca[r], wc_ref[...], preferred_element_type=F32) for r in parts]
    yb = [jnp.dot(oa[r], wd_ref[...], preferred_element_type=F32) for r in parts]
    mixed = [(sa_ref[r, :] * a + sb_ref[r, :] * b).astype(BF16) for r, a, b in zip(parts, ya, yb)]
    x2 = [x[r] + jnp.dot(m, wm_ref[...], preferred_element_type=F32) for r, m in zip(parts, mixed)]
    for r, v in zip(parts, x2):
        x2_ref[r, :] = v
    h2 = [v * lax.rsqrt(jnp.mean(v * v, axis=-1, keepdims=True) + EPS) * g2_ref[...] for v in x2]
    logits = [_dot_split(h, rw_ref[...]) + rb_ref[...] for h in h2]
    for r, lg in zip(parts, logits):
        lt = lg.T[:N_EXPERTS, :]
        row = lax.broadcasted_iota(jnp.int32, lt.shape, 0).astype(F32)
        top_vals, top_idx = [], []
        for k in range(TOP_K):
            m = jnp.max(lt, axis=0, keepdims=True)
            idx = jnp.min(jnp.where(lt == m, row, float(N_EXPERTS)), axis=0, keepdims=True)
            top_vals.append(m)
            top_idx.append(idx)
            lt = jnp.where(row == idx, -jnp.inf, lt)
        exps = [jnp.exp(v - top_vals[0]) for v in top_vals]
        den = exps[0] + exps[1] + exps[2] + exps[3]
        slot = lax.broadcasted_iota(jnp.int32, (2 * TOP_K, tokens), 0)
        packed = jnp.zeros((2 * TOP_K, tokens), F32)
        for k in range(TOP_K):
            packed = jnp.where(slot == k, top_idx[k], packed)
            packed = jnp.where(slot == TOP_K + k, exps[k] / den, packed)
        packed = jnp.concatenate([packed, jnp.zeros((LANES - 2 * TOP_K, tokens), F32)], axis=0)
        tr_ref[r, :] = packed.T


def _mix(x_p, x_s, cact_p, cact_s, oact_p, oact_s, siga, sigb, w_conv_out, w_delta_out, w_merge_out, norm2_g,
         router_w, router_b, tm):
    (n_p, d), n_s = x_p.shape, x_s.shape[0]
    n = n_p + n_s
    row = lambda i: (i, 0)
    const = lambda i: (0, 0)
    return pl.pallas_call(
        functools.partial(_mix_kernel, n_prompt_tiles=n_p // tm),
        grid=(n // tm,),
        in_specs=[
            *_two_source_specs(tm, d, n_p // tm),
            *_two_source_specs(tm, D_CONV, n_p // tm),
            *_two_source_specs(tm, DN_WIDTH, n_p // tm),
            pl.BlockSpec((tm, d), row),
            pl.BlockSpec((tm, d), row),
            pl.BlockSpec((D_CONV, d), const),
            pl.BlockSpec((DN_WIDTH, d), const),
            pl.BlockSpec((d, d), const),
            pl.BlockSpec((1, d), const),
            pl.BlockSpec((d, 2 * LANES), const),
            pl.BlockSpec((1, LANES), const),
        ],
        out_specs=[
            pl.BlockSpec((tm, d), row),
            pl.BlockSpec((tm, LANES), row),
        ],
        out_shape=[
            jax.ShapeDtypeStruct((n, d), F32),
            jax.ShapeDtypeStruct((n, LANES), F32),
        ],
        compiler_params=_params(1),
    )(x_p, x_s, cact_p, cact_s, oact_p, oact_s, siga, sigb, w_conv_out, w_delta_out, w_merge_out, norm2_g,
      router_w, router_b)


def _fill_rows_per_step(n_fill, n_steps):
    per_step = SUBLANES
    while per_step * n_steps < n_fill:
        per_step *= 2
    assert n_fill % per_step == 0
    return per_step


def _route(top_idx, tm, n_blocks, n_steps):
    n = top_idx.shape[0]
    n_fill = n_blocks * tm - n * TOP_K
    assert n_fill == N_EXPERTS * tm
    flat_e = top_idx.reshape(-1)
    experts = jnp.arange(N_EXPERTS, dtype=jnp.int32)
    onehot = (flat_e[:, None] == experts[None, :]).astype(jnp.int32)
    csum = jnp.cumsum(onehot, axis=0)
    rank = jnp.sum(csum * onehot, axis=1) - 1
    counts = csum[-1]
    padded = (counts + tm - 1) // tm * tm
    pad_end = jnp.cumsum(padded)
    pad_start = pad_end - padded
    dest = jnp.sum(onehot * pad_start[None, :], axis=1) + rank
    nvalid = (pad_end[-1] // tm).astype(jnp.int32)
    blk = jnp.arange(n_blocks, dtype=jnp.int32)
    owner = jnp.sum((pad_end[None, :] <= (blk * tm)[:, None]).astype(jnp.int32), axis=1)
    block_e = jnp.minimum(owner, N_EXPERTS - 1)
    block_e = jnp.where(blk < nvalid, block_e, jnp.sum(jnp.where(blk == nvalid - 1, block_e, 0)))
    present = counts > 0
    later = present[None, :] & (experts[None, :] > experts[:, None])
    next_present = jnp.min(jnp.where(later, experts[None, :], N_EXPERTS), axis=1)
    next_present = jnp.where(next_present == N_EXPERTS, -1, next_present)
    parity = (jnp.cumsum(present.astype(jnp.int32)) - 1) % 2
    of_block = (block_e[:, None] == experts[None, :]).astype(jnp.int32)
    sched = jnp.stack([block_e, jnp.sum(of_block * next_present[None, :], axis=1),
                       jnp.sum(of_block * parity[None, :], axis=1)])
    n_pad = padded - counts
    spill = tm - n_pad
    spill_start = pad_end[-1] + jnp.cumsum(spill) - spill
    j = jnp.arange(tm, dtype=jnp.int32)[None, :]
    fill = jnp.where(j < n_pad[:, None], (pad_start + counts)[:, None] + j, (spill_start - n_pad)[:, None] + j)
    fill_step = _fill_rows_per_step(n_fill, n_steps)
    fill = jnp.pad(fill.reshape(-1, fill_step), ((0, n_steps - n_fill // fill_step), (0, 0)))
    table = jnp.concatenate([dest.reshape(n_steps, -1), fill], axis=1)
    return dest.reshape(n, TOP_K), table, sched, nvalid.reshape(1)


def _to_token_tiles(x, ref, row0=0):
    t = x.shape[0]
    for s in range(SUBLANES):
        ref[pl.ds(row0 + s, t, stride=SUBLANES), :] = x[:, s * LANES:(s + 1) * LANES]


def _from_token_tiles(ref, t, row0=0):
    return jnp.concatenate([ref[pl.ds(row0 + s, t, stride=SUBLANES), :] for s in range(SUBLANES)], axis=1)


def _tile(ref, row):
    return ref.at[pl.ds(pl.multiple_of(row * SUBLANES, SUBLANES), SUBLANES)]


def _dispatch_kernel(tab_ref, x2_ref, g2_ref, xs_ref, buf, sems, *, tokens, fill_step, n_fill_steps):
    i = pl.program_id(0)
    last = pl.num_programs(0) - 1
    slot = i % 2
    base = slot * tokens
    fill_sem = 2

    x2 = x2_ref[...]
    h2 = x2 * lax.rsqrt(jnp.mean(x2 * x2, axis=-1, keepdims=True) + EPS) * g2_ref[...]
    _to_token_tiles(h2, buf, pl.multiple_of(base * SUBLANES, SUBLANES))

    def wait_tiles(sem_idx, count):
        while count > 0:
            rows = min(count, tokens) * SUBLANES
            pltpu.make_async_copy(buf.at[pl.ds(0, rows)], xs_ref.at[pl.ds(0, rows)], sems.at[sem_idx]).wait()
            count -= min(count, tokens)

    for t in range(tokens):
        for k in range(TOP_K):
            pltpu.make_async_copy(_tile(buf, base + t), _tile(xs_ref, tab_ref[i, t * TOP_K + k]),
                                  sems.at[slot]).start(priority=k % DMA_PRIORITIES)

    @pl.when(i < n_fill_steps)
    def _():
        for p in range(fill_step):
            pltpu.make_async_copy(_tile(buf, base), _tile(xs_ref, tab_ref[i, tokens * TOP_K + p]),
                                  sems.at[fill_sem]).start(priority=p % DMA_PRIORITIES)
        wait_tiles(fill_sem, fill_step)

    @pl.when(i > 0)
    def _():
        wait_tiles(1 - slot, tokens * TOP_K)

    @pl.when(i == last)
    def _():
        wait_tiles(slot, tokens * TOP_K)


def _dispatch(x2, norm_g, table, n_rows, tokens):
    n, d = x2.shape
    assert d == SUBLANES * LANES
    fill_step = table.shape[1] - tokens * TOP_K
    n_fill_steps = (n_rows - n * TOP_K) // fill_step
    return pl.pallas_call(
        functools.partial(_dispatch_kernel, tokens=tokens, fill_step=fill_step, n_fill_steps=n_fill_steps),
        grid_spec=pltpu.PrefetchScalarGridSpec(
            num_scalar_prefetch=1,
            grid=(n // tokens,),
            in_specs=[pl.BlockSpec((tokens, d), lambda i, tab: (i, 0)),
                      pl.BlockSpec((1, d), lambda i, tab: (0, 0))],
            out_specs=pl.BlockSpec(memory_space=pl.ANY),
            scratch_shapes=[pltpu.VMEM((2 * tokens * SUBLANES, LANES), F32), pltpu.SemaphoreType.DMA((3,))],
        ),
        out_shape=jax.ShapeDtypeStruct((n_rows * SUBLANES, LANES), F32),
        compiler_params=_params(1),
    )(table, x2, norm_g)


def _moe_ffn_kernel(sched_ref, nv_ref, xs_ref, wgu_hbm, bgu_ref, wd_hbm, bd_ref, out_ref,
                    wgu_f32, wd_f32, wgu_bf, wd_bf, sems):
    i = pl.program_id(0)
    expert, next_expert, half = sched_ref[0, i], sched_ref[1, i], sched_ref[2, i]

    def weight_copies(e, h):
        return (pltpu.make_async_copy(wgu_hbm.at[e], wgu_f32.at[h], sems.at[h]),
                pltpu.make_async_copy(wd_hbm.at[e], wd_f32.at[h], sems.at[h]))

    @pl.when(i == 0)
    def _():
        for c in weight_copies(expert, half):
            c.start()

    @pl.when(jnp.logical_or(i == 0, expert != sched_ref[0, jnp.maximum(i - 1, 0)]))
    def _():
        for c in weight_copies(expert, half):
            c.wait()
        wgu_bf[...] = wgu_f32[half].astype(BF16)
        wd_bf[...] = wd_f32[half].astype(BF16)

        @pl.when(next_expert >= 0)
        def _():
            for c in weight_copies(next_expert, 1 - half):
                c.start()

    @pl.when(i < nv_ref[0])
    def _():
        f = wd_bf.shape[0]
        tm = xs_ref.shape[0] // SUBLANES
        sub = tm // FFN_SPLIT
        halves = [pl.multiple_of(h * sub * SUBLANES, SUBLANES) for h in range(FFN_SPLIT)]
        xs = [_from_token_tiles(xs_ref, sub, r0).astype(BF16) for r0 in halves]
        gus = [jnp.dot(x, wgu_bf[...], preferred_element_type=F32) + bgu_ref[0] for x in xs]
        acts = []
        for gu in gus:
            gt = jnp.minimum(gu[:, :f], SWIGLU_LIMIT)
            up = jnp.clip(gu[:, f:], -SWIGLU_LIMIT, SWIGLU_LIMIT)
            acts.append(((up + 1.0) * (gt * _sigmoid(SWIGLU_ALPHA * gt))).astype(BF16))
        ys = [jnp.dot(act, wd_bf[...], preferred_element_type=F32) + bd_ref[0] for act in acts]
        for r0, y in zip(halves, ys):
            _to_token_tiles(y, out_ref, r0)

    @pl.when(i >= nv_ref[0])
    def _():
        out_ref[...] = jnp.zeros(out_ref.shape, out_ref.dtype)


def _moe_ffn(xs, sched, nvalid, w_gate_up, b_gate_up, w_down, b_down, tm, n_blocks):
    ne, d, f2 = w_gate_up.shape
    f = f2 // 2
    used = lambda i, sc, nv: (jnp.minimum(i, nv[0] - 1), 0)
    return pl.pallas_call(
        _moe_ffn_kernel,
        grid_spec=pltpu.PrefetchScalarGridSpec(
            num_scalar_prefetch=2,
            grid=(n_blocks,),
            in_specs=[
                pl.BlockSpec((tm * SUBLANES, LANES), used),
                pl.BlockSpec(memory_space=pl.ANY),
                pl.BlockSpec((1, 1, f2), lambda i, sc, nv: (sc[0, i], 0, 0)),
                pl.BlockSpec(memory_space=pl.ANY),
                pl.BlockSpec((1, 1, d), lambda i, sc, nv: (sc[0, i], 0, 0)),
            ],
            out_specs=pl.BlockSpec((tm * SUBLANES, LANES), lambda i, sc, nv: (i, 0)),
            scratch_shapes=[pltpu.VMEM((2, d, f2), F32), pltpu.VMEM((2, f, d), F32),
                            pltpu.VMEM((d, f2), BF16), pltpu.VMEM((f, d), BF16), pltpu.SemaphoreType.DMA((2,))],
        ),
        out_shape=jax.ShapeDtypeStruct((n_blocks * tm * SUBLANES, LANES), F32),
        compiler_params=_params(1),
    )(sched, nvalid, xs, w_gate_up, b_gate_up.reshape(ne, 1, f2), w_down, b_down.reshape(ne, 1, d))


def _combine_kernel(slot_ref, yb_ref, x2_ref, tg_ref, fg_ref, yp_ref, ys_ref, buf, sems, *, tc, n_prompt_tiles):
    i = pl.program_id(0)
    n_steps = pl.num_programs(0)

    def region(slot, k):
        return (slot * TOP_K + k) * tc

    def fetch(step, slot):
        for t in range(tc):
            for k in range(TOP_K):
                pltpu.make_async_copy(
                    _tile(yb_ref, slot_ref[step, t * TOP_K + k]), _tile(buf, region(slot, k) + t),
                    sems.at[slot]).start(priority=k % DMA_PRIORITIES)

    @pl.when(i == 0)
    def _():
        fetch(0, 0)

    @pl.when(i + 1 < n_steps)
    def _():
        fetch(i + 1, (i + 1) % 2)

    slot = i % 2
    for k in range(TOP_K):
        pltpu.make_async_copy(yb_ref.at[pl.ds(0, tc * SUBLANES)], buf.at[pl.ds(0, tc * SUBLANES)],
                              sems.at[slot]).wait()
    tg = tg_ref[...]
    y = x2_ref[...]
    for k in range(TOP_K):
        rows = _from_token_tiles(buf, tc, pl.multiple_of(region(slot, k) * SUBLANES, SUBLANES))
        y = y + tg[:, TOP_K + k:TOP_K + k + 1] * rows
    out = y * lax.rsqrt(jnp.mean(y * y, axis=-1, keepdims=True) + EPS) * fg_ref[...]

    @pl.when(i < n_prompt_tiles)
    def _():
        yp_ref[...] = out

    @pl.when(i >= n_prompt_tiles)
    def _():
        ys_ref[...] = out


def _combine(slot2d, yb, x2, tg, final_g, n_p, tc):
    n, d = x2.shape
    n_s = n - n_p
    assert n_p % tc == 0 and n_s % tc == 0
    npt = n_p // tc
    kern = functools.partial(_combine_kernel, tc=tc, n_prompt_tiles=npt)
    out_p, out_s = _two_source_specs(tc, d, npt)
    return pl.pallas_call(
        kern,
        grid_spec=pltpu.PrefetchScalarGridSpec(
            num_scalar_prefetch=1,
            grid=(n // tc,),
            in_specs=[
                pl.BlockSpec(memory_space=pl.ANY),
                pl.BlockSpec((tc, d), lambda i, s: (i, 0)),
                pl.BlockSpec((tc, LANES), lambda i, s: (i, 0)),
                pl.BlockSpec((1, d), lambda i, s: (0, 0)),
            ],
            out_specs=[out_p, out_s],
            scratch_shapes=[pltpu.VMEM((2 * TOP_K * tc * SUBLANES, LANES), F32), pltpu.SemaphoreType.DMA((2,))],
        ),
        out_shape=[jax.ShapeDtypeStruct((n_p, d), F32), jax.ShapeDtypeStruct((n_s, d), F32)],
        compiler_params=_params(1),
    )(slot2d, yb, x2, tg, final_g)


def _pad_lanes(v, width=LANES):
    v = v.reshape(1, -1)
    return jnp.pad(v, ((0, 0), (0, width - v.shape[1])))


def kernel(x_prompt, x_sample, state_conv, state_short_conv, state_delta, norm1_g, w_in, conv_dw_w,
           conv_dw_b, conv_ln_g, conv_ln_b, w_conv_out, short_conv_w, a_log, dt_bias, delta_norm_g,
           w_delta_out, w_merge_out, norm2_g, router_w, router_b, w_gate_up, b_gate_up, w_down, b_down,
           final_norm_g):
    depth = w_in.shape[0]
    assert depth == 1
    bp, tp, d = x_prompt.shape
    bs, ts, _ = x_sample.shape
    n_p, n_s = bp * tp, bs * ts
    n = n_p + n_s
    l = 0
    x_p = x_prompt.reshape(n_p, d)
    x_s = x_sample.reshape(n_s, d)

    o_ab = 2 * D_CONV + 4 * DN_WIDTH
    w = w_in[l]
    w_main = jnp.concatenate([w[:, :o_ab], w[:, o_ab + 2 * DN_HEADS:]], axis=1).astype(BF16)
    w_ab = _split_bf16(jnp.pad(w[:, o_ab:o_ab + 2 * DN_HEADS], ((0, 0), (0, LANES - 2 * DN_HEADS))))

    glu, qkv_pre, z, gb, siga, sigb = _inproj(
        x_p, x_s, norm1_g[l].reshape(1, d), w_main, w_ab, _pad_lanes(a_log[l]), _pad_lanes(dt_bias[l]), TOKEN_TILE)

    dw = (conv_dw_w[l], conv_dw_b[l].reshape(1, -1), conv_ln_g[l].reshape(1, -1), conv_ln_b[l].reshape(1, -1))
    st_c_p = jnp.zeros((bp, CONV_WIDTH - 1, D_CONV), F32)
    seq_tile = SEQ_TILE if tp % SEQ_TILE == 0 else TOKEN_TILE
    cact_p, conv_p = _conv_branch(glu, st_c_p, *dw, row0=0, bsz=bp, t_len=tp, bb=1, tt=seq_tile)
    cact_s, conv_s = _conv_branch(glu, state_conv[l], *dw, row0=n_p, bsz=bs, t_len=ts, bb=8, tt=ts)

    st_s_p = jnp.zeros((bp, SHORT_WIDTH - 1, 3 * DN_WIDTH), F32)
    s0_p = jnp.zeros((bp, DN_HEADS, DN_HEAD_DIM, DN_HEAD_DIM), F32)
    ng = delta_norm_g[l].reshape(1, -1)
    oact_p, short_p, s_p = _delta_prompt(qkv_pre, z, gb, st_s_p, s0_p, short_conv_w[l], ng,
                                         bsz=bp, t_len=tp, tt=seq_tile)
    oact_s, short_s, s_s = _delta_sample(qkv_pre, z, gb, state_short_conv[l], state_delta[l], short_conv_w[l], ng,
                                         row0=n_p, bsz=bs, seq_len=ts)

    rw = _split_bf16(jnp.pad(router_w[l], ((0, 0), (0, LANES - N_EXPERTS))))
    x2, tg = _mix(x_p, x_s, cact_p, cact_s, oact_p, oact_s, siga, sigb, w_conv_out[l].astype(BF16),
                  w_delta_out[l].astype(BF16), w_merge_out[l].astype(BF16), norm2_g[l].reshape(1, d),
                  rw, _pad_lanes(router_b[l]),
                  MIX_TILE if n_p % MIX_TILE == 0 and n_s % MIX_TILE == 0 else TOKEN_TILE)

    n_blocks = -(-(n * TOP_K) // MOE_TILE) + N_EXPERTS
    n_steps = n // TOKEN_TILE
    dest, table, sched, nvalid = _route(tg[:, :TOP_K].astype(jnp.int32), MOE_TILE, n_blocks, n_steps)
    xs = _dispatch(x2, norm2_g[l].reshape(1, d), table, n_blocks * MOE_TILE, TOKEN_TILE)
    yb = _moe_ffn(xs, sched, nvalid, w_gate_up[l], b_gate_up[l], w_down[l], b_down[l], MOE_TILE, n_blocks)
    y_p, y_s = _combine(dest.reshape(n_steps, -1), yb, x2, tg, final_norm_g.reshape(1, d), n_p, TOKEN_TILE)

    return (y_p.reshape(bp, tp, d), y_s.reshape(bs, ts, d), conv_p[None], short_p[None], s_p[None],
            conv_s[None], short_s[None], s_s[None])
```

```python
import functools

import jax
import jax.numpy as jnp
from jax import lax
from jax.experimental import pallas as pl
from jax.experimental.pallas import tpu as pltpu

F32 = jnp.float32
BF16 = jnp.bfloat16
EPS = 1e-6

LANES = 128
SUBLANES = 8
VMEM_LIMIT_BYTES = 56 * 1024 * 1024
DMA_PRIORITIES = 2

D_CONV = 512
CONV_WIDTH = 31
DN_HEADS = 4
DN_HEAD_DIM = 128
DN_WIDTH = DN_HEADS * DN_HEAD_DIM
SHORT_WIDTH = 4
N_EXPERTS = 32
TOP_K = 4
SWIGLU_LIMIT = 7.0
SWIGLU_ALPHA = 1.702

CHUNK = 128
CONV_HALO = 32
SHORT_HALO = 8

MIX_TILE = 512
TOKEN_TILE = 256
SEQ_TILE = 512
MOE_TILE = 256
FFN_BLOCKS_PER_STEP = 2


def _sigmoid(x):
    return 1.0 / (1.0 + jnp.exp(-x))


def _silu(x):
    return x * _sigmoid(x)


def _split_bf16(w):
    hi = w.astype(BF16)
    lo = (w - hi.astype(F32)).astype(BF16)
    return jnp.concatenate([hi, lo], axis=-1)


def _dot_split(x, w_split):
    n = w_split.shape[-1] // 2
    x_hi = x.astype(BF16)
    x_lo = (x - x_hi.astype(F32)).astype(BF16)
    r = jnp.dot(x_hi, w_split, preferred_element_type=F32)
    return r[:, :n] + r[:, n:] + jnp.dot(x_lo, w_split[:, :n], preferred_element_type=F32)


def _dot_delta(a, b, dims=(((1,), (0,)), ((), ()))):
    return lax.dot_general(a.astype(BF16), b.astype(BF16), dims, preferred_element_type=F32)


def _params(n_axes):
    return pltpu.CompilerParams(dimension_semantics=("arbitrary",) * n_axes, vmem_limit_bytes=VMEM_LIMIT_BYTES)


def _two_source_specs(tm, d, n_first_tiles):
    first = pl.BlockSpec((tm, d), lambda i, *_: (jnp.minimum(i, n_first_tiles - 1), 0))
    second = pl.BlockSpec((tm, d), lambda i, *_: (jnp.maximum(i - n_first_tiles, 0), 0))
    return first, second


def _inproj_kernel(xp_ref, xs_ref, g_ref, w_ref, wab_ref, alog_ref, dtb_ref,
                   glu_ref, qkv_ref, z_ref, gb_ref, sa_ref, sb_ref, *, n_prompt_tiles):
    x = jnp.where(pl.program_id(0) < n_prompt_tiles, xp_ref[...], xs_ref[...])
    h = x * lax.rsqrt(jnp.mean(x * x, axis=-1, keepdims=True) + EPS) * g_ref[...]
    hb = h.astype(BF16)

    def mm(lo, hi):
        return jnp.dot(hb, w_ref[:, lo:hi], preferred_element_type=F32)

    o_gate, o_qkv, o_z = D_CONV, 2 * D_CONV, 2 * D_CONV + 3 * DN_WIDTH
    o_ga = o_z + DN_WIDTH
    d = x.shape[-1]
    glu_ref[...] = mm(0, o_gate) * _sigmoid(mm(o_gate, o_qkv))
    qkv_ref[...] = mm(o_qkv, o_z)
    z_ref[...] = mm(o_z, o_ga)
    sa_ref[...] = _sigmoid(mm(o_ga, o_ga + d))
    sb_ref[...] = _sigmoid(mm(o_ga + d, o_ga + 2 * d))
    ab = _dot_split(h, wab_ref[...])
    xa = ab + dtb_ref[...]
    softplus = jnp.maximum(xa, 0.0) + jnp.log(1.0 + jnp.exp(-jnp.abs(xa)))
    g = -jnp.exp(alog_ref[...]) * softplus
    lane = lax.broadcasted_iota(jnp.int32, ab.shape, 1)
    gb_ref[...] = jnp.where(lane < DN_HEADS, g, _sigmoid(ab))


def _inproj(x_p, x_s, norm_g, w_main, w_ab, alog, dtb, tm):
    (n_p, d), n_s = x_p.shape, x_s.shape[0]
    assert n_p % tm == 0 and n_s % tm == 0
    n = n_p + n_s
    wcols = w_main.shape[1]
    row = lambda i: (i, 0)
    const = lambda i: (0, 0)
    outs = [(D_CONV, F32), (3 * DN_WIDTH, F32), (DN_WIDTH, F32), (LANES, F32), (d, F32), (d, F32)]
    return pl.pallas_call(
        functools.partial(_inproj_kernel, n_prompt_tiles=n_p // tm),
        grid=(n // tm,),
        in_specs=[
            *_two_source_specs(tm, d, n_p // tm),
            pl.BlockSpec((1, d), const),
            pl.BlockSpec((d, wcols), const),
            pl.BlockSpec((d, 2 * LANES), const),
            pl.BlockSpec((1, LANES), const),
            pl.BlockSpec((1, LANES), const),
        ],
        out_specs=[pl.BlockSpec((tm, c), row) for c, _ in outs],
        out_shape=[jax.ShapeDtypeStruct((n, c), dt) for c, dt in outs],
        compiler_params=_params(1),
    )(x_p, x_s, norm_g, w_main, w_ab, alog, dtb)


def _conv_kernel(glu_ref, st_ref, w_ref, b_ref, lg_ref, lb_ref, out_ref, nst_ref, e_ref, sh_ref, *, bb, tt, rows):
    t = pl.program_id(1)

    hist = CONV_WIDTH - 1

    @pl.when(t == 0)
    def _():
        e_ref[:, 0:SUBLANES, :] = jnp.zeros((bb, SUBLANES, D_CONV), F32)
        e_ref[:, CONV_HALO - hist:CONV_HALO, :] = st_ref[...]

    for b in range(bb):
        e_ref[b, CONV_HALO:CONV_HALO + tt, :] = glu_ref[b * tt:(b + 1) * tt, :]
    off = CONV_HALO - (CONV_WIDTH - 1)
    span = sh_ref.shape[1]
    for b in range(bb):
        for s in range(1, SUBLANES):
            sh_ref[s - 1] = e_ref[b, s:s + span, :]
        for c in range(tt // rows):
            r0 = c * rows
            acc = jnp.zeros((rows, D_CONV), F32) + b_ref[...]
            for j in range(CONV_WIDTH):
                q, s = divmod(j + off, SUBLANES)
                lo = r0 + q * SUBLANES
                src = e_ref[b, lo:lo + rows, :] if s == 0 else sh_ref[s - 1, lo:lo + rows, :]
                acc = acc + w_ref[j:j + 1, :] * src
            mu = jnp.mean(acc, axis=-1, keepdims=True)
            xc = acc - mu
            var = jnp.mean(xc * xc, axis=-1, keepdims=True)
            y = xc * lax.rsqrt(var + EPS) * lg_ref[...] + lb_ref[...]
            out_ref[b * tt + r0:b * tt + r0 + rows, :] = _silu(y).astype(out_ref.dtype)
    nst_ref[...] = e_ref[:, tt + CONV_HALO - hist:tt + CONV_HALO, :]
    e_ref[:, 0:CONV_HALO, :] = e_ref[:, tt:tt + CONV_HALO, :]


def _conv_branch(glu, state32, dw_w, dw_b, ln_g, ln_b, *, row0, bsz, t_len, bb, tt):
    c = glu.shape[1]
    assert bsz % bb == 0 and t_len % tt == 0 and row0 % (bb * tt) == 0
    nt = t_len // tt
    blk0 = row0 // (bb * tt)
    rows = min(tt, 32)
    kern = functools.partial(_conv_kernel, bb=bb, tt=tt, rows=rows)
    const = lambda b, t: (0, 0)
    return pl.pallas_call(
        kern,
        grid=(bsz // bb, nt),
        in_specs=[
            pl.BlockSpec((bb * tt, c), lambda b, t: (blk0 + b * nt + t, 0)),
            pl.BlockSpec((bb, CONV_WIDTH - 1, c), lambda b, t: (b, 0, 0)),
            pl.BlockSpec((CONV_WIDTH, c), const),
            pl.BlockSpec((1, c), const),
            pl.BlockSpec((1, c), const),
            pl.BlockSpec((1, c), const),
        ],
        out_specs=[
            pl.BlockSpec((bb * tt, c), lambda b, t: (b * nt + t, 0)),
            pl.BlockSpec((bb, CONV_WIDTH - 1, c), lambda b, t: (b, 0, 0)),
        ],
        out_shape=[
            jax.ShapeDtypeStruct((bsz * t_len, c), BF16),
            jax.ShapeDtypeStruct((bsz, CONV_WIDTH - 1, c), F32),
        ],
        scratch_shapes=[pltpu.VMEM((bb, CONV_HALO + tt, c), F32),
                        pltpu.VMEM((SUBLANES - 1, tt + CONV_HALO - SUBLANES, c), F32)],
        compiler_params=_params(2),
    )(glu, state32, dw_w, dw_b, ln_g, ln_b)


def _chunk_masks(seq_len):
    i = lax.broadcasted_iota(jnp.int32, (CHUNK, CHUNK), 0)
    j = lax.broadcasted_iota(jnp.int32, (CHUNK, CHUNK), 1)
    same = (i // seq_len) == (j // seq_len)
    incl = same & (i >= j)
    strict = same & (i > j)
    last = j == (i // seq_len) * seq_len + (seq_len - 1)
    levels = []
    blk = 1
    while blk < seq_len:
        levels.append(((i // (2 * blk)) == (j // (2 * blk))) & (((i // blk) % 2) == 1) & (((j // blk) % 2) == 0))
        blk *= 2
    eye = i == j
    return incl, strict, last, levels, eye


def _lane_col(x, lane):
    return jnp.broadcast_to(x[:, lane:lane + 1], (x.shape[0], LANES))


def _l2norm(x):
    return x * lax.rsqrt(jnp.sum(x * x, axis=-1, keepdims=True) + EPS)


def _select_sum(mask01, x):
    hi = x.astype(BF16)
    r1 = x - hi.astype(F32)
    mid = r1.astype(BF16)
    lo = (r1 - mid.astype(F32)).astype(BF16)
    w = x.shape[1]
    parts = jnp.dot(mask01, jnp.concatenate([hi, mid, lo], axis=1), preferred_element_type=F32)
    return parts[:, :w] + parts[:, w:2 * w] + parts[:, 2 * w:]


def _chunks_prepare(qkvs, gbts, masks, seq_len):
    incl, strict, last, levels, eye = masks
    nt = (((1,), (1,)), ((), ()))
    lower01 = jnp.where(incl, 1.0, 0.0).astype(BF16)
    probs = []
    for qkv, gbt in zip(qkvs, gbts):
        gc = _select_sum(lower01, gbt)
        gct = gc.T
        if seq_len == CHUNK:
            glast = jnp.broadcast_to(gc[CHUNK - 1:CHUNK, :], gc.shape)
        else:
            glast = _select_sum(jnp.where(last, 1.0, 0.0).astype(BF16), gc)
        for h in range(DN_HEADS):
            q = _l2norm(qkv[:, h * DN_HEAD_DIM:(h + 1) * DN_HEAD_DIM]) * (DN_HEAD_DIM ** -0.5)
            k = _l2norm(qkv[:, DN_WIDTH + h * DN_HEAD_DIM:DN_WIDTH + (h + 1) * DN_HEAD_DIM])
            v = qkv[:, 2 * DN_WIDTH + h * DN_HEAD_DIM:2 * DN_WIDTH + (h + 1) * DN_HEAD_DIM]
            gcol = _lane_col(gc, h)
            grow = jnp.broadcast_to(gct[h:h + 1, :], (CHUNK, CHUNK))
            beta = _lane_col(gbt, DN_HEADS + h)
            gl = _lane_col(glast, h)
            decay = jnp.exp(jnp.where(incl, gcol - grow, -jnp.inf))
            egc = jnp.exp(gcol)
            kb = k * beta
            probs.append(dict(q=q, k=k, kb=kb, decay=decay, rhs=jnp.concatenate([v * beta, kb * egc], axis=1),
                              qexp=q * egc, kdec=k * jnp.exp(gl - gcol), egl=jnp.exp(gl)))
    for p in probs:
        p['a'] = jnp.where(strict, _dot_delta(p['kb'], p['k'], nt) * p['decay'], 0.0)
        p['scores'] = _dot_delta(p['q'], p['k'], nt) * p['decay']
    for p in probs:
        p['x'] = jnp.where(eye, 1.0, 0.0) - jnp.where(levels[0], p['a'], 0.0)
    for m in levels[1:]:
        for p in probs:
            p['xa'] = _dot_delta(p['x'], jnp.where(m, p['a'], 0.0))
        for p in probs:
            p['x'] = p['x'] - _dot_delta(p['xa'], p['x'])
    out = []
    for c in range(len(qkvs)):
        heads = []
        for h in range(DN_HEADS):
            p = probs[c * DN_HEADS + h]
            sol = _dot_delta(p['x'], p['rhs'])
            heads.append((sol[:, :DN_HEAD_DIM], sol[:, DN_HEAD_DIM:], p['scores'], p['qexp'], p['kdec'], p['egl']))
        out.append(heads)
    return out


def _gated_out_norm(o, z, ng):
    y = o * lax.rsqrt(jnp.mean(o * o, axis=-1, keepdims=True) + EPS) * ng
    return y * _silu(z)


def _short_conv(e_ref, w_ref, tt):
    off = SHORT_HALO - (SHORT_WIDTH - 1)
    acc = w_ref[0:1, :] * e_ref[off:off + tt, :]
    for j in range(1, SHORT_WIDTH):
        acc = acc + w_ref[j:j + 1, :] * e_ref[off + j:off + j + tt, :]
    return _silu(acc)


def _delta_prompt_kernel(qkv_ref, z_ref, gb_ref, st_ref, s0_ref, w_ref, ng_ref,
                         o_ref, nst_ref, sout_ref, e_ref, s_ref, *, tt):
    t = pl.program_id(1)

    hist = SHORT_WIDTH - 1

    @pl.when(t == 0)
    def _():
        e_ref[0:SUBLANES, :] = jnp.zeros((SUBLANES, 3 * DN_WIDTH), F32)
        e_ref[SHORT_HALO - hist:SHORT_HALO, :] = st_ref[0]
        s_ref[...] = s0_ref[0]

    e_ref[SHORT_HALO:SHORT_HALO + tt, :] = qkv_ref[...]
    qkv = _short_conv(e_ref, w_ref, tt)
    nst_ref[0] = e_ref[tt + SHORT_HALO - hist:tt + SHORT_HALO, :]
    e_ref[0:SHORT_HALO, :] = e_ref[tt:tt + SHORT_HALO, :]

    masks = _chunk_masks(CHUNK)
    tn = (((0,), (0,)), ((), ()))
    n_chunks = tt // CHUNK
    prep = _chunks_prepare([qkv[c * CHUNK:(c + 1) * CHUNK, :] for c in range(n_chunks)],
                           [gb_ref[c * CHUNK:(c + 1) * CHUNK, :] for c in range(n_chunks)], masks, CHUNK)
    heads = range(DN_HEADS)
    s = [s_ref[h] for h in heads]
    for c in range(n_chunks):
        r0 = c * CHUNK
        value, kcum, scores, qexp, kdec, egl = zip(*prep[c])
        both = [_dot_delta(jnp.concatenate([kcum[h], qexp[h]], axis=0), s[h]) for h in heads]
        v_new = [value[h] - both[h][:CHUNK] for h in heads]
        o = [both[h][CHUNK:] + _dot_delta(scores[h], v_new[h]) for h in heads]
        s = [s[h] * egl[h][0:1, :] + _dot_delta(kdec[h], v_new[h], tn) for h in heads]
        for h in heads:
            lanes = slice(h * DN_HEAD_DIM, (h + 1) * DN_HEAD_DIM)
            o_ref[r0:r0 + CHUNK, lanes] = _gated_out_norm(
                o[h], z_ref[r0:r0 + CHUNK, lanes], ng_ref[...]).astype(o_ref.dtype)
    for h in heads:
        s_ref[h] = s[h]
        sout_ref[0, h] = s[h]


def _delta_prompt(qkv_pre, z, gb, state8, s0, conv_w, norm_g, *, bsz, t_len, tt):
    n = bsz * t_len
    assert t_len % tt == 0 and tt % CHUNK == 0
    nt = t_len // tt
    kern = functools.partial(_delta_prompt_kernel, tt=tt)
    tile = lambda b, t: (b * nt + t, 0)
    per_b = lambda b, t: (b, 0, 0)
    per_b4 = lambda b, t: (b, 0, 0, 0)
    return pl.pallas_call(
        kern,
        grid=(bsz, nt),
        in_specs=[
            pl.BlockSpec((tt, 3 * DN_WIDTH), tile),
            pl.BlockSpec((tt, DN_WIDTH), tile),
            pl.BlockSpec((tt, LANES), tile),
            pl.BlockSpec((1, SHORT_WIDTH - 1, 3 * DN_WIDTH), per_b),
            pl.BlockSpec((1, DN_HEADS, DN_HEAD_DIM, DN_HEAD_DIM), per_b4),
            pl.BlockSpec((SHORT_WIDTH, 3 * DN_WIDTH), lambda b, t: (0, 0)),
            pl.BlockSpec((1, DN_HEAD_DIM), lambda b, t: (0, 0)),
        ],
        out_specs=[
            pl.BlockSpec((tt, DN_WIDTH), tile),
            pl.BlockSpec((1, SHORT_WIDTH - 1, 3 * DN_WIDTH), per_b),
            pl.BlockSpec((1, DN_HEADS, DN_HEAD_DIM, DN_HEAD_DIM), per_b4),
        ],
        out_shape=[
            jax.ShapeDtypeStruct((n, DN_WIDTH), BF16),
            jax.ShapeDtypeStruct((bsz, SHORT_WIDTH - 1, 3 * DN_WIDTH), F32),
            jax.ShapeDtypeStruct((bsz, DN_HEADS, DN_HEAD_DIM, DN_HEAD_DIM), F32),
        ],
        scratch_shapes=[
            pltpu.VMEM((SHORT_HALO + tt, 3 * DN_WIDTH), F32),
            pltpu.VMEM((DN_HEADS, DN_HEAD_DIM, DN_HEAD_DIM), F32),
        ],
        compiler_params=_params(2),
    )(qkv_pre, z, gb, state8, s0, conv_w, norm_g)


def _delta_sample_kernel(qkv_ref, z_ref, gb_ref, st_ref, s0_ref, w_ref, ng_ref, o_ref, nst_ref, sout_ref,
                         e_ref, *, nseq, seq_len):
    qkv_rows = []
    hist = SHORT_WIDTH - 1
    e_ref[0:SUBLANES, :] = jnp.zeros((SUBLANES, 3 * DN_WIDTH), F32)
    for b in range(nseq):
        e_ref[SHORT_HALO - hist:SHORT_HALO, :] = st_ref[b]
        e_ref[SHORT_HALO:SHORT_HALO + seq_len, :] = qkv_ref[b * seq_len:(b + 1) * seq_len, :]
        qkv_rows.append(_short_conv(e_ref, w_ref, seq_len))
        nst_ref[b] = e_ref[seq_len + SHORT_HALO - hist:seq_len + SHORT_HALO, :]
    qkv = jnp.concatenate(qkv_rows, axis=0)
    masks = _chunk_masks(seq_len)
    prep = _chunks_prepare([qkv], [gb_ref[...]], masks, seq_len)[0]
    tn = (((0,), (0,)), ((), ()))
    rows = [slice(b * seq_len, (b + 1) * seq_len) for b in range(nseq)]
    both = [[_dot_delta(jnp.concatenate([prep[h][1][r], prep[h][3][r]], axis=0), s0_ref[b, h])
             for b, r in enumerate(rows)] for h in range(DN_HEADS)]
    v_new = [[prep[h][0][r] - both[h][b][:seq_len] for b, r in enumerate(rows)] for h in range(DN_HEADS)]
    for h in range(DN_HEADS):
        kdec, egl = prep[h][4], prep[h][5]
        for b, r in enumerate(rows):
            sout_ref[b, h] = (s0_ref[b, h] * egl[b * seq_len:b * seq_len + 1, :]
                              + _dot_delta(kdec[r], v_new[h][b], tn))
    for h in range(DN_HEADS):
        o = (jnp.concatenate([both[h][b][seq_len:] for b in range(nseq)], axis=0)
             + _dot_delta(prep[h][2], jnp.concatenate(v_new[h], axis=0)))
        lanes = slice(h * DN_HEAD_DIM, (h + 1) * DN_HEAD_DIM)
        o_ref[:, lanes] = _gated_out_norm(o, z_ref[:, lanes], ng_ref[...]).astype(o_ref.dtype)


def _delta_sample(qkv_pre, z, gb, state8, s0, conv_w, norm_g, *, row0, bsz, seq_len):
    n = bsz * seq_len
    assert CHUNK % seq_len == 0
    nseq = CHUNK // seq_len
    assert bsz % nseq == 0 and row0 % CHUNK == 0
    blk0 = row0 // CHUNK
    kern = functools.partial(_delta_sample_kernel, nseq=nseq, seq_len=seq_len)
    tile = lambda i: (blk0 + i, 0)
    blk3 = lambda i: (i, 0, 0)
    blk4 = lambda i: (i, 0, 0, 0)
    return pl.pallas_call(
        kern,
        grid=(bsz // nseq,),
        in_specs=[
            pl.BlockSpec((CHUNK, 3 * DN_WIDTH), tile),
            pl.BlockSpec((CHUNK, DN_WIDTH), tile),
            pl.BlockSpec((CHUNK, LANES), tile),
            pl.BlockSpec((nseq, SHORT_WIDTH - 1, 3 * DN_WIDTH), blk3),
            pl.BlockSpec((nseq, DN_HEADS, DN_HEAD_DIM, DN_HEAD_DIM), blk4),
            pl.BlockSpec((SHORT_WIDTH, 3 * DN_WIDTH), lambda i: (0, 0)),
            pl.BlockSpec((1, DN_HEAD_DIM), lambda i: (0, 0)),
        ],
        out_specs=[
            pl.BlockSpec((CHUNK, DN_WIDTH), lambda i: (i, 0)),
            pl.BlockSpec((nseq, SHORT_WIDTH - 1, 3 * DN_WIDTH), blk3),
            pl.BlockSpec((nseq, DN_HEADS, DN_HEAD_DIM, DN_HEAD_DIM), blk4),
        ],
        out_shape=[
            jax.ShapeDtypeStruct((n, DN_WIDTH), BF16),
            jax.ShapeDtypeStruct((bsz, SHORT_WIDTH - 1, 3 * DN_WIDTH), F32),
            jax.ShapeDtypeStruct((bsz, DN_HEADS, DN_HEAD_DIM, DN_HEAD_DIM), F32),
        ],
        scratch_shapes=[pltpu.VMEM((SHORT_HALO + seq_len, 3 * DN_WIDTH), F32)],
        compiler_params=_params(1),
    )(qkv_pre, z, gb, state8, s0, conv_w, norm_g)


def _mix_kernel(xp_ref, xs_ref, cap_ref, cas_ref, oap_ref, oas_ref, sa_ref, sb_ref, wc_ref, wd_ref, wm_ref,
                g2_ref, rw_ref, rb_ref, x2_ref, tr_ref, *, n_prompt_tiles):
    is_prompt = pl.program_id(0) < n_prompt_tiles
    x = jnp.where(is_prompt, xp_ref[...], xs_ref[...])
    ca = jnp.where(is_prompt, cap_ref[...], cas_ref[...])
    oa = jnp.where(is_prompt, oap_ref[...], oas_ref[...])
    ya = jnp.dot(ca, wc_ref[...], preferred_element_type=F32)
    yb = jnp.dot(oa, wd_ref[...], preferred_element_type=F32)
    mixed = sa_ref[...] * ya + sb_ref[...] * yb
    x2 = x + jnp.dot(mixed.astype(BF16), wm_ref[...], preferred_element_type=F32)
    x2_ref[...] = x2
    h2 = x2 * lax.rsqrt(jnp.mean(x2 * x2, axis=-1, keepdims=True) + EPS) * g2_ref[...]
    logits = _dot_split(h2, rw_ref[...]) + rb_ref[...]
    lt = logits.T[:N_EXPERTS, :]
    tokens = lt.shape[1]
    row = lax.broadcasted_iota(jnp.int32, lt.shape, 0).astype(F32)
    top_vals, top_idx = [], []
    for k in range(TOP_K):
        m = jnp.max(lt, axis=0, keepdims=True)
        idx = jnp.min(jnp.where(lt == m, row, float(N_EXPERTS)), axis=0, keepdims=True)
        top_vals.append(m)
        top_idx.append(idx)
        lt = jnp.where(row == idx, -jnp.inf, lt)
    exps = [jnp.exp(v - top_vals[0]) for v in top_vals]
    den = exps[0] + exps[1] + exps[2] + exps[3]
    slot = lax.broadcasted_iota(jnp.int32, (2 * TOP_K, tokens), 0)
    packed = jnp.zeros((2 * TOP_K, tokens), F32)
    for k in range(TOP_K):
        packed = jnp.where(slot == k, top_idx[k], packed)
        packed = jnp.where(slot == TOP_K + k, exps[k] / den, packed)
    packed = jnp.concatenate([packed, jnp.zeros((LANES - 2 * TOP_K, tokens), F32)], axis=0)
    tr_ref[...] = packed.T


def _mix(x_p, x_s, cact_p, cact_s, oact_p, oact_s, siga, sigb, w_conv_out, w_delta_out, w_merge_out, norm2_g,
         router_w, router_b, tm):
    (n_p, d), n_s = x_p.shape, x_s.shape[0]
    n = n_p + n_s
    row = lambda i: (i, 0)
    const = lambda i: (0, 0)
    return pl.pallas_call(
        functools.partial(_mix_kernel, n_prompt_tiles=n_p // tm),
        grid=(n // tm,),
        in_specs=[
            *_two_source_specs(tm, d, n_p // tm),
            *_two_source_specs(tm, D_CONV, n_p // tm),
            *_two_source_specs(tm, DN_WIDTH, n_p // tm),
            pl.BlockSpec((tm, d), row),
            pl.BlockSpec((tm, d), row),
            pl.BlockSpec((D_CONV, d), const),
            pl.BlockSpec((DN_WIDTH, d), const),
            pl.BlockSpec((d, d), const),
            pl.BlockSpec((1, d), const),
            pl.BlockSpec((d, 2 * LANES), const),
            pl.BlockSpec((1, LANES), const),
        ],
        out_specs=[
            pl.BlockSpec((tm, d), row),
            pl.BlockSpec((tm, LANES), row),
        ],
        out_shape=[
            jax.ShapeDtypeStruct((n, d), F32),
            jax.ShapeDtypeStruct((n, LANES), F32),
        ],
        compiler_params=_params(1),
    )(x_p, x_s, cact_p, cact_s, oact_p, oact_s, siga, sigb, w_conv_out, w_delta_out, w_merge_out, norm2_g,
      router_w, router_b)


def _fill_rows_per_step(n_fill, n_steps):
    per_step = SUBLANES
    while per_step * n_steps < n_fill:
        per_step *= 2
    assert n_fill % per_step == 0
    return per_step


def _route(top_idx, tm, n_blocks, n_steps):
    n = top_idx.shape[0]
    n_fill = n_blocks * tm - n * TOP_K
    assert n_fill == N_EXPERTS * tm
    flat_e = top_idx.reshape(-1)
    experts = jnp.arange(N_EXPERTS, dtype=jnp.int32)
    onehot = (flat_e[:, None] == experts[None, :]).astype(jnp.int32)
    csum = jnp.cumsum(onehot, axis=0)
    rank = jnp.sum(csum * onehot, axis=1) - 1
    counts = csum[-1]
    padded = (counts + tm - 1) // tm * tm
    pad_end = jnp.cumsum(padded)
    pad_start = pad_end - padded
    dest = jnp.sum(onehot * pad_start[None, :], axis=1) + rank
    nvalid = (pad_end[-1] // tm).astype(jnp.int32)
    blk = jnp.arange(n_blocks, dtype=jnp.int32)
    owner = jnp.sum((pad_end[None, :] <= (blk * tm)[:, None]).astype(jnp.int32), axis=1)
    block_e = jnp.minimum(owner, N_EXPERTS - 1)
    block_e = jnp.where(blk < nvalid, block_e, jnp.sum(jnp.where(blk == nvalid - 1, block_e, 0)))
    present = counts > 0
    later = present[None, :] & (experts[None, :] > experts[:, None])
    next_present = jnp.min(jnp.where(later, experts[None, :], N_EXPERTS), axis=1)
    next_present = jnp.where(next_present == N_EXPERTS, -1, next_present)
    parity = (jnp.cumsum(present.astype(jnp.int32)) - 1) % 2
    of_block = (block_e[:, None] == experts[None, :]).astype(jnp.int32)
    sched = jnp.stack([block_e, jnp.sum(of_block * next_present[None, :], axis=1),
                       jnp.sum(of_block * parity[None, :], axis=1)])
    n_pad = padded - counts
    spill = tm - n_pad
    spill_start = pad_end[-1] + jnp.cumsum(spill) - spill
    j = jnp.arange(tm, dtype=jnp.int32)[None, :]
    fill = jnp.where(j < n_pad[:, None], (pad_start + counts)[:, None] + j, (spill_start - n_pad)[:, None] + j)
    fill_step = _fill_rows_per_step(n_fill, n_steps)
    fill = jnp.pad(fill.reshape(-1, fill_step), ((0, n_steps - n_fill // fill_step), (0, 0)))
    table = jnp.concatenate([dest.reshape(n_steps, -1), fill], axis=1)
    return dest.reshape(n, TOP_K), table, sched, nvalid.reshape(1)


def _to_token_tiles(x, ref, row0=0):
    t = x.shape[0]
    for s in range(SUBLANES):
        ref[pl.ds(row0 + s, t, stride=SUBLANES), :] = x[:, s * LANES:(s + 1) * LANES]


def _from_token_tiles(ref, t, row0=0):
    return jnp.concatenate([ref[pl.ds(row0 + s, t, stride=SUBLANES), :] for s in range(SUBLANES)], axis=1)


def _tile(ref, row):
    return ref.at[pl.ds(pl.multiple_of(row * SUBLANES, SUBLANES), SUBLANES)]


def _dispatch_kernel(tab_ref, x2_ref, g2_ref, xs_ref, buf, sems, *, tokens, fill_step, n_fill_steps):
    i = pl.program_id(0)
    last = pl.num_programs(0) - 1
    slot = i % 2
    base = slot * tokens
    fill_sem = 2

    x2 = x2_ref[...]
    h2 = x2 * lax.rsqrt(jnp.mean(x2 * x2, axis=-1, keepdims=True) + EPS) * g2_ref[...]
    _to_token_tiles(h2, buf, pl.multiple_of(base * SUBLANES, SUBLANES))

    def wait_tiles(sem_idx, count):
        while count > 0:
            rows = min(count, tokens) * SUBLANES
            pltpu.make_async_copy(buf.at[pl.ds(0, rows)], xs_ref.at[pl.ds(0, rows)], sems.at[sem_idx]).wait()
            count -= min(count, tokens)

    for t in range(tokens):
        for k in range(TOP_K):
            pltpu.make_async_copy(_tile(buf, base + t), _tile(xs_ref, tab_ref[i, t * TOP_K + k]),
                                  sems.at[slot]).start(priority=k % DMA_PRIORITIES)

    @pl.when(i < n_fill_steps)
    def _():
        for p in range(fill_step):
            pltpu.make_async_copy(_tile(buf, base), _tile(xs_ref, tab_ref[i, tokens * TOP_K + p]),
                                  sems.at[fill_sem]).start(priority=p % DMA_PRIORITIES)
        wait_tiles(fill_sem, fill_step)

    @pl.when(i > 0)
    def _():
        wait_tiles(1 - slot, tokens * TOP_K)

    @pl.when(i == last)
    def _():
        wait_tiles(slot, tokens * TOP_K)


def _dispatch(x2, norm_g, table, n_rows, tokens):
    n, d = x2.shape
    assert d == SUBLANES * LANES
    fill_step = table.shape[1] - tokens * TOP_K
    n_fill_steps = (n_rows - n * TOP_K) // fill_step
    return pl.pallas_call(
        functools.partial(_dispatch_kernel, tokens=tokens, fill_step=fill_step, n_fill_steps=n_fill_steps),
        grid_spec=pltpu.PrefetchScalarGridSpec(
            num_scalar_prefetch=1,
            grid=(n // tokens,),
            in_specs=[pl.BlockSpec((tokens, d), lambda i, tab: (i, 0)),
                      pl.BlockSpec((1, d), lambda i, tab: (0, 0))],
            out_specs=pl.BlockSpec(memory_space=pl.ANY),
            scratch_shapes=[pltpu.VMEM((2 * tokens * SUBLANES, LANES), F32), pltpu.SemaphoreType.DMA((3,))],
        ),
        out_shape=jax.ShapeDtypeStruct((n_rows * SUBLANES, LANES), F32),
        compiler_params=_params(1),
    )(table, x2, norm_g)


def _moe_ffn_kernel(sched_ref, nv_ref, xs_ref, wgu_hbm, bgu_ref, wd_hbm, bd_ref, out_ref,
                    wgu_f32, wd_f32, wgu_bf, wd_bf, sems, *, blocks_per_step):
    tm = xs_ref.shape[0] // SUBLANES // blocks_per_step
    f = wd_bf.shape[0]

    def weight_copies(e, h):
        return (pltpu.make_async_copy(wgu_hbm.at[e], wgu_f32.at[h], sems.at[h]),
                pltpu.make_async_copy(wd_hbm.at[e], wd_f32.at[h], sems.at[h]))

    def one_block(i, r0):
        expert, next_expert, half = sched_ref[0, i], sched_ref[1, i], sched_ref[2, i]

        @pl.when(i == 0)
        def _():
            for c in weight_copies(expert, half):
                c.start()

        @pl.when(jnp.logical_or(i == 0, expert != sched_ref[0, jnp.maximum(i - 1, 0)]))
        def _():
            for c in weight_copies(expert, half):
                c.wait()
            wgu_bf[...] = wgu_f32[half].astype(BF16)
            wd_bf[...] = wd_f32[half].astype(BF16)

            @pl.when(next_expert >= 0)
            def _():
                for c in weight_copies(next_expert, 1 - half):
                    c.start()

        @pl.when(i < nv_ref[0])
        def _():
            x = _from_token_tiles(xs_ref, tm, r0).astype(BF16)
            gu = jnp.dot(x, wgu_bf[...], preferred_element_type=F32) + bgu_ref[expert]
            gt = jnp.minimum(gu[:, :f], SWIGLU_LIMIT)
            up = jnp.clip(gu[:, f:], -SWIGLU_LIMIT, SWIGLU_LIMIT)
            act = (up + 1.0) * (gt * _sigmoid(SWIGLU_ALPHA * gt))
            y = jnp.dot(act.astype(BF16), wd_bf[...], preferred_element_type=F32) + bd_ref[expert]
            _to_token_tiles(y, out_ref, r0)

        @pl.when(i >= nv_ref[0])
        def _():
            out_ref[r0:r0 + tm * SUBLANES, :] = jnp.zeros((tm * SUBLANES, LANES), out_ref.dtype)

    for h in range(blocks_per_step):
        one_block(pl.program_id(0) * blocks_per_step + h, h * tm * SUBLANES)


def _moe_ffn(xs, sched, nvalid, w_gate_up, b_gate_up, w_down, b_down, tm, n_blocks):
    ne, d, f2 = w_gate_up.shape
    f = f2 // 2
    per_step = FFN_BLOCKS_PER_STEP if n_blocks % FFN_BLOCKS_PER_STEP == 0 else 1
    rows = per_step * tm * SUBLANES
    used = lambda p, sc, nv: (jnp.minimum(p, (nv[0] - 1) // per_step), 0)
    whole = lambda p, sc, nv: (0, 0, 0)
    return pl.pallas_call(
        functools.partial(_moe_ffn_kernel, blocks_per_step=per_step),
        grid_spec=pltpu.PrefetchScalarGridSpec(
            num_scalar_prefetch=2,
            grid=(n_blocks // per_step,),
            in_specs=[
                pl.BlockSpec((rows, LANES), used),
                pl.BlockSpec(memory_space=pl.ANY),
                pl.BlockSpec((ne, 1, f2), whole),
                pl.BlockSpec(memory_space=pl.ANY),
                pl.BlockSpec((ne, 1, d), whole),
            ],
            out_specs=pl.BlockSpec((rows, LANES), lambda p, sc, nv: (p, 0)),
            scratch_shapes=[pltpu.VMEM((2, d, f2), F32), pltpu.VMEM((2, f, d), F32),
                            pltpu.VMEM((d, f2), BF16), pltpu.VMEM((f, d), BF16), pltpu.SemaphoreType.DMA((2,))],
        ),
        out_shape=jax.ShapeDtypeStruct((n_blocks * tm * SUBLANES, LANES), F32),
        compiler_params=_params(1),
    )(sched, nvalid, xs, w_gate_up, b_gate_up.reshape(ne, 1, f2), w_down, b_down.reshape(ne, 1, d))


def _combine_kernel(slot_ref, yb_ref, x2_ref, tg_ref, fg_ref, yp_ref, ys_ref, buf, sems, *, tc, n_prompt_tiles):
    i = pl.program_id(0)
    n_steps = pl.num_programs(0)

    def region(slot, k):
        return (slot * TOP_K + k) * tc

    def fetch(step, slot):
        for t in range(tc):
            for k in range(TOP_K):
                pltpu.make_async_copy(
                    _tile(yb_ref, slot_ref[step, t * TOP_K + k]), _tile(buf, region(slot, k) + t),
                    sems.at[slot]).start(priority=k % DMA_PRIORITIES)

    @pl.when(i == 0)
    def _():
        fetch(0, 0)

    @pl.when(i + 1 < n_steps)
    def _():
        fetch(i + 1, (i + 1) % 2)

    slot = i % 2
    for k in range(TOP_K):
        pltpu.make_async_copy(yb_ref.at[pl.ds(0, tc * SUBLANES)], buf.at[pl.ds(0, tc * SUBLANES)],
                              sems.at[slot]).wait()
    tg = tg_ref[...]
    y = x2_ref[...]
    for k in range(TOP_K):
        rows = _from_token_tiles(buf, tc, pl.multiple_of(region(slot, k) * SUBLANES, SUBLANES))
        y = y + tg[:, TOP_K + k:TOP_K + k + 1] * rows
    out = y * lax.rsqrt(jnp.mean(y * y, axis=-1, keepdims=True) + EPS) * fg_ref[...]

    @pl.when(i < n_prompt_tiles)
    def _():
        yp_ref[...] = out

    @pl.when(i >= n_prompt_tiles)
    def _():
        ys_ref[...] = out


def _combine(slot2d, yb, x2, tg, final_g, n_p, tc):
    n, d = x2.shape
    n_s = n - n_p
    assert n_p % tc == 0 and n_s % tc == 0
    npt = n_p // tc
    kern = functools.partial(_combine_kernel, tc=tc, n_prompt_tiles=npt)
    out_p, out_s = _two_source_specs(tc, d, npt)
    return pl.pallas_call(
        kern,
        grid_spec=pltpu.PrefetchScalarGridSpec(
            num_scalar_prefetch=1,
            grid=(n // tc,),
            in_specs=[
                pl.BlockSpec(memory_space=pl.ANY),
                pl.BlockSpec((tc, d), lambda i, s: (i, 0)),
                pl.BlockSpec((tc, LANES), lambda i, s: (i, 0)),
                pl.BlockSpec((1, d), lambda i, s: (0, 0)),
            ],
            out_specs=[out_p, out_s],
            scratch_shapes=[pltpu.VMEM((2 * TOP_K * tc * SUBLANES, LANES), F32), pltpu.SemaphoreType.DMA((2,))],
        ),
        out_shape=[jax.ShapeDtypeStruct((n_p, d), F32), jax.ShapeDtypeStruct((n_s, d), F32)],
        compiler_params=_params(1),
    )(slot2d, yb, x2, tg, final_g)


def _pad_lanes(v, width=LANES):
    v = v.reshape(1, -1)
    return jnp.pad(v, ((0, 0), (0, width - v.shape[1])))


def kernel(x_prompt, x_sample, state_conv, state_short_conv, state_delta, norm1_g, w_in, conv_dw_w,
           conv_dw_b, conv_ln_g, conv_ln_b, w_conv_out, short_conv_w, a_log, dt_bias, delta_norm_g,
           w_delta_out, w_merge_out, norm2_g, router_w, router_b, w_gate_up, b_gate_up, w_down, b_down,
           final_norm_g):
    depth = w_in.shape[0]
    assert depth == 1
    bp, tp, d = x_prompt.shape
    bs, ts, _ = x_sample.shape
    n_p, n_s = bp * tp, bs * ts
    n = n_p + n_s
    l = 0
    x_p = x_prompt.reshape(n_p, d)
    x_s = x_sample.reshape(n_s, d)

    o_ab = 2 * D_CONV + 4 * DN_WIDTH
    w = w_in[l]
    w_main = jnp.concatenate([w[:, :o_ab], w[:, o_ab + 2 * DN_HEADS:]], axis=1).astype(BF16)
    w_ab = _split_bf16(jnp.pad(w[:, o_ab:o_ab + 2 * DN_HEADS], ((0, 0), (0, LANES - 2 * DN_HEADS))))

    glu, qkv_pre, z, gb, siga, sigb = _inproj(
        x_p, x_s, norm1_g[l].reshape(1, d), w_main, w_ab, _pad_lanes(a_log[l]), _pad_lanes(dt_bias[l]), TOKEN_TILE)

    dw = (conv_dw_w[l], conv_dw_b[l].reshape(1, -1), conv_ln_g[l].reshape(1, -1), conv_ln_b[l].reshape(1, -1))
    st_c_p = jnp.zeros((bp, CONV_WIDTH - 1, D_CONV), F32)
    seq_tile = SEQ_TILE if tp % SEQ_TILE == 0 else TOKEN_TILE
    cact_p, conv_p = _conv_branch(glu, st_c_p, *dw, row0=0, bsz=bp, t_len=tp, bb=1, tt=seq_tile)
    cact_s, conv_s = _conv_branch(glu, state_conv[l], *dw, row0=n_p, bsz=bs, t_len=ts, bb=8, tt=ts)

    st_s_p = jnp.zeros((bp, SHORT_WIDTH - 1, 3 * DN_WIDTH), F32)
    s0_p = jnp.zeros((bp, DN_HEADS, DN_HEAD_DIM, DN_HEAD_DIM), F32)
    ng = delta_norm_g[l].reshape(1, -1)
    oact_p, short_p, s_p = _delta_prompt(qkv_pre, z, gb, st_s_p, s0_p, short_conv_w[l], ng,
                                         bsz=bp, t_len=tp, tt=seq_tile)
    oact_s, short_s, s_s = _delta_sample(qkv_pre, z, gb, state_short_conv[l], state_delta[l], short_conv_w[l], ng,
                                         row0=n_p, bsz=bs, seq_len=ts)

    rw = _split_bf16(jnp.pad(router_w[l], ((0, 0), (0, LANES - N_EXPERTS))))
    x2, tg = _mix(x_p, x_s, cact_p, cact_s, oact_p, oact_s, siga, sigb, w_conv_out[l].astype(BF16),
                  w_delta_out[l].astype(BF16), w_merge_out[l].astype(BF16), norm2_g[l].reshape(1, d),
                  rw, _pad_lanes(router_b[l]),
                  MIX_TILE if n_p % MIX_TILE == 0 and n_s % MIX_TILE == 0 else TOKEN_TILE)

    n_blocks = -(-(n * TOP_K) // MOE_TILE) + N_EXPERTS
    n_steps = n // TOKEN_TILE
    dest, table, sched, nvalid = _route(tg[:, :TOP_K].astype(jnp.int32), MOE_TILE, n_blocks, n_steps)
    xs = _dispatch(x2, norm2_g[l].reshape(1, d), table, n_blocks * MOE_TILE, TOKEN_TILE)
    yb = _moe_ffn(xs, sched, nvalid, w_gate_up[l], b_gate_up[l], w_down[l], b_down[l], MOE_TILE, n_blocks)
    y_p, y_s = _combine(dest.reshape(n_steps, -1), yb, x2, tg, final_norm_g.reshape(1, d), n_p, TOKEN_TILE)

    return (y_p.reshape(bp, tp, d), y_s.reshape(bs, ts, d), conv_p[None], short_p[None], s_p[None],
            conv_s[None], short_s[None], s_s[None])
```

```python
import functools

import jax
import jax.numpy as jnp
from jax import lax
from jax.experimental import pallas as pl
from jax.experimental.pallas import tpu as pltpu

F32 = jnp.float32
BF16 = jnp.bfloat16
EPS = 1e-6

LANES = 128
SUBLANES = 8
VMEM_LIMIT_BYTES = 56 * 1024 * 1024
DMA_PRIORITIES = 2

D_CONV = 512
CONV_WIDTH = 31
DN_HEADS = 4
DN_HEAD_DIM = 128
DN_WIDTH = DN_HEADS * DN_HEAD_DIM
SHORT_WIDTH = 4
N_EXPERTS = 32
TOP_K = 4
SWIGLU_LIMIT = 7.0
SWIGLU_ALPHA = 1.702

CHUNK = 128
CONV_HALO = 32
SHORT_HALO = 8

MIX_TILE = 512
TOKEN_TILE = 256
SEQ_TILE = 512
MOE_TILE = 256
FFN_BLOCKS_PER_STEP = 4


def _sigmoid(x):
    return 1.0 / (1.0 + jnp.exp(-x))


def _silu(x):
    return x * _sigmoid(x)


def _split_bf16(w):
    hi = w.astype(BF16)
    lo = (w - hi.astype(F32)).astype(BF16)
    return jnp.concatenate([hi, lo], axis=-1)


def _dot_split(x, w_split):
    n = w_split.shape[-1] // 2
    x_hi = x.astype(BF16)
    x_lo = (x - x_hi.astype(F32)).astype(BF16)
    r = jnp.dot(x_hi, w_split, preferred_element_type=F32)
    return r[:, :n] + r[:, n:] + jnp.dot(x_lo, w_split[:, :n], preferred_element_type=F32)


def _dot_delta(a, b, dims=(((1,), (0,)), ((), ()))):
    return lax.dot_general(a.astype(BF16), b.astype(BF16), dims, preferred_element_type=F32)


def _params(n_axes):
    return pltpu.CompilerParams(dimension_semantics=("arbitrary",) * n_axes, vmem_limit_bytes=VMEM_LIMIT_BYTES)


def _two_source_specs(tm, d, n_first_tiles):
    first = pl.BlockSpec((tm, d), lambda i, *_: (jnp.minimum(i, n_first_tiles - 1), 0))
    second = pl.BlockSpec((tm, d), lambda i, *_: (jnp.maximum(i - n_first_tiles, 0), 0))
    return first, second


def _inproj_kernel(xp_ref, xs_ref, g_ref, w_ref, wab_ref, alog_ref, dtb_ref,
                   glu_ref, qkv_ref, z_ref, gb_ref, sa_ref, sb_ref, *, n_prompt_tiles):
    x = jnp.where(pl.program_id(0) < n_prompt_tiles, xp_ref[...], xs_ref[...])
    h = x * lax.rsqrt(jnp.mean(x * x, axis=-1, keepdims=True) + EPS) * g_ref[...]
    hb = h.astype(BF16)

    def mm(lo, hi):
        return jnp.dot(hb, w_ref[:, lo:hi], preferred_element_type=F32)

    o_gate, o_qkv, o_z = D_CONV, 2 * D_CONV, 2 * D_CONV + 3 * DN_WIDTH
    o_ga = o_z + DN_WIDTH
    d = x.shape[-1]
    glu_ref[...] = mm(0, o_gate) * _sigmoid(mm(o_gate, o_qkv))
    qkv_ref[...] = mm(o_qkv, o_z)
    z_ref[...] = mm(o_z, o_ga)
    sa_ref[...] = _sigmoid(mm(o_ga, o_ga + d))
    sb_ref[...] = _sigmoid(mm(o_ga + d, o_ga + 2 * d))
    ab = _dot_split(h, wab_ref[...])
    xa = ab + dtb_ref[...]
    softplus = jnp.maximum(xa, 0.0) + jnp.log(1.0 + jnp.exp(-jnp.abs(xa)))
    g = -jnp.exp(alog_ref[...]) * softplus
    lane = lax.broadcasted_iota(jnp.int32, ab.shape, 1)
    gb_ref[...] = jnp.where(lane < DN_HEADS, g, _sigmoid(ab))


def _inproj(x_p, x_s, norm_g, w_main, w_ab, alog, dtb, tm):
    (n_p, d), n_s = x_p.shape, x_s.shape[0]
    assert n_p % tm == 0 and n_s % tm == 0
    n = n_p + n_s
    wcols = w_main.shape[1]
    row = lambda i: (i, 0)
    const = lambda i: (0, 0)
    outs = [(D_CONV, F32), (3 * DN_WIDTH, F32), (DN_WIDTH, F32), (LANES, F32), (d, F32), (d, F32)]
    return pl.pallas_call(
        functools.partial(_inproj_kernel, n_prompt_tiles=n_p // tm),
        grid=(n // tm,),
        in_specs=[
            *_two_source_specs(tm, d, n_p // tm),
            pl.BlockSpec((1, d), const),
            pl.BlockSpec((d, wcols), const),
            pl.BlockSpec((d, 2 * LANES), const),
            pl.BlockSpec((1, LANES), const),
            pl.BlockSpec((1, LANES), const),
        ],
        out_specs=[pl.BlockSpec((tm, c), row) for c, _ in outs],
        out_shape=[jax.ShapeDtypeStruct((n, c), dt) for c, dt in outs],
        compiler_params=_params(1),
    )(x_p, x_s, norm_g, w_main, w_ab, alog, dtb)


def _conv_kernel(glu_ref, st_ref, w_ref, b_ref, lg_ref, lb_ref, out_ref, nst_ref, e_ref, sh_ref, *, bb, tt, rows):
    t = pl.program_id(1)

    hist = CONV_WIDTH - 1

    @pl.when(t == 0)
    def _():
        e_ref[:, 0:SUBLANES, :] = jnp.zeros((bb, SUBLANES, D_CONV), F32)
        e_ref[:, CONV_HALO - hist:CONV_HALO, :] = st_ref[...]

    for b in range(bb):
        e_ref[b, CONV_HALO:CONV_HALO + tt, :] = glu_ref[b * tt:(b + 1) * tt, :]
    off = CONV_HALO - (CONV_WIDTH - 1)
    span = sh_ref.shape[1]
    for b in range(bb):
        for s in range(1, SUBLANES):
            sh_ref[s - 1] = e_ref[b, s:s + span, :]
        for c in range(tt // rows):
            r0 = c * rows
            acc = jnp.zeros((rows, D_CONV), F32) + b_ref[...]
            for j in range(CONV_WIDTH):
                q, s = divmod(j + off, SUBLANES)
                lo = r0 + q * SUBLANES
                src = e_ref[b, lo:lo + rows, :] if s == 0 else sh_ref[s - 1, lo:lo + rows, :]
                acc = acc + w_ref[j:j + 1, :] * src
            mu = jnp.mean(acc, axis=-1, keepdims=True)
            xc = acc - mu
            var = jnp.mean(xc * xc, axis=-1, keepdims=True)
            y = xc * lax.rsqrt(var + EPS) * lg_ref[...] + lb_ref[...]
            out_ref[b * tt + r0:b * tt + r0 + rows, :] = _silu(y).astype(out_ref.dtype)
    nst_ref[...] = e_ref[:, tt + CONV_HALO - hist:tt + CONV_HALO, :]
    e_ref[:, 0:CONV_HALO, :] = e_ref[:, tt:tt + CONV_HALO, :]


def _conv_branch(glu, state32, dw_w, dw_b, ln_g, ln_b, *, row0, bsz, t_len, bb, tt):
    c = glu.shape[1]
    assert bsz % bb == 0 and t_len % tt == 0 and row0 % (bb * tt) == 0
    nt = t_len // tt
    blk0 = row0 // (bb * tt)
    rows = min(tt, 32)
    kern = functools.partial(_conv_kernel, bb=bb, tt=tt, rows=rows)
    const = lambda b, t: (0, 0)
    return pl.pallas_call(
        kern,
        grid=(bsz // bb, nt),
        in_specs=[
            pl.BlockSpec((bb * tt, c), lambda b, t: (blk0 + b * nt + t, 0)),
            pl.BlockSpec((bb, CONV_WIDTH - 1, c), lambda b, t: (b, 0, 0)),
            pl.BlockSpec((CONV_WIDTH, c), const),
            pl.BlockSpec((1, c), const),
            pl.BlockSpec((1, c), const),
            pl.BlockSpec((1, c), const),
        ],
        out_specs=[
            pl.BlockSpec((bb * tt, c), lambda b, t: (b * nt + t, 0)),
            pl.BlockSpec((bb, CONV_WIDTH - 1, c), lambda b, t: (b, 0, 0)),
        ],
        out_shape=[
            jax.ShapeDtypeStruct((bsz * t_len, c), BF16),
            jax.ShapeDtypeStruct((bsz, CONV_WIDTH - 1, c), F32),
        ],
        scratch_shapes=[pltpu.VMEM((bb, CONV_HALO + tt, c), F32),
                        pltpu.VMEM((SUBLANES - 1, tt + CONV_HALO - SUBLANES, c), F32)],
        compiler_params=_params(2),
    )(glu, state32, dw_w, dw_b, ln_g, ln_b)


def _chunk_masks(seq_len):
    i = lax.broadcasted_iota(jnp.int32, (CHUNK, CHUNK), 0)
    j = lax.broadcasted_iota(jnp.int32, (CHUNK, CHUNK), 1)
    same = (i // seq_len) == (j // seq_len)
    incl = same & (i >= j)
    strict = same & (i > j)
    last = j == (i // seq_len) * seq_len + (seq_len - 1)
    levels = []
    blk = 1
    while blk < seq_len:
        levels.append(((i // (2 * blk)) == (j // (2 * blk))) & (((i // blk) % 2) == 1) & (((j // blk) % 2) == 0))
        blk *= 2
    eye = i == j
    return incl, strict, last, levels, eye


def _lane_col(x, lane):
    return jnp.broadcast_to(x[:, lane:lane + 1], (x.shape[0], LANES))


def _l2norm(x):
    return x * lax.rsqrt(jnp.sum(x * x, axis=-1, keepdims=True) + EPS)


def _select_sum(mask01, x):
    hi = x.astype(BF16)
    r1 = x - hi.astype(F32)
    mid = r1.astype(BF16)
    lo = (r1 - mid.astype(F32)).astype(BF16)
    w = x.shape[1]
    parts = jnp.dot(mask01, jnp.concatenate([hi, mid, lo], axis=1), preferred_element_type=F32)
    return parts[:, :w] + parts[:, w:2 * w] + parts[:, 2 * w:]


def _chunks_prepare(qkvs, gbts, masks, seq_len):
    incl, strict, last, levels, eye = masks
    nt = (((1,), (1,)), ((), ()))
    lower01 = jnp.where(incl, 1.0, 0.0).astype(BF16)
    probs = []
    for qkv, gbt in zip(qkvs, gbts):
        gc = _select_sum(lower01, gbt)
        gct = gc.T
        if seq_len == CHUNK:
            glast = jnp.broadcast_to(gc[CHUNK - 1:CHUNK, :], gc.shape)
        else:
            glast = _select_sum(jnp.where(last, 1.0, 0.0).astype(BF16), gc)
        for h in range(DN_HEADS):
            q = _l2norm(qkv[:, h * DN_HEAD_DIM:(h + 1) * DN_HEAD_DIM]) * (DN_HEAD_DIM ** -0.5)
            k = _l2norm(qkv[:, DN_WIDTH + h * DN_HEAD_DIM:DN_WIDTH + (h + 1) * DN_HEAD_DIM])
            v = qkv[:, 2 * DN_WIDTH + h * DN_HEAD_DIM:2 * DN_WIDTH + (h + 1) * DN_HEAD_DIM]
            gcol = _lane_col(gc, h)
            grow = jnp.broadcast_to(gct[h:h + 1, :], (CHUNK, CHUNK))
            beta = _lane_col(gbt, DN_HEADS + h)
            gl = _lane_col(glast, h)
            decay = jnp.exp(jnp.where(incl, gcol - grow, -jnp.inf))
            egc = jnp.exp(gcol)
            kb = k * beta
            probs.append(dict(q=q, k=k, kb=kb, decay=decay, rhs=jnp.concatenate([v * beta, kb * egc], axis=1),
                              qexp=q * egc, kdec=k * jnp.exp(gl - gcol), egl=jnp.exp(gl)))
    for p in probs:
        p['a'] = jnp.where(strict, _dot_delta(p['kb'], p['k'], nt) * p['decay'], 0.0)
        p['scores'] = _dot_delta(p['q'], p['k'], nt) * p['decay']
    for p in probs:
        p['x'] = jnp.where(eye, 1.0, 0.0) - jnp.where(levels[0], p['a'], 0.0)
    for m in levels[1:]:
        for p in probs:
            p['xa'] = _dot_delta(p['x'], jnp.where(m, p['a'], 0.0))
        for p in probs:
            p['x'] = p['x'] - _dot_delta(p['xa'], p['x'])
    out = []
    for c in range(len(qkvs)):
        heads = []
        for h in range(DN_HEADS):
            p = probs[c * DN_HEADS + h]
            sol = _dot_delta(p['x'], p['rhs'])
            heads.append((sol[:, :DN_HEAD_DIM], sol[:, DN_HEAD_DIM:], p['scores'], p['qexp'], p['kdec'], p['egl']))
        out.append(heads)
    return out


def _gated_out_norm(o, z, ng):
    y = o * lax.rsqrt(jnp.mean(o * o, axis=-1, keepdims=True) + EPS) * ng
    return y * _silu(z)


def _short_conv(e_ref, w_ref, tt):
    off = SHORT_HALO - (SHORT_WIDTH - 1)
    acc = w_ref[0:1, :] * e_ref[off:off + tt, :]
    for j in range(1, SHORT_WIDTH):
        acc = acc + w_ref[j:j + 1, :] * e_ref[off + j:off + j + tt, :]
    return _silu(acc)


def _delta_prompt_kernel(qkv_ref, z_ref, gb_ref, st_ref, s0_ref, w_ref, ng_ref,
                         o_ref, nst_ref, sout_ref, e_ref, s_ref, *, tt):
    t = pl.program_id(1)

    hist = SHORT_WIDTH - 1

    @pl.when(t == 0)
    def _():
        e_ref[0:SUBLANES, :] = jnp.zeros((SUBLANES, 3 * DN_WIDTH), F32)
        e_ref[SHORT_HALO - hist:SHORT_HALO, :] = st_ref[0]
        s_ref[...] = s0_ref[0]

    e_ref[SHORT_HALO:SHORT_HALO + tt, :] = qkv_ref[...]
    qkv = _short_conv(e_ref, w_ref, tt)
    nst_ref[0] = e_ref[tt + SHORT_HALO - hist:tt + SHORT_HALO, :]
    e_ref[0:SHORT_HALO, :] = e_ref[tt:tt + SHORT_HALO, :]

    masks = _chunk_masks(CHUNK)
    tn = (((0,), (0,)), ((), ()))
    n_chunks = tt // CHUNK
    prep = _chunks_prepare([qkv[c * CHUNK:(c + 1) * CHUNK, :] for c in range(n_chunks)],
                           [gb_ref[c * CHUNK:(c + 1) * CHUNK, :] for c in range(n_chunks)], masks, CHUNK)
    heads = range(DN_HEADS)
    s = [s_ref[h] for h in heads]
    for c in range(n_chunks):
        r0 = c * CHUNK
        value, kcum, scores, qexp, kdec, egl = zip(*prep[c])
        both = [_dot_delta(jnp.concatenate([kcum[h], qexp[h]], axis=0), s[h]) for h in heads]
        v_new = [value[h] - both[h][:CHUNK] for h in heads]
        o = [both[h][CHUNK:] + _dot_delta(scores[h], v_new[h]) for h in heads]
        s = [s[h] * egl[h][0:1, :] + _dot_delta(kdec[h], v_new[h], tn) for h in heads]
        for h in heads:
            lanes = slice(h * DN_HEAD_DIM, (h + 1) * DN_HEAD_DIM)
            o_ref[r0:r0 + CHUNK, lanes] = _gated_out_norm(
                o[h], z_ref[r0:r0 + CHUNK, lanes], ng_ref[...]).astype(o_ref.dtype)
    for h in heads:
        s_ref[h] = s[h]
        sout_ref[0, h] = s[h]


def _delta_prompt(qkv_pre, z, gb, state8, s0, conv_w, norm_g, *, bsz, t_len, tt):
    n = bsz * t_len
    assert t_len % tt == 0 and tt % CHUNK == 0
    nt = t_len // tt
    kern = functools.partial(_delta_prompt_kernel, tt=tt)
    tile = lambda b, t: (b * nt + t, 0)
    per_b = lambda b, t: (b, 0, 0)
    per_b4 = lambda b, t: (b, 0, 0, 0)
    return pl.pallas_call(
        kern,
        grid=(bsz, nt),
        in_specs=[
            pl.BlockSpec((tt, 3 * DN_WIDTH), tile),
            pl.BlockSpec((tt, DN_WIDTH), tile),
            pl.BlockSpec((tt, LANES), tile),
            pl.BlockSpec((1, SHORT_WIDTH - 1, 3 * DN_WIDTH), per_b),
            pl.BlockSpec((1, DN_HEADS, DN_HEAD_DIM, DN_HEAD_DIM), per_b4),
            pl.BlockSpec((SHORT_WIDTH, 3 * DN_WIDTH), lambda b, t: (0, 0)),
            pl.BlockSpec((1, DN_HEAD_DIM), lambda b, t: (0, 0)),
        ],
        out_specs=[
            pl.BlockSpec((tt, DN_WIDTH), tile),
            pl.BlockSpec((1, SHORT_WIDTH - 1, 3 * DN_WIDTH), per_b),
            pl.BlockSpec((1, DN_HEADS, DN_HEAD_DIM, DN_HEAD_DIM), per_b4),
        ],
        out_shape=[
            jax.ShapeDtypeStruct((n, DN_WIDTH), BF16),
            jax.ShapeDtypeStruct((bsz, SHORT_WIDTH - 1, 3 * DN_WIDTH), F32),
            jax.ShapeDtypeStruct((bsz, DN_HEADS, DN_HEAD_DIM, DN_HEAD_DIM), F32),
        ],
        scratch_shapes=[
            pltpu.VMEM((SHORT_HALO + tt, 3 * DN_WIDTH), F32),
            pltpu.VMEM((DN_HEADS, DN_HEAD_DIM, DN_HEAD_DIM), F32),
        ],
        compiler_params=_params(2),
    )(qkv_pre, z, gb, state8, s0, conv_w, norm_g)


def _delta_sample_kernel(qkv_ref, z_ref, gb_ref, st_ref, s0_ref, w_ref, ng_ref, o_ref, nst_ref, sout_ref,
                         e_ref, *, nseq, seq_len):
    qkv_rows = []
    hist = SHORT_WIDTH - 1
    e_ref[0:SUBLANES, :] = jnp.zeros((SUBLANES, 3 * DN_WIDTH), F32)
    for b in range(nseq):
        e_ref[SHORT_HALO - hist:SHORT_HALO, :] = st_ref[b]
        e_ref[SHORT_HALO:SHORT_HALO + seq_len, :] = qkv_ref[b * seq_len:(b + 1) * seq_len, :]
        qkv_rows.append(_short_conv(e_ref, w_ref, seq_len))
        nst_ref[b] = e_ref[seq_len + SHORT_HALO - hist:seq_len + SHORT_HALO, :]
    qkv = jnp.concatenate(qkv_rows, axis=0)
    masks = _chunk_masks(seq_len)
    prep = _chunks_prepare([qkv], [gb_ref[...]], masks, seq_len)[0]
    tn = (((0,), (0,)), ((), ()))
    rows = [slice(b * seq_len, (b + 1) * seq_len) for b in range(nseq)]
    both = [[_dot_delta(jnp.concatenate([prep[h][1][r], prep[h][3][r]], axis=0), s0_ref[b, h])
             for b, r in enumerate(rows)] for h in range(DN_HEADS)]
    v_new = [[prep[h][0][r] - both[h][b][:seq_len] for b, r in enumerate(rows)] for h in range(DN_HEADS)]
    for h in range(DN_HEADS):
        kdec, egl = prep[h][4], prep[h][5]
        for b, r in enumerate(rows):
            sout_ref[b, h] = (s0_ref[b, h] * egl[b * seq_len:b * seq_len + 1, :]
                              + _dot_delta(kdec[r], v_new[h][b], tn))
    for h in range(DN_HEADS):
        o = (jnp.concatenate([both[h][b][seq_len:] for b in range(nseq)], axis=0)
             + _dot_delta(prep[h][2], jnp.concatenate(v_new[h], axis=0)))
        lanes = slice(h * DN_HEAD_DIM, (h + 1) * DN_HEAD_DIM)
        o_ref[:, lanes] = _gated_out_norm(o, z_ref[:, lanes], ng_ref[...]).astype(o_ref.dtype)


def _delta_sample(qkv_pre, z, gb, state8, s0, conv_w, norm_g, *, row0, bsz, seq_len):
    n = bsz * seq_len
    assert CHUNK % seq_len == 0
    nseq = CHUNK // seq_len
    assert bsz % nseq == 0 and row0 % CHUNK == 0
    blk0 = row0 // CHUNK
    kern = functools.partial(_delta_sample_kernel, nseq=nseq, seq_len=seq_len)
    tile = lambda i: (blk0 + i, 0)
    blk3 = lambda i: (i, 0, 0)
    blk4 = lambda i: (i, 0, 0, 0)
    return pl.pallas_call(
        kern,
        grid=(bsz // nseq,),
        in_specs=[
            pl.BlockSpec((CHUNK, 3 * DN_WIDTH), tile),
            pl.BlockSpec((CHUNK, DN_WIDTH), tile),
            pl.BlockSpec((CHUNK, LANES), tile),
            pl.BlockSpec((nseq, SHORT_WIDTH - 1, 3 * DN_WIDTH), blk3),
            pl.BlockSpec((nseq, DN_HEADS, DN_HEAD_DIM, DN_HEAD_DIM), blk4),
            pl.BlockSpec((SHORT_WIDTH, 3 * DN_WIDTH), lambda i: (0, 0)),
            pl.BlockSpec((1, DN_HEAD_DIM), lambda i: (0, 0)),
        ],
        out_specs=[
            pl.BlockSpec((CHUNK, DN_WIDTH), lambda i: (i, 0)),
            pl.BlockSpec((nseq, SHORT_WIDTH - 1, 3 * DN_WIDTH), blk3),
            pl.BlockSpec((nseq, DN_HEADS, DN_HEAD_DIM, DN_HEAD_DIM), blk4),
        ],
        out_shape=[
            jax.ShapeDtypeStruct((n, DN_WIDTH), BF16),
            jax.ShapeDtypeStruct((bsz, SHORT_WIDTH - 1, 3 * DN_WIDTH), F32),
            jax.ShapeDtypeStruct((bsz, DN_HEADS, DN_HEAD_DIM, DN_HEAD_DIM), F32),
        ],
        scratch_shapes=[pltpu.VMEM((SHORT_HALO + seq_len, 3 * DN_WIDTH), F32)],
        compiler_params=_params(1),
    )(qkv_pre, z, gb, state8, s0, conv_w, norm_g)


def _mix_kernel(xp_ref, xs_ref, cap_ref, cas_ref, oap_ref, oas_ref, sa_ref, sb_ref, wc_ref, wd_ref, wm_ref,
                g2_ref, rw_ref, rb_ref, x2_ref, tr_ref, *, n_prompt_tiles):
    is_prompt = pl.program_id(0) < n_prompt_tiles
    x = jnp.where(is_prompt, xp_ref[...], xs_ref[...])
    ca = jnp.where(is_prompt, cap_ref[...], cas_ref[...])
    oa = jnp.where(is_prompt, oap_ref[...], oas_ref[...])
    ya = jnp.dot(ca, wc_ref[...], preferred_element_type=F32)
    yb = jnp.dot(oa, wd_ref[...], preferred_element_type=F32)
    mixed = sa_ref[...] * ya + sb_ref[...] * yb
    x2 = x + jnp.dot(mixed.astype(BF16), wm_ref[...], preferred_element_type=F32)
    x2_ref[...] = x2
    h2 = x2 * lax.rsqrt(jnp.mean(x2 * x2, axis=-1, keepdims=True) + EPS) * g2_ref[...]
    logits = _dot_split(h2, rw_ref[...]) + rb_ref[...]
    lt = logits.T[:N_EXPERTS, :]
    tokens = lt.shape[1]
    row = lax.broadcasted_iota(jnp.int32, lt.shape, 0).astype(F32)
    top_vals, top_idx = [], []
    for k in range(TOP_K):
        m = jnp.max(lt, axis=0, keepdims=True)
        idx = jnp.min(jnp.where(lt == m, row, float(N_EXPERTS)), axis=0, keepdims=True)
        top_vals.append(m)
        top_idx.append(idx)
        lt = jnp.where(row == idx, -jnp.inf, lt)
    exps = [jnp.exp(v - top_vals[0]) for v in top_vals]
    den = exps[0] + exps[1] + exps[2] + exps[3]
    slot = lax.broadcasted_iota(jnp.int32, (2 * TOP_K, tokens), 0)
    packed = jnp.zeros((2 * TOP_K, tokens), F32)
    for k in range(TOP_K):
        packed = jnp.where(slot == k, top_idx[k], packed)
        packed = jnp.where(slot == TOP_K + k, exps[k] / den, packed)
    packed = jnp.concatenate([packed, jnp.zeros((LANES - 2 * TOP_K, tokens), F32)], axis=0)
    tr_ref[...] = packed.T


def _mix(x_p, x_s, cact_p, cact_s, oact_p, oact_s, siga, sigb, w_conv_out, w_delta_out, w_merge_out, norm2_g,
         router_w, router_b, tm):
    (n_p, d), n_s = x_p.shape, x_s.shape[0]
    n = n_p + n_s
    row = lambda i: (i, 0)
    const = lambda i: (0, 0)
    return pl.pallas_call(
        functools.partial(_mix_kernel, n_prompt_tiles=n_p // tm),
        grid=(n // tm,),
        in_specs=[
            *_two_source_specs(tm, d, n_p // tm),
            *_two_source_specs(tm, D_CONV, n_p // tm),
            *_two_source_specs(tm, DN_WIDTH, n_p // tm),
            pl.BlockSpec((tm, d), row),
            pl.BlockSpec((tm, d), row),
            pl.BlockSpec((D_CONV, d), const),
            pl.BlockSpec((DN_WIDTH, d), const),
            pl.BlockSpec((d, d), const),
            pl.BlockSpec((1, d), const),
            pl.BlockSpec((d, 2 * LANES), const),
            pl.BlockSpec((1, LANES), const),
        ],
        out_specs=[
            pl.BlockSpec((tm, d), row),
            pl.BlockSpec((tm, LANES), row),
        ],
        out_shape=[
            jax.ShapeDtypeStruct((n, d), F32),
            jax.ShapeDtypeStruct((n, LANES), F32),
        ],
        compiler_params=_params(1),
    )(x_p, x_s, cact_p, cact_s, oact_p, oact_s, siga, sigb, w_conv_out, w_delta_out, w_merge_out, norm2_g,
      router_w, router_b)


def _fill_rows_per_step(n_fill, n_steps):
    per_step = SUBLANES
    while per_step * n_steps < n_fill:
        per_step *= 2
    assert n_fill % per_step == 0
    return per_step


def _route(top_idx, tm, n_blocks, n_steps):
    n = top_idx.shape[0]
    n_fill = n_blocks * tm - n * TOP_K
    assert n_fill == N_EXPERTS * tm
    flat_e = top_idx.reshape(-1)
    experts = jnp.arange(N_EXPERTS, dtype=jnp.int32)
    onehot = (flat_e[:, None] == experts[None, :]).astype(jnp.int32)
    csum = jnp.cumsum(onehot, axis=0)
    rank = jnp.sum(csum * onehot, axis=1) - 1
    counts = csum[-1]
    padded = (counts + tm - 1) // tm * tm
    pad_end = jnp.cumsum(padded)
    pad_start = pad_end - padded
    dest = jnp.sum(onehot * pad_start[None, :], axis=1) + rank
    nvalid = (pad_end[-1] // tm).astype(jnp.int32)
    blk = jnp.arange(n_blocks, dtype=jnp.int32)
    owner = jnp.sum((pad_end[None, :] <= (blk * tm)[:, None]).astype(jnp.int32), axis=1)
    block_e = jnp.minimum(owner, N_EXPERTS - 1)
    block_e = jnp.where(blk < nvalid, block_e, jnp.sum(jnp.where(blk == nvalid - 1, block_e, 0)))
    present = counts > 0
    later = present[None, :] & (experts[None, :] > experts[:, None])
    next_present = jnp.min(jnp.where(later, experts[None, :], N_EXPERTS), axis=1)
    next_present = jnp.where(next_present == N_EXPERTS, -1, next_present)
    parity = (jnp.cumsum(present.astype(jnp.int32)) - 1) % 2
    of_block = (block_e[:, None] == experts[None, :]).astype(jnp.int32)
    sched = jnp.stack([block_e, jnp.sum(of_block * next_present[None, :], axis=1),
                       jnp.sum(of_block * parity[None, :], axis=1)])
    n_pad = padded - counts
    spill = tm - n_pad
    spill_start = pad_end[-1] + jnp.cumsum(spill) - spill
    j = jnp.arange(tm, dtype=jnp.int32)[None, :]
    fill = jnp.where(j < n_pad[:, None], (pad_start + counts)[:, None] + j, (spill_start - n_pad)[:, None] + j)
    fill_step = _fill_rows_per_step(n_fill, n_steps)
    fill = jnp.pad(fill.reshape(-1, fill_step), ((0, n_steps - n_fill // fill_step), (0, 0)))
    table = jnp.concatenate([dest.reshape(n_steps, -1), fill], axis=1)
    return dest.reshape(n, TOP_K), table, sched, nvalid.reshape(1)


def _to_token_tiles(x, ref, row0=0):
    t = x.shape[0]
    for s in range(SUBLANES):
        ref[pl.ds(row0 + s, t, stride=SUBLANES), :] = x[:, s * LANES:(s + 1) * LANES]


def _from_token_tiles(ref, t, row0=0):
    return jnp.concatenate([ref[pl.ds(row0 + s, t, stride=SUBLANES), :] for s in range(SUBLANES)], axis=1)


def _tile(ref, row):
    return ref.at[pl.ds(pl.multiple_of(row * SUBLANES, SUBLANES), SUBLANES)]


def _dispatch_kernel(tab_ref, x2_ref, g2_ref, xs_ref, buf, sems, *, tokens, fill_step, n_fill_steps):
    i = pl.program_id(0)
    last = pl.num_programs(0) - 1
    slot = i % 2
    base = slot * tokens
    fill_sem = 2

    x2 = x2_ref[...]
    h2 = x2 * lax.rsqrt(jnp.mean(x2 * x2, axis=-1, keepdims=True) + EPS) * g2_ref[...]
    _to_token_tiles(h2, buf, pl.multiple_of(base * SUBLANES, SUBLANES))

    def wait_tiles(sem_idx, count):
        while count > 0:
            rows = min(count, tokens) * SUBLANES
            pltpu.make_async_copy(buf.at[pl.ds(0, rows)], xs_ref.at[pl.ds(0, rows)], sems.at[sem_idx]).wait()
            count -= min(count, tokens)

    for t in range(tokens):
        for k in range(TOP_K):
            pltpu.make_async_copy(_tile(buf, base + t), _tile(xs_ref, tab_ref[i, t * TOP_K + k]),
                                  sems.at[slot]).start(priority=k % DMA_PRIORITIES)

    @pl.when(i < n_fill_steps)
    def _():
        for p in range(fill_step):
            pltpu.make_async_copy(_tile(buf, base), _tile(xs_ref, tab_ref[i, tokens * TOP_K + p]),
                                  sems.at[fill_sem]).start(priority=p % DMA_PRIORITIES)
        wait_tiles(fill_sem, fill_step)

    @pl.when(i > 0)
    def _():
        wait_tiles(1 - slot, tokens * TOP_K)

    @pl.when(i == last)
    def _():
        wait_tiles(slot, tokens * TOP_K)


def _dispatch(x2, norm_g, table, n_rows, tokens):
    n, d = x2.shape
    assert d == SUBLANES * LANES
    fill_step = table.shape[1] - tokens * TOP_K
    n_fill_steps = (n_rows - n * TOP_K) // fill_step
    return pl.pallas_call(
        functools.partial(_dispatch_kernel, tokens=tokens, fill_step=fill_step, n_fill_steps=n_fill_steps),
        grid_spec=pltpu.PrefetchScalarGridSpec(
            num_scalar_prefetch=1,
            grid=(n // tokens,),
            in_specs=[pl.BlockSpec((tokens, d), lambda i, tab: (i, 0)),
                      pl.BlockSpec((1, d), lambda i, tab: (0, 0))],
            out_specs=pl.BlockSpec(memory_space=pl.ANY),
            scratch_shapes=[pltpu.VMEM((2 * tokens * SUBLANES, LANES), F32), pltpu.SemaphoreType.DMA((3,))],
        ),
        out_shape=jax.ShapeDtypeStruct((n_rows * SUBLANES, LANES), F32),
        compiler_params=_params(1),
    )(table, x2, norm_g)


def _moe_ffn_kernel(sched_ref, nv_ref, xs_ref, wgu_hbm, bgu_ref, wd_hbm, bd_ref, out_ref,
                    wgu_f32, wd_f32, wgu_bf, wd_bf, sems, *, blocks_per_step):
    tm = xs_ref.shape[0] // SUBLANES // blocks_per_step
    f = wd_bf.shape[0]

    def weight_copies(e, h):
        return (pltpu.make_async_copy(wgu_hbm.at[e], wgu_f32.at[h], sems.at[h]),
                pltpu.make_async_copy(wd_hbm.at[e], wd_f32.at[h], sems.at[h]))

    def one_block(i, r0):
        expert, next_expert, half = sched_ref[0, i], sched_ref[1, i], sched_ref[2, i]

        @pl.when(i == 0)
        def _():
            for c in weight_copies(expert, half):
                c.start()

        @pl.when(jnp.logical_or(i == 0, expert != sched_ref[0, jnp.maximum(i - 1, 0)]))
        def _():
            for c in weight_copies(expert, half):
                c.wait()
            wgu_bf[...] = wgu_f32[half].astype(BF16)
            wd_bf[...] = wd_f32[half].astype(BF16)

            @pl.when(next_expert >= 0)
            def _():
                for c in weight_copies(next_expert, 1 - half):
                    c.start()

        @pl.when(i < nv_ref[0])
        def _():
            x = _from_token_tiles(xs_ref, tm, r0).astype(BF16)
            gu = jnp.dot(x, wgu_bf[...], preferred_element_type=F32) + bgu_ref[expert]
            gt = jnp.minimum(gu[:, :f], SWIGLU_LIMIT)
            up = jnp.clip(gu[:, f:], -SWIGLU_LIMIT, SWIGLU_LIMIT)
            act = (up + 1.0) * (gt * _sigmoid(SWIGLU_ALPHA * gt))
            y = jnp.dot(act.astype(BF16), wd_bf[...], preferred_element_type=F32) + bd_ref[expert]
            _to_token_tiles(y, out_ref, r0)

        @pl.when(i >= nv_ref[0])
        def _():
            out_ref[r0:r0 + tm * SUBLANES, :] = jnp.zeros((tm * SUBLANES, LANES), out_ref.dtype)

    for h in range(blocks_per_step):
        one_block(pl.program_id(0) * blocks_per_step + h, h * tm * SUBLANES)


def _moe_ffn(xs, sched, nvalid, w_gate_up, b_gate_up, w_down, b_down, tm, n_blocks):
    ne, d, f2 = w_gate_up.shape
    f = f2 // 2
    per_step = FFN_BLOCKS_PER_STEP if n_blocks % FFN_BLOCKS_PER_STEP == 0 else 1
    rows = per_step * tm * SUBLANES
    used = lambda p, sc, nv: (jnp.minimum(p, (nv[0] - 1) // per_step), 0)
    whole = lambda p, sc, nv: (0, 0, 0)
    return pl.pallas_call(
        functools.partial(_moe_ffn_kernel, blocks_per_step=per_step),
        grid_spec=pltpu.PrefetchScalarGridSpec(
            num_scalar_prefetch=2,
            grid=(n_blocks // per_step,),
            in_specs=[
                pl.BlockSpec((rows, LANES), used),
                pl.BlockSpec(memory_space=pl.ANY),
                pl.BlockSpec((ne, 1, f2), whole),
                pl.BlockSpec(memory_space=pl.ANY),
                pl.BlockSpec((ne, 1, d), whole),
            ],
            out_specs=pl.BlockSpec((rows, LANES), lambda p, sc, nv: (p, 0)),
            scratch_shapes=[pltpu.VMEM((2, d, f2), F32), pltpu.VMEM((2, f, d), F32),
                            pltpu.VMEM((d, f2), BF16), pltpu.VMEM((f, d), BF16), pltpu.SemaphoreType.DMA((2,))],
        ),
        out_shape=jax.ShapeDtypeStruct((n_blocks * tm * SUBLANES, LANES), F32),
        compiler_params=_params(1),
    )(sched, nvalid, xs, w_gate_up, b_gate_up.reshape(ne, 1, f2), w_down, b_down.reshape(ne, 1, d))


def _combine_kernel(slot_ref, yb_ref, x2_ref, tg_ref, fg_ref, yp_ref, ys_ref, buf, sems, *, tc, n_prompt_tiles):
    i = pl.program_id(0)
    n_steps = pl.num_programs(0)

    def region(slot, k):
        return (slot * TOP_K + k) * tc

    def fetch(step, slot):
        for t in range(tc):
            for k in range(TOP_K):
                pltpu.make_async_copy(
                    _tile(yb_ref, slot_ref[step, t * TOP_K + k]), _tile(buf, region(slot, k) + t),
                    sems.at[slot]).start(priority=k % DMA_PRIORITIES)

    @pl.when(i == 0)
    def _():
        fetch(0, 0)

    @pl.when(i + 1 < n_steps)
    def _():
        fetch(i + 1, (i + 1) % 2)

    slot = i % 2
    for k in range(TOP_K):
        pltpu.make_async_copy(yb_ref.at[pl.ds(0, tc * SUBLANES)], buf.at[pl.ds(0, tc * SUBLANES)],
                              sems.at[slot]).wait()
    tg = tg_ref[...]
    y = x2_ref[...]
    for k in range(TOP_K):
        rows = _from_token_tiles(buf, tc, pl.multiple_of(region(slot, k) * SUBLANES, SUBLANES))
        y = y + tg[:, TOP_K + k:TOP_K + k + 1] * rows
    out = y * lax.rsqrt(jnp.mean(y * y, axis=-1, keepdims=True) + EPS) * fg_ref[...]

    @pl.when(i < n_prompt_tiles)
    def _():
        yp_ref[...] = out

    @pl.when(i >= n_prompt_tiles)
    def _():
        ys_ref[...] = out


def _combine(slot2d, yb, x2, tg, final_g, n_p, tc):
    n, d = x2.shape
    n_s = n - n_p
    assert n_p % tc == 0 and n_s % tc == 0
    npt = n_p // tc
    kern = functools.partial(_combine_kernel, tc=tc, n_prompt_tiles=npt)
    out_p, out_s = _two_source_specs(tc, d, npt)
    return pl.pallas_call(
        kern,
        grid_spec=pltpu.PrefetchScalarGridSpec(
            num_scalar_prefetch=1,
            grid=(n // tc,),
            in_specs=[
                pl.BlockSpec(memory_space=pl.ANY),
                pl.BlockSpec((tc, d), lambda i, s: (i, 0)),
                pl.BlockSpec((tc, LANES), lambda i, s: (i, 0)),
                pl.BlockSpec((1, d), lambda i, s: (0, 0)),
            ],
            out_specs=[out_p, out_s],
            scratch_shapes=[pltpu.VMEM((2 * TOP_K * tc * SUBLANES, LANES), F32), pltpu.SemaphoreType.DMA((2,))],
        ),
        out_shape=[jax.ShapeDtypeStruct((n_p, d), F32), jax.ShapeDtypeStruct((n_s, d), F32)],
        compiler_params=_params(1),
    )(slot2d, yb, x2, tg, final_g)


def _pad_lanes(v, width=LANES):
    v = v.reshape(1, -1)
    return jnp.pad(v, ((0, 0), (0, width - v.shape[1])))


def kernel(x_prompt, x_sample, state_conv, state_short_conv, state_delta, norm1_g, w_in, conv_dw_w,
           conv_dw_b, conv_ln_g, conv_ln_b, w_conv_out, short_conv_w, a_log, dt_bias, delta_norm_g,
           w_delta_out, w_merge_out, norm2_g, router_w, router_b, w_gate_up, b_gate_up, w_down, b_down,
           final_norm_g):
    depth = w_in.shape[0]
    assert depth == 1
    bp, tp, d = x_prompt.shape
    bs, ts, _ = x_sample.shape
    n_p, n_s = bp * tp, bs * ts
    n = n_p + n_s
    l = 0
    x_p = x_prompt.reshape(n_p, d)
    x_s = x_sample.reshape(n_s, d)

    o_ab = 2 * D_CONV + 4 * DN_WIDTH
    w = w_in[l]
    w_main = jnp.concatenate([w[:, :o_ab], w[:, o_ab + 2 * DN_HEADS:]], axis=1).astype(BF16)
    w_ab = _split_bf16(jnp.pad(w[:, o_ab:o_ab + 2 * DN_HEADS], ((0, 0), (0, LANES - 2 * DN_HEADS))))

    glu, qkv_pre, z, gb, siga, sigb = _inproj(
        x_p, x_s, norm1_g[l].reshape(1, d), w_main, w_ab, _pad_lanes(a_log[l]), _pad_lanes(dt_bias[l]), TOKEN_TILE)

    dw = (conv_dw_w[l], conv_dw_b[l].reshape(1, -1), conv_ln_g[l].reshape(1, -1), conv_ln_b[l].reshape(1, -1))
    st_c_p = jnp.zeros((bp, CONV_WIDTH - 1, D_CONV), F32)
    seq_tile = SEQ_TILE if tp % SEQ_TILE == 0 else TOKEN_TILE
    cact_p, conv_p = _conv_branch(glu, st_c_p, *dw, row0=0, bsz=bp, t_len=tp, bb=1, tt=seq_tile)
    cact_s, conv_s = _conv_branch(glu, state_conv[l], *dw, row0=n_p, bsz=bs, t_len=ts, bb=8, tt=ts)

    st_s_p = jnp.zeros((bp, SHORT_WIDTH - 1, 3 * DN_WIDTH), F32)
    s0_p = jnp.zeros((bp, DN_HEADS, DN_HEAD_DIM, DN_HEAD_DIM), F32)
    ng = delta_norm_g[l].reshape(1, -1)
    oact_p, short_p, s_p = _delta_prompt(qkv_pre, z, gb, st_s_p, s0_p, short_conv_w[l], ng,
                                         bsz=bp, t_len=tp, tt=seq_tile)
    oact_s, short_s, s_s = _delta_sample(qkv_pre, z, gb, state_short_conv[l], state_delta[l], short_conv_w[l], ng,
                                         row0=n_p, bsz=bs, seq_len=ts)

    rw = _split_bf16(jnp.pad(router_w[l], ((0, 0), (0, LANES - N_EXPERTS))))
    x2, tg = _mix(x_p, x_s, cact_p, cact_s, oact_p, oact_s, siga, sigb, w_conv_out[l].astype(BF16),
                  w_delta_out[l].astype(BF16), w_merge_out[l].astype(BF16), norm2_g[l].reshape(1, d),
                  rw, _pad_lanes(router_b[l]),
                  MIX_TILE if n_p % MIX_TILE == 0 and n_s % MIX_TILE == 0 else TOKEN_TILE)

    n_blocks = -(-(n * TOP_K) // MOE_TILE) + N_EXPERTS
    n_steps = n // TOKEN_TILE
    dest, table, sched, nvalid = _route(tg[:, :TOP_K].astype(jnp.int32), MOE_TILE, n_blocks, n_steps)
    xs = _dispatch(x2, norm2_g[l].reshape(1, d), table, n_blocks * MOE_TILE, TOKEN_TILE)
    yb = _moe_ffn(xs, sched, nvalid, w_gate_up[l], b_gate_up[l], w_down[l], b_down[l], MOE_TILE, n_blocks)
    y_p, y_s = _combine(dest.reshape(n_steps, -1), yb, x2, tg, final_norm_g.reshape(1, d), n_p, TOKEN_TILE)

    return (y_p.reshape(bp, tp, d), y_s.reshape(bs, ts, d), conv_p[None], short_p[None], s_p[None],
            conv_s[None], short_s[None], s_s[None])
```

```python
import functools

import jax
import jax.numpy as jnp
from jax import lax
from jax.experimental import pallas as pl
from jax.experimental.pallas import tpu as pltpu

F32 = jnp.float32
BF16 = jnp.bfloat16
EPS = 1e-6

LANES = 128
SUBLANES = 8
VMEM_LIMIT_BYTES = 56 * 1024 * 1024
DMA_PRIORITIES = 2

D_CONV = 512
CONV_WIDTH = 31
DN_HEADS = 4
DN_HEAD_DIM = 128
DN_WIDTH = DN_HEADS * DN_HEAD_DIM
SHORT_WIDTH = 4
N_EXPERTS = 32
TOP_K = 4
SWIGLU_LIMIT = 7.0
SWIGLU_ALPHA = 1.702

CHUNK = 128
CONV_HALO = 32
SHORT_HALO = 8

MIX_TILE = 512
TOKEN_TILE = 256
ROUTE_TILE = 512
SEQ_TILE = 512
MOE_TILE = 256
FFN_BLOCKS_PER_STEP = 2


def _sigmoid(x):
    return 1.0 / (1.0 + jnp.exp(-x))


def _silu(x):
    return x * _sigmoid(x)


def _split_bf16(w):
    hi = w.astype(BF16)
    lo = (w - hi.astype(F32)).astype(BF16)
    return jnp.concatenate([hi, lo], axis=-1)


def _dot_split(x, w_split):
    n = w_split.shape[-1] // 2
    x_hi = x.astype(BF16)
    x_lo = (x - x_hi.astype(F32)).astype(BF16)
    r = jnp.dot(x_hi, w_split, preferred_element_type=F32)
    return r[:, :n] + r[:, n:] + jnp.dot(x_lo, w_split[:, :n], preferred_element_type=F32)


def _dot_delta(a, b, dims=(((1,), (0,)), ((), ()))):
    return lax.dot_general(a.astype(BF16), b.astype(BF16), dims, preferred_element_type=F32)


def _params(n_axes):
    return pltpu.CompilerParams(dimension_semantics=("arbitrary",) * n_axes, vmem_limit_bytes=VMEM_LIMIT_BYTES)


def _two_source_specs(tm, d, n_first_tiles):
    first = pl.BlockSpec((tm, d), lambda i, *_: (jnp.minimum(i, n_first_tiles - 1), 0))
    second = pl.BlockSpec((tm, d), lambda i, *_: (jnp.maximum(i - n_first_tiles, 0), 0))
    return first, second


def _inproj_kernel(xp_ref, xs_ref, g_ref, w_ref, wg_ref, wab_ref, alog_ref, dtb_ref,
                   glu_ref, qkv_ref, z_ref, gb_ref, sa_ref, sb_ref, *, n_prompt_tiles):
    x = jnp.where(pl.program_id(0) < n_prompt_tiles, xp_ref[...], xs_ref[...])
    h = x * lax.rsqrt(jnp.mean(x * x, axis=-1, keepdims=True) + EPS) * g_ref[...]
    hb = h.astype(BF16)

    def mm(ref, lo, hi):
        return jnp.dot(hb, ref[:, lo:hi], preferred_element_type=F32)

    o_gate, o_qkv, o_z = D_CONV, 2 * D_CONV, 2 * D_CONV + 3 * DN_WIDTH
    o_ga = o_z + DN_WIDTH
    d = x.shape[-1]
    glu_ref[...] = mm(w_ref, 0, o_gate) * _sigmoid(mm(w_ref, o_gate, o_qkv))
    qkv_ref[...] = mm(w_ref, o_qkv, o_z)
    z_ref[...] = mm(w_ref, o_z, o_ga)
    sa_ref[...] = _sigmoid(mm(wg_ref, 0, d))
    sb_ref[...] = _sigmoid(mm(wg_ref, d, 2 * d))
    ab = _dot_split(h, wab_ref[...])
    xa = ab + dtb_ref[...]
    softplus = jnp.maximum(xa, 0.0) + jnp.log(1.0 + jnp.exp(-jnp.abs(xa)))
    g = -jnp.exp(alog_ref[...]) * softplus
    lane = lax.broadcasted_iota(jnp.int32, ab.shape, 1)
    gb_ref[...] = jnp.where(lane < DN_HEADS, g, _sigmoid(ab))


def _inproj(x_p, x_s, norm_g, w_main, w_gates, w_ab, alog, dtb, tm):
    (n_p, d), n_s = x_p.shape, x_s.shape[0]
    assert n_p % tm == 0 and n_s % tm == 0
    n = n_p + n_s
    wcols = w_main.shape[1]
    row = lambda i: (i, 0)
    const = lambda i: (0, 0)
    outs = [(D_CONV, F32), (3 * DN_WIDTH, F32), (DN_WIDTH, F32), (LANES, F32), (d, F32), (d, F32)]
    return pl.pallas_call(
        functools.partial(_inproj_kernel, n_prompt_tiles=n_p // tm),
        grid=(n // tm,),
        in_specs=[
            *_two_source_specs(tm, d, n_p // tm),
            pl.BlockSpec((1, d), const),
            pl.BlockSpec((d, wcols), const),
            pl.BlockSpec((d, 2 * d), const),
            pl.BlockSpec((d, 2 * LANES), const),
            pl.BlockSpec((1, LANES), const),
            pl.BlockSpec((1, LANES), const),
        ],
        out_specs=[pl.BlockSpec((tm, c), row) for c, _ in outs],
        out_shape=[jax.ShapeDtypeStruct((n, c), dt) for c, dt in outs],
        compiler_params=_params(1),
    )(x_p, x_s, norm_g, w_main, w_gates, w_ab, alog, dtb)


def _conv_kernel(glu_ref, st_ref, w_ref, b_ref, lg_ref, lb_ref, out_ref, nst_ref, e_ref, sh_ref, *, bb, tt, rows):
    t = pl.program_id(1)

    hist = CONV_WIDTH - 1

    @pl.when(t == 0)
    def _():
        e_ref[:, 0:SUBLANES, :] = jnp.zeros((bb, SUBLANES, D_CONV), F32)
        e_ref[:, CONV_HALO - hist:CONV_HALO, :] = st_ref[...]

    for b in range(bb):
        e_ref[b, CONV_HALO:CONV_HALO + tt, :] = glu_ref[b * tt:(b + 1) * tt, :]
    off = CONV_HALO - (CONV_WIDTH - 1)
    span = sh_ref.shape[1]
    for b in range(bb):
        for s in range(1, SUBLANES):
            sh_ref[s - 1] = e_ref[b, s:s + span, :]
        for c in range(tt // rows):
            r0 = c * rows
            acc = jnp.zeros((rows, D_CONV), F32) + b_ref[...]
            for j in range(CONV_WIDTH):
                q, s = divmod(j + off, SUBLANES)
                lo = r0 + q * SUBLANES
                src = e_ref[b, lo:lo + rows, :] if s == 0 else sh_ref[s - 1, lo:lo + rows, :]
                acc = acc + w_ref[j:j + 1, :] * src
            mu = jnp.mean(acc, axis=-1, keepdims=True)
            xc = acc - mu
            var = jnp.mean(xc * xc, axis=-1, keepdims=True)
            y = xc * lax.rsqrt(var + EPS) * lg_ref[...] + lb_ref[...]
            out_ref[b * tt + r0:b * tt + r0 + rows, :] = _silu(y).astype(out_ref.dtype)
    nst_ref[...] = e_ref[:, tt + CONV_HALO - hist:tt + CONV_HALO, :]
    e_ref[:, 0:CONV_HALO, :] = e_ref[:, tt:tt + CONV_HALO, :]


def _conv_branch(glu, state32, dw_w, dw_b, ln_g, ln_b, *, row0, bsz, t_len, bb, tt):
    c = glu.shape[1]
    assert bsz % bb == 0 and t_len % tt == 0 and row0 % (bb * tt) == 0
    nt = t_len // tt
    blk0 = row0 // (bb * tt)
    rows = min(tt, 32)
    kern = functools.partial(_conv_kernel, bb=bb, tt=tt, rows=rows)
    const = lambda b, t: (0, 0)
    return pl.pallas_call(
        kern,
        grid=(bsz // bb, nt),
        in_specs=[
            pl.BlockSpec((bb * tt, c), lambda b, t: (blk0 + b * nt + t, 0)),
            pl.BlockSpec((bb, CONV_WIDTH - 1, c), lambda b, t: (b, 0, 0)),
            pl.BlockSpec((CONV_WIDTH, c), const),
            pl.BlockSpec((1, c), const),
            pl.BlockSpec((1, c), const),
            pl.BlockSpec((1, c), const),
        ],
        out_specs=[
            pl.BlockSpec((bb * tt, c), lambda b, t: (b * nt + t, 0)),
            pl.BlockSpec((bb, CONV_WIDTH - 1, c), lambda b, t: (b, 0, 0)),
        ],
        out_shape=[
            jax.ShapeDtypeStruct((bsz * t_len, c), BF16),
            jax.ShapeDtypeStruct((bsz, CONV_WIDTH - 1, c), F32),
        ],
        scratch_shapes=[pltpu.VMEM((bb, CONV_HALO + tt, c), F32),
                        pltpu.VMEM((SUBLANES - 1, tt + CONV_HALO - SUBLANES, c), F32)],
        compiler_params=_params(2),
    )(glu, state32, dw_w, dw_b, ln_g, ln_b)


def _chunk_masks(seq_len):
    i = lax.broadcasted_iota(jnp.int32, (CHUNK, CHUNK), 0)
    j = lax.broadcasted_iota(jnp.int32, (CHUNK, CHUNK), 1)
    same = (i // seq_len) == (j // seq_len)
    incl = same & (i >= j)
    strict = same & (i > j)
    last = j == (i // seq_len) * seq_len + (seq_len - 1)
    levels = []
    blk = 1
    while blk < seq_len:
        levels.append(((i // (2 * blk)) == (j // (2 * blk))) & (((i // blk) % 2) == 1) & (((j // blk) % 2) == 0))
        blk *= 2
    eye = i == j
    return incl, strict, last, levels, eye


def _lane_col(x, lane):
    return jnp.broadcast_to(x[:, lane:lane + 1], (x.shape[0], LANES))


def _l2norm(x):
    return x * lax.rsqrt(jnp.sum(x * x, axis=-1, keepdims=True) + EPS)


def _select_sum(mask01, x):
    hi = x.astype(BF16)
    r1 = x - hi.astype(F32)
    mid = r1.astype(BF16)
    lo = (r1 - mid.astype(F32)).astype(BF16)
    w = x.shape[1]
    parts = jnp.dot(mask01, jnp.concatenate([hi, mid, lo], axis=1), preferred_element_type=F32)
    return parts[:, :w] + parts[:, w:2 * w] + parts[:, 2 * w:]


def _chunks_prepare(qkvs, gbts, masks, seq_len):
    incl, strict, last, levels, eye = masks
    nt = (((1,), (1,)), ((), ()))
    lower01 = jnp.where(incl, 1.0, 0.0).astype(BF16)
    probs = []
    for qkv, gbt in zip(qkvs, gbts):
        gc = _select_sum(lower01, gbt)
        gct = gc.T
        if seq_len == CHUNK:
            glast = jnp.broadcast_to(gc[CHUNK - 1:CHUNK, :], gc.shape)
        else:
            glast = _select_sum(jnp.where(last, 1.0, 0.0).astype(BF16), gc)
        for h in range(DN_HEADS):
            q = _l2norm(qkv[:, h * DN_HEAD_DIM:(h + 1) * DN_HEAD_DIM]) * (DN_HEAD_DIM ** -0.5)
            k = _l2norm(qkv[:, DN_WIDTH + h * DN_HEAD_DIM:DN_WIDTH + (h + 1) * DN_HEAD_DIM])
            v = qkv[:, 2 * DN_WIDTH + h * DN_HEAD_DIM:2 * DN_WIDTH + (h + 1) * DN_HEAD_DIM]
            gcol = _lane_col(gc, h)
            grow = jnp.broadcast_to(gct[h:h + 1, :], (CHUNK, CHUNK))
            beta = _lane_col(gbt, DN_HEADS + h)
            gl = _lane_col(glast, h)
            decay = jnp.exp(jnp.where(incl, gcol - grow, -jnp.inf))
            egc = jnp.exp(gcol)
            kb = k * beta
            probs.append(dict(q=q, k=k, kb=kb, decay=decay, rhs=jnp.concatenate([v * beta, kb * egc], axis=1),
                              qexp=q * egc, kdec=k * jnp.exp(gl - gcol), egl=jnp.exp(gl)))
    for p in probs:
        p['a'] = jnp.where(strict, _dot_delta(p['kb'], p['k'], nt) * p['decay'], 0.0)
        p['scores'] = _dot_delta(p['q'], p['k'], nt) * p['decay']
    for p in probs:
        p['x'] = jnp.where(eye, 1.0, 0.0) - jnp.where(levels[0], p['a'], 0.0)
    for m in levels[1:]:
        for p in probs:
            p['xa'] = _dot_delta(p['x'], jnp.where(m, p['a'], 0.0))
        for p in probs:
            p['x'] = p['x'] - _dot_delta(p['xa'], p['x'])
    out = []
    for c in range(len(qkvs)):
        heads = []
        for h in range(DN_HEADS):
            p = probs[c * DN_HEADS + h]
            sol = _dot_delta(p['x'], p['rhs'])
            heads.append((sol[:, :DN_HEAD_DIM], sol[:, DN_HEAD_DIM:], p['scores'], p['qexp'], p['kdec'], p['egl']))
        out.append(heads)
    return out


def _gated_out_norm(o, z, ng):
    y = o * lax.rsqrt(jnp.mean(o * o, axis=-1, keepdims=True) + EPS) * ng
    return y * _silu(z)


def _short_conv(e_ref, w_ref, tt):
    off = SHORT_HALO - (SHORT_WIDTH - 1)
    acc = w_ref[0:1, :] * e_ref[off:off + tt, :]
    for j in range(1, SHORT_WIDTH):
        acc = acc + w_ref[j:j + 1, :] * e_ref[off + j:off + j + tt, :]
    return _silu(acc)


def _delta_prompt_kernel(qkv_ref, z_ref, gb_ref, st_ref, s0_ref, w_ref, ng_ref,
                         o_ref, nst_ref, sout_ref, e_ref, s_ref, *, tt):
    t = pl.program_id(1)

    hist = SHORT_WIDTH - 1

    @pl.when(t == 0)
    def _():
        e_ref[0:SUBLANES, :] = jnp.zeros((SUBLANES, 3 * DN_WIDTH), F32)
        e_ref[SHORT_HALO - hist:SHORT_HALO, :] = st_ref[0]
        s_ref[...] = s0_ref[0]

    e_ref[SHORT_HALO:SHORT_HALO + tt, :] = qkv_ref[...]
    qkv = _short_conv(e_ref, w_ref, tt)
    nst_ref[0] = e_ref[tt + SHORT_HALO - hist:tt + SHORT_HALO, :]
    e_ref[0:SHORT_HALO, :] = e_ref[tt:tt + SHORT_HALO, :]

    masks = _chunk_masks(CHUNK)
    tn = (((0,), (0,)), ((), ()))
    n_chunks = tt // CHUNK
    prep = _chunks_prepare([qkv[c * CHUNK:(c + 1) * CHUNK, :] for c in range(n_chunks)],
                           [gb_ref[c * CHUNK:(c + 1) * CHUNK, :] for c in range(n_chunks)], masks, CHUNK)
    heads = range(DN_HEADS)
    s = [s_ref[h] for h in heads]
    for c in range(n_chunks):
        r0 = c * CHUNK
        value, kcum, scores, qexp, kdec, egl = zip(*prep[c])
        both = [_dot_delta(jnp.concatenate([kcum[h], qexp[h]], axis=0), s[h]) for h in heads]
        v_new = [value[h] - both[h][:CHUNK] for h in heads]
        o = [both[h][CHUNK:] + _dot_delta(scores[h], v_new[h]) for h in heads]
        s = [s[h] * egl[h][0:1, :] + _dot_delta(kdec[h], v_new[h], tn) for h in heads]
        for h in heads:
            lanes = slice(h * DN_HEAD_DIM, (h + 1) * DN_HEAD_DIM)
            o_ref[r0:r0 + CHUNK, lanes] = _gated_out_norm(
                o[h], z_ref[r0:r0 + CHUNK, lanes], ng_ref[...]).astype(o_ref.dtype)
    for h in heads:
        s_ref[h] = s[h]
        sout_ref[0, h] = s[h]


def _delta_prompt(qkv_pre, z, gb, state8, s0, conv_w, norm_g, *, bsz, t_len, tt):
    n = bsz * t_len
    assert t_len % tt == 0 and tt % CHUNK == 0
    nt = t_len // tt
    kern = functools.partial(_delta_prompt_kernel, tt=tt)
    tile = lambda b, t: (b * nt + t, 0)
    per_b = lambda b, t: (b, 0, 0)
    per_b4 = lambda b, t: (b, 0, 0, 0)
    return pl.pallas_call(
        kern,
        grid=(bsz, nt),
        in_specs=[
            pl.BlockSpec((tt, 3 * DN_WIDTH), tile),
            pl.BlockSpec((tt, DN_WIDTH), tile),
            pl.BlockSpec((tt, LANES), tile),
            pl.BlockSpec((1, SHORT_WIDTH - 1, 3 * DN_WIDTH), per_b),
            pl.BlockSpec((1, DN_HEADS, DN_HEAD_DIM, DN_HEAD_DIM), per_b4),
            pl.BlockSpec((SHORT_WIDTH, 3 * DN_WIDTH), lambda b, t: (0, 0)),
            pl.BlockSpec((1, DN_HEAD_DIM), lambda b, t: (0, 0)),
        ],
        out_specs=[
            pl.BlockSpec((tt, DN_WIDTH), tile),
            pl.BlockSpec((1, SHORT_WIDTH - 1, 3 * DN_WIDTH), per_b),
            pl.BlockSpec((1, DN_HEADS, DN_HEAD_DIM, DN_HEAD_DIM), per_b4),
        ],
        out_shape=[
            jax.ShapeDtypeStruct((n, DN_WIDTH), BF16),
            jax.ShapeDtypeStruct((bsz, SHORT_WIDTH - 1, 3 * DN_WIDTH), F32),
            jax.ShapeDtypeStruct((bsz, DN_HEADS, DN_HEAD_DIM, DN_HEAD_DIM), F32),
        ],
        scratch_shapes=[
            pltpu.VMEM((SHORT_HALO + tt, 3 * DN_WIDTH), F32),
            pltpu.VMEM((DN_HEADS, DN_HEAD_DIM, DN_HEAD_DIM), F32),
        ],
        compiler_params=_params(2),
    )(qkv_pre, z, gb, state8, s0, conv_w, norm_g)


def _delta_sample_kernel(qkv_ref, z_ref, gb_ref, st_ref, s0_ref, w_ref, ng_ref, o_ref, nst_ref, sout_ref,
                         e_ref, *, nseq, seq_len):
    qkv_rows = []
    hist = SHORT_WIDTH - 1
    e_ref[0:SUBLANES, :] = jnp.zeros((SUBLANES, 3 * DN_WIDTH), F32)
    for b in range(nseq):
        e_ref[SHORT_HALO - hist:SHORT_HALO, :] = st_ref[b]
        e_ref[SHORT_HALO:SHORT_HALO + seq_len, :] = qkv_ref[b * seq_len:(b + 1) * seq_len, :]
        qkv_rows.append(_short_conv(e_ref, w_ref, seq_len))
        nst_ref[b] = e_ref[seq_len + SHORT_HALO - hist:seq_len + SHORT_HALO, :]
    qkv = jnp.concatenate(qkv_rows, axis=0)
    masks = _chunk_masks(seq_len)
    prep = _chunks_prepare([qkv], [gb_ref[...]], masks, seq_len)[0]
    tn = (((0,), (0,)), ((), ()))
    rows = [slice(b * seq_len, (b + 1) * seq_len) for b in range(nseq)]
    both = [[_dot_delta(jnp.concatenate([prep[h][1][r], prep[h][3][r]], axis=0), s0_ref[b, h])
             for b, r in enumerate(rows)] for h in range(DN_HEADS)]
    v_new = [[prep[h][0][r] - both[h][b][:seq_len] for b, r in enumerate(rows)] for h in range(DN_HEADS)]
    for h in range(DN_HEADS):
        kdec, egl = prep[h][4], prep[h][5]
        for b, r in enumerate(rows):
            sout_ref[b, h] = (s0_ref[b, h] * egl[b * seq_len:b * seq_len + 1, :]
                              + _dot_delta(kdec[r], v_new[h][b], tn))
    for h in range(DN_HEADS):
        o = (jnp.concatenate([both[h][b][seq_len:] for b in range(nseq)], axis=0)
             + _dot_delta(prep[h][2], jnp.concatenate(v_new[h], axis=0)))
        lanes = slice(h * DN_HEAD_DIM, (h + 1) * DN_HEAD_DIM)
        o_ref[:, lanes] = _gated_out_norm(o, z_ref[:, lanes], ng_ref[...]).astype(o_ref.dtype)


def _delta_sample(qkv_pre, z, gb, state8, s0, conv_w, norm_g, *, row0, bsz, seq_len):
    n = bsz * seq_len
    assert CHUNK % seq_len == 0
    nseq = CHUNK // seq_len
    assert bsz % nseq == 0 and row0 % CHUNK == 0
    blk0 = row0 // CHUNK
    kern = functools.partial(_delta_sample_kernel, nseq=nseq, seq_len=seq_len)
    tile = lambda i: (blk0 + i, 0)
    blk3 = lambda i: (i, 0, 0)
    blk4 = lambda i: (i, 0, 0, 0)
    return pl.pallas_call(
        kern,
        grid=(bsz // nseq,),
        in_specs=[
            pl.BlockSpec((CHUNK, 3 * DN_WIDTH), tile),
            pl.BlockSpec((CHUNK, DN_WIDTH), tile),
            pl.BlockSpec((CHUNK, LANES), tile),
            pl.BlockSpec((nseq, SHORT_WIDTH - 1, 3 * DN_WIDTH), blk3),
            pl.BlockSpec((nseq, DN_HEADS, DN_HEAD_DIM, DN_HEAD_DIM), blk4),
            pl.BlockSpec((SHORT_WIDTH, 3 * DN_WIDTH), lambda i: (0, 0)),
            pl.BlockSpec((1, DN_HEAD_DIM), lambda i: (0, 0)),
        ],
        out_specs=[
            pl.BlockSpec((CHUNK, DN_WIDTH), lambda i: (i, 0)),
            pl.BlockSpec((nseq, SHORT_WIDTH - 1, 3 * DN_WIDTH), blk3),
            pl.BlockSpec((nseq, DN_HEADS, DN_HEAD_DIM, DN_HEAD_DIM), blk4),
        ],
        out_shape=[
            jax.ShapeDtypeStruct((n, DN_WIDTH), BF16),
            jax.ShapeDtypeStruct((bsz, SHORT_WIDTH - 1, 3 * DN_WIDTH), F32),
            jax.ShapeDtypeStruct((bsz, DN_HEADS, DN_HEAD_DIM, DN_HEAD_DIM), F32),
        ],
        scratch_shapes=[pltpu.VMEM((SHORT_HALO + seq_len, 3 * DN_WIDTH), F32)],
        compiler_params=_params(1),
    )(qkv_pre, z, gb, state8, s0, conv_w, norm_g)


def _mix_kernel(xp_ref, xs_ref, cap_ref, cas_ref, oap_ref, oas_ref, sa_ref, sb_ref, wc_ref, wd_ref, wm_ref,
                g2_ref, rw_ref, rb_ref, x2_ref, tr_ref, *, n_prompt_tiles):
    is_prompt = pl.program_id(0) < n_prompt_tiles
    x = jnp.where(is_prompt, xp_ref[...], xs_ref[...])
    ca = jnp.where(is_prompt, cap_ref[...], cas_ref[...])
    oa = jnp.where(is_prompt, oap_ref[...], oas_ref[...])
    ya = jnp.dot(ca, wc_ref[...], preferred_element_type=F32)
    yb = jnp.dot(oa, wd_ref[...], preferred_element_type=F32)
    mixed = sa_ref[...] * ya + sb_ref[...] * yb
    x2 = x + jnp.dot(mixed.astype(BF16), wm_ref[...], preferred_element_type=F32)
    x2_ref[...] = x2
    h2 = x2 * lax.rsqrt(jnp.mean(x2 * x2, axis=-1, keepdims=True) + EPS) * g2_ref[...]
    logits = _dot_split(h2, rw_ref[...]) + rb_ref[...]
    lt = logits.T[:N_EXPERTS, :]
    tokens = lt.shape[1]
    row = lax.broadcasted_iota(jnp.int32, lt.shape, 0).astype(F32)
    top_vals, top_idx = [], []
    for k in range(TOP_K):
        m = jnp.max(lt, axis=0, keepdims=True)
        idx = jnp.min(jnp.where(lt == m, row, float(N_EXPERTS)), axis=0, keepdims=True)
        top_vals.append(m)
        top_idx.append(idx)
        lt = jnp.where(row == idx, -jnp.inf, lt)
    exps = [jnp.exp(v - top_vals[0]) for v in top_vals]
    den = exps[0] + exps[1] + exps[2] + exps[3]
    slot = lax.broadcasted_iota(jnp.int32, (2 * TOP_K, tokens), 0)
    packed = jnp.zeros((2 * TOP_K, tokens), F32)
    for k in range(TOP_K):
        packed = jnp.where(slot == k, top_idx[k], packed)
        packed = jnp.where(slot == TOP_K + k, exps[k] / den, packed)
    packed = jnp.concatenate([packed, jnp.zeros((LANES - 2 * TOP_K, tokens), F32)], axis=0)
    tr_ref[...] = packed.T


def _mix(x_p, x_s, cact_p, cact_s, oact_p, oact_s, siga, sigb, w_conv_out, w_delta_out, w_merge_out, norm2_g,
         router_w, router_b, tm):
    (n_p, d), n_s = x_p.shape, x_s.shape[0]
    n = n_p + n_s
    row = lambda i: (i, 0)
    const = lambda i: (0, 0)
    return pl.pallas_call(
        functools.partial(_mix_kernel, n_prompt_tiles=n_p // tm),
        grid=(n // tm,),
        in_specs=[
            *_two_source_specs(tm, d, n_p // tm),
            *_two_source_specs(tm, D_CONV, n_p // tm),
            *_two_source_specs(tm, DN_WIDTH, n_p // tm),
            pl.BlockSpec((tm, d), row),
            pl.BlockSpec((tm, d), row),
            pl.BlockSpec((D_CONV, d), const),
            pl.BlockSpec((DN_WIDTH, d), const),
            pl.BlockSpec((d, d), const),
            pl.BlockSpec((1, d), const),
            pl.BlockSpec((d, 2 * LANES), const),
            pl.BlockSpec((1, LANES), const),
        ],
        out_specs=[
            pl.BlockSpec((tm, d), row),
            pl.BlockSpec((tm, LANES), row),
        ],
        out_shape=[
            jax.ShapeDtypeStruct((n, d), F32),
            jax.ShapeDtypeStruct((n, LANES), F32),
        ],
        compiler_params=_params(1),
    )(x_p, x_s, cact_p, cact_s, oact_p, oact_s, siga, sigb, w_conv_out, w_delta_out, w_merge_out, norm2_g,
      router_w, router_b)


def _fill_rows_per_step(n_fill, n_steps):
    per_step = SUBLANES
    while per_step * n_steps < n_fill:
        per_step *= 2
    assert n_fill % per_step == 0
    return per_step


def _route(top_idx, tm, n_blocks, n_steps):
    n = top_idx.shape[0]
    n_fill = n_blocks * tm - n * TOP_K
    assert n_fill == N_EXPERTS * tm
    experts = jnp.arange(N_EXPERTS, dtype=jnp.int32)
    chosen = top_idx[:, :, None] == experts[None, None, :]
    per_token = jnp.sum(chosen, axis=1, dtype=jnp.int32)
    csum = jnp.cumsum(per_token, axis=0)
    counts = csum[-1]
    padded = (counts + tm - 1) // tm * tm
    pad_end = jnp.cumsum(padded)
    pad_start = pad_end - padded
    first_free = (pad_start[None, :] + csum - per_token)[:, None, :]
    dest = jnp.sum(jnp.where(chosen, first_free, 0), axis=2).reshape(-1)
    nvalid = (pad_end[-1] // tm).astype(jnp.int32)
    blk = jnp.arange(n_blocks, dtype=jnp.int32)
    owner = jnp.sum((pad_end[None, :] <= (blk * tm)[:, None]).astype(jnp.int32), axis=1)
    block_e = jnp.minimum(owner, N_EXPERTS - 1)
    block_e = jnp.where(blk < nvalid, block_e, jnp.sum(jnp.where(blk == nvalid - 1, block_e, 0)))
    present = counts > 0
    later = present[None, :] & (experts[None, :] > experts[:, None])
    next_present = jnp.min(jnp.where(later, experts[None, :], N_EXPERTS), axis=1)
    next_present = jnp.where(next_present == N_EXPERTS, -1, next_present)
    parity = (jnp.cumsum(present.astype(jnp.int32)) - 1) % 2
    of_block = (block_e[:, None] == experts[None, :]).astype(jnp.int32)
    sched = jnp.stack([block_e, jnp.sum(of_block * next_present[None, :], axis=1),
                       jnp.sum(of_block * parity[None, :], axis=1)])
    n_pad = padded - counts
    spill = tm - n_pad
    spill_start = pad_end[-1] + jnp.cumsum(spill) - spill
    j = jnp.arange(tm, dtype=jnp.int32)[None, :]
    fill = jnp.where(j < n_pad[:, None], (pad_start + counts)[:, None] + j, (spill_start - n_pad)[:, None] + j)
    fill_step = _fill_rows_per_step(n_fill, n_steps)
    fill = jnp.pad(fill.reshape(-1, fill_step), ((0, n_steps - n_fill // fill_step), (0, 0)))
    table = jnp.concatenate([dest.reshape(n_steps, -1), fill], axis=1)
    return dest.reshape(n, TOP_K), table, sched, nvalid.reshape(1)


def _to_token_tiles(x, ref, row0=0):
    t = x.shape[0]
    for s in range(SUBLANES):
        ref[pl.ds(row0 + s, t, stride=SUBLANES), :] = x[:, s * LANES:(s + 1) * LANES]


def _from_token_tiles(ref, t, row0=0):
    return jnp.concatenate([ref[pl.ds(row0 + s, t, stride=SUBLANES), :] for s in range(SUBLANES)], axis=1)


def _tile(ref, row):
    return ref.at[pl.ds(pl.multiple_of(row * SUBLANES, SUBLANES), SUBLANES)]


def _dispatch_kernel(tab_ref, x2_ref, g2_ref, xs_ref, buf, sems, *, tokens, fill_step, n_fill_steps):
    i = pl.program_id(0)
    last = pl.num_programs(0) - 1
    slot = i % 2
    base = slot * tokens
    fill_sem = 2

    x2 = x2_ref[...]
    h2 = x2 * lax.rsqrt(jnp.mean(x2 * x2, axis=-1, keepdims=True) + EPS) * g2_ref[...]
    _to_token_tiles(h2, buf, pl.multiple_of(base * SUBLANES, SUBLANES))

    def wait_tiles(sem_idx, count):
        while count > 0:
            rows = min(count, tokens) * SUBLANES
            pltpu.make_async_copy(buf.at[pl.ds(0, rows)], xs_ref.at[pl.ds(0, rows)], sems.at[sem_idx]).wait()
            count -= min(count, tokens)

    for t in range(tokens):
        for k in range(TOP_K):
            pltpu.make_async_copy(_tile(buf, base + t), _tile(xs_ref, tab_ref[i, t * TOP_K + k]),
                                  sems.at[slot]).start(priority=k % DMA_PRIORITIES)

    @pl.when(i < n_fill_steps)
    def _():
        for p in range(fill_step):
            pltpu.make_async_copy(_tile(buf, base), _tile(xs_ref, tab_ref[i, tokens * TOP_K + p]),
                                  sems.at[fill_sem]).start(priority=p % DMA_PRIORITIES)
        wait_tiles(fill_sem, fill_step)

    @pl.when(i > 0)
    def _():
        wait_tiles(1 - slot, tokens * TOP_K)

    @pl.when(i == last)
    def _():
        wait_tiles(slot, tokens * TOP_K)


def _dispatch(x2, norm_g, table, n_rows, tokens):
    n, d = x2.shape
    assert d == SUBLANES * LANES
    fill_step = table.shape[1] - tokens * TOP_K
    n_fill_steps = (n_rows - n * TOP_K) // fill_step
    return pl.pallas_call(
        functools.partial(_dispatch_kernel, tokens=tokens, fill_step=fill_step, n_fill_steps=n_fill_steps),
        grid_spec=pltpu.PrefetchScalarGridSpec(
            num_scalar_prefetch=1,
            grid=(n // tokens,),
            in_specs=[pl.BlockSpec((tokens, d), lambda i, tab: (i, 0)),
                      pl.BlockSpec((1, d), lambda i, tab: (0, 0))],
            out_specs=pl.BlockSpec(memory_space=pl.ANY),
            scratch_shapes=[pltpu.VMEM((2 * tokens * SUBLANES, LANES), F32), pltpu.SemaphoreType.DMA((3,))],
        ),
        out_shape=jax.ShapeDtypeStruct((n_rows * SUBLANES, LANES), F32),
        compiler_params=_params(1),
    )(table, x2, norm_g)


def _moe_ffn_kernel(sched_ref, nv_ref, xs_ref, wgu_hbm, bgu_ref, wd_hbm, bd_ref, out_ref,
                    wgu_f32, wd_f32, wgu_bf, wd_bf, sems, *, blocks_per_step):
    tm = xs_ref.shape[0] // SUBLANES // blocks_per_step
    f = wd_bf.shape[0]

    def weight_copies(e, h):
        return (pltpu.make_async_copy(wgu_hbm.at[e], wgu_f32.at[h], sems.at[h]),
                pltpu.make_async_copy(wd_hbm.at[e], wd_f32.at[h], sems.at[h]))

    def one_block(i, r0):
        expert, next_expert, half = sched_ref[0, i], sched_ref[1, i], sched_ref[2, i]

        @pl.when(i == 0)
        def _():
            for c in weight_copies(expert, half):
                c.start()

        @pl.when(jnp.logical_or(i == 0, expert != sched_ref[0, jnp.maximum(i - 1, 0)]))
        def _():
            for c in weight_copies(expert, half):
                c.wait()
            wgu_bf[...] = wgu_f32[half].astype(BF16)
            wd_bf[...] = wd_f32[half].astype(BF16)

            @pl.when(next_expert >= 0)
            def _():
                for c in weight_copies(next_expert, 1 - half):
                    c.start()

        @pl.when(i < nv_ref[0])
        def _():
            x = _from_token_tiles(xs_ref, tm, r0).astype(BF16)
            gu = jnp.dot(x, wgu_bf[...], preferred_element_type=F32) + bgu_ref[expert]
            gt = jnp.minimum(gu[:, :f], SWIGLU_LIMIT)
            up = jnp.clip(gu[:, f:], -SWIGLU_LIMIT, SWIGLU_LIMIT)
            act = (up + 1.0) * (gt * _sigmoid(SWIGLU_ALPHA * gt))
            y = jnp.dot(act.astype(BF16), wd_bf[...], preferred_element_type=F32) + bd_ref[expert]
            _to_token_tiles(y, out_ref, r0)

        @pl.when(i >= nv_ref[0])
        def _():
            out_ref[r0:r0 + tm * SUBLANES, :] = jnp.zeros((tm * SUBLANES, LANES), out_ref.dtype)

    for h in range(blocks_per_step):
        one_block(pl.program_id(0) * blocks_per_step + h, h * tm * SUBLANES)


def _moe_ffn(xs, sched, nvalid, w_gate_up, b_gate_up, w_down, b_down, tm, n_blocks):
    ne, d, f2 = w_gate_up.shape
    f = f2 // 2
    per_step = FFN_BLOCKS_PER_STEP if n_blocks % FFN_BLOCKS_PER_STEP == 0 else 1
    rows = per_step * tm * SUBLANES
    used = lambda p, sc, nv: (jnp.minimum(p, (nv[0] - 1) // per_step), 0)
    whole = lambda p, sc, nv: (0, 0, 0)
    return pl.pallas_call(
        functools.partial(_moe_ffn_kernel, blocks_per_step=per_step),
        grid_spec=pltpu.PrefetchScalarGridSpec(
            num_scalar_prefetch=2,
            grid=(n_blocks // per_step,),
            in_specs=[
                pl.BlockSpec((rows, LANES), used),
                pl.BlockSpec(memory_space=pl.ANY),
                pl.BlockSpec((ne, 1, f2), whole),
                pl.BlockSpec(memory_space=pl.ANY),
                pl.BlockSpec((ne, 1, d), whole),
            ],
            out_specs=pl.BlockSpec((rows, LANES), lambda p, sc, nv: (p, 0)),
            scratch_shapes=[pltpu.VMEM((2, d, f2), F32), pltpu.VMEM((2, f, d), F32),
                            pltpu.VMEM((d, f2), BF16), pltpu.VMEM((f, d), BF16), pltpu.SemaphoreType.DMA((2,))],
        ),
        out_shape=jax.ShapeDtypeStruct((n_blocks * tm * SUBLANES, LANES), F32),
        compiler_params=_params(1),
    )(sched, nvalid, xs, w_gate_up, b_gate_up.reshape(ne, 1, f2), w_down, b_down.reshape(ne, 1, d))


def _combine_kernel(slot_ref, yb_ref, x2_ref, tg_ref, fg_ref, yp_ref, ys_ref, buf, sems, *, tc, n_prompt_tiles):
    i = pl.program_id(0)
    n_steps = pl.num_programs(0)

    def region(slot, k):
        return (slot * TOP_K + k) * tc

    def fetch(step, slot):
        for t in range(tc):
            for k in range(TOP_K):
                pltpu.make_async_copy(
                    _tile(yb_ref, slot_ref[step, t * TOP_K + k]), _tile(buf, region(slot, k) + t),
                    sems.at[slot]).start(priority=k % DMA_PRIORITIES)

    @pl.when(i == 0)
    def _():
        fetch(0, 0)

    @pl.when(i + 1 < n_steps)
    def _():
        fetch(i + 1, (i + 1) % 2)

    slot = i % 2
    for k in range(TOP_K):
        pltpu.make_async_copy(yb_ref.at[pl.ds(0, tc * SUBLANES)], buf.at[pl.ds(0, tc * SUBLANES)],
                              sems.at[slot]).wait()
    tg = tg_ref[...]
    y = x2_ref[...]
    for k in range(TOP_K):
        rows = _from_token_tiles(buf, tc, pl.multiple_of(region(slot, k) * SUBLANES, SUBLANES))
        y = y + tg[:, TOP_K + k:TOP_K + k + 1] * rows
    out = y * lax.rsqrt(jnp.mean(y * y, axis=-1, keepdims=True) + EPS) * fg_ref[...]

    @pl.when(i < n_prompt_tiles)
    def _():
        yp_ref[...] = out

    @pl.when(i >= n_prompt_tiles)
    def _():
        ys_ref[...] = out


def _combine(slot2d, yb, x2, tg, final_g, n_p, tc):
    n, d = x2.shape
    n_s = n - n_p
    assert n_p % tc == 0 and n_s % tc == 0
    npt = n_p // tc
    kern = functools.partial(_combine_kernel, tc=tc, n_prompt_tiles=npt)
    out_p, out_s = _two_source_specs(tc, d, npt)
    return pl.pallas_call(
        kern,
        grid_spec=pltpu.PrefetchScalarGridSpec(
            num_scalar_prefetch=1,
            grid=(n // tc,),
            in_specs=[
                pl.BlockSpec(memory_space=pl.ANY),
                pl.BlockSpec((tc, d), lambda i, s: (i, 0)),
                pl.BlockSpec((tc, LANES), lambda i, s: (i, 0)),
                pl.BlockSpec((1, d), lambda i, s: (0, 0)),
            ],
            out_specs=[out_p, out_s],
            scratch_shapes=[pltpu.VMEM((2 * TOP_K * tc * SUBLANES, LANES), F32), pltpu.SemaphoreType.DMA((2,))],
        ),
        out_shape=[jax.ShapeDtypeStruct((n_p, d), F32), jax.ShapeDtypeStruct((n_s, d), F32)],
        compiler_params=_params(1),
    )(slot2d, yb, x2, tg, final_g)


def _pad_lanes(v, width=LANES):
    v = v.reshape(1, -1)
    return jnp.pad(v, ((0, 0), (0, width - v.shape[1])))


def kernel(x_prompt, x_sample, state_conv, state_short_conv, state_delta, norm1_g, w_in, conv_dw_w,
           conv_dw_b, conv_ln_g, conv_ln_b, w_conv_out, short_conv_w, a_log, dt_bias, delta_norm_g,
           w_delta_out, w_merge_out, norm2_g, router_w, router_b, w_gate_up, b_gate_up, w_down, b_down,
           final_norm_g):
    depth = w_in.shape[0]
    assert depth == 1
    bp, tp, d = x_prompt.shape
    bs, ts, _ = x_sample.shape
    n_p, n_s = bp * tp, bs * ts
    n = n_p + n_s
    l = 0
    x_p = x_prompt.reshape(n_p, d)
    x_s = x_sample.reshape(n_s, d)

    o_ab = 2 * D_CONV + 4 * DN_WIDTH
    w = w_in[l]
    w_main = w[:, :o_ab].astype(BF16)
    w_gates = w[:, o_ab + 2 * DN_HEADS:].astype(BF16)
    w_ab = _split_bf16(jnp.pad(w[:, o_ab:o_ab + 2 * DN_HEADS], ((0, 0), (0, LANES - 2 * DN_HEADS))))

    glu, qkv_pre, z, gb, siga, sigb = _inproj(
        x_p, x_s, norm1_g[l].reshape(1, d), w_main, w_gates, w_ab, _pad_lanes(a_log[l]), _pad_lanes(dt_bias[l]),
        TOKEN_TILE)

    dw = (conv_dw_w[l], conv_dw_b[l].reshape(1, -1), conv_ln_g[l].reshape(1, -1), conv_ln_b[l].reshape(1, -1))
    st_c_p = jnp.zeros((bp, CONV_WIDTH - 1, D_CONV), F32)
    seq_tile = SEQ_TILE if tp % SEQ_TILE == 0 else TOKEN_TILE
    cact_p, conv_p = _conv_branch(glu, st_c_p, *dw, row0=0, bsz=bp, t_len=tp, bb=1, tt=seq_tile)
    cact_s, conv_s = _conv_branch(glu, state_conv[l], *dw, row0=n_p, bsz=bs, t_len=ts, bb=8, tt=ts)

    st_s_p = jnp.zeros((bp, SHORT_WIDTH - 1, 3 * DN_WIDTH), F32)
    s0_p = jnp.zeros((bp, DN_HEADS, DN_HEAD_DIM, DN_HEAD_DIM), F32)
    ng = delta_norm_g[l].reshape(1, -1)
    oact_p, short_p, s_p = _delta_prompt(qkv_pre, z, gb, st_s_p, s0_p, short_conv_w[l], ng,
                                         bsz=bp, t_len=tp, tt=seq_tile)
    oact_s, short_s, s_s = _delta_sample(qkv_pre, z, gb, state_short_conv[l], state_delta[l], short_conv_w[l], ng,
                                         row0=n_p, bsz=bs, seq_len=ts)

    rw = _split_bf16(jnp.pad(router_w[l], ((0, 0), (0, LANES - N_EXPERTS))))
    x2, tg = _mix(x_p, x_s, cact_p, cact_s, oact_p, oact_s, siga, sigb, w_conv_out[l].astype(BF16),
                  w_delta_out[l].astype(BF16), w_merge_out[l].astype(BF16), norm2_g[l].reshape(1, d),
                  rw, _pad_lanes(router_b[l]),
                  MIX_TILE if n_p % MIX_TILE == 0 and n_s % MIX_TILE == 0 else TOKEN_TILE)

    n_blocks = -(-(n * TOP_K) // MOE_TILE) + N_EXPERTS
    route_tile = ROUTE_TILE if n_p % ROUTE_TILE == 0 and n_s % ROUTE_TILE == 0 else TOKEN_TILE
    n_steps = n // route_tile
    dest, table, sched, nvalid = _route(tg[:, :TOP_K].astype(jnp.int32), MOE_TILE, n_blocks, n_steps)
    xs = _dispatch(x2, norm2_g[l].reshape(1, d), table, n_blocks * MOE_TILE, route_tile)
    yb = _moe_ffn(xs, sched, nvalid, w_gate_up[l], b_gate_up[l], w_down[l], b_down[l], MOE_TILE, n_blocks)
    y_p, y_s = _combine(dest.reshape(n_steps, -1), yb, x2, tg, final_norm_g.reshape(1, d), n_p, route_tile)

    return (y_p.reshape(bp, tp, d), y_s.reshape(bs, ts, d), conv_p[None], short_p[None], s_p[None],
            conv_s[None], short_s[None], s_s[None])
```

```python
import functools

import jax
import jax.numpy as jnp
from jax import lax
from jax.experimental import pallas as pl
from jax.experimental.pallas import tpu as pltpu

F32 = jnp.float32
BF16 = jnp.bfloat16
EPS = 1e-6

LANES = 128
SUBLANES = 8
VMEM_LIMIT_BYTES = 56 * 1024 * 1024
DMA_PRIORITIES = 2

D_CONV = 512
CONV_WIDTH = 31
DN_HEADS = 4
DN_HEAD_DIM = 128
DN_WIDTH = DN_HEADS * DN_HEAD_DIM
SHORT_WIDTH = 4
N_EXPERTS = 32
TOP_K = 4
SWIGLU_LIMIT = 7.0
SWIGLU_ALPHA = 1.702

CHUNK = 128
CONV_HALO = 32
SHORT_HALO = 8

MIX_TILE = 512
TOKEN_TILE = 256
ROUTE_TILE = 512
SEQ_TILE = 512
MOE_TILE = 256
FFN_BLOCKS_PER_STEP = 2


def _sigmoid(x):
    return 1.0 / (1.0 + jnp.exp(-x))


def _silu(x):
    return x * _sigmoid(x)


def _split_bf16(w):
    hi = w.astype(BF16)
    lo = (w - hi.astype(F32)).astype(BF16)
    return jnp.concatenate([hi, lo], axis=-1)


def _dot_split(x, w_split):
    n = w_split.shape[-1] // 2
    x_hi = x.astype(BF16)
    x_lo = (x - x_hi.astype(F32)).astype(BF16)
    r = jnp.dot(x_hi, w_split, preferred_element_type=F32)
    return r[:, :n] + r[:, n:] + jnp.dot(x_lo, w_split[:, :n], preferred_element_type=F32)


def _dot_delta(a, b, dims=(((1,), (0,)), ((), ()))):
    return lax.dot_general(a.astype(BF16), b.astype(BF16), dims, preferred_element_type=F32)


def _params(n_axes):
    return pltpu.CompilerParams(dimension_semantics=("arbitrary",) * n_axes, vmem_limit_bytes=VMEM_LIMIT_BYTES)


def _two_source_specs(tm, d, n_first_tiles):
    first = pl.BlockSpec((tm, d), lambda i, *_: (jnp.minimum(i, n_first_tiles - 1), 0))
    second = pl.BlockSpec((tm, d), lambda i, *_: (jnp.maximum(i - n_first_tiles, 0), 0))
    return first, second


def _inproj_kernel(xp_ref, xs_ref, g_ref, w_ref, wg_ref, wab_ref, alog_ref, dtb_ref,
                   glu_ref, qkv_ref, z_ref, gb_ref, sa_ref, sb_ref, *, n_prompt_tiles):
    x = jnp.where(pl.program_id(0) < n_prompt_tiles, xp_ref[...], xs_ref[...])
    h = x * lax.rsqrt(jnp.mean(x * x, axis=-1, keepdims=True) + EPS) * g_ref[...]
    hb = h.astype(BF16)

    def mm(ref, lo, hi):
        return jnp.dot(hb, ref[:, lo:hi], preferred_element_type=F32)

    o_gate, o_qkv, o_z = D_CONV, 2 * D_CONV, 2 * D_CONV + 3 * DN_WIDTH
    o_ga = o_z + DN_WIDTH
    d = x.shape[-1]
    glu_ref[...] = mm(w_ref, 0, o_gate) * _sigmoid(mm(w_ref, o_gate, o_qkv))
    qkv_ref[...] = mm(w_ref, o_qkv, o_z)
    z_ref[...] = mm(w_ref, o_z, o_ga)
    sa_ref[...] = _sigmoid(mm(wg_ref, 0, d))
    sb_ref[...] = _sigmoid(mm(wg_ref, d, 2 * d))
    ab = _dot_split(h, wab_ref[...])
    xa = ab + dtb_ref[...]
    softplus = jnp.maximum(xa, 0.0) + jnp.log(1.0 + jnp.exp(-jnp.abs(xa)))
    g = -jnp.exp(alog_ref[...]) * softplus
    lane = lax.broadcasted_iota(jnp.int32, ab.shape, 1)
    gb_ref[...] = jnp.where(lane < DN_HEADS, g, _sigmoid(ab))


def _inproj(x_p, x_s, norm_g, w_main, w_gates, w_ab, alog, dtb, tm):
    (n_p, d), n_s = x_p.shape, x_s.shape[0]
    assert n_p % tm == 0 and n_s % tm == 0
    n = n_p + n_s
    wcols = w_main.shape[1]
    row = lambda i: (i, 0)
    const = lambda i: (0, 0)
    outs = [(D_CONV, F32), (3 * DN_WIDTH, F32), (DN_WIDTH, F32), (LANES, F32), (d, F32), (d, F32)]
    return pl.pallas_call(
        functools.partial(_inproj_kernel, n_prompt_tiles=n_p // tm),
        grid=(n // tm,),
        in_specs=[
            *_two_source_specs(tm, d, n_p // tm),
            pl.BlockSpec((1, d), const),
            pl.BlockSpec((d, wcols), const),
            pl.BlockSpec((d, 2 * d), const),
            pl.BlockSpec((d, 2 * LANES), const),
            pl.BlockSpec((1, LANES), const),
            pl.BlockSpec((1, LANES), const),
        ],
        out_specs=[pl.BlockSpec((tm, c), row) for c, _ in outs],
        out_shape=[jax.ShapeDtypeStruct((n, c), dt) for c, dt in outs],
        compiler_params=_params(1),
    )(x_p, x_s, norm_g, w_main, w_gates, w_ab, alog, dtb)


def _conv_kernel(glu_ref, st_ref, w_ref, b_ref, lg_ref, lb_ref, out_ref, nst_ref, e_ref, sh_ref, *, bb, tt, rows):
    t = pl.program_id(1)

    hist = CONV_WIDTH - 1

    @pl.when(t == 0)
    def _():
        e_ref[:, 0:SUBLANES, :] = jnp.zeros((bb, SUBLANES, D_CONV), F32)
        e_ref[:, CONV_HALO - hist:CONV_HALO, :] = st_ref[...]

    for b in range(bb):
        e_ref[b, CONV_HALO:CONV_HALO + tt, :] = glu_ref[b * tt:(b + 1) * tt, :]
    off = CONV_HALO - (CONV_WIDTH - 1)
    span = sh_ref.shape[1]
    for b in range(bb):
        for s in range(1, SUBLANES):
            sh_ref[s - 1] = e_ref[b, s:s + span, :]
        for c in range(tt // rows):
            r0 = c * rows
            acc = jnp.zeros((rows, D_CONV), F32) + b_ref[...]
            for j in range(CONV_WIDTH):
                q, s = divmod(j + off, SUBLANES)
                lo = r0 + q * SUBLANES
                src = e_ref[b, lo:lo + rows, :] if s == 0 else sh_ref[s - 1, lo:lo + rows, :]
                acc = acc + w_ref[j:j + 1, :] * src
            mu = jnp.mean(acc, axis=-1, keepdims=True)
            xc = acc - mu
            var = jnp.mean(xc * xc, axis=-1, keepdims=True)
            y = xc * lax.rsqrt(var + EPS) * lg_ref[...] + lb_ref[...]
            out_ref[b * tt + r0:b * tt + r0 + rows, :] = _silu(y).astype(out_ref.dtype)
    nst_ref[...] = e_ref[:, tt + CONV_HALO - hist:tt + CONV_HALO, :]
    e_ref[:, 0:CONV_HALO, :] = e_ref[:, tt:tt + CONV_HALO, :]


def _conv_branch(glu, state32, dw_w, dw_b, ln_g, ln_b, *, row0, bsz, t_len, bb, tt):
    c = glu.shape[1]
    assert bsz % bb == 0 and t_len % tt == 0 and row0 % (bb * tt) == 0
    nt = t_len // tt
    blk0 = row0 // (bb * tt)
    rows = min(tt, 32)
    kern = functools.partial(_conv_kernel, bb=bb, tt=tt, rows=rows)
    const = lambda b, t: (0, 0)
    return pl.pallas_call(
        kern,
        grid=(bsz // bb, nt),
        in_specs=[
            pl.BlockSpec((bb * tt, c), lambda b, t: (blk0 + b * nt + t, 0)),
            pl.BlockSpec((None, bb, CONV_WIDTH - 1, c), lambda b, t: (0, b, 0, 0)),
            pl.BlockSpec((CONV_WIDTH, c), const),
            pl.BlockSpec((1, c), const),
            pl.BlockSpec((1, c), const),
            pl.BlockSpec((1, c), const),
        ],
        out_specs=[
            pl.BlockSpec((bb * tt, c), lambda b, t: (b * nt + t, 0)),
            pl.BlockSpec((None, bb, CONV_WIDTH - 1, c), lambda b, t: (0, b, 0, 0)),
        ],
        out_shape=[
            jax.ShapeDtypeStruct((bsz * t_len, c), BF16),
            jax.ShapeDtypeStruct((1, bsz, CONV_WIDTH - 1, c), F32),
        ],
        scratch_shapes=[pltpu.VMEM((bb, CONV_HALO + tt, c), F32),
                        pltpu.VMEM((SUBLANES - 1, tt + CONV_HALO - SUBLANES, c), F32)],
        compiler_params=_params(2),
    )(glu, state32, dw_w, dw_b, ln_g, ln_b)


def _chunk_masks(seq_len):
    i = lax.broadcasted_iota(jnp.int32, (CHUNK, CHUNK), 0)
    j = lax.broadcasted_iota(jnp.int32, (CHUNK, CHUNK), 1)
    same = (i // seq_len) == (j // seq_len)
    incl = same & (i >= j)
    strict = same & (i > j)
    last = j == (i // seq_len) * seq_len + (seq_len - 1)
    levels = []
    blk = 1
    while blk < seq_len:
        levels.append(((i // (2 * blk)) == (j // (2 * blk))) & (((i // blk) % 2) == 1) & (((j // blk) % 2) == 0))
        blk *= 2
    eye = i == j
    return incl, strict, last, levels, eye


def _lane_col(x, lane):
    return jnp.broadcast_to(x[:, lane:lane + 1], (x.shape[0], LANES))


def _l2norm(x):
    return x * lax.rsqrt(jnp.sum(x * x, axis=-1, keepdims=True) + EPS)


def _select_sum(mask01, x):
    hi = x.astype(BF16)
    r1 = x - hi.astype(F32)
    mid = r1.astype(BF16)
    lo = (r1 - mid.astype(F32)).astype(BF16)
    w = x.shape[1]
    parts = jnp.dot(mask01, jnp.concatenate([hi, mid, lo], axis=1), preferred_element_type=F32)
    return parts[:, :w] + parts[:, w:2 * w] + parts[:, 2 * w:]


def _chunks_prepare(qkvs, gbts, masks, seq_len):
    incl, strict, last, levels, eye = masks
    nt = (((1,), (1,)), ((), ()))
    lower01 = jnp.where(incl, 1.0, 0.0).astype(BF16)
    probs = []
    for qkv, gbt in zip(qkvs, gbts):
        gc = _select_sum(lower01, gbt)
        gct = gc.T
        if seq_len == CHUNK:
            glast = jnp.broadcast_to(gc[CHUNK - 1:CHUNK, :], gc.shape)
        else:
            glast = _select_sum(jnp.where(last, 1.0, 0.0).astype(BF16), gc)
        for h in range(DN_HEADS):
            q = _l2norm(qkv[:, h * DN_HEAD_DIM:(h + 1) * DN_HEAD_DIM]) * (DN_HEAD_DIM ** -0.5)
            k = _l2norm(qkv[:, DN_WIDTH + h * DN_HEAD_DIM:DN_WIDTH + (h + 1) * DN_HEAD_DIM])
            v = qkv[:, 2 * DN_WIDTH + h * DN_HEAD_DIM:2 * DN_WIDTH + (h + 1) * DN_HEAD_DIM]
            gcol = _lane_col(gc, h)
            grow = jnp.broadcast_to(gct[h:h + 1, :], (CHUNK, CHUNK))
            beta = _lane_col(gbt, DN_HEADS + h)
            gl = _lane_col(glast, h)
            decay = jnp.exp(jnp.where(incl, gcol - grow, -jnp.inf))
            egc = jnp.exp(gcol)
            kb = k * beta
            probs.append(dict(q=q, k=k, kb=kb, decay=decay, rhs=jnp.concatenate([v * beta, kb * egc], axis=1),
                              qexp=q * egc, kdec=k * jnp.exp(gl - gcol), egl=jnp.exp(gl)))
    for p in probs:
        p['a'] = jnp.where(strict, _dot_delta(p['kb'], p['k'], nt) * p['decay'], 0.0)
        p['scores'] = _dot_delta(p['q'], p['k'], nt) * p['decay']
    for p in probs:
        p['x'] = jnp.where(eye, 1.0, 0.0) - jnp.where(levels[0], p['a'], 0.0)
    for m in levels[1:]:
        for p in probs:
            p['xa'] = _dot_delta(p['x'], jnp.where(m, p['a'], 0.0))
        for p in probs:
            p['x'] = p['x'] - _dot_delta(p['xa'], p['x'])
    out = []
    for c in range(len(qkvs)):
        heads = []
        for h in range(DN_HEADS):
            p = probs[c * DN_HEADS + h]
            sol = _dot_delta(p['x'], p['rhs'])
            heads.append((sol[:, :DN_HEAD_DIM], sol[:, DN_HEAD_DIM:], p['scores'], p['qexp'], p['kdec'], p['egl']))
        out.append(heads)
    return out


def _gated_out_norm(o, z, ng):
    y = o * lax.rsqrt(jnp.mean(o * o, axis=-1, keepdims=True) + EPS) * ng
    return y * _silu(z)


def _short_conv(e_ref, w_ref, tt):
    off = SHORT_HALO - (SHORT_WIDTH - 1)
    acc = w_ref[0:1, :] * e_ref[off:off + tt, :]
    for j in range(1, SHORT_WIDTH):
        acc = acc + w_ref[j:j + 1, :] * e_ref[off + j:off + j + tt, :]
    return _silu(acc)


def _delta_prompt_kernel(qkv_ref, z_ref, gb_ref, st_ref, s0_ref, w_ref, ng_ref,
                         o_ref, nst_ref, sout_ref, e_ref, s_ref, *, tt):
    t = pl.program_id(1)

    hist = SHORT_WIDTH - 1

    @pl.when(t == 0)
    def _():
        e_ref[0:SUBLANES, :] = jnp.zeros((SUBLANES, 3 * DN_WIDTH), F32)
        e_ref[SHORT_HALO - hist:SHORT_HALO, :] = st_ref[0]
        s_ref[...] = s0_ref[0]

    e_ref[SHORT_HALO:SHORT_HALO + tt, :] = qkv_ref[...]
    qkv = _short_conv(e_ref, w_ref, tt)
    nst_ref[0] = e_ref[tt + SHORT_HALO - hist:tt + SHORT_HALO, :]
    e_ref[0:SHORT_HALO, :] = e_ref[tt:tt + SHORT_HALO, :]

    masks = _chunk_masks(CHUNK)
    tn = (((0,), (0,)), ((), ()))
    n_chunks = tt // CHUNK
    prep = _chunks_prepare([qkv[c * CHUNK:(c + 1) * CHUNK, :] for c in range(n_chunks)],
                           [gb_ref[c * CHUNK:(c + 1) * CHUNK, :] for c in range(n_chunks)], masks, CHUNK)
    heads = range(DN_HEADS)
    s = [s_ref[h] for h in heads]
    for c in range(n_chunks):
        r0 = c * CHUNK
        value, kcum, scores, qexp, kdec, egl = zip(*prep[c])
        both = [_dot_delta(jnp.concatenate([kcum[h], qexp[h]], axis=0), s[h]) for h in heads]
        v_new = [value[h] - both[h][:CHUNK] for h in heads]
        o = [both[h][CHUNK:] + _dot_delta(scores[h], v_new[h]) for h in heads]
        s = [s[h] * egl[h][0:1, :] + _dot_delta(kdec[h], v_new[h], tn) for h in heads]
        for h in heads:
            lanes = slice(h * DN_HEAD_DIM, (h + 1) * DN_HEAD_DIM)
            o_ref[r0:r0 + CHUNK, lanes] = _gated_out_norm(
                o[h], z_ref[r0:r0 + CHUNK, lanes], ng_ref[...]).astype(o_ref.dtype)
    for h in heads:
        s_ref[h] = s[h]
        sout_ref[0, h] = s[h]


def _delta_prompt(qkv_pre, z, gb, state8, s0, conv_w, norm_g, *, bsz, t_len, tt):
    n = bsz * t_len
    assert t_len % tt == 0 and tt % CHUNK == 0
    nt = t_len // tt
    kern = functools.partial(_delta_prompt_kernel, tt=tt)
    tile = lambda b, t: (b * nt + t, 0)
    per_b = lambda b, t: (0, b, 0, 0)
    per_b4 = lambda b, t: (0, b, 0, 0, 0)
    return pl.pallas_call(
        kern,
        grid=(bsz, nt),
        in_specs=[
            pl.BlockSpec((tt, 3 * DN_WIDTH), tile),
            pl.BlockSpec((tt, DN_WIDTH), tile),
            pl.BlockSpec((tt, LANES), tile),
            pl.BlockSpec((None, 1, SHORT_WIDTH - 1, 3 * DN_WIDTH), per_b),
            pl.BlockSpec((None, 1, DN_HEADS, DN_HEAD_DIM, DN_HEAD_DIM), per_b4),
            pl.BlockSpec((SHORT_WIDTH, 3 * DN_WIDTH), lambda b, t: (0, 0)),
            pl.BlockSpec((1, DN_HEAD_DIM), lambda b, t: (0, 0)),
        ],
        out_specs=[
            pl.BlockSpec((tt, DN_WIDTH), tile),
            pl.BlockSpec((None, 1, SHORT_WIDTH - 1, 3 * DN_WIDTH), per_b),
            pl.BlockSpec((None, 1, DN_HEADS, DN_HEAD_DIM, DN_HEAD_DIM), per_b4),
        ],
        out_shape=[
            jax.ShapeDtypeStruct((n, DN_WIDTH), BF16),
            jax.ShapeDtypeStruct((1, bsz, SHORT_WIDTH - 1, 3 * DN_WIDTH), F32),
            jax.ShapeDtypeStruct((1, bsz, DN_HEADS, DN_HEAD_DIM, DN_HEAD_DIM), F32),
        ],
        scratch_shapes=[
            pltpu.VMEM((SHORT_HALO + tt, 3 * DN_WIDTH), F32),
            pltpu.VMEM((DN_HEADS, DN_HEAD_DIM, DN_HEAD_DIM), F32),
        ],
        compiler_params=_params(2),
    )(qkv_pre, z, gb, state8, s0, conv_w, norm_g)


def _delta_sample_kernel(qkv_ref, z_ref, gb_ref, st_ref, s0_ref, w_ref, ng_ref, o_ref, nst_ref, sout_ref,
                         e_ref, *, nseq, seq_len):
    qkv_rows = []
    hist = SHORT_WIDTH - 1
    e_ref[0:SUBLANES, :] = jnp.zeros((SUBLANES, 3 * DN_WIDTH), F32)
    for b in range(nseq):
        e_ref[SHORT_HALO - hist:SHORT_HALO, :] = st_ref[b]
        e_ref[SHORT_HALO:SHORT_HALO + seq_len, :] = qkv_ref[b * seq_len:(b + 1) * seq_len, :]
        qkv_rows.append(_short_conv(e_ref, w_ref, seq_len))
        nst_ref[b] = e_ref[seq_len + SHORT_HALO - hist:seq_len + SHORT_HALO, :]
    qkv = jnp.concatenate(qkv_rows, axis=0)
    masks = _chunk_masks(seq_len)
    prep = _chunks_prepare([qkv], [gb_ref[...]], masks, seq_len)[0]
    tn = (((0,), (0,)), ((), ()))
    rows = [slice(b * seq_len, (b + 1) * seq_len) for b in range(nseq)]
    both = [[_dot_delta(jnp.concatenate([prep[h][1][r], prep[h][3][r]], axis=0), s0_ref[b, h])
             for b, r in enumerate(rows)] for h in range(DN_HEADS)]
    v_new = [[prep[h][0][r] - both[h][b][:seq_len] for b, r in enumerate(rows)] for h in range(DN_HEADS)]
    for h in range(DN_HEADS):
        kdec, egl = prep[h][4], prep[h][5]
        for b, r in enumerate(rows):
            sout_ref[b, h] = (s0_ref[b, h] * egl[b * seq_len:b * seq_len + 1, :]
                              + _dot_delta(kdec[r], v_new[h][b], tn))
    for h in range(DN_HEADS):
        o = (jnp.concatenate([both[h][b][seq_len:] for b in range(nseq)], axis=0)
             + _dot_delta(prep[h][2], jnp.concatenate(v_new[h], axis=0)))
        lanes = slice(h * DN_HEAD_DIM, (h + 1) * DN_HEAD_DIM)
        o_ref[:, lanes] = _gated_out_norm(o, z_ref[:, lanes], ng_ref[...]).astype(o_ref.dtype)


def _delta_sample(qkv_pre, z, gb, state8, s0, conv_w, norm_g, *, row0, bsz, seq_len):
    n = bsz * seq_len
    assert CHUNK % seq_len == 0
    nseq = CHUNK // seq_len
    assert bsz % nseq == 0 and row0 % CHUNK == 0
    blk0 = row0 // CHUNK
    kern = functools.partial(_delta_sample_kernel, nseq=nseq, seq_len=seq_len)
    tile = lambda i: (blk0 + i, 0)
    blk3 = lambda i: (0, i, 0, 0)
    blk4 = lambda i: (0, i, 0, 0, 0)
    return pl.pallas_call(
        kern,
        grid=(bsz // nseq,),
        in_specs=[
            pl.BlockSpec((CHUNK, 3 * DN_WIDTH), tile),
            pl.BlockSpec((CHUNK, DN_WIDTH), tile),
            pl.BlockSpec((CHUNK, LANES), tile),
            pl.BlockSpec((None, nseq, SHORT_WIDTH - 1, 3 * DN_WIDTH), blk3),
            pl.BlockSpec((None, nseq, DN_HEADS, DN_HEAD_DIM, DN_HEAD_DIM), blk4),
            pl.BlockSpec((SHORT_WIDTH, 3 * DN_WIDTH), lambda i: (0, 0)),
            pl.BlockSpec((1, DN_HEAD_DIM), lambda i: (0, 0)),
        ],
        out_specs=[
            pl.BlockSpec((CHUNK, DN_WIDTH), lambda i: (i, 0)),
            pl.BlockSpec((None, nseq, SHORT_WIDTH - 1, 3 * DN_WIDTH), blk3),
            pl.BlockSpec((None, nseq, DN_HEADS, DN_HEAD_DIM, DN_HEAD_DIM), blk4),
        ],
        out_shape=[
            jax.ShapeDtypeStruct((n, DN_WIDTH), BF16),
            jax.ShapeDtypeStruct((1, bsz, SHORT_WIDTH - 1, 3 * DN_WIDTH), F32),
            jax.ShapeDtypeStruct((1, bsz, DN_HEADS, DN_HEAD_DIM, DN_HEAD_DIM), F32),
        ],
        scratch_shapes=[pltpu.VMEM((SHORT_HALO + seq_len, 3 * DN_WIDTH), F32)],
        compiler_params=_params(1),
    )(qkv_pre, z, gb, state8, s0, conv_w, norm_g)


def _mix_kernel(xp_ref, xs_ref, cap_ref, cas_ref, oap_ref, oas_ref, sa_ref, sb_ref, wc_ref, wd_ref, wm_ref,
                g2_ref, rw_ref, rb_ref, x2_ref, tr_ref, *, n_prompt_tiles):
    is_prompt = pl.program_id(0) < n_prompt_tiles
    x = jnp.where(is_prompt, xp_ref[...], xs_ref[...])
    ca = jnp.where(is_prompt, cap_ref[...], cas_ref[...])
    oa = jnp.where(is_prompt, oap_ref[...], oas_ref[...])
    ya = jnp.dot(ca, wc_ref[...], preferred_element_type=F32)
    yb = jnp.dot(oa, wd_ref[...], preferred_element_type=F32)
    mixed = sa_ref[...] * ya + sb_ref[...] * yb
    x2 = x + jnp.dot(mixed.astype(BF16), wm_ref[...], preferred_element_type=F32)
    x2_ref[...] = x2
    h2 = x2 * lax.rsqrt(jnp.mean(x2 * x2, axis=-1, keepdims=True) + EPS) * g2_ref[...]
    logits = _dot_split(h2, rw_ref[...]) + rb_ref[...]
    lt = logits.T[:N_EXPERTS, :]
    tokens = lt.shape[1]
    row = lax.broadcasted_iota(jnp.int32, lt.shape, 0).astype(F32)
    top_vals, top_idx = [], []
    for k in range(TOP_K):
        m = jnp.max(lt, axis=0, keepdims=True)
        idx = jnp.min(jnp.where(lt == m, row, float(N_EXPERTS)), axis=0, keepdims=True)
        top_vals.append(m)
        top_idx.append(idx)
        lt = jnp.where(row == idx, -jnp.inf, lt)
    exps = [jnp.exp(v - top_vals[0]) for v in top_vals]
    den = exps[0] + exps[1] + exps[2] + exps[3]
    slot = lax.broadcasted_iota(jnp.int32, (2 * TOP_K, tokens), 0)
    packed = jnp.zeros((2 * TOP_K, tokens), F32)
    for k in range(TOP_K):
        packed = jnp.where(slot == k, top_idx[k], packed)
        packed = jnp.where(slot == TOP_K + k, exps[k] / den, packed)
    packed = jnp.concatenate([packed, jnp.zeros((LANES - 2 * TOP_K, tokens), F32)], axis=0)
    tr_ref[...] = packed.T


def _mix(x_p, x_s, cact_p, cact_s, oact_p, oact_s, siga, sigb, w_conv_out, w_delta_out, w_merge_out, norm2_g,
         router_w, router_b, tm):
    (n_p, d), n_s = x_p.shape, x_s.shape[0]
    n = n_p + n_s
    row = lambda i: (i, 0)
    const = lambda i: (0, 0)
    return pl.pallas_call(
        functools.partial(_mix_kernel, n_prompt_tiles=n_p // tm),
        grid=(n // tm,),
        in_specs=[
            *_two_source_specs(tm, d, n_p // tm),
            *_two_source_specs(tm, D_CONV, n_p // tm),
            *_two_source_specs(tm, DN_WIDTH, n_p // tm),
            pl.BlockSpec((tm, d), row),
            pl.BlockSpec((tm, d), row),
            pl.BlockSpec((D_CONV, d), const),
            pl.BlockSpec((DN_WIDTH, d), const),
            pl.BlockSpec((d, d), const),
            pl.BlockSpec((1, d), const),
            pl.BlockSpec((d, 2 * LANES), const),
            pl.BlockSpec((1, LANES), const),
        ],
        out_specs=[
            pl.BlockSpec((tm, d), row),
            pl.BlockSpec((tm, LANES), row),
        ],
        out_shape=[
            jax.ShapeDtypeStruct((n, d), F32),
            jax.ShapeDtypeStruct((n, LANES), F32),
        ],
        compiler_params=_params(1),
    )(x_p, x_s, cact_p, cact_s, oact_p, oact_s, siga, sigb, w_conv_out, w_delta_out, w_merge_out, norm2_g,
      router_w, router_b)


def _fill_rows_per_step(n_fill, n_steps):
    per_step = SUBLANES
    while per_step * n_steps < n_fill:
        per_step *= 2
    assert n_fill % per_step == 0
    return per_step


def _route(top_idx, tm, n_blocks, n_steps):
    n = top_idx.shape[0]
    n_fill = n_blocks * tm - n * TOP_K
    assert n_fill == N_EXPERTS * tm
    experts = jnp.arange(N_EXPERTS, dtype=jnp.int32)
    chosen = top_idx[:, :, None] == experts[None, None, :]
    per_token = jnp.sum(chosen, axis=1, dtype=jnp.int32)
    csum = jnp.cumsum(per_token, axis=0)
    counts = csum[-1]
    padded = (counts + tm - 1) // tm * tm
    pad_end = jnp.cumsum(padded)
    pad_start = pad_end - padded
    first_free = (pad_start[None, :] + csum - per_token)[:, None, :]
    dest = jnp.sum(jnp.where(chosen, first_free, 0), axis=2).reshape(-1)
    nvalid = (pad_end[-1] // tm).astype(jnp.int32)
    blk = jnp.arange(n_blocks, dtype=jnp.int32)
    owner = jnp.sum((pad_end[None, :] <= (blk * tm)[:, None]).astype(jnp.int32), axis=1)
    block_e = jnp.minimum(owner, N_EXPERTS - 1)
    block_e = jnp.where(blk < nvalid, block_e, jnp.sum(jnp.where(blk == nvalid - 1, block_e, 0)))
    present = counts > 0
    later = present[None, :] & (experts[None, :] > experts[:, None])
    next_present = jnp.min(jnp.where(later, experts[None, :], N_EXPERTS), axis=1)
    next_present = jnp.where(next_present == N_EXPERTS, -1, next_present)
    parity = (jnp.cumsum(present.astype(jnp.int32)) - 1) % 2
    of_block = (block_e[:, None] == experts[None, :]).astype(jnp.int32)
    sched = jnp.stack([block_e, jnp.sum(of_block * next_present[None, :], axis=1),
                       jnp.sum(of_block * parity[None, :], axis=1)])
    n_pad = padded - counts
    spill = tm - n_pad
    spill_start = pad_end[-1] + jnp.cumsum(spill) - spill
    j = jnp.arange(tm, dtype=jnp.int32)[None, :]
    fill = jnp.where(j < n_pad[:, None], (pad_start + counts)[:, None] + j, (spill_start - n_pad)[:, None] + j)
    fill_step = _fill_rows_per_step(n_fill, n_steps)
    fill = jnp.pad(fill.reshape(-1, fill_step), ((0, n_steps - n_fill // fill_step), (0, 0)))
    table = jnp.concatenate([dest.reshape(n_steps, -1), fill], axis=1)
    return dest.reshape(n, TOP_K), table, sched, nvalid.reshape(1)


def _to_token_tiles(x, ref, row0=0):
    t = x.shape[0]
    for s in range(SUBLANES):
        ref[pl.ds(row0 + s, t, stride=SUBLANES), :] = x[:, s * LANES:(s + 1) * LANES]


def _from_token_tiles(ref, t, row0=0):
    return jnp.concatenate([ref[pl.ds(row0 + s, t, stride=SUBLANES), :] for s in range(SUBLANES)], axis=1)


def _tile(ref, row):
    return ref.at[pl.ds(pl.multiple_of(row * SUBLANES, SUBLANES), SUBLANES)]


def _dispatch_kernel(tab_ref, x2_ref, g2_ref, xs_ref, buf, sems, *, tokens, fill_step, n_fill_steps):
    i = pl.program_id(0)
    last = pl.num_programs(0) - 1
    slot = i % 2
    base = slot * tokens
    fill_sem = 2

    x2 = x2_ref[...]
    h2 = x2 * lax.rsqrt(jnp.mean(x2 * x2, axis=-1, keepdims=True) + EPS) * g2_ref[...]
    _to_token_tiles(h2, buf, pl.multiple_of(base * SUBLANES, SUBLANES))

    def wait_tiles(sem_idx, count):
        while count > 0:
            rows = min(count, tokens) * SUBLANES
            pltpu.make_async_copy(buf.at[pl.ds(0, rows)], xs_ref.at[pl.ds(0, rows)], sems.at[sem_idx]).wait()
            count -= min(count, tokens)

    for t in range(tokens):
        for k in range(TOP_K):
            pltpu.make_async_copy(_tile(buf, base + t), _tile(xs_ref, tab_ref[i, t * TOP_K + k]),
                                  sems.at[slot]).start(priority=k % DMA_PRIORITIES)

    @pl.when(i < n_fill_steps)
    def _():
        for p in range(fill_step):
            pltpu.make_async_copy(_tile(buf, base), _tile(xs_ref, tab_ref[i, tokens * TOP_K + p]),
                                  sems.at[fill_sem]).start(priority=p % DMA_PRIORITIES)
        wait_tiles(fill_sem, fill_step)

    @pl.when(i > 0)
    def _():
        wait_tiles(1 - slot, tokens * TOP_K)

    @pl.when(i == last)
    def _():
        wait_tiles(slot, tokens * TOP_K)


def _dispatch(x2, norm_g, table, n_rows, tokens):
    n, d = x2.shape
    assert d == SUBLANES * LANES
    fill_step = table.shape[1] - tokens * TOP_K
    n_fill_steps = (n_rows - n * TOP_K) // fill_step
    return pl.pallas_call(
        functools.partial(_dispatch_kernel, tokens=tokens, fill_step=fill_step, n_fill_steps=n_fill_steps),
        grid_spec=pltpu.PrefetchScalarGridSpec(
            num_scalar_prefetch=1,
            grid=(n // tokens,),
            in_specs=[pl.BlockSpec((tokens, d), lambda i, tab: (i, 0)),
                      pl.BlockSpec((1, d), lambda i, tab: (0, 0))],
            out_specs=pl.BlockSpec(memory_space=pl.ANY),
            scratch_shapes=[pltpu.VMEM((2 * tokens * SUBLANES, LANES), F32), pltpu.SemaphoreType.DMA((3,))],
        ),
        out_shape=jax.ShapeDtypeStruct((n_rows * SUBLANES, LANES), F32),
        compiler_params=_params(1),
    )(table, x2, norm_g)


def _moe_ffn_kernel(sched_ref, nv_ref, xs_ref, wgu_hbm, bgu_ref, wd_hbm, bd_ref, out_ref,
                    wgu_f32, wd_f32, wgu_bf, wd_bf, sems, *, blocks_per_step):
    tm = xs_ref.shape[0] // SUBLANES // blocks_per_step
    f = wd_bf.shape[0]

    def weight_copies(e, h):
        return (pltpu.make_async_copy(wgu_hbm.at[e], wgu_f32.at[h], sems.at[h]),
                pltpu.make_async_copy(wd_hbm.at[e], wd_f32.at[h], sems.at[h]))

    def one_block(i, r0):
        expert, next_expert, half = sched_ref[0, i], sched_ref[1, i], sched_ref[2, i]

        @pl.when(i == 0)
        def _():
            for c in weight_copies(expert, half):
                c.start()

        @pl.when(jnp.logical_or(i == 0, expert != sched_ref[0, jnp.maximum(i - 1, 0)]))
        def _():
            for c in weight_copies(expert, half):
                c.wait()
            wgu_bf[...] = wgu_f32[half].astype(BF16)
            wd_bf[...] = wd_f32[half].astype(BF16)

            @pl.when(next_expert >= 0)
            def _():
                for c in weight_copies(next_expert, 1 - half):
                    c.start()

        @pl.when(i < nv_ref[0])
        def _():
            x = _from_token_tiles(xs_ref, tm, r0).astype(BF16)
            gu = jnp.dot(x, wgu_bf[...], preferred_element_type=F32) + bgu_ref[expert]
            gt = jnp.minimum(gu[:, :f], SWIGLU_LIMIT)
            up = jnp.clip(gu[:, f:], -SWIGLU_LIMIT, SWIGLU_LIMIT)
            act = (up + 1.0) * (gt * _sigmoid(SWIGLU_ALPHA * gt))
            y = jnp.dot(act.astype(BF16), wd_bf[...], preferred_element_type=F32) + bd_ref[expert]
            _to_token_tiles(y, out_ref, r0)

        @pl.when(i >= nv_ref[0])
        def _():
            out_ref[r0:r0 + tm * SUBLANES, :] = jnp.zeros((tm * SUBLANES, LANES), out_ref.dtype)

    for h in range(blocks_per_step):
        one_block(pl.program_id(0) * blocks_per_step + h, h * tm * SUBLANES)


def _moe_ffn(xs, sched, nvalid, w_gate_up, b_gate_up, w_down, b_down, tm, n_blocks):
    ne, d, f2 = w_gate_up.shape
    f = f2 // 2
    per_step = FFN_BLOCKS_PER_STEP if n_blocks % FFN_BLOCKS_PER_STEP == 0 else 1
    rows = per_step * tm * SUBLANES
    used = lambda p, sc, nv: (jnp.minimum(p, (nv[0] - 1) // per_step), 0)
    whole = lambda p, sc, nv: (0, 0, 0)
    return pl.pallas_call(
        functools.partial(_moe_ffn_kernel, blocks_per_step=per_step),
        grid_spec=pltpu.PrefetchScalarGridSpec(
            num_scalar_prefetch=2,
            grid=(n_blocks // per_step,),
            in_specs=[
                pl.BlockSpec((rows, LANES), used),
                pl.BlockSpec(memory_space=pl.ANY),
                pl.BlockSpec((ne, 1, f2), whole),
                pl.BlockSpec(memory_space=pl.ANY),
                pl.BlockSpec((ne, 1, d), whole),
            ],
            out_specs=pl.BlockSpec((rows, LANES), lambda p, sc, nv: (p, 0)),
            scratch_shapes=[pltpu.VMEM((2, d, f2), F32), pltpu.VMEM((2, f, d), F32),
                            pltpu.VMEM((d, f2), BF16), pltpu.VMEM((f, d), BF16), pltpu.SemaphoreType.DMA((2,))],
        ),
        out_shape=jax.ShapeDtypeStruct((n_blocks * tm * SUBLANES, LANES), F32),
        compiler_params=_params(1),
    )(sched, nvalid, xs, w_gate_up, b_gate_up.reshape(ne, 1, f2), w_down, b_down.reshape(ne, 1, d))


def _combine_kernel(slot_ref, yb_ref, x2_ref, tg_ref, fg_ref, yp_ref, ys_ref, buf, sems, *, tc, n_prompt_tiles):
    i = pl.program_id(0)
    n_steps = pl.num_programs(0)

    def region(slot, k):
        return (slot * TOP_K + k) * tc

    def fetch(step, slot):
        for t in range(tc):
            for k in range(TOP_K):
                pltpu.make_async_copy(
                    _tile(yb_ref, slot_ref[step, t * TOP_K + k]), _tile(buf, region(slot, k) + t),
                    sems.at[slot]).start(priority=k % DMA_PRIORITIES)

    @pl.when(i == 0)
    def _():
        fetch(0, 0)

    @pl.when(i + 1 < n_steps)
    def _():
        fetch(i + 1, (i + 1) % 2)

    slot = i % 2
    for k in range(TOP_K):
        pltpu.make_async_copy(yb_ref.at[pl.ds(0, tc * SUBLANES)], buf.at[pl.ds(0, tc * SUBLANES)],
                              sems.at[slot]).wait()
    tg = tg_ref[...]
    y = x2_ref[...]
    for k in range(TOP_K):
        rows = _from_token_tiles(buf, tc, pl.multiple_of(region(slot, k) * SUBLANES, SUBLANES))
        y = y + tg[:, TOP_K + k:TOP_K + k + 1] * rows
    out = y * lax.rsqrt(jnp.mean(y * y, axis=-1, keepdims=True) + EPS) * fg_ref[...]

    @pl.when(i < n_prompt_tiles)
    def _():
        yp_ref[...] = out

    @pl.when(i >= n_prompt_tiles)
    def _():
        ys_ref[...] = out


def _combine(slot2d, yb, x2, tg, final_g, n_p, tc):
    n, d = x2.shape
    n_s = n - n_p
    assert n_p % tc == 0 and n_s % tc == 0
    npt = n_p // tc
    kern = functools.partial(_combine_kernel, tc=tc, n_prompt_tiles=npt)
    out_p, out_s = _two_source_specs(tc, d, npt)
    return pl.pallas_call(
        kern,
        grid_spec=pltpu.PrefetchScalarGridSpec(
            num_scalar_prefetch=1,
            grid=(n // tc,),
            in_specs=[
                pl.BlockSpec(memory_space=pl.ANY),
                pl.BlockSpec((tc, d), lambda i, s: (i, 0)),
                pl.BlockSpec((tc, LANES), lambda i, s: (i, 0)),
                pl.BlockSpec((1, d), lambda i, s: (0, 0)),
            ],
            out_specs=[out_p, out_s],
            scratch_shapes=[pltpu.VMEM((2 * TOP_K * tc * SUBLANES, LANES), F32), pltpu.SemaphoreType.DMA((2,))],
        ),
        out_shape=[jax.ShapeDtypeStruct((n_p, d), F32), jax.ShapeDtypeStruct((n_s, d), F32)],
        compiler_params=_params(1),
    )(slot2d, yb, x2, tg, final_g)


def _pad_lanes(v, width=LANES):
    v = v.reshape(1, -1)
    return jnp.pad(v, ((0, 0), (0, width - v.shape[1])))


def kernel(x_prompt, x_sample, state_conv, state_short_conv, state_delta, norm1_g, w_in, conv_dw_w,
           conv_dw_b, conv_ln_g, conv_ln_b, w_conv_out, short_conv_w, a_log, dt_bias, delta_norm_g,
           w_delta_out, w_merge_out, norm2_g, router_w, router_b, w_gate_up, b_gate_up, w_down, b_down,
           final_norm_g):
    depth = w_in.shape[0]
    assert depth == 1
    bp, tp, d = x_prompt.shape
    bs, ts, _ = x_sample.shape
    n_p, n_s = bp * tp, bs * ts
    n = n_p + n_s
    l = 0
    x_p = x_prompt.reshape(n_p, d)
    x_s = x_sample.reshape(n_s, d)

    o_ab = 2 * D_CONV + 4 * DN_WIDTH
    w = w_in[l]
    w_main = w[:, :o_ab].astype(BF16)
    w_gates = w[:, o_ab + 2 * DN_HEADS:].astype(BF16)
    w_ab = _split_bf16(jnp.pad(w[:, o_ab:o_ab + 2 * DN_HEADS], ((0, 0), (0, LANES - 2 * DN_HEADS))))

    glu, qkv_pre, z, gb, siga, sigb = _inproj(
        x_p, x_s, norm1_g[l].reshape(1, d), w_main, w_gates, w_ab, _pad_lanes(a_log[l]), _pad_lanes(dt_bias[l]),
        TOKEN_TILE)

    dw = (conv_dw_w[l], conv_dw_b[l].reshape(1, -1), conv_ln_g[l].reshape(1, -1), conv_ln_b[l].reshape(1, -1))
    st_c_p = jnp.zeros((depth, bp, CONV_WIDTH - 1, D_CONV), F32)
    seq_tile = SEQ_TILE if tp % SEQ_TILE == 0 else TOKEN_TILE
    cact_p, conv_p = _conv_branch(glu, st_c_p, *dw, row0=0, bsz=bp, t_len=tp, bb=1, tt=seq_tile)
    cact_s, conv_s = _conv_branch(glu, state_conv, *dw, row0=n_p, bsz=bs, t_len=ts, bb=8, tt=ts)

    st_s_p = jnp.zeros((depth, bp, SHORT_WIDTH - 1, 3 * DN_WIDTH), F32)
    s0_p = jnp.zeros((depth, bp, DN_HEADS, DN_HEAD_DIM, DN_HEAD_DIM), F32)
    ng = delta_norm_g[l].reshape(1, -1)
    oact_p, short_p, s_p = _delta_prompt(qkv_pre, z, gb, st_s_p, s0_p, short_conv_w[l], ng,
                                         bsz=bp, t_len=tp, tt=seq_tile)
    oact_s, short_s, s_s = _delta_sample(qkv_pre, z, gb, state_short_conv, state_delta, short_conv_w[l], ng,
                                         row0=n_p, bsz=bs, seq_len=ts)

    rw = _split_bf16(jnp.pad(router_w[l], ((0, 0), (0, LANES - N_EXPERTS))))
    x2, tg = _mix(x_p, x_s, cact_p, cact_s, oact_p, oact_s, siga, sigb, w_conv_out[l].astype(BF16),
                  w_delta_out[l].astype(BF16), w_merge_out[l].astype(BF16), norm2_g[l].reshape(1, d),
                  rw, _pad_lanes(router_b[l]),
                  MIX_TILE if n_p % MIX_TILE == 0 and n_s % MIX_TILE == 0 else TOKEN_TILE)

    n_blocks = -(-(n * TOP_K) // MOE_TILE) + N_EXPERTS
    route_tile = ROUTE_TILE if n_p % ROUTE_TILE == 0 and n_s % ROUTE_TILE == 0 else TOKEN_TILE
    n_steps = n // route_tile
    dest, table, sched, nvalid = _route(tg[:, :TOP_K].astype(jnp.int32), MOE_TILE, n_blocks, n_steps)
    xs = _dispatch(x2, norm2_g[l].reshape(1, d), table, n_blocks * MOE_TILE, route_tile)
    yb = _moe_ffn(xs, sched, nvalid, w_gate_up[l], b_gate_up[l], w_down[l], b_down[l], MOE_TILE, n_blocks)
    y_p, y_s = _combine(dest.reshape(n_steps, -1), yb, x2, tg, final_norm_g.reshape(1, d), n_p, route_tile)

    return (y_p.reshape(bp, tp, d), y_s.reshape(bs, ts, d), conv_p, short_p, s_p, conv_s, short_s, s_s)
```

```python
import functools

import jax
import jax.numpy as jnp
from jax import lax
from jax.experimental import pallas as pl
from jax.experimental.pallas import tpu as pltpu

F32 = jnp.float32
BF16 = jnp.bfloat16
EPS = 1e-6

LANES = 128
SUBLANES = 8
VMEM_LIMIT_BYTES = 56 * 1024 * 1024
DMA_PRIORITIES = 2

D_CONV = 512
CONV_WIDTH = 31
DN_HEADS = 4
DN_HEAD_DIM = 128
DN_WIDTH = DN_HEADS * DN_HEAD_DIM
SHORT_WIDTH = 4
N_EXPERTS = 32
TOP_K = 4
SWIGLU_LIMIT = 7.0
SWIGLU_ALPHA = 1.702

CHUNK = 128
CONV_HALO = 32
SHORT_HALO = 8

MIX_TILE = 512
TOKEN_TILE = 256
ROUTE_TILE = 512
SEQ_TILE = 512
MOE_TILE = 256
FFN_BLOCKS_PER_STEP = 2


def _sigmoid(x):
    return 1.0 / (1.0 + jnp.exp(-x))


def _silu(x):
    return x * _sigmoid(x)


def _split_bf16(w):
    hi = w.astype(BF16)
    lo = (w - hi.astype(F32)).astype(BF16)
    return jnp.concatenate([hi, lo], axis=-1)


def _dot_split(x, w_split):
    n = w_split.shape[-1] // 2
    x_hi = x.astype(BF16)
    x_lo = (x - x_hi.astype(F32)).astype(BF16)
    r = jnp.dot(x_hi, w_split, preferred_element_type=F32)
    return r[:, :n] + r[:, n:] + jnp.dot(x_lo, w_split[:, :n], preferred_element_type=F32)


def _dot_delta(a, b, dims=(((1,), (0,)), ((), ()))):
    return lax.dot_general(a.astype(BF16), b.astype(BF16), dims, preferred_element_type=F32)


def _params(n_axes):
    return pltpu.CompilerParams(dimension_semantics=("arbitrary",) * n_axes, vmem_limit_bytes=VMEM_LIMIT_BYTES)


def _two_source_specs(tm, d, n_first_tiles):
    first = pl.BlockSpec((tm, d), lambda i, *_: (jnp.minimum(i, n_first_tiles - 1), 0))
    second = pl.BlockSpec((tm, d), lambda i, *_: (jnp.maximum(i - n_first_tiles, 0), 0))
    return first, second


def _inproj_kernel(xp_ref, xs_ref, g_ref, w_ref, wg_ref, wab_ref, alog_ref, dtb_ref,
                   glu_ref, qkv_ref, z_ref, gb_ref, sa_ref, sb_ref, *, n_prompt_tiles):
    is_prompt = pl.program_id(0) < n_prompt_tiles
    o_gate, o_qkv, o_z = D_CONV, 2 * D_CONV, 2 * D_CONV + 3 * DN_WIDTH
    o_ga = o_z + DN_WIDTH
    d = xp_ref.shape[-1]
    for r0 in range(0, xp_ref.shape[0], TOKEN_TILE):
        r = slice(r0, r0 + TOKEN_TILE)
        x = jnp.where(is_prompt, xp_ref[r, :], xs_ref[r, :])
        h = x * lax.rsqrt(jnp.mean(x * x, axis=-1, keepdims=True) + EPS) * g_ref[...]
        hb = h.astype(BF16)

        def mm(ref, lo, hi):
            return jnp.dot(hb, ref[:, lo:hi], preferred_element_type=F32)

        glu_ref[r, :] = mm(w_ref, 0, o_gate) * _sigmoid(mm(w_ref, o_gate, o_qkv))
        qkv_ref[r, :] = mm(w_ref, o_qkv, o_z)
        z_ref[r, :] = mm(w_ref, o_z, o_ga)
        sa_ref[r, :] = _sigmoid(mm(wg_ref, 0, d))
        sb_ref[r, :] = _sigmoid(mm(wg_ref, d, 2 * d))
        ab = _dot_split(h, wab_ref[...])
        xa = ab + dtb_ref[...]
        softplus = jnp.maximum(xa, 0.0) + jnp.log(1.0 + jnp.exp(-jnp.abs(xa)))
        g = -jnp.exp(alog_ref[...]) * softplus
        lane = lax.broadcasted_iota(jnp.int32, ab.shape, 1)
        gb_ref[r, :] = jnp.where(lane < DN_HEADS, g, _sigmoid(ab))


def _inproj(x_p, x_s, norm_g, w_main, w_gates, w_ab, alog, dtb, tm):
    (n_p, d), n_s = x_p.shape, x_s.shape[0]
    assert n_p % tm == 0 and n_s % tm == 0
    n = n_p + n_s
    wcols = w_main.shape[1]
    row = lambda i: (i, 0)
    const = lambda i: (0, 0)
    outs = [(D_CONV, F32), (3 * DN_WIDTH, F32), (DN_WIDTH, F32), (LANES, F32), (d, F32), (d, F32)]
    return pl.pallas_call(
        functools.partial(_inproj_kernel, n_prompt_tiles=n_p // tm),
        grid=(n // tm,),
        in_specs=[
            *_two_source_specs(tm, d, n_p // tm),
            pl.BlockSpec((1, d), const),
            pl.BlockSpec((d, wcols), const),
            pl.BlockSpec((d, 2 * d), const),
            pl.BlockSpec((d, 2 * LANES), const),
            pl.BlockSpec((1, LANES), const),
            pl.BlockSpec((1, LANES), const),
        ],
        out_specs=[pl.BlockSpec((tm, c), row) for c, _ in outs],
        out_shape=[jax.ShapeDtypeStruct((n, c), dt) for c, dt in outs],
        compiler_params=_params(1),
    )(x_p, x_s, norm_g, w_main, w_gates, w_ab, alog, dtb)


def _conv_kernel(glu_ref, st_ref, w_ref, b_ref, lg_ref, lb_ref, out_ref, nst_ref, e_ref, sh_ref, *, bb, tt, rows):
    t = pl.program_id(1)

    hist = CONV_WIDTH - 1

    @pl.when(t == 0)
    def _():
        e_ref[:, 0:SUBLANES, :] = jnp.zeros((bb, SUBLANES, D_CONV), F32)
        e_ref[:, CONV_HALO - hist:CONV_HALO, :] = st_ref[...]

    for b in range(bb):
        e_ref[b, CONV_HALO:CONV_HALO + tt, :] = glu_ref[b * tt:(b + 1) * tt, :]
    off = CONV_HALO - (CONV_WIDTH - 1)
    span = sh_ref.shape[1]
    for b in range(bb):
        for s in range(1, SUBLANES):
            sh_ref[s - 1] = e_ref[b, s:s + span, :]
        for c in range(tt // rows):
            r0 = c * rows
            acc = jnp.zeros((rows, D_CONV), F32) + b_ref[...]
            for j in range(CONV_WIDTH):
                q, s = divmod(j + off, SUBLANES)
                lo = r0 + q * SUBLANES
                src = e_ref[b, lo:lo + rows, :] if s == 0 else sh_ref[s - 1, lo:lo + rows, :]
                acc = acc + w_ref[j:j + 1, :] * src
            mu = jnp.mean(acc, axis=-1, keepdims=True)
            xc = acc - mu
            var = jnp.mean(xc * xc, axis=-1, keepdims=True)
            y = xc * lax.rsqrt(var + EPS) * lg_ref[...] + lb_ref[...]
            out_ref[b * tt + r0:b * tt + r0 + rows, :] = _silu(y).astype(out_ref.dtype)
    nst_ref[...] = e_ref[:, tt + CONV_HALO - hist:tt + CONV_HALO, :]
    e_ref[:, 0:CONV_HALO, :] = e_ref[:, tt:tt + CONV_HALO, :]


def _conv_branch(glu, state32, dw_w, dw_b, ln_g, ln_b, *, row0, bsz, t_len, bb, tt):
    c = glu.shape[1]
    assert bsz % bb == 0 and t_len % tt == 0 and row0 % (bb * tt) == 0
    nt = t_len // tt
    blk0 = row0 // (bb * tt)
    rows = min(tt, 32)
    kern = functools.partial(_conv_kernel, bb=bb, tt=tt, rows=rows)
    const = lambda b, t: (0, 0)
    return pl.pallas_call(
        kern,
        grid=(bsz // bb, nt),
        in_specs=[
            pl.BlockSpec((bb * tt, c), lambda b, t: (blk0 + b * nt + t, 0)),
            pl.BlockSpec((None, bb, CONV_WIDTH - 1, c), lambda b, t: (0, b, 0, 0)),
            pl.BlockSpec((CONV_WIDTH, c), const),
            pl.BlockSpec((1, c), const),
            pl.BlockSpec((1, c), const),
            pl.BlockSpec((1, c), const),
        ],
        out_specs=[
            pl.BlockSpec((bb * tt, c), lambda b, t: (b * nt + t, 0)),
            pl.BlockSpec((None, bb, CONV_WIDTH - 1, c), lambda b, t: (0, b, 0, 0)),
        ],
        out_shape=[
            jax.ShapeDtypeStruct((bsz * t_len, c), BF16),
            jax.ShapeDtypeStruct((1, bsz, CONV_WIDTH - 1, c), F32),
        ],
        scratch_shapes=[pltpu.VMEM((bb, CONV_HALO + tt, c), F32),
                        pltpu.VMEM((SUBLANES - 1, tt + CONV_HALO - SUBLANES, c), F32)],
        compiler_params=_params(2),
    )(glu, state32, dw_w, dw_b, ln_g, ln_b)


def _chunk_masks(seq_len):
    i = lax.broadcasted_iota(jnp.int32, (CHUNK, CHUNK), 0)
    j = lax.broadcasted_iota(jnp.int32, (CHUNK, CHUNK), 1)
    same = (i // seq_len) == (j // seq_len)
    incl = same & (i >= j)
    strict = same & (i > j)
    last = j == (i // seq_len) * seq_len + (seq_len - 1)
    levels = []
    blk = 1
    while blk < seq_len:
        levels.append(((i // (2 * blk)) == (j // (2 * blk))) & (((i // blk) % 2) == 1) & (((j // blk) % 2) == 0))
        blk *= 2
    eye = i == j
    return incl, strict, last, levels, eye


def _lane_col(x, lane):
    return jnp.broadcast_to(x[:, lane:lane + 1], (x.shape[0], LANES))


def _l2norm(x):
    return x * lax.rsqrt(jnp.sum(x * x, axis=-1, keepdims=True) + EPS)


def _select_sum(mask01, x):
    hi = x.astype(BF16)
    r1 = x - hi.astype(F32)
    mid = r1.astype(BF16)
    lo = (r1 - mid.astype(F32)).astype(BF16)
    w = x.shape[1]
    parts = jnp.dot(mask01, jnp.concatenate([hi, mid, lo], axis=1), preferred_element_type=F32)
    return parts[:, :w] + parts[:, w:2 * w] + parts[:, 2 * w:]


def _chunks_prepare(qkvs, gbts, masks, seq_len):
    incl, strict, last, levels, eye = masks
    nt = (((1,), (1,)), ((), ()))
    lower01 = jnp.where(incl, 1.0, 0.0).astype(BF16)
    probs = []
    for qkv, gbt in zip(qkvs, gbts):
        gc = _select_sum(lower01, gbt)
        gct = gc.T
        if seq_len == CHUNK:
            glast = jnp.broadcast_to(gc[CHUNK - 1:CHUNK, :], gc.shape)
        else:
            glast = _select_sum(jnp.where(last, 1.0, 0.0).astype(BF16), gc)
        for h in range(DN_HEADS):
            q = _l2norm(qkv[:, h * DN_HEAD_DIM:(h + 1) * DN_HEAD_DIM]) * (DN_HEAD_DIM ** -0.5)
            k = _l2norm(qkv[:, DN_WIDTH + h * DN_HEAD_DIM:DN_WIDTH + (h + 1) * DN_HEAD_DIM])
            v = qkv[:, 2 * DN_WIDTH + h * DN_HEAD_DIM:2 * DN_WIDTH + (h + 1) * DN_HEAD_DIM]
            gcol = _lane_col(gc, h)
            grow = jnp.broadcast_to(gct[h:h + 1, :], (CHUNK, CHUNK))
            beta = _lane_col(gbt, DN_HEADS + h)
            gl = _lane_col(glast, h)
            decay = jnp.exp(jnp.where(incl, gcol - grow, -jnp.inf))
            egc = jnp.exp(gcol)
            kb = k * beta
            probs.append(dict(q=q, k=k, kb=kb, decay=decay, rhs=jnp.concatenate([v * beta, kb * egc], axis=1),
                              qexp=q * egc, kdec=k * jnp.exp(gl - gcol), egl=jnp.exp(gl)))
    for p in probs:
        p['a'] = jnp.where(strict, _dot_delta(p['kb'], p['k'], nt) * p['decay'], 0.0)
        p['scores'] = _dot_delta(p['q'], p['k'], nt) * p['decay']
    for p in probs:
        p['x'] = jnp.where(eye, 1.0, 0.0) - jnp.where(levels[0], p['a'], 0.0)
    for m in levels[1:]:
        for p in probs:
            p['xa'] = _dot_delta(p['x'], jnp.where(m, p['a'], 0.0))
        for p in probs:
            p['x'] = p['x'] - _dot_delta(p['xa'], p['x'])
    out = []
    for c in range(len(qkvs)):
        heads = []
        for h in range(DN_HEADS):
            p = probs[c * DN_HEADS + h]
            sol = _dot_delta(p['x'], p['rhs'])
            heads.append((sol[:, :DN_HEAD_DIM], sol[:, DN_HEAD_DIM:], p['scores'], p['qexp'], p['kdec'], p['egl']))
        out.append(heads)
    return out


def _gated_out_norm(o, z, ng):
    y = o * lax.rsqrt(jnp.mean(o * o, axis=-1, keepdims=True) + EPS) * ng
    return y * _silu(z)


def _short_conv(e_ref, w_ref, tt):
    off = SHORT_HALO - (SHORT_WIDTH - 1)
    acc = w_ref[0:1, :] * e_ref[off:off + tt, :]
    for j in range(1, SHORT_WIDTH):
        acc = acc + w_ref[j:j + 1, :] * e_ref[off + j:off + j + tt, :]
    return _silu(acc)


def _delta_prompt_kernel(qkv_ref, z_ref, gb_ref, st_ref, s0_ref, w_ref, ng_ref,
                         o_ref, nst_ref, sout_ref, e_ref, s_ref, *, tt):
    t = pl.program_id(1)

    hist = SHORT_WIDTH - 1

    @pl.when(t == 0)
    def _():
        e_ref[0:SUBLANES, :] = jnp.zeros((SUBLANES, 3 * DN_WIDTH), F32)
        e_ref[SHORT_HALO - hist:SHORT_HALO, :] = st_ref[0]
        s_ref[...] = s0_ref[0]

    e_ref[SHORT_HALO:SHORT_HALO + tt, :] = qkv_ref[...]
    qkv = _short_conv(e_ref, w_ref, tt)
    nst_ref[0] = e_ref[tt + SHORT_HALO - hist:tt + SHORT_HALO, :]
    e_ref[0:SHORT_HALO, :] = e_ref[tt:tt + SHORT_HALO, :]

    masks = _chunk_masks(CHUNK)
    tn = (((0,), (0,)), ((), ()))
    n_chunks = tt // CHUNK
    prep = _chunks_prepare([qkv[c * CHUNK:(c + 1) * CHUNK, :] for c in range(n_chunks)],
                           [gb_ref[c * CHUNK:(c + 1) * CHUNK, :] for c in range(n_chunks)], masks, CHUNK)
    heads = range(DN_HEADS)
    s = [s_ref[h] for h in heads]
    for c in range(n_chunks):
        r0 = c * CHUNK
        value, kcum, scores, qexp, kdec, egl = zip(*prep[c])
        both = [_dot_delta(jnp.concatenate([kcum[h], qexp[h]], axis=0), s[h]) for h in heads]
        v_new = [value[h] - both[h][:CHUNK] for h in heads]
        o = [both[h][CHUNK:] + _dot_delta(scores[h], v_new[h]) for h in heads]
        s = [s[h] * egl[h][0:1, :] + _dot_delta(kdec[h], v_new[h], tn) for h in heads]
        for h in heads:
            lanes = slice(h * DN_HEAD_DIM, (h + 1) * DN_HEAD_DIM)
            o_ref[r0:r0 + CHUNK, lanes] = _gated_out_norm(
                o[h], z_ref[r0:r0 + CHUNK, lanes], ng_ref[...]).astype(o_ref.dtype)
    for h in heads:
        s_ref[h] = s[h]
        sout_ref[0, h] = s[h]


def _delta_prompt(qkv_pre, z, gb, state8, s0, conv_w, norm_g, *, bsz, t_len, tt):
    n = bsz * t_len
    assert t_len % tt == 0 and tt % CHUNK == 0
    nt = t_len // tt
    kern = functools.partial(_delta_prompt_kernel, tt=tt)
    tile = lambda b, t: (b * nt + t, 0)
    per_b = lambda b, t: (0, b, 0, 0)
    per_b4 = lambda b, t: (0, b, 0, 0, 0)
    return pl.pallas_call(
        kern,
        grid=(bsz, nt),
        in_specs=[
            pl.BlockSpec((tt, 3 * DN_WIDTH), tile),
            pl.BlockSpec((tt, DN_WIDTH), tile),
            pl.BlockSpec((tt, LANES), tile),
            pl.BlockSpec((None, 1, SHORT_WIDTH - 1, 3 * DN_WIDTH), per_b),
            pl.BlockSpec((None, 1, DN_HEADS, DN_HEAD_DIM, DN_HEAD_DIM), per_b4),
            pl.BlockSpec((SHORT_WIDTH, 3 * DN_WIDTH), lambda b, t: (0, 0)),
            pl.BlockSpec((1, DN_HEAD_DIM), lambda b, t: (0, 0)),
        ],
        out_specs=[
            pl.BlockSpec((tt, DN_WIDTH), tile),
            pl.BlockSpec((None, 1, SHORT_WIDTH - 1, 3 * DN_WIDTH), per_b),
            pl.BlockSpec((None, 1, DN_HEADS, DN_HEAD_DIM, DN_HEAD_DIM), per_b4),
        ],
        out_shape=[
            jax.ShapeDtypeStruct((n, DN_WIDTH), BF16),
            jax.ShapeDtypeStruct((1, bsz, SHORT_WIDTH - 1, 3 * DN_WIDTH), F32),
            jax.ShapeDtypeStruct((1, bsz, DN_HEADS, DN_HEAD_DIM, DN_HEAD_DIM), F32),
        ],
        scratch_shapes=[
            pltpu.VMEM((SHORT_HALO + tt, 3 * DN_WIDTH), F32),
            pltpu.VMEM((DN_HEADS, DN_HEAD_DIM, DN_HEAD_DIM), F32),
        ],
        compiler_params=_params(2),
    )(qkv_pre, z, gb, state8, s0, conv_w, norm_g)


def _delta_sample_kernel(qkv_ref, z_ref, gb_ref, st_ref, s0_ref, w_ref, ng_ref, o_ref, nst_ref, sout_ref,
                         e_ref, *, nseq, seq_len):
    qkv_rows = []
    hist = SHORT_WIDTH - 1
    e_ref[0:SUBLANES, :] = jnp.zeros((SUBLANES, 3 * DN_WIDTH), F32)
    for b in range(nseq):
        e_ref[SHORT_HALO - hist:SHORT_HALO, :] = st_ref[b]
        e_ref[SHORT_HALO:SHORT_HALO + seq_len, :] = qkv_ref[b * seq_len:(b + 1) * seq_len, :]
        qkv_rows.append(_short_conv(e_ref, w_ref, seq_len))
        nst_ref[b] = e_ref[seq_len + SHORT_HALO - hist:seq_len + SHORT_HALO, :]
    qkv = jnp.concatenate(qkv_rows, axis=0)
    masks = _chunk_masks(seq_len)
    prep = _chunks_prepare([qkv], [gb_ref[...]], masks, seq_len)[0]
    tn = (((0,), (0,)), ((), ()))
    rows = [slice(b * seq_len, (b + 1) * seq_len) for b in range(nseq)]
    both = [[_dot_delta(jnp.concatenate([prep[h][1][r], prep[h][3][r]], axis=0), s0_ref[b, h])
             for b, r in enumerate(rows)] for h in range(DN_HEADS)]
    v_new = [[prep[h][0][r] - both[h][b][:seq_len] for b, r in enumerate(rows)] for h in range(DN_HEADS)]
    for h in range(DN_HEADS):
        kdec, egl = prep[h][4], prep[h][5]
        for b, r in enumerate(rows):
            sout_ref[b, h] = (s0_ref[b, h] * egl[b * seq_len:b * seq_len + 1, :]
                              + _dot_delta(kdec[r], v_new[h][b], tn))
    for h in range(DN_HEADS):
        o = (jnp.concatenate([both[h][b][seq_len:] for b in range(nseq)], axis=0)
             + _dot_delta(prep[h][2], jnp.concatenate(v_new[h], axis=0)))
        lanes = slice(h * DN_HEAD_DIM, (h + 1) * DN_HEAD_DIM)
        o_ref[:, lanes] = _gated_out_norm(o, z_ref[:, lanes], ng_ref[...]).astype(o_ref.dtype)


def _delta_sample(qkv_pre, z, gb, state8, s0, conv_w, norm_g, *, row0, bsz, seq_len):
    n = bsz * seq_len
    assert CHUNK % seq_len == 0
    nseq = CHUNK // seq_len
    assert bsz % nseq == 0 and row0 % CHUNK == 0
    blk0 = row0 // CHUNK
    kern = functools.partial(_delta_sample_kernel, nseq=nseq, seq_len=seq_len)
    tile = lambda i: (blk0 + i, 0)
    blk3 = lambda i: (0, i, 0, 0)
    blk4 = lambda i: (0, i, 0, 0, 0)
    return pl.pallas_call(
        kern,
        grid=(bsz // nseq,),
        in_specs=[
            pl.BlockSpec((CHUNK, 3 * DN_WIDTH), tile),
            pl.BlockSpec((CHUNK, DN_WIDTH), tile),
            pl.BlockSpec((CHUNK, LANES), tile),
            pl.BlockSpec((None, nseq, SHORT_WIDTH - 1, 3 * DN_WIDTH), blk3),
            pl.BlockSpec((None, nseq, DN_HEADS, DN_HEAD_DIM, DN_HEAD_DIM), blk4),
            pl.BlockSpec((SHORT_WIDTH, 3 * DN_WIDTH), lambda i: (0, 0)),
            pl.BlockSpec((1, DN_HEAD_DIM), lambda i: (0, 0)),
        ],
        out_specs=[
            pl.BlockSpec((CHUNK, DN_WIDTH), lambda i: (i, 0)),
            pl.BlockSpec((None, nseq, SHORT_WIDTH - 1, 3 * DN_WIDTH), blk3),
            pl.BlockSpec((None, nseq, DN_HEADS, DN_HEAD_DIM, DN_HEAD_DIM), blk4),
        ],
        out_shape=[
            jax.ShapeDtypeStruct((n, DN_WIDTH), BF16),
            jax.ShapeDtypeStruct((1, bsz, SHORT_WIDTH - 1, 3 * DN_WIDTH), F32),
            jax.ShapeDtypeStruct((1, bsz, DN_HEADS, DN_HEAD_DIM, DN_HEAD_DIM), F32),
        ],
        scratch_shapes=[pltpu.VMEM((SHORT_HALO + seq_len, 3 * DN_WIDTH), F32)],
        compiler_params=_params(1),
    )(qkv_pre, z, gb, state8, s0, conv_w, norm_g)


def _mix_kernel(xp_ref, xs_ref, cap_ref, cas_ref, oap_ref, oas_ref, sa_ref, sb_ref, wc_ref, wd_ref, wm_ref,
                g2_ref, rw_ref, rb_ref, x2_ref, tr_ref, *, n_prompt_tiles):
    is_prompt = pl.program_id(0) < n_prompt_tiles
    x = jnp.where(is_prompt, xp_ref[...], xs_ref[...])
    ca = jnp.where(is_prompt, cap_ref[...], cas_ref[...])
    oa = jnp.where(is_prompt, oap_ref[...], oas_ref[...])
    ya = jnp.dot(ca, wc_ref[...], preferred_element_type=F32)
    yb = jnp.dot(oa, wd_ref[...], preferred_element_type=F32)
    mixed = sa_ref[...] * ya + sb_ref[...] * yb
    x2 = x + jnp.dot(mixed.astype(BF16), wm_ref[...], preferred_element_type=F32)
    x2_ref[...] = x2
    h2 = x2 * lax.rsqrt(jnp.mean(x2 * x2, axis=-1, keepdims=True) + EPS) * g2_ref[...]
    logits = _dot_split(h2, rw_ref[...]) + rb_ref[...]
    lt = logits.T[:N_EXPERTS, :]
    tokens = lt.shape[1]
    row = lax.broadcasted_iota(jnp.int32, lt.shape, 0).astype(F32)
    top_vals, top_idx = [], []
    for k in range(TOP_K):
        m = jnp.max(lt, axis=0, keepdims=True)
        idx = jnp.min(jnp.where(lt == m, row, float(N_EXPERTS)), axis=0, keepdims=True)
        top_vals.append(m)
        top_idx.append(idx)
        lt = jnp.where(row == idx, -jnp.inf, lt)
    exps = [jnp.exp(v - top_vals[0]) for v in top_vals]
    den = exps[0] + exps[1] + exps[2] + exps[3]
    slot = lax.broadcasted_iota(jnp.int32, (2 * TOP_K, tokens), 0)
    packed = jnp.zeros((2 * TOP_K, tokens), F32)
    for k in range(TOP_K):
        packed = jnp.where(slot == k, top_idx[k], packed)
        packed = jnp.where(slot == TOP_K + k, exps[k] / den, packed)
    packed = jnp.concatenate([packed, jnp.zeros((LANES - 2 * TOP_K, tokens), F32)], axis=0)
    tr_ref[...] = packed.T


def _mix(x_p, x_s, cact_p, cact_s, oact_p, oact_s, siga, sigb, w_conv_out, w_delta_out, w_merge_out, norm2_g,
         router_w, router_b, tm):
    (n_p, d), n_s = x_p.shape, x_s.shape[0]
    n = n_p + n_s
    row = lambda i: (i, 0)
    const = lambda i: (0, 0)
    return pl.pallas_call(
        functools.partial(_mix_kernel, n_prompt_tiles=n_p // tm),
        grid=(n // tm,),
        in_specs=[
            *_two_source_specs(tm, d, n_p // tm),
            *_two_source_specs(tm, D_CONV, n_p // tm),
            *_two_source_specs(tm, DN_WIDTH, n_p // tm),
            pl.BlockSpec((tm, d), row),
            pl.BlockSpec((tm, d), row),
            pl.BlockSpec((D_CONV, d), const),
            pl.BlockSpec((DN_WIDTH, d), const),
            pl.BlockSpec((d, d), const),
            pl.BlockSpec((1, d), const),
            pl.BlockSpec((d, 2 * LANES), const),
            pl.BlockSpec((1, LANES), const),
        ],
        out_specs=[
            pl.BlockSpec((tm, d), row),
            pl.BlockSpec((tm, LANES), row),
        ],
        out_shape=[
            jax.ShapeDtypeStruct((n, d), F32),
            jax.ShapeDtypeStruct((n, LANES), F32),
        ],
        compiler_params=_params(1),
    )(x_p, x_s, cact_p, cact_s, oact_p, oact_s, siga, sigb, w_conv_out, w_delta_out, w_merge_out, norm2_g,
      router_w, router_b)


def _fill_rows_per_step(n_fill, n_steps):
    per_step = SUBLANES
    while per_step * n_steps < n_fill:
        per_step *= 2
    assert n_fill % per_step == 0
    return per_step


def _route(top_idx, tm, n_blocks, n_steps):
    n = top_idx.shape[0]
    n_fill = n_blocks * tm - n * TOP_K
    assert n_fill == N_EXPERTS * tm
    experts = jnp.arange(N_EXPERTS, dtype=jnp.int32)
    chosen = top_idx[:, :, None] == experts[None, None, :]
    per_token = jnp.sum(chosen, axis=1, dtype=jnp.int32)
    csum = jnp.cumsum(per_token, axis=0)
    counts = csum[-1]
    padded = (counts + tm - 1) // tm * tm
    pad_end = jnp.cumsum(padded)
    pad_start = pad_end - padded
    first_free = (pad_start[None, :] + csum - per_token)[:, None, :]
    dest = jnp.sum(jnp.where(chosen, first_free, 0), axis=2).reshape(-1)
    nvalid = (pad_end[-1] // tm).astype(jnp.int32)
    blk = jnp.arange(n_blocks, dtype=jnp.int32)
    owner = jnp.sum((pad_end[None, :] <= (blk * tm)[:, None]).astype(jnp.int32), axis=1)
    block_e = jnp.minimum(owner, N_EXPERTS - 1)
    block_e = jnp.where(blk < nvalid, block_e, jnp.sum(jnp.where(blk == nvalid - 1, block_e, 0)))
    present = counts > 0
    later = present[None, :] & (experts[None, :] > experts[:, None])
    next_present = jnp.min(jnp.where(later, experts[None, :], N_EXPERTS), axis=1)
    next_present = jnp.where(next_present == N_EXPERTS, -1, next_present)
    parity = (jnp.cumsum(present.astype(jnp.int32)) - 1) % 2
    of_block = (block_e[:, None] == experts[None, :]).astype(jnp.int32)
    sched = jnp.stack([block_e, jnp.sum(of_block * next_present[None, :], axis=1),
                       jnp.sum(of_block * parity[None, :], axis=1)])
    n_pad = padded - counts
    spill = tm - n_pad
    spill_start = pad_end[-1] + jnp.cumsum(spill) - spill
    j = jnp.arange(tm, dtype=jnp.int32)[None, :]
    fill = jnp.where(j < n_pad[:, None], (pad_start + counts)[:, None] + j, (spill_start - n_pad)[:, None] + j)
    fill_step = _fill_rows_per_step(n_fill, n_steps)
    fill = jnp.pad(fill.reshape(-1, fill_step), ((0, n_steps - n_fill // fill_step), (0, 0)))
    table = jnp.concatenate([dest.reshape(n_steps, -1), fill], axis=1)
    return dest.reshape(n, TOP_K), table, sched, nvalid.reshape(1)


def _to_token_tiles(x, ref, row0=0):
    t = x.shape[0]
    for s in range(SUBLANES):
        ref[pl.ds(row0 + s, t, stride=SUBLANES), :] = x[:, s * LANES:(s + 1) * LANES]


def _from_token_tiles(ref, t, row0=0):
    return jnp.concatenate([ref[pl.ds(row0 + s, t, stride=SUBLANES), :] for s in range(SUBLANES)], axis=1)


def _tile(ref, row):
    return ref.at[pl.ds(pl.multiple_of(row * SUBLANES, SUBLANES), SUBLANES)]


def _dispatch_kernel(tab_ref, x2_ref, g2_ref, xs_ref, buf, sems, *, tokens, fill_step, n_fill_steps):
    i = pl.program_id(0)
    last = pl.num_programs(0) - 1
    slot = i % 2
    base = slot * tokens
    fill_sem = 2

    x2 = x2_ref[...]
    h2 = x2 * lax.rsqrt(jnp.mean(x2 * x2, axis=-1, keepdims=True) + EPS) * g2_ref[...]
    _to_token_tiles(h2, buf, pl.multiple_of(base * SUBLANES, SUBLANES))

    def wait_tiles(sem_idx, count):
        while count > 0:
            rows = min(count, tokens) * SUBLANES
            pltpu.make_async_copy(buf.at[pl.ds(0, rows)], xs_ref.at[pl.ds(0, rows)], sems.at[sem_idx]).wait()
            count -= min(count, tokens)

    for t in range(tokens):
        for k in range(TOP_K):
            pltpu.make_async_copy(_tile(buf, base + t), _tile(xs_ref, tab_ref[i, t * TOP_K + k]),
                                  sems.at[slot]).start(priority=k % DMA_PRIORITIES)

    @pl.when(i < n_fill_steps)
    def _():
        for p in range(fill_step):
            pltpu.make_async_copy(_tile(buf, base), _tile(xs_ref, tab_ref[i, tokens * TOP_K + p]),
                                  sems.at[fill_sem]).start(priority=p % DMA_PRIORITIES)
        wait_tiles(fill_sem, fill_step)

    @pl.when(i > 0)
    def _():
        wait_tiles(1 - slot, tokens * TOP_K)

    @pl.when(i == last)
    def _():
        wait_tiles(slot, tokens * TOP_K)


def _dispatch(x2, norm_g, table, n_rows, tokens):
    n, d = x2.shape
    assert d == SUBLANES * LANES
    fill_step = table.shape[1] - tokens * TOP_K
    n_fill_steps = (n_rows - n * TOP_K) // fill_step
    return pl.pallas_call(
        functools.partial(_dispatch_kernel, tokens=tokens, fill_step=fill_step, n_fill_steps=n_fill_steps),
        grid_spec=pltpu.PrefetchScalarGridSpec(
            num_scalar_prefetch=1,
            grid=(n // tokens,),
            in_specs=[pl.BlockSpec((tokens, d), lambda i, tab: (i, 0)),
                      pl.BlockSpec((1, d), lambda i, tab: (0, 0))],
            out_specs=pl.BlockSpec(memory_space=pl.ANY),
            scratch_shapes=[pltpu.VMEM((2 * tokens * SUBLANES, LANES), F32), pltpu.SemaphoreType.DMA((3,))],
        ),
        out_shape=jax.ShapeDtypeStruct((n_rows * SUBLANES, LANES), F32),
        compiler_params=_params(1),
    )(table, x2, norm_g)


def _moe_ffn_kernel(sched_ref, nv_ref, xs_ref, wgu_hbm, bgu_ref, wd_hbm, bd_ref, out_ref,
                    wgu_f32, wd_f32, wgu_bf, wd_bf, sems, *, blocks_per_step):
    tm = xs_ref.shape[0] // SUBLANES // blocks_per_step
    f = wd_bf.shape[0]

    def weight_copies(e, h):
        return (pltpu.make_async_copy(wgu_hbm.at[e], wgu_f32.at[h], sems.at[h]),
                pltpu.make_async_copy(wd_hbm.at[e], wd_f32.at[h], sems.at[h]))

    def one_block(i, r0):
        expert, next_expert, half = sched_ref[0, i], sched_ref[1, i], sched_ref[2, i]

        @pl.when(i == 0)
        def _():
            for c in weight_copies(expert, half):
                c.start()

        @pl.when(jnp.logical_or(i == 0, expert != sched_ref[0, jnp.maximum(i - 1, 0)]))
        def _():
            for c in weight_copies(expert, half):
                c.wait()
            wgu_bf[...] = wgu_f32[half].astype(BF16)
            wd_bf[...] = wd_f32[half].astype(BF16)

            @pl.when(next_expert >= 0)
            def _():
                for c in weight_copies(next_expert, 1 - half):
                    c.start()

        @pl.when(i < nv_ref[0])
        def _():
            x = _from_token_tiles(xs_ref, tm, r0).astype(BF16)
            gu = jnp.dot(x, wgu_bf[...], preferred_element_type=F32) + bgu_ref[expert]
            gt = jnp.minimum(gu[:, :f], SWIGLU_LIMIT)
            up = jnp.clip(gu[:, f:], -SWIGLU_LIMIT, SWIGLU_LIMIT)
            act = (up + 1.0) * (gt * _sigmoid(SWIGLU_ALPHA * gt))
            y = jnp.dot(act.astype(BF16), wd_bf[...], preferred_element_type=F32) + bd_ref[expert]
            _to_token_tiles(y, out_ref, r0)

        @pl.when(i >= nv_ref[0])
        def _():
            out_ref[r0:r0 + tm * SUBLANES, :] = jnp.zeros((tm * SUBLANES, LANES), out_ref.dtype)

    for h in range(blocks_per_step):
        one_block(pl.program_id(0) * blocks_per_step + h, h * tm * SUBLANES)


def _moe_ffn(xs, sched, nvalid, w_gate_up, b_gate_up, w_down, b_down, tm, n_blocks):
    ne, d, f2 = w_gate_up.shape
    f = f2 // 2
    per_step = FFN_BLOCKS_PER_STEP if n_blocks % FFN_BLOCKS_PER_STEP == 0 else 1
    rows = per_step * tm * SUBLANES
    used = lambda p, sc, nv: (jnp.minimum(p, (nv[0] - 1) // per_step), 0)
    whole = lambda p, sc, nv: (0, 0, 0)
    return pl.pallas_call(
        functools.partial(_moe_ffn_kernel, blocks_per_step=per_step),
        grid_spec=pltpu.PrefetchScalarGridSpec(
            num_scalar_prefetch=2,
            grid=(n_blocks // per_step,),
            in_specs=[
                pl.BlockSpec((rows, LANES), used),
                pl.BlockSpec(memory_space=pl.ANY),
                pl.BlockSpec((ne, 1, f2), whole),
                pl.BlockSpec(memory_space=pl.ANY),
                pl.BlockSpec((ne, 1, d), whole),
            ],
            out_specs=pl.BlockSpec((rows, LANES), lambda p, sc, nv: (p, 0)),
            scratch_shapes=[pltpu.VMEM((2, d, f2), F32), pltpu.VMEM((2, f, d), F32),
                            pltpu.VMEM((d, f2), BF16), pltpu.VMEM((f, d), BF16), pltpu.SemaphoreType.DMA((2,))],
        ),
        out_shape=jax.ShapeDtypeStruct((n_blocks * tm * SUBLANES, LANES), F32),
        compiler_params=_params(1),
    )(sched, nvalid, xs, w_gate_up, b_gate_up.reshape(ne, 1, f2), w_down, b_down.reshape(ne, 1, d))


def _combine_kernel(slot_ref, yb_ref, x2_ref, tg_ref, fg_ref, yp_ref, ys_ref, buf, sems, *, tc, n_prompt_tiles):
    i = pl.program_id(0)
    n_steps = pl.num_programs(0)

    def region(slot, k):
        return (slot * TOP_K + k) * tc

    def fetch(step, slot):
        for t in range(tc):
            for k in range(TOP_K):
                pltpu.make_async_copy(
                    _tile(yb_ref, slot_ref[step, t * TOP_K + k]), _tile(buf, region(slot, k) + t),
                    sems.at[slot]).start(priority=k % DMA_PRIORITIES)

    @pl.when(i == 0)
    def _():
        fetch(0, 0)

    @pl.when(i + 1 < n_steps)
    def _():
        fetch(i + 1, (i + 1) % 2)

    slot = i % 2
    for k in range(TOP_K):
        pltpu.make_async_copy(yb_ref.at[pl.ds(0, tc * SUBLANES)], buf.at[pl.ds(0, tc * SUBLANES)],
                              sems.at[slot]).wait()
    tg = tg_ref[...]
    y = x2_ref[...]
    for k in range(TOP_K):
        rows = _from_token_tiles(buf, tc, pl.multiple_of(region(slot, k) * SUBLANES, SUBLANES))
        y = y + tg[:, TOP_K + k:TOP_K + k + 1] * rows
    out = y * lax.rsqrt(jnp.mean(y * y, axis=-1, keepdims=True) + EPS) * fg_ref[...]

    @pl.when(i < n_prompt_tiles)
    def _():
        yp_ref[...] = out

    @pl.when(i >= n_prompt_tiles)
    def _():
        ys_ref[...] = out


def _combine(slot2d, yb, x2, tg, final_g, n_p, tc):
    n, d = x2.shape
    n_s = n - n_p
    assert n_p % tc == 0 and n_s % tc == 0
    npt = n_p // tc
    kern = functools.partial(_combine_kernel, tc=tc, n_prompt_tiles=npt)
    out_p, out_s = _two_source_specs(tc, d, npt)
    return pl.pallas_call(
        kern,
        grid_spec=pltpu.PrefetchScalarGridSpec(
            num_scalar_prefetch=1,
            grid=(n // tc,),
            in_specs=[
                pl.BlockSpec(memory_space=pl.ANY),
                pl.BlockSpec((tc, d), lambda i, s: (i, 0)),
                pl.BlockSpec((tc, LANES), lambda i, s: (i, 0)),
                pl.BlockSpec((1, d), lambda i, s: (0, 0)),
            ],
            out_specs=[out_p, out_s],
            scratch_shapes=[pltpu.VMEM((2 * TOP_K * tc * SUBLANES, LANES), F32), pltpu.SemaphoreType.DMA((2,))],
        ),
        out_shape=[jax.ShapeDtypeStruct((n_p, d), F32), jax.ShapeDtypeStruct((n_s, d), F32)],
        compiler_params=_params(1),
    )(slot2d, yb, x2, tg, final_g)


def _pad_lanes(v, width=LANES):
    v = v.reshape(1, -1)
    return jnp.pad(v, ((0, 0), (0, width - v.shape[1])))


def kernel(x_prompt, x_sample, state_conv, state_short_conv, state_delta, norm1_g, w_in, conv_dw_w,
           conv_dw_b, conv_ln_g, conv_ln_b, w_conv_out, short_conv_w, a_log, dt_bias, delta_norm_g,
           w_delta_out, w_merge_out, norm2_g, router_w, router_b, w_gate_up, b_gate_up, w_down, b_down,
           final_norm_g):
    depth = w_in.shape[0]
    assert depth == 1
    bp, tp, d = x_prompt.shape
    bs, ts, _ = x_sample.shape
    n_p, n_s = bp * tp, bs * ts
    n = n_p + n_s
    l = 0
    x_p = x_prompt.reshape(n_p, d)
    x_s = x_sample.reshape(n_s, d)

    o_ab = 2 * D_CONV + 4 * DN_WIDTH
    w = w_in[l]
    w_main = w[:, :o_ab].astype(BF16)
    w_gates = w[:, o_ab + 2 * DN_HEADS:].astype(BF16)
    w_ab = _split_bf16(jnp.pad(w[:, o_ab:o_ab + 2 * DN_HEADS], ((0, 0), (0, LANES - 2 * DN_HEADS))))

    glu, qkv_pre, z, gb, siga, sigb = _inproj(
        x_p, x_s, norm1_g[l].reshape(1, d), w_main, w_gates, w_ab, _pad_lanes(a_log[l]), _pad_lanes(dt_bias[l]),
        MIX_TILE if n_p % MIX_TILE == 0 and n_s % MIX_TILE == 0 else TOKEN_TILE)

    dw = (conv_dw_w[l], conv_dw_b[l].reshape(1, -1), conv_ln_g[l].reshape(1, -1), conv_ln_b[l].reshape(1, -1))
    st_c_p = jnp.zeros((depth, bp, CONV_WIDTH - 1, D_CONV), F32)
    seq_tile = SEQ_TILE if tp % SEQ_TILE == 0 else TOKEN_TILE
    cact_p, conv_p = _conv_branch(glu, st_c_p, *dw, row0=0, bsz=bp, t_len=tp, bb=1, tt=seq_tile)
    cact_s, conv_s = _conv_branch(glu, state_conv, *dw, row0=n_p, bsz=bs, t_len=ts, bb=8, tt=ts)

    st_s_p = jnp.zeros((depth, bp, SHORT_WIDTH - 1, 3 * DN_WIDTH), F32)
    s0_p = jnp.zeros((depth, bp, DN_HEADS, DN_HEAD_DIM, DN_HEAD_DIM), F32)
    ng = delta_norm_g[l].reshape(1, -1)
    oact_p, short_p, s_p = _delta_prompt(qkv_pre, z, gb, st_s_p, s0_p, short_conv_w[l], ng,
                                         bsz=bp, t_len=tp, tt=seq_tile)
    oact_s, short_s, s_s = _delta_sample(qkv_pre, z, gb, state_short_conv, state_delta, short_conv_w[l], ng,
                                         row0=n_p, bsz=bs, seq_len=ts)

    rw = _split_bf16(jnp.pad(router_w[l], ((0, 0), (0, LANES - N_EXPERTS))))
    x2, tg = _mix(x_p, x_s, cact_p, cact_s, oact_p, oact_s, siga, sigb, w_conv_out[l].astype(BF16),
                  w_delta_out[l].astype(BF16), w_merge_out[l].astype(BF16), norm2_g[l].reshape(1, d),
                  rw, _pad_lanes(router_b[l]),
                  MIX_TILE if n_p % MIX_TILE == 0 and n_s % MIX_TILE == 0 else TOKEN_TILE)

    n_blocks = -(-(n * TOP_K) // MOE_TILE) + N_EXPERTS
    route_tile = ROUTE_TILE if n_p % ROUTE_TILE == 0 and n_s % ROUTE_TILE == 0 else TOKEN_TILE
    n_steps = n // route_tile
    dest, table, sched, nvalid = _route(tg[:, :TOP_K].astype(jnp.int32), MOE_TILE, n_blocks, n_steps)
    xs = _dispatch(x2, norm2_g[l].reshape(1, d), table, n_blocks * MOE_TILE, route_tile)
    yb = _moe_ffn(xs, sched, nvalid, w_gate_up[l], b_gate_up[l], w_down[l], b_down[l], MOE_TILE, n_blocks)
    y_p, y_s = _combine(dest.reshape(n_steps, -1), yb, x2, tg, final_norm_g.reshape(1, d), n_p, route_tile)

    return (y_p.reshape(bp, tp, d), y_s.reshape(bs, ts, d), conv_p, short_p, s_p, conv_s, short_s, s_s)
```

```python
import functools

import jax
import jax.numpy as jnp
from jax import lax
from jax.experimental import pallas as pl
from jax.experimental.pallas import tpu as pltpu

F32 = jnp.float32
BF16 = jnp.bfloat16
EPS = 1e-6

LANES = 128
SUBLANES = 8
VMEM_LIMIT_BYTES = 56 * 1024 * 1024
DMA_PRIORITIES = 2

D_CONV = 512
CONV_WIDTH = 31
DN_HEADS = 4
DN_HEAD_DIM = 128
DN_WIDTH = DN_HEADS * DN_HEAD_DIM
SHORT_WIDTH = 4
N_EXPERTS = 32
TOP_K = 4
SWIGLU_LIMIT = 7.0
SWIGLU_ALPHA = 1.702

CHUNK = 128
CONV_HALO = 32
SHORT_HALO = 8

MIX_TILE = 512
TOKEN_TILE = 256
ROUTE_TILE = 512
SEQ_TILE = 512
MOE_TILE = 256
FFN_BLOCKS_PER_STEP = 2


def _sigmoid(x):
    return 1.0 / (1.0 + jnp.exp(-x))


def _silu(x):
    return x * _sigmoid(x)


def _split_bf16(w):
    hi = w.astype(BF16)
    lo = (w - hi.astype(F32)).astype(BF16)
    return jnp.concatenate([hi, lo], axis=-1)


def _dot_split(x, w_split):
    n = w_split.shape[-1] // 2
    x_hi = x.astype(BF16)
    x_lo = (x - x_hi.astype(F32)).astype(BF16)
    r = jnp.dot(x_hi, w_split, preferred_element_type=F32)
    return r[:, :n] + r[:, n:] + jnp.dot(x_lo, w_split[:, :n], preferred_element_type=F32)


def _dot_delta(a, b, dims=(((1,), (0,)), ((), ()))):
    return lax.dot_general(a.astype(BF16), b.astype(BF16), dims, preferred_element_type=F32)


def _params(n_axes):
    return pltpu.CompilerParams(dimension_semantics=("arbitrary",) * n_axes, vmem_limit_bytes=VMEM_LIMIT_BYTES)


def _two_source_specs(tm, d, n_first_tiles):
    first = pl.BlockSpec((tm, d), lambda i, *_: (jnp.minimum(i, n_first_tiles - 1), 0))
    second = pl.BlockSpec((tm, d), lambda i, *_: (jnp.maximum(i - n_first_tiles, 0), 0))
    return first, second


def _inproj_kernel(xp_ref, xs_ref, g_ref, w_ref, wg_ref, wab_ref, alog_ref, dtb_ref,
                   glu_ref, qkv_ref, z_ref, gb_ref, sa_ref, sb_ref, *, n_prompt_tiles):
    is_prompt = pl.program_id(0) < n_prompt_tiles
    o_gate, o_qkv, o_z = D_CONV, 2 * D_CONV, 2 * D_CONV + 3 * DN_WIDTH
    o_ga = o_z + DN_WIDTH
    d = xp_ref.shape[-1]
    for r0 in range(0, xp_ref.shape[0], TOKEN_TILE):
        r = slice(r0, r0 + TOKEN_TILE)
        x = jnp.where(is_prompt, xp_ref[r, :], xs_ref[r, :])
        h = x * lax.rsqrt(jnp.mean(x * x, axis=-1, keepdims=True) + EPS) * g_ref[...]
        hb = h.astype(BF16)

        def mm(ref, lo, hi):
            return jnp.dot(hb, ref[:, lo:hi], preferred_element_type=F32)

        glu_ref[r, :] = mm(w_ref, 0, o_gate) * _sigmoid(mm(w_ref, o_gate, o_qkv))
        qkv_ref[r, :] = mm(w_ref, o_qkv, o_z)
        z_ref[r, :] = mm(w_ref, o_z, o_ga)
        sa_ref[r, :] = _sigmoid(mm(wg_ref, 0, d))
        sb_ref[r, :] = _sigmoid(mm(wg_ref, d, 2 * d))
        ab = _dot_split(h, wab_ref[...])
        xa = ab + dtb_ref[...]
        softplus = jnp.maximum(xa, 0.0) + jnp.log(1.0 + jnp.exp(-jnp.abs(xa)))
        g = -jnp.exp(alog_ref[...]) * softplus
        lane = lax.broadcasted_iota(jnp.int32, ab.shape, 1)
        gb_ref[r, :] = jnp.where(lane < DN_HEADS, g, _sigmoid(ab))


def _inproj(x_p, x_s, norm_g, w_main, w_gates, w_ab, alog, dtb, tm):
    (n_p, d), n_s = x_p.shape, x_s.shape[0]
    assert n_p % tm == 0 and n_s % tm == 0
    n = n_p + n_s
    wcols = w_main.shape[1]
    row = lambda i: (i, 0)
    const = lambda i: (0, 0)
    outs = [(D_CONV, F32), (3 * DN_WIDTH, F32), (DN_WIDTH, F32), (LANES, F32), (d, F32), (d, F32)]
    return pl.pallas_call(
        functools.partial(_inproj_kernel, n_prompt_tiles=n_p // tm),
        grid=(n // tm,),
        in_specs=[
            *_two_source_specs(tm, d, n_p // tm),
            pl.BlockSpec((1, d), const),
            pl.BlockSpec((d, wcols), const),
            pl.BlockSpec((d, 2 * d), const),
            pl.BlockSpec((d, 2 * LANES), const),
            pl.BlockSpec((1, LANES), const),
            pl.BlockSpec((1, LANES), const),
        ],
        out_specs=[pl.BlockSpec((tm, c), row) for c, _ in outs],
        out_shape=[jax.ShapeDtypeStruct((n, c), dt) for c, dt in outs],
        compiler_params=_params(1),
    )(x_p, x_s, norm_g, w_main, w_gates, w_ab, alog, dtb)


def _conv_kernel(glu_ref, st_ref, w_ref, b_ref, lg_ref, lb_ref, out_ref, nst_ref, e_ref, sh_ref, *, bb, tt, rows):
    t = pl.program_id(1)

    hist = CONV_WIDTH - 1

    @pl.when(t == 0)
    def _():
        e_ref[:, 0:SUBLANES, :] = jnp.zeros((bb, SUBLANES, D_CONV), F32)
        e_ref[:, CONV_HALO - hist:CONV_HALO, :] = st_ref[...]

    for b in range(bb):
        e_ref[b, CONV_HALO:CONV_HALO + tt, :] = glu_ref[b * tt:(b + 1) * tt, :]
    off = CONV_HALO - (CONV_WIDTH - 1)
    span = sh_ref.shape[1]
    for b in range(bb):
        for s in range(1, SUBLANES):
            sh_ref[s - 1] = e_ref[b, s:s + span, :]
        for c in range(tt // rows):
            r0 = c * rows
            acc = jnp.zeros((rows, D_CONV), F32) + b_ref[...]
            for j in range(CONV_WIDTH):
                q, s = divmod(j + off, SUBLANES)
                lo = r0 + q * SUBLANES
                src = e_ref[b, lo:lo + rows, :] if s == 0 else sh_ref[s - 1, lo:lo + rows, :]
                acc = acc + w_ref[j:j + 1, :] * src
            mu = jnp.mean(acc, axis=-1, keepdims=True)
            xc = acc - mu
            var = jnp.mean(xc * xc, axis=-1, keepdims=True)
            y = xc * lax.rsqrt(var + EPS) * lg_ref[...] + lb_ref[...]
            out_ref[b * tt + r0:b * tt + r0 + rows, :] = _silu(y).astype(out_ref.dtype)
    nst_ref[...] = e_ref[:, tt + CONV_HALO - hist:tt + CONV_HALO, :]
    e_ref[:, 0:CONV_HALO, :] = e_ref[:, tt:tt + CONV_HALO, :]


def _conv_branch(glu, state32, dw_w, dw_b, ln_g, ln_b, *, row0, bsz, t_len, bb, tt):
    c = glu.shape[1]
    assert bsz % bb == 0 and t_len % tt == 0 and row0 % (bb * tt) == 0
    nt = t_len // tt
    blk0 = row0 // (bb * tt)
    rows = min(tt, 32)
    kern = functools.partial(_conv_kernel, bb=bb, tt=tt, rows=rows)
    const = lambda b, t: (0, 0)
    return pl.pallas_call(
        kern,
        grid=(bsz // bb, nt),
        in_specs=[
            pl.BlockSpec((bb * tt, c), lambda b, t: (blk0 + b * nt + t, 0)),
            pl.BlockSpec((None, bb, CONV_WIDTH - 1, c), lambda b, t: (0, b, 0, 0)),
            pl.BlockSpec((CONV_WIDTH, c), const),
            pl.BlockSpec((1, c), const),
            pl.BlockSpec((1, c), const),
            pl.BlockSpec((1, c), const),
        ],
        out_specs=[
            pl.BlockSpec((bb * tt, c), lambda b, t: (b * nt + t, 0)),
            pl.BlockSpec((None, bb, CONV_WIDTH - 1, c), lambda b, t: (0, b, 0, 0)),
        ],
        out_shape=[
            jax.ShapeDtypeStruct((bsz * t_len, c), BF16),
            jax.ShapeDtypeStruct((1, bsz, CONV_WIDTH - 1, c), F32),
        ],
        scratch_shapes=[pltpu.VMEM((bb, CONV_HALO + tt, c), F32),
                        pltpu.VMEM((SUBLANES - 1, tt + CONV_HALO - SUBLANES, c), F32)],
        compiler_params=_params(2),
    )(glu, state32, dw_w, dw_b, ln_g, ln_b)


def _chunk_masks(seq_len):
    i = lax.broadcasted_iota(jnp.int32, (CHUNK, CHUNK), 0)
    j = lax.broadcasted_iota(jnp.int32, (CHUNK, CHUNK), 1)
    same = (i // seq_len) == (j // seq_len)
    incl = same & (i >= j)
    strict = same & (i > j)
    last = j == (i // seq_len) * seq_len + (seq_len - 1)
    levels = []
    blk = 1
    while blk < seq_len:
        levels.append(((i // (2 * blk)) == (j // (2 * blk))) & (((i // blk) % 2) == 1) & (((j // blk) % 2) == 0))
        blk *= 2
    eye = i == j
    return incl, strict, last, levels, eye


def _lane_col(x, lane):
    return jnp.broadcast_to(x[:, lane:lane + 1], (x.shape[0], LANES))


def _l2norm(x):
    return x * lax.rsqrt(jnp.sum(x * x, axis=-1, keepdims=True) + EPS)


def _select_sum(mask01, x):
    hi = x.astype(BF16)
    r1 = x - hi.astype(F32)
    mid = r1.astype(BF16)
    lo = (r1 - mid.astype(F32)).astype(BF16)
    w = x.shape[1]
    parts = jnp.dot(mask01, jnp.concatenate([hi, mid, lo], axis=1), preferred_element_type=F32)
    return parts[:, :w] + parts[:, w:2 * w] + parts[:, 2 * w:]


def _chunks_prepare(qkvs, gbts, masks, seq_len):
    incl, strict, last, levels, eye = masks
    nt = (((1,), (1,)), ((), ()))
    lower01 = jnp.where(incl, 1.0, 0.0).astype(BF16)
    probs = []
    for qkv, gbt in zip(qkvs, gbts):
        gc = _select_sum(lower01, gbt)
        gct = gc.T
        if seq_len == CHUNK:
            glast = jnp.broadcast_to(gc[CHUNK - 1:CHUNK, :], gc.shape)
        else:
            glast = _select_sum(jnp.where(last, 1.0, 0.0).astype(BF16), gc)
        for h in range(DN_HEADS):
            q = _l2norm(qkv[:, h * DN_HEAD_DIM:(h + 1) * DN_HEAD_DIM]) * (DN_HEAD_DIM ** -0.5)
            k = _l2norm(qkv[:, DN_WIDTH + h * DN_HEAD_DIM:DN_WIDTH + (h + 1) * DN_HEAD_DIM])
            v = qkv[:, 2 * DN_WIDTH + h * DN_HEAD_DIM:2 * DN_WIDTH + (h + 1) * DN_HEAD_DIM]
            gcol = _lane_col(gc, h)
            grow = jnp.broadcast_to(gct[h:h + 1, :], (CHUNK, CHUNK))
            beta = _lane_col(gbt, DN_HEADS + h)
            gl = _lane_col(glast, h)
            decay = jnp.exp(jnp.where(incl, gcol - grow, -jnp.inf))
            egc = jnp.exp(gcol)
            kb = k * beta
            probs.append(dict(q=q, k=k, kb=kb, decay=decay, rhs=jnp.concatenate([v * beta, kb * egc], axis=1),
                              qexp=q * egc, kdec=k * jnp.exp(gl - gcol), egl=jnp.exp(gl)))
    for p in probs:
        p['a'] = jnp.where(strict, _dot_delta(p['kb'], p['k'], nt) * p['decay'], 0.0)
        p['scores'] = _dot_delta(p['q'], p['k'], nt) * p['decay']
    for p in probs:
        p['x'] = jnp.where(eye, 1.0, 0.0) - jnp.where(levels[0], p['a'], 0.0)
    for m in levels[1:]:
        for p in probs:
            p['xa'] = _dot_delta(p['x'], jnp.where(m, p['a'], 0.0))
        for p in probs:
            p['x'] = p['x'] - _dot_delta(p['xa'], p['x'])
    out = []
    for c in range(len(qkvs)):
        heads = []
        for h in range(DN_HEADS):
            p = probs[c * DN_HEADS + h]
            sol = _dot_delta(p['x'], p['rhs'])
            heads.append((sol[:, :DN_HEAD_DIM], sol[:, DN_HEAD_DIM:], p['scores'], p['qexp'], p['kdec'], p['egl']))
        out.append(heads)
    return out


def _gated_out_norm(o, z, ng):
    y = o * lax.rsqrt(jnp.mean(o * o, axis=-1, keepdims=True) + EPS) * ng
    return y * _silu(z)


def _short_conv(e_ref, w_ref, tt):
    off = SHORT_HALO - (SHORT_WIDTH - 1)
    acc = w_ref[0:1, :] * e_ref[off:off + tt, :]
    for j in range(1, SHORT_WIDTH):
        acc = acc + w_ref[j:j + 1, :] * e_ref[off + j:off + j + tt, :]
    return _silu(acc)


def _delta_prompt_kernel(qkv_ref, z_ref, gb_ref, st_ref, s0_ref, w_ref, ng_ref,
                         o_ref, nst_ref, sout_ref, e_ref, s_ref, *, tt):
    t = pl.program_id(1)

    hist = SHORT_WIDTH - 1

    @pl.when(t == 0)
    def _():
        e_ref[0:SUBLANES, :] = jnp.zeros((SUBLANES, 3 * DN_WIDTH), F32)
        e_ref[SHORT_HALO - hist:SHORT_HALO, :] = st_ref[0]
        s_ref[...] = s0_ref[0]

    e_ref[SHORT_HALO:SHORT_HALO + tt, :] = qkv_ref[...]
    qkv = _short_conv(e_ref, w_ref, tt)
    nst_ref[0] = e_ref[tt + SHORT_HALO - hist:tt + SHORT_HALO, :]
    e_ref[0:SHORT_HALO, :] = e_ref[tt:tt + SHORT_HALO, :]

    masks = _chunk_masks(CHUNK)
    tn = (((0,), (0,)), ((), ()))
    n_chunks = tt // CHUNK
    prep = _chunks_prepare([qkv[c * CHUNK:(c + 1) * CHUNK, :] for c in range(n_chunks)],
                           [gb_ref[c * CHUNK:(c + 1) * CHUNK, :] for c in range(n_chunks)], masks, CHUNK)
    heads = range(DN_HEADS)
    s = [s_ref[h] for h in heads]
    for c in range(n_chunks):
        r0 = c * CHUNK
        value, kcum, scores, qexp, kdec, egl = zip(*prep[c])
        both = [_dot_delta(jnp.concatenate([kcum[h], qexp[h]], axis=0), s[h]) for h in heads]
        v_new = [value[h] - both[h][:CHUNK] for h in heads]
        o = [both[h][CHUNK:] + _dot_delta(scores[h], v_new[h]) for h in heads]
        s = [s[h] * egl[h][0:1, :] + _dot_delta(kdec[h], v_new[h], tn) for h in heads]
        for h in heads:
            lanes = slice(h * DN_HEAD_DIM, (h + 1) * DN_HEAD_DIM)
            o_ref[r0:r0 + CHUNK, lanes] = _gated_out_norm(
                o[h], z_ref[r0:r0 + CHUNK, lanes], ng_ref[...]).astype(o_ref.dtype)
    for h in heads:
        s_ref[h] = s[h]
        sout_ref[0, h] = s[h]


def _delta_prompt(qkv_pre, z, gb, state8, s0, conv_w, norm_g, *, bsz, t_len, tt):
    n = bsz * t_len
    assert t_len % tt == 0 and tt % CHUNK == 0
    nt = t_len // tt
    kern = functools.partial(_delta_prompt_kernel, tt=tt)
    tile = lambda b, t: (b * nt + t, 0)
    per_b = lambda b, t: (0, b, 0, 0)
    per_b4 = lambda b, t: (0, b, 0, 0, 0)
    return pl.pallas_call(
        kern,
        grid=(bsz, nt),
        in_specs=[
            pl.BlockSpec((tt, 3 * DN_WIDTH), tile),
            pl.BlockSpec((tt, DN_WIDTH), tile),
            pl.BlockSpec((tt, LANES), tile),
            pl.BlockSpec((None, 1, SHORT_WIDTH - 1, 3 * DN_WIDTH), per_b),
            pl.BlockSpec((None, 1, DN_HEADS, DN_HEAD_DIM, DN_HEAD_DIM), per_b4),
            pl.BlockSpec((SHORT_WIDTH, 3 * DN_WIDTH), lambda b, t: (0, 0)),
            pl.BlockSpec((1, DN_HEAD_DIM), lambda b, t: (0, 0)),
        ],
        out_specs=[
            pl.BlockSpec((tt, DN_WIDTH), tile),
            pl.BlockSpec((None, 1, SHORT_WIDTH - 1, 3 * DN_WIDTH), per_b),
            pl.BlockSpec((None, 1, DN_HEADS, DN_HEAD_DIM, DN_HEAD_DIM), per_b4),
        ],
        out_shape=[
            jax.ShapeDtypeStruct((n, DN_WIDTH), BF16),
            jax.ShapeDtypeStruct((1, bsz, SHORT_WIDTH - 1, 3 * DN_WIDTH), F32),
            jax.ShapeDtypeStruct((1, bsz, DN_HEADS, DN_HEAD_DIM, DN_HEAD_DIM), F32),
        ],
        scratch_shapes=[
            pltpu.VMEM((SHORT_HALO + tt, 3 * DN_WIDTH), F32),
            pltpu.VMEM((DN_HEADS, DN_HEAD_DIM, DN_HEAD_DIM), F32),
        ],
        compiler_params=_params(2),
    )(qkv_pre, z, gb, state8, s0, conv_w, norm_g)


def _delta_sample_kernel(qkv_ref, z_ref, gb_ref, st_ref, s0_ref, w_ref, ng_ref, o_ref, nst_ref, sout_ref,
                         e_ref, *, nseq, seq_len):
    qkv_rows = []
    hist = SHORT_WIDTH - 1
    e_ref[0:SUBLANES, :] = jnp.zeros((SUBLANES, 3 * DN_WIDTH), F32)
    for b in range(nseq):
        e_ref[SHORT_HALO - hist:SHORT_HALO, :] = st_ref[b]
        e_ref[SHORT_HALO:SHORT_HALO + seq_len, :] = qkv_ref[b * seq_len:(b + 1) * seq_len, :]
        qkv_rows.append(_short_conv(e_ref, w_ref, seq_len))
        nst_ref[b] = e_ref[seq_len + SHORT_HALO - hist:seq_len + SHORT_HALO, :]
    qkv = jnp.concatenate(qkv_rows, axis=0)
    masks = _chunk_masks(seq_len)
    prep = _chunks_prepare([qkv], [gb_ref[...]], masks, seq_len)[0]
    tn = (((0,), (0,)), ((), ()))
    rows = [slice(b * seq_len, (b + 1) * seq_len) for b in range(nseq)]
    both = [[_dot_delta(jnp.concatenate([prep[h][1][r], prep[h][3][r]], axis=0), s0_ref[b, h])
             for b, r in enumerate(rows)] for h in range(DN_HEADS)]
    v_new = [[prep[h][0][r] - both[h][b][:seq_len] for b, r in enumerate(rows)] for h in range(DN_HEADS)]
    for h in range(DN_HEADS):
        kdec, egl = prep[h][4], prep[h][5]
        for b, r in enumerate(rows):
            sout_ref[b, h] = (s0_ref[b, h] * egl[b * seq_len:b * seq_len + 1, :]
                              + _dot_delta(kdec[r], v_new[h][b], tn))
    for h in range(DN_HEADS):
        o = (jnp.concatenate([both[h][b][seq_len:] for b in range(nseq)], axis=0)
             + _dot_delta(prep[h][2], jnp.concatenate(v_new[h], axis=0)))
        lanes = slice(h * DN_HEAD_DIM, (h + 1) * DN_HEAD_DIM)
        o_ref[:, lanes] = _gated_out_norm(o, z_ref[:, lanes], ng_ref[...]).astype(o_ref.dtype)


def _delta_sample(qkv_pre, z, gb, state8, s0, conv_w, norm_g, *, row0, bsz, seq_len):
    n = bsz * seq_len
    assert CHUNK % seq_len == 0
    nseq = CHUNK // seq_len
    assert bsz % nseq == 0 and row0 % CHUNK == 0
    blk0 = row0 // CHUNK
    kern = functools.partial(_delta_sample_kernel, nseq=nseq, seq_len=seq_len)
    tile = lambda i: (blk0 + i, 0)
    blk3 = lambda i: (0, i, 0, 0)
    blk4 = lambda i: (0, i, 0, 0, 0)
    return pl.pallas_call(
        kern,
        grid=(bsz // nseq,),
        in_specs=[
            pl.BlockSpec((CHUNK, 3 * DN_WIDTH), tile),
            pl.BlockSpec((CHUNK, DN_WIDTH), tile),
            pl.BlockSpec((CHUNK, LANES), tile),
            pl.BlockSpec((None, nseq, SHORT_WIDTH - 1, 3 * DN_WIDTH), blk3),
            pl.BlockSpec((None, nseq, DN_HEADS, DN_HEAD_DIM, DN_HEAD_DIM), blk4),
            pl.BlockSpec((SHORT_WIDTH, 3 * DN_WIDTH), lambda i: (0, 0)),
            pl.BlockSpec((1, DN_HEAD_DIM), lambda i: (0, 0)),
        ],
        out_specs=[
            pl.BlockSpec((CHUNK, DN_WIDTH), lambda i: (i, 0)),
            pl.BlockSpec((None, nseq, SHORT_WIDTH - 1, 3 * DN_WIDTH), blk3),
            pl.BlockSpec((None, nseq, DN_HEADS, DN_HEAD_DIM, DN_HEAD_DIM), blk4),
        ],
        out_shape=[
            jax.ShapeDtypeStruct((n, DN_WIDTH), BF16),
            jax.ShapeDtypeStruct((1, bsz, SHORT_WIDTH - 1, 3 * DN_WIDTH), F32),
            jax.ShapeDtypeStruct((1, bsz, DN_HEADS, DN_HEAD_DIM, DN_HEAD_DIM), F32),
        ],
        scratch_shapes=[pltpu.VMEM((SHORT_HALO + seq_len, 3 * DN_WIDTH), F32)],
        compiler_params=_params(1),
    )(qkv_pre, z, gb, state8, s0, conv_w, norm_g)


def _mix_kernel(xp_ref, xs_ref, cap_ref, cas_ref, oap_ref, oas_ref, sa_ref, sb_ref, wc_ref, wd_ref, wm_ref,
                g2_ref, rw_ref, rb_ref, x2_ref, tr_ref, *, n_prompt_tiles):
    is_prompt = pl.program_id(0) < n_prompt_tiles
    x = jnp.where(is_prompt, xp_ref[...], xs_ref[...])
    ca = jnp.where(is_prompt, cap_ref[...], cas_ref[...])
    oa = jnp.where(is_prompt, oap_ref[...], oas_ref[...])
    ya = jnp.dot(ca, wc_ref[...], preferred_element_type=F32)
    yb = jnp.dot(oa, wd_ref[...], preferred_element_type=F32)
    mixed = sa_ref[...] * ya + sb_ref[...] * yb
    x2 = x + jnp.dot(mixed.astype(BF16), wm_ref[...], preferred_element_type=F32)
    x2_ref[...] = x2
    h2 = x2 * lax.rsqrt(jnp.mean(x2 * x2, axis=-1, keepdims=True) + EPS) * g2_ref[...]
    logits = _dot_split(h2, rw_ref[...]) + rb_ref[...]
    lt = logits.T[:N_EXPERTS, :]
    tokens = lt.shape[1]
    row = lax.broadcasted_iota(jnp.int32, lt.shape, 0).astype(F32)
    top_vals, top_idx = [], []
    for k in range(TOP_K):
        m = jnp.max(lt, axis=0, keepdims=True)
        idx = jnp.min(jnp.where(lt == m, row, float(N_EXPERTS)), axis=0, keepdims=True)
        top_vals.append(m)
        top_idx.append(idx)
        lt = jnp.where(row == idx, -jnp.inf, lt)
    exps = [jnp.exp(v - top_vals[0]) for v in top_vals]
    den = exps[0] + exps[1] + exps[2] + exps[3]
    slot = lax.broadcasted_iota(jnp.int32, (2 * TOP_K, tokens), 0)
    packed = jnp.zeros((2 * TOP_K, tokens), F32)
    for k in range(TOP_K):
        packed = jnp.where(slot == k, top_idx[k], packed)
        packed = jnp.where(slot == TOP_K + k, exps[k] / den, packed)
    packed = jnp.concatenate([packed, jnp.zeros((LANES - 2 * TOP_K, tokens), F32)], axis=0)
    tr_ref[...] = packed.T


def _mix(x_p, x_s, cact_p, cact_s, oact_p, oact_s, siga, sigb, w_conv_out, w_delta_out, w_merge_out, norm2_g,
         router_w, router_b, tm):
    (n_p, d), n_s = x_p.shape, x_s.shape[0]
    n = n_p + n_s
    row = lambda i: (i, 0)
    const = lambda i: (0, 0)
    return pl.pallas_call(
        functools.partial(_mix_kernel, n_prompt_tiles=n_p // tm),
        grid=(n // tm,),
        in_specs=[
            *_two_source_specs(tm, d, n_p // tm),
            *_two_source_specs(tm, D_CONV, n_p // tm),
            *_two_source_specs(tm, DN_WIDTH, n_p // tm),
            pl.BlockSpec((tm, d), row),
            pl.BlockSpec((tm, d), row),
            pl.BlockSpec((D_CONV, d), const),
            pl.BlockSpec((DN_WIDTH, d), const),
            pl.BlockSpec((d, d), const),
            pl.BlockSpec((1, d), const),
            pl.BlockSpec((d, 2 * LANES), const),
            pl.BlockSpec((1, LANES), const),
        ],
        out_specs=[
            pl.BlockSpec((tm, d), row),
            pl.BlockSpec((tm, LANES), row),
        ],
        out_shape=[
            jax.ShapeDtypeStruct((n, d), F32),
            jax.ShapeDtypeStruct((n, LANES), F32),
        ],
        compiler_params=_params(1),
    )(x_p, x_s, cact_p, cact_s, oact_p, oact_s, siga, sigb, w_conv_out, w_delta_out, w_merge_out, norm2_g,
      router_w, router_b)


def _fill_rows_per_step(n_fill, n_steps):
    per_step = SUBLANES
    while per_step * n_steps < n_fill:
        per_step *= 2
    assert n_fill % per_step == 0
    return per_step


def _route(top_idx, tm, n_blocks, n_steps):
    n = top_idx.shape[1]
    n_fill = n_blocks * tm - n * TOP_K
    assert n_fill == N_EXPERTS * tm
    experts = jnp.arange(N_EXPERTS, dtype=jnp.int32)
    chosen = top_idx[:, :, None] == experts[None, None, :]
    per_token = jnp.sum(chosen, axis=0, dtype=jnp.int32)
    csum = jnp.cumsum(per_token, axis=0)
    counts = csum[-1]
    padded = (counts + tm - 1) // tm * tm
    pad_end = jnp.cumsum(padded)
    pad_start = pad_end - padded
    first_free = (pad_start[None, :] + csum - per_token)[None, :, :]
    dest = jnp.sum(jnp.where(chosen, first_free, 0), axis=2)
    dest = dest.reshape(TOP_K, n_steps, -1).transpose(1, 0, 2).reshape(n_steps, -1)
    nvalid = (pad_end[-1] // tm).astype(jnp.int32)
    blk = jnp.arange(n_blocks, dtype=jnp.int32)
    owner = jnp.sum((pad_end[None, :] <= (blk * tm)[:, None]).astype(jnp.int32), axis=1)
    block_e = jnp.minimum(owner, N_EXPERTS - 1)
    block_e = jnp.where(blk < nvalid, block_e, jnp.sum(jnp.where(blk == nvalid - 1, block_e, 0)))
    present = counts > 0
    later = present[None, :] & (experts[None, :] > experts[:, None])
    next_present = jnp.min(jnp.where(later, experts[None, :], N_EXPERTS), axis=1)
    next_present = jnp.where(next_present == N_EXPERTS, -1, next_present)
    parity = (jnp.cumsum(present.astype(jnp.int32)) - 1) % 2
    of_block = (block_e[:, None] == experts[None, :]).astype(jnp.int32)
    sched = jnp.stack([block_e, jnp.sum(of_block * next_present[None, :], axis=1),
                       jnp.sum(of_block * parity[None, :], axis=1)])
    n_pad = padded - counts
    spill = tm - n_pad
    spill_start = pad_end[-1] + jnp.cumsum(spill) - spill
    j = jnp.arange(tm, dtype=jnp.int32)[None, :]
    fill = jnp.where(j < n_pad[:, None], (pad_start + counts)[:, None] + j, (spill_start - n_pad)[:, None] + j)
    fill_step = _fill_rows_per_step(n_fill, n_steps)
    fill = jnp.pad(fill.reshape(-1, fill_step), ((0, n_steps - n_fill // fill_step), (0, 0)))
    table = jnp.concatenate([dest, fill], axis=1)
    return table, sched, nvalid.reshape(1)


def _to_token_tiles(x, ref, row0=0):
    t = x.shape[0]
    for s in range(SUBLANES):
        ref[pl.ds(row0 + s, t, stride=SUBLANES), :] = x[:, s * LANES:(s + 1) * LANES]


def _from_token_tiles(ref, t, row0=0):
    return jnp.concatenate([ref[pl.ds(row0 + s, t, stride=SUBLANES), :] for s in range(SUBLANES)], axis=1)


def _tile(ref, row):
    return ref.at[pl.ds(pl.multiple_of(row * SUBLANES, SUBLANES), SUBLANES)]


def _dispatch_kernel(tab_ref, x2_ref, g2_ref, xs_ref, buf, sems, *, tokens, fill_step, n_fill_steps):
    i = pl.program_id(0)
    last = pl.num_programs(0) - 1
    slot = i % 2
    base = slot * tokens
    fill_sem = 2

    x2 = x2_ref[...]
    h2 = x2 * lax.rsqrt(jnp.mean(x2 * x2, axis=-1, keepdims=True) + EPS) * g2_ref[...]
    _to_token_tiles(h2, buf, pl.multiple_of(base * SUBLANES, SUBLANES))

    def wait_tiles(sem_idx, count):
        while count > 0:
            rows = min(count, tokens) * SUBLANES
            pltpu.make_async_copy(buf.at[pl.ds(0, rows)], xs_ref.at[pl.ds(0, rows)], sems.at[sem_idx]).wait()
            count -= min(count, tokens)

    for t in range(tokens):
        for k in range(TOP_K):
            pltpu.make_async_copy(_tile(buf, base + t), _tile(xs_ref, tab_ref[i, k * tokens + t]),
                                  sems.at[slot]).start(priority=k % DMA_PRIORITIES)

    @pl.when(i < n_fill_steps)
    def _():
        for p in range(fill_step):
            pltpu.make_async_copy(_tile(buf, base), _tile(xs_ref, tab_ref[i, tokens * TOP_K + p]),
                                  sems.at[fill_sem]).start(priority=p % DMA_PRIORITIES)
        wait_tiles(fill_sem, fill_step)

    @pl.when(i > 0)
    def _():
        wait_tiles(1 - slot, tokens * TOP_K)

    @pl.when(i == last)
    def _():
        wait_tiles(slot, tokens * TOP_K)


def _dispatch(x2, norm_g, table, n_rows, tokens):
    n, d = x2.shape
    assert d == SUBLANES * LANES
    fill_step = table.shape[1] - tokens * TOP_K
    n_fill_steps = (n_rows - n * TOP_K) // fill_step
    return pl.pallas_call(
        functools.partial(_dispatch_kernel, tokens=tokens, fill_step=fill_step, n_fill_steps=n_fill_steps),
        grid_spec=pltpu.PrefetchScalarGridSpec(
            num_scalar_prefetch=1,
            grid=(n // tokens,),
            in_specs=[pl.BlockSpec((tokens, d), lambda i, tab: (i, 0)),
                      pl.BlockSpec((1, d), lambda i, tab: (0, 0))],
            out_specs=pl.BlockSpec(memory_space=pl.ANY),
            scratch_shapes=[pltpu.VMEM((2 * tokens * SUBLANES, LANES), F32), pltpu.SemaphoreType.DMA((3,))],
        ),
        out_shape=jax.ShapeDtypeStruct((n_rows * SUBLANES, LANES), F32),
        compiler_params=_params(1),
    )(table, x2, norm_g)


def _moe_ffn_kernel(sched_ref, nv_ref, xs_ref, wgu_hbm, bgu_ref, wd_hbm, bd_ref, out_ref,
                    wgu_f32, wd_f32, wgu_bf, wd_bf, sems, *, blocks_per_step):
    tm = xs_ref.shape[0] // SUBLANES // blocks_per_step
    f = wd_bf.shape[0]

    def weight_copies(e, h):
        return (pltpu.make_async_copy(wgu_hbm.at[e], wgu_f32.at[h], sems.at[h]),
                pltpu.make_async_copy(wd_hbm.at[e], wd_f32.at[h], sems.at[h]))

    def one_block(i, r0):
        expert, next_expert, half = sched_ref[0, i], sched_ref[1, i], sched_ref[2, i]

        @pl.when(i == 0)
        def _():
            for c in weight_copies(expert, half):
                c.start()

        @pl.when(jnp.logical_or(i == 0, expert != sched_ref[0, jnp.maximum(i - 1, 0)]))
        def _():
            for c in weight_copies(expert, half):
                c.wait()
            wgu_bf[...] = wgu_f32[half].astype(BF16)
            wd_bf[...] = wd_f32[half].astype(BF16)

            @pl.when(next_expert >= 0)
            def _():
                for c in weight_copies(next_expert, 1 - half):
                    c.start()

        @pl.when(i < nv_ref[0])
        def _():
            x = _from_token_tiles(xs_ref, tm, r0).astype(BF16)
            gu = jnp.dot(x, wgu_bf[...], preferred_element_type=F32) + bgu_ref[expert]
            gt = jnp.minimum(gu[:, :f], SWIGLU_LIMIT)
            up = jnp.clip(gu[:, f:], -SWIGLU_LIMIT, SWIGLU_LIMIT)
            act = (up + 1.0) * (gt * _sigmoid(SWIGLU_ALPHA * gt))
            y = jnp.dot(act.astype(BF16), wd_bf[...], preferred_element_type=F32) + bd_ref[expert]
            _to_token_tiles(y, out_ref, r0)

        @pl.when(i >= nv_ref[0])
        def _():
            out_ref[r0:r0 + tm * SUBLANES, :] = jnp.zeros((tm * SUBLANES, LANES), out_ref.dtype)

    for h in range(blocks_per_step):
        one_block(pl.program_id(0) * blocks_per_step + h, h * tm * SUBLANES)


def _moe_ffn(xs, sched, nvalid, w_gate_up, b_gate_up, w_down, b_down, tm, n_blocks):
    ne, d, f2 = w_gate_up.shape
    f = f2 // 2
    per_step = FFN_BLOCKS_PER_STEP if n_blocks % FFN_BLOCKS_PER_STEP == 0 else 1
    rows = per_step * tm * SUBLANES
    used = lambda p, sc, nv: (jnp.minimum(p, (nv[0] - 1) // per_step), 0)
    whole = lambda p, sc, nv: (0, 0, 0)
    return pl.pallas_call(
        functools.partial(_moe_ffn_kernel, blocks_per_step=per_step),
        grid_spec=pltpu.PrefetchScalarGridSpec(
            num_scalar_prefetch=2,
            grid=(n_blocks // per_step,),
            in_specs=[
                pl.BlockSpec((rows, LANES), used),
                pl.BlockSpec(memory_space=pl.ANY),
                pl.BlockSpec((ne, 1, f2), whole),
                pl.BlockSpec(memory_space=pl.ANY),
                pl.BlockSpec((ne, 1, d), whole),
            ],
            out_specs=pl.BlockSpec((rows, LANES), lambda p, sc, nv: (p, 0)),
            scratch_shapes=[pltpu.VMEM((2, d, f2), F32), pltpu.VMEM((2, f, d), F32),
                            pltpu.VMEM((d, f2), BF16), pltpu.VMEM((f, d), BF16), pltpu.SemaphoreType.DMA((2,))],
        ),
        out_shape=jax.ShapeDtypeStruct((n_blocks * tm * SUBLANES, LANES), F32),
        compiler_params=_params(1),
    )(sched, nvalid, xs, w_gate_up, b_gate_up.reshape(ne, 1, f2), w_down, b_down.reshape(ne, 1, d))


def _combine_kernel(slot_ref, yb_ref, x2_ref, tg_ref, fg_ref, yp_ref, ys_ref, buf, sems, *, tc, n_prompt_tiles):
    i = pl.program_id(0)
    n_steps = pl.num_programs(0)

    def region(slot, k):
        return (slot * TOP_K + k) * tc

    def fetch(step, slot):
        for t in range(tc):
            for k in range(TOP_K):
                pltpu.make_async_copy(
                    _tile(yb_ref, slot_ref[step, k * tc + t]), _tile(buf, region(slot, k) + t),
                    sems.at[slot]).start(priority=k % DMA_PRIORITIES)

    @pl.when(i == 0)
    def _():
        fetch(0, 0)

    @pl.when(i + 1 < n_steps)
    def _():
        fetch(i + 1, (i + 1) % 2)

    slot = i % 2
    for k in range(TOP_K):
        pltpu.make_async_copy(yb_ref.at[pl.ds(0, tc * SUBLANES)], buf.at[pl.ds(0, tc * SUBLANES)],
                              sems.at[slot]).wait()
    tg = tg_ref[...]
    y = x2_ref[...]
    for k in range(TOP_K):
        rows = _from_token_tiles(buf, tc, pl.multiple_of(region(slot, k) * SUBLANES, SUBLANES))
        y = y + tg[:, TOP_K + k:TOP_K + k + 1] * rows
    out = y * lax.rsqrt(jnp.mean(y * y, axis=-1, keepdims=True) + EPS) * fg_ref[...]

    @pl.when(i < n_prompt_tiles)
    def _():
        yp_ref[...] = out

    @pl.when(i >= n_prompt_tiles)
    def _():
        ys_ref[...] = out


def _combine(slot2d, yb, x2, tg, final_g, n_p, tc):
    n, d = x2.shape
    n_s = n - n_p
    assert n_p % tc == 0 and n_s % tc == 0
    npt = n_p // tc
    kern = functools.partial(_combine_kernel, tc=tc, n_prompt_tiles=npt)
    out_p, out_s = _two_source_specs(tc, d, npt)
    return pl.pallas_call(
        kern,
        grid_spec=pltpu.PrefetchScalarGridSpec(
            num_scalar_prefetch=1,
            grid=(n // tc,),
            in_specs=[
                pl.BlockSpec(memory_space=pl.ANY),
                pl.BlockSpec((tc, d), lambda i, s: (i, 0)),
                pl.BlockSpec((tc, LANES), lambda i, s: (i, 0)),
                pl.BlockSpec((1, d), lambda i, s: (0, 0)),
            ],
            out_specs=[out_p, out_s],
            scratch_shapes=[pltpu.VMEM((2 * TOP_K * tc * SUBLANES, LANES), F32), pltpu.SemaphoreType.DMA((2,))],
        ),
        out_shape=[jax.ShapeDtypeStruct((n_p, d), F32), jax.ShapeDtypeStruct((n_s, d), F32)],
        compiler_params=_params(1),
    )(slot2d, yb, x2, tg, final_g)


def _pad_lanes(v, width=LANES):
    v = v.reshape(1, -1)
    return jnp.pad(v, ((0, 0), (0, width - v.shape[1])))


def kernel(x_prompt, x_sample, state_conv, state_short_conv, state_delta, norm1_g, w_in, conv_dw_w,
           conv_dw_b, conv_ln_g, conv_ln_b, w_conv_out, short_conv_w, a_log, dt_bias, delta_norm_g,
           w_delta_out, w_merge_out, norm2_g, router_w, router_b, w_gate_up, b_gate_up, w_down, b_down,
           final_norm_g):
    depth = w_in.shape[0]
    assert depth == 1
    bp, tp, d = x_prompt.shape
    bs, ts, _ = x_sample.shape
    n_p, n_s = bp * tp, bs * ts
    n = n_p + n_s
    l = 0
    x_p = x_prompt.reshape(n_p, d)
    x_s = x_sample.reshape(n_s, d)

    o_ab = 2 * D_CONV + 4 * DN_WIDTH
    w = w_in[l]
    w_bf = w.astype(BF16)
    w_main = w_bf[:, :o_ab]
    w_gates = w_bf[:, o_ab + 2 * DN_HEADS:]
    w_ab = _split_bf16(jnp.pad(w[:, o_ab:o_ab + 2 * DN_HEADS], ((0, 0), (0, LANES - 2 * DN_HEADS))))

    glu, qkv_pre, z, gb, siga, sigb = _inproj(
        x_p, x_s, norm1_g[l].reshape(1, d), w_main, w_gates, w_ab, _pad_lanes(a_log[l]), _pad_lanes(dt_bias[l]),
        MIX_TILE if n_p % MIX_TILE == 0 and n_s % MIX_TILE == 0 else TOKEN_TILE)

    dw = (conv_dw_w[l], conv_dw_b[l].reshape(1, -1), conv_ln_g[l].reshape(1, -1), conv_ln_b[l].reshape(1, -1))
    st_c_p = jnp.zeros((depth, bp, CONV_WIDTH - 1, D_CONV), F32)
    seq_tile = SEQ_TILE if tp % SEQ_TILE == 0 else TOKEN_TILE
    cact_p, conv_p = _conv_branch(glu, st_c_p, *dw, row0=0, bsz=bp, t_len=tp, bb=1, tt=seq_tile)
    cact_s, conv_s = _conv_branch(glu, state_conv, *dw, row0=n_p, bsz=bs, t_len=ts, bb=8, tt=ts)

    st_s_p = jnp.zeros((depth, bp, SHORT_WIDTH - 1, 3 * DN_WIDTH), F32)
    s0_p = jnp.zeros((depth, bp, DN_HEADS, DN_HEAD_DIM, DN_HEAD_DIM), F32)
    ng = delta_norm_g[l].reshape(1, -1)
    oact_p, short_p, s_p = _delta_prompt(qkv_pre, z, gb, st_s_p, s0_p, short_conv_w[l], ng,
                                         bsz=bp, t_len=tp, tt=seq_tile)
    oact_s, short_s, s_s = _delta_sample(qkv_pre, z, gb, state_short_conv, state_delta, short_conv_w[l], ng,
                                         row0=n_p, bsz=bs, seq_len=ts)

    rw = _split_bf16(jnp.pad(router_w[l], ((0, 0), (0, LANES - N_EXPERTS))))
    x2, tg = _mix(x_p, x_s, cact_p, cact_s, oact_p, oact_s, siga, sigb, w_conv_out[l].astype(BF16),
                  w_delta_out[l].astype(BF16), w_merge_out[l].astype(BF16), norm2_g[l].reshape(1, d),
                  rw, _pad_lanes(router_b[l]),
                  MIX_TILE if n_p % MIX_TILE == 0 and n_s % MIX_TILE == 0 else TOKEN_TILE)

    n_blocks = -(-(n * TOP_K) // MOE_TILE) + N_EXPERTS
    route_tile = ROUTE_TILE if n_p % ROUTE_TILE == 0 and n_s % ROUTE_TILE == 0 else TOKEN_TILE
    n_steps = n // route_tile
    table, sched, nvalid = _route(tg[:, :TOP_K].T.astype(jnp.int32), MOE_TILE, n_blocks, n_steps)
    xs = _dispatch(x2, norm2_g[l].reshape(1, d), table, n_blocks * MOE_TILE, route_tile)
    yb = _moe_ffn(xs, sched, nvalid, w_gate_up[l], b_gate_up[l], w_down[l], b_down[l], MOE_TILE, n_blocks)
    y_p, y_s = _combine(table, yb, x2, tg, final_norm_g.reshape(1, d), n_p, route_tile)

    return (y_p.reshape(bp, tp, d), y_s.reshape(bs, ts, d), conv_p, short_p, s_p, conv_s, short_s, s_s)
```

```python
import functools

import jax
import jax.numpy as jnp
from jax import lax
from jax.experimental import pallas as pl
from jax.experimental.pallas import tpu as pltpu

F32 = jnp.float32
BF16 = jnp.bfloat16
EPS = 1e-6

LANES = 128
SUBLANES = 8
VMEM_LIMIT_BYTES = 56 * 1024 * 1024
DMA_PRIORITIES = 2

D_CONV = 512
CONV_WIDTH = 31
DN_HEADS = 4
DN_HEAD_DIM = 128
DN_WIDTH = DN_HEADS * DN_HEAD_DIM
SHORT_WIDTH = 4
N_EXPERTS = 32
TOP_K = 4
SWIGLU_LIMIT = 7.0
SWIGLU_ALPHA = 1.702

CHUNK = 128
CONV_HALO = 32
SHORT_HALO = 8

MIX_TILE = 512
TOKEN_TILE = 256
ROUTE_TILE = 512
SEQ_TILE = 512
MOE_TILE = 256
FFN_BLOCKS_PER_STEP = 2


def _sigmoid(x):
    return 1.0 / (1.0 + jnp.exp(-x))


def _silu(x):
    return x * _sigmoid(x)


def _split_bf16(w):
    hi = w.astype(BF16)
    lo = (w - hi.astype(F32)).astype(BF16)
    return jnp.concatenate([hi, lo], axis=-1)


def _dot_split(x, w_split):
    n = w_split.shape[-1] // 2
    x_hi = x.astype(BF16)
    x_lo = (x - x_hi.astype(F32)).astype(BF16)
    r = jnp.dot(x_hi, w_split, preferred_element_type=F32)
    return r[:, :n] + r[:, n:] + jnp.dot(x_lo, w_split[:, :n], preferred_element_type=F32)


def _dot_delta(a, b, dims=(((1,), (0,)), ((), ()))):
    return lax.dot_general(a.astype(BF16), b.astype(BF16), dims, preferred_element_type=F32)


def _params(n_axes):
    return pltpu.CompilerParams(dimension_semantics=("arbitrary",) * n_axes, vmem_limit_bytes=VMEM_LIMIT_BYTES)


def _two_source_specs(tm, d, n_first_tiles):
    first = pl.BlockSpec((tm, d), lambda i, *_: (jnp.minimum(i, n_first_tiles - 1), 0))
    second = pl.BlockSpec((tm, d), lambda i, *_: (jnp.maximum(i - n_first_tiles, 0), 0))
    return first, second


def _inproj_kernel(xp_ref, xs_ref, g_ref, w_ref, wg_ref, wab_ref, alog_ref, dtb_ref,
                   glu_ref, qkv_ref, z_ref, gb_ref, sa_ref, sb_ref, *, n_prompt_tiles):
    is_prompt = pl.program_id(0) < n_prompt_tiles
    o_gate, o_qkv, o_z = D_CONV, 2 * D_CONV, 2 * D_CONV + 3 * DN_WIDTH
    o_ga = o_z + DN_WIDTH
    d = xp_ref.shape[-1]
    for r0 in range(0, xp_ref.shape[0], TOKEN_TILE):
        r = slice(r0, r0 + TOKEN_TILE)
        x = jnp.where(is_prompt, xp_ref[r, :], xs_ref[r, :])
        h = x * lax.rsqrt(jnp.mean(x * x, axis=-1, keepdims=True) + EPS) * g_ref[...]
        hb = h.astype(BF16)

        def mm(ref, lo, hi):
            return lax.dot_general(hb, ref[lo:hi, :], (((1,), (1,)), ((), ())), preferred_element_type=F32)

        glu_ref[r, :] = mm(w_ref, 0, o_gate) * _sigmoid(mm(w_ref, o_gate, o_qkv))
        qkv_ref[r, :] = mm(w_ref, o_qkv, o_z)
        z_ref[r, :] = mm(w_ref, o_z, o_ga)
        sa_ref[r, :] = _sigmoid(mm(wg_ref, 0, d))
        sb_ref[r, :] = _sigmoid(mm(wg_ref, d, 2 * d))
        ab = _dot_split(h, wab_ref[...])
        xa = ab + dtb_ref[...]
        softplus = jnp.maximum(xa, 0.0) + jnp.log(1.0 + jnp.exp(-jnp.abs(xa)))
        g = -jnp.exp(alog_ref[...]) * softplus
        lane = lax.broadcasted_iota(jnp.int32, ab.shape, 1)
        gb_ref[r, :] = jnp.where(lane < DN_HEADS, g, _sigmoid(ab))


def _inproj(x_p, x_s, norm_g, w_main, w_gates, w_ab, alog, dtb, tm):
    (n_p, d), n_s = x_p.shape, x_s.shape[0]
    assert n_p % tm == 0 and n_s % tm == 0
    n = n_p + n_s
    wcols = w_main.shape[0]
    row = lambda i: (i, 0)
    const = lambda i: (0, 0)
    outs = [(D_CONV, F32), (3 * DN_WIDTH, F32), (DN_WIDTH, F32), (LANES, F32), (d, F32), (d, F32)]
    return pl.pallas_call(
        functools.partial(_inproj_kernel, n_prompt_tiles=n_p // tm),
        grid=(n // tm,),
        in_specs=[
            *_two_source_specs(tm, d, n_p // tm),
            pl.BlockSpec((1, d), const),
            pl.BlockSpec((wcols, d), const),
            pl.BlockSpec((2 * d, d), const),
            pl.BlockSpec((d, 2 * LANES), const),
            pl.BlockSpec((1, LANES), const),
            pl.BlockSpec((1, LANES), const),
        ],
        out_specs=[pl.BlockSpec((tm, c), row) for c, _ in outs],
        out_shape=[jax.ShapeDtypeStruct((n, c), dt) for c, dt in outs],
        compiler_params=_params(1),
    )(x_p, x_s, norm_g, w_main, w_gates, w_ab, alog, dtb)


def _conv_kernel(glu_ref, st_ref, w_ref, b_ref, lg_ref, lb_ref, out_ref, nst_ref, e_ref, sh_ref, *, bb, tt, rows):
    t = pl.program_id(1)

    hist = CONV_WIDTH - 1

    @pl.when(t == 0)
    def _():
        e_ref[:, 0:SUBLANES, :] = jnp.zeros((bb, SUBLANES, D_CONV), F32)
        e_ref[:, CONV_HALO - hist:CONV_HALO, :] = st_ref[...]

    for b in range(bb):
        e_ref[b, CONV_HALO:CONV_HALO + tt, :] = glu_ref[b * tt:(b + 1) * tt, :]
    off = CONV_HALO - (CONV_WIDTH - 1)
    span = sh_ref.shape[1]
    for b in range(bb):
        for s in range(1, SUBLANES):
            sh_ref[s - 1] = e_ref[b, s:s + span, :]
        for c in range(tt // rows):
            r0 = c * rows
            acc = jnp.zeros((rows, D_CONV), F32) + b_ref[...]
            for j in range(CONV_WIDTH):
                q, s = divmod(j + off, SUBLANES)
                lo = r0 + q * SUBLANES
                src = e_ref[b, lo:lo + rows, :] if s == 0 else sh_ref[s - 1, lo:lo + rows, :]
                acc = acc + w_ref[j:j + 1, :] * src
            mu = jnp.mean(acc, axis=-1, keepdims=True)
            xc = acc - mu
            var = jnp.mean(xc * xc, axis=-1, keepdims=True)
            y = xc * lax.rsqrt(var + EPS) * lg_ref[...] + lb_ref[...]
            out_ref[b * tt + r0:b * tt + r0 + rows, :] = _silu(y).astype(out_ref.dtype)
    nst_ref[...] = e_ref[:, tt + CONV_HALO - hist:tt + CONV_HALO, :]
    e_ref[:, 0:CONV_HALO, :] = e_ref[:, tt:tt + CONV_HALO, :]


def _conv_branch(glu, state32, dw_w, dw_b, ln_g, ln_b, *, row0, bsz, t_len, bb, tt):
    c = glu.shape[1]
    assert bsz % bb == 0 and t_len % tt == 0 and row0 % (bb * tt) == 0
    nt = t_len // tt
    blk0 = row0 // (bb * tt)
    rows = min(tt, 32)
    kern = functools.partial(_conv_kernel, bb=bb, tt=tt, rows=rows)
    const = lambda b, t: (0, 0)
    return pl.pallas_call(
        kern,
        grid=(bsz // bb, nt),
        in_specs=[
            pl.BlockSpec((bb * tt, c), lambda b, t: (blk0 + b * nt + t, 0)),
            pl.BlockSpec((None, bb, CONV_WIDTH - 1, c), lambda b, t: (0, b, 0, 0)),
            pl.BlockSpec((CONV_WIDTH, c), const),
            pl.BlockSpec((1, c), const),
            pl.BlockSpec((1, c), const),
            pl.BlockSpec((1, c), const),
        ],
        out_specs=[
            pl.BlockSpec((bb * tt, c), lambda b, t: (b * nt + t, 0)),
            pl.BlockSpec((None, bb, CONV_WIDTH - 1, c), lambda b, t: (0, b, 0, 0)),
        ],
        out_shape=[
            jax.ShapeDtypeStruct((bsz * t_len, c), BF16),
            jax.ShapeDtypeStruct((1, bsz, CONV_WIDTH - 1, c), F32),
        ],
        scratch_shapes=[pltpu.VMEM((bb, CONV_HALO + tt, c), F32),
                        pltpu.VMEM((SUBLANES - 1, tt + CONV_HALO - SUBLANES, c), F32)],
        compiler_params=_params(2),
    )(glu, state32, dw_w, dw_b, ln_g, ln_b)


def _chunk_masks(seq_len):
    i = lax.broadcasted_iota(jnp.int32, (CHUNK, CHUNK), 0)
    j = lax.broadcasted_iota(jnp.int32, (CHUNK, CHUNK), 1)
    same = (i // seq_len) == (j // seq_len)
    incl = same & (i >= j)
    strict = same & (i > j)
    last = j == (i // seq_len) * seq_len + (seq_len - 1)
    levels = []
    blk = 1
    while blk < seq_len:
        levels.append(((i // (2 * blk)) == (j // (2 * blk))) & (((i // blk) % 2) == 1) & (((j // blk) % 2) == 0))
        blk *= 2
    eye = i == j
    return incl, strict, last, levels, eye


def _lane_col(x, lane):
    return jnp.broadcast_to(x[:, lane:lane + 1], (x.shape[0], LANES))


def _l2norm(x):
    return x * lax.rsqrt(jnp.sum(x * x, axis=-1, keepdims=True) + EPS)


def _select_sum(mask01, x):
    hi = x.astype(BF16)
    r1 = x - hi.astype(F32)
    mid = r1.astype(BF16)
    lo = (r1 - mid.astype(F32)).astype(BF16)
    w = x.shape[1]
    parts = jnp.dot(mask01, jnp.concatenate([hi, mid, lo], axis=1), preferred_element_type=F32)
    return parts[:, :w] + parts[:, w:2 * w] + parts[:, 2 * w:]


def _chunks_prepare(qkvs, gbts, masks, seq_len):
    incl, strict, last, levels, eye = masks
    nt = (((1,), (1,)), ((), ()))
    lower01 = jnp.where(incl, 1.0, 0.0).astype(BF16)
    probs = []
    for qkv, gbt in zip(qkvs, gbts):
        gc = _select_sum(lower01, gbt)
        gct = gc.T
        if seq_len == CHUNK:
            glast = jnp.broadcast_to(gc[CHUNK - 1:CHUNK, :], gc.shape)
        else:
            glast = _select_sum(jnp.where(last, 1.0, 0.0).astype(BF16), gc)
        for h in range(DN_HEADS):
            q = _l2norm(qkv[:, h * DN_HEAD_DIM:(h + 1) * DN_HEAD_DIM]) * (DN_HEAD_DIM ** -0.5)
            k = _l2norm(qkv[:, DN_WIDTH + h * DN_HEAD_DIM:DN_WIDTH + (h + 1) * DN_HEAD_DIM])
            v = qkv[:, 2 * DN_WIDTH + h * DN_HEAD_DIM:2 * DN_WIDTH + (h + 1) * DN_HEAD_DIM]
            gcol = _lane_col(gc, h)
            grow = jnp.broadcast_to(gct[h:h + 1, :], (CHUNK, CHUNK))
            beta = _lane_col(gbt, DN_HEADS + h)
            gl = _lane_col(glast, h)
            decay = jnp.exp(jnp.where(incl, gcol - grow, -jnp.inf))
            egc = jnp.exp(gcol)
            kb = k * beta
            probs.append(dict(q=q, k=k, kb=kb, decay=decay, rhs=jnp.concatenate([v * beta, kb * egc], axis=1),
                              qexp=q * egc, kdec=k * jnp.exp(gl - gcol), egl=jnp.exp(gl)))
    for p in probs:
        p['a'] = jnp.where(strict, _dot_delta(p['kb'], p['k'], nt) * p['decay'], 0.0)
        p['scores'] = _dot_delta(p['q'], p['k'], nt) * p['decay']
    for p in probs:
        p['x'] = jnp.where(eye, 1.0, 0.0) - jnp.where(levels[0], p['a'], 0.0)
    for m in levels[1:]:
        for p in probs:
            p['xa'] = _dot_delta(p['x'], jnp.where(m, p['a'], 0.0))
        for p in probs:
            p['x'] = p['x'] - _dot_delta(p['xa'], p['x'])
    out = []
    for c in range(len(qkvs)):
        heads = []
        for h in range(DN_HEADS):
            p = probs[c * DN_HEADS + h]
            sol = _dot_delta(p['x'], p['rhs'])
            heads.append((sol[:, :DN_HEAD_DIM], sol[:, DN_HEAD_DIM:], p['scores'], p['qexp'], p['kdec'], p['egl']))
        out.append(heads)
    return out


def _gated_out_norm(o, z, ng):
    y = o * lax.rsqrt(jnp.mean(o * o, axis=-1, keepdims=True) + EPS) * ng
    return y * _silu(z)


def _short_conv(e_ref, w_ref, tt):
    off = SHORT_HALO - (SHORT_WIDTH - 1)
    acc = w_ref[0:1, :] * e_ref[off:off + tt, :]
    for j in range(1, SHORT_WIDTH):
        acc = acc + w_ref[j:j + 1, :] * e_ref[off + j:off + j + tt, :]
    return _silu(acc)


def _delta_prompt_kernel(qkv_ref, z_ref, gb_ref, st_ref, s0_ref, w_ref, ng_ref,
                         o_ref, nst_ref, sout_ref, e_ref, s_ref, *, tt):
    t = pl.program_id(1)

    hist = SHORT_WIDTH - 1

    @pl.when(t == 0)
    def _():
        e_ref[0:SUBLANES, :] = jnp.zeros((SUBLANES, 3 * DN_WIDTH), F32)
        e_ref[SHORT_HALO - hist:SHORT_HALO, :] = st_ref[0]
        s_ref[...] = s0_ref[0]

    e_ref[SHORT_HALO:SHORT_HALO + tt, :] = qkv_ref[...]
    qkv = _short_conv(e_ref, w_ref, tt)
    nst_ref[0] = e_ref[tt + SHORT_HALO - hist:tt + SHORT_HALO, :]
    e_ref[0:SHORT_HALO, :] = e_ref[tt:tt + SHORT_HALO, :]

    masks = _chunk_masks(CHUNK)
    tn = (((0,), (0,)), ((), ()))
    n_chunks = tt // CHUNK
    prep = _chunks_prepare([qkv[c * CHUNK:(c + 1) * CHUNK, :] for c in range(n_chunks)],
                           [gb_ref[c * CHUNK:(c + 1) * CHUNK, :] for c in range(n_chunks)], masks, CHUNK)
    heads = range(DN_HEADS)
    s = [s_ref[h] for h in heads]
    for c in range(n_chunks):
        r0 = c * CHUNK
        value, kcum, scores, qexp, kdec, egl = zip(*prep[c])
        both = [_dot_delta(jnp.concatenate([kcum[h], qexp[h]], axis=0), s[h]) for h in heads]
        v_new = [value[h] - both[h][:CHUNK] for h in heads]
        o = [both[h][CHUNK:] + _dot_delta(scores[h], v_new[h]) for h in heads]
        s = [s[h] * egl[h][0:1, :] + _dot_delta(kdec[h], v_new[h], tn) for h in heads]
        for h in heads:
            lanes = slice(h * DN_HEAD_DIM, (h + 1) * DN_HEAD_DIM)
            o_ref[r0:r0 + CHUNK, lanes] = _gated_out_norm(
                o[h], z_ref[r0:r0 + CHUNK, lanes], ng_ref[...]).astype(o_ref.dtype)
    for h in heads:
        s_ref[h] = s[h]
        sout_ref[0, h] = s[h]


def _delta_prompt(qkv_pre, z, gb, state8, s0, conv_w, norm_g, *, bsz, t_len, tt):
    n = bsz * t_len
    assert t_len % tt == 0 and tt % CHUNK == 0
    nt = t_len // tt
    kern = functools.partial(_delta_prompt_kernel, tt=tt)
    tile = lambda b, t: (b * nt + t, 0)
    per_b = lambda b, t: (0, b, 0, 0)
    per_b4 = lambda b, t: (0, b, 0, 0, 0)
    return pl.pallas_call(
        kern,
        grid=(bsz, nt),
        in_specs=[
            pl.BlockSpec((tt, 3 * DN_WIDTH), tile),
            pl.BlockSpec((tt, DN_WIDTH), tile),
            pl.BlockSpec((tt, LANES), tile),
            pl.BlockSpec((None, 1, SHORT_WIDTH - 1, 3 * DN_WIDTH), per_b),
            pl.BlockSpec((None, 1, DN_HEADS, DN_HEAD_DIM, DN_HEAD_DIM), per_b4),
            pl.BlockSpec((SHORT_WIDTH, 3 * DN_WIDTH), lambda b, t: (0, 0)),
            pl.BlockSpec((1, DN_HEAD_DIM), lambda b, t: (0, 0)),
        ],
        out_specs=[
            pl.BlockSpec((tt, DN_WIDTH), tile),
            pl.BlockSpec((None, 1, SHORT_WIDTH - 1, 3 * DN_WIDTH), per_b),
            pl.BlockSpec((None, 1, DN_HEADS, DN_HEAD_DIM, DN_HEAD_DIM), per_b4),
        ],
        out_shape=[
            jax.ShapeDtypeStruct((n, DN_WIDTH), BF16),
            jax.ShapeDtypeStruct((1, bsz, SHORT_WIDTH - 1, 3 * DN_WIDTH), F32),
            jax.ShapeDtypeStruct((1, bsz, DN_HEADS, DN_HEAD_DIM, DN_HEAD_DIM), F32),
        ],
        scratch_shapes=[
            pltpu.VMEM((SHORT_HALO + tt, 3 * DN_WIDTH), F32),
            pltpu.VMEM((DN_HEADS, DN_HEAD_DIM, DN_HEAD_DIM), F32),
        ],
        compiler_params=_params(2),
    )(qkv_pre, z, gb, state8, s0, conv_w, norm_g)


def _delta_sample_kernel(qkv_ref, z_ref, gb_ref, st_ref, s0_ref, w_ref, ng_ref, o_ref, nst_ref, sout_ref,
                         e_ref, *, nseq, seq_len):
    qkv_rows = []
    hist = SHORT_WIDTH - 1
    e_ref[0:SUBLANES, :] = jnp.zeros((SUBLANES, 3 * DN_WIDTH), F32)
    for b in range(nseq):
        e_ref[SHORT_HALO - hist:SHORT_HALO, :] = st_ref[b]
        e_ref[SHORT_HALO:SHORT_HALO + seq_len, :] = qkv_ref[b * seq_len:(b + 1) * seq_len, :]
        qkv_rows.append(_short_conv(e_ref, w_ref, seq_len))
        nst_ref[b] = e_ref[seq_len + SHORT_HALO - hist:seq_len + SHORT_HALO, :]
    qkv = jnp.concatenate(qkv_rows, axis=0)
    masks = _chunk_masks(seq_len)
    prep = _chunks_prepare([qkv], [gb_ref[...]], masks, seq_len)[0]
    tn = (((0,), (0,)), ((), ()))
    rows = [slice(b * seq_len, (b + 1) * seq_len) for b in range(nseq)]
    both = [[_dot_delta(jnp.concatenate([prep[h][1][r], prep[h][3][r]], axis=0), s0_ref[b, h])
             for b, r in enumerate(rows)] for h in range(DN_HEADS)]
    v_new = [[prep[h][0][r] - both[h][b][:seq_len] for b, r in enumerate(rows)] for h in range(DN_HEADS)]
    for h in range(DN_HEADS):
        kdec, egl = prep[h][4], prep[h][5]
        for b, r in enumerate(rows):
            sout_ref[b, h] = (s0_ref[b, h] * egl[b * seq_len:b * seq_len + 1, :]
                              + _dot_delta(kdec[r], v_new[h][b], tn))
    for h in range(DN_HEADS):
        o = (jnp.concatenate([both[h][b][seq_len:] for b in range(nseq)], axis=0)
             + _dot_delta(prep[h][2], jnp.concatenate(v_new[h], axis=0)))
        lanes = slice(h * DN_HEAD_DIM, (h + 1) * DN_HEAD_DIM)
        o_ref[:, lanes] = _gated_out_norm(o, z_ref[:, lanes], ng_ref[...]).astype(o_ref.dtype)


def _delta_sample(qkv_pre, z, gb, state8, s0, conv_w, norm_g, *, row0, bsz, seq_len):
    n = bsz * seq_len
    assert CHUNK % seq_len == 0
    nseq = CHUNK // seq_len
    assert bsz % nseq == 0 and row0 % CHUNK == 0
    blk0 = row0 // CHUNK
    kern = functools.partial(_delta_sample_kernel, nseq=nseq, seq_len=seq_len)
    tile = lambda i: (blk0 + i, 0)
    blk3 = lambda i: (0, i, 0, 0)
    blk4 = lambda i: (0, i, 0, 0, 0)
    return pl.pallas_call(
        kern,
        grid=(bsz // nseq,),
        in_specs=[
            pl.BlockSpec((CHUNK, 3 * DN_WIDTH), tile),
            pl.BlockSpec((CHUNK, DN_WIDTH), tile),
            pl.BlockSpec((CHUNK, LANES), tile),
            pl.BlockSpec((None, nseq, SHORT_WIDTH - 1, 3 * DN_WIDTH), blk3),
            pl.BlockSpec((None, nseq, DN_HEADS, DN_HEAD_DIM, DN_HEAD_DIM), blk4),
            pl.BlockSpec((SHORT_WIDTH, 3 * DN_WIDTH), lambda i: (0, 0)),
            pl.BlockSpec((1, DN_HEAD_DIM), lambda i: (0, 0)),
        ],
        out_specs=[
            pl.BlockSpec((CHUNK, DN_WIDTH), lambda i: (i, 0)),
            pl.BlockSpec((None, nseq, SHORT_WIDTH - 1, 3 * DN_WIDTH), blk3),
            pl.BlockSpec((None, nseq, DN_HEADS, DN_HEAD_DIM, DN_HEAD_DIM), blk4),
        ],
        out_shape=[
            jax.ShapeDtypeStruct((n, DN_WIDTH), BF16),
            jax.ShapeDtypeStruct((1, bsz, SHORT_WIDTH - 1, 3 * DN_WIDTH), F32),
            jax.ShapeDtypeStruct((1, bsz, DN_HEADS, DN_HEAD_DIM, DN_HEAD_DIM), F32),
        ],
        scratch_shapes=[pltpu.VMEM((SHORT_HALO + seq_len, 3 * DN_WIDTH), F32)],
        compiler_params=_params(1),
    )(qkv_pre, z, gb, state8, s0, conv_w, norm_g)


def _mix_kernel(xp_ref, xs_ref, cap_ref, cas_ref, oap_ref, oas_ref, sa_ref, sb_ref, wc_ref, wd_ref, wm_ref,
                g2_ref, rw_ref, rb_ref, x2_ref, tr_ref, *, n_prompt_tiles):
    is_prompt = pl.program_id(0) < n_prompt_tiles
    x = jnp.where(is_prompt, xp_ref[...], xs_ref[...])
    ca = jnp.where(is_prompt, cap_ref[...], cas_ref[...])
    oa = jnp.where(is_prompt, oap_ref[...], oas_ref[...])
    ya = jnp.dot(ca, wc_ref[...], preferred_element_type=F32)
    yb = jnp.dot(oa, wd_ref[...], preferred_element_type=F32)
    mixed = sa_ref[...] * ya + sb_ref[...] * yb
    x2 = x + jnp.dot(mixed.astype(BF16), wm_ref[...], preferred_element_type=F32)
    x2_ref[...] = x2
    h2 = x2 * lax.rsqrt(jnp.mean(x2 * x2, axis=-1, keepdims=True) + EPS) * g2_ref[...]
    logits = _dot_split(h2, rw_ref[...]) + rb_ref[...]
    lt = logits.T[:N_EXPERTS, :]
    tokens = lt.shape[1]
    row = lax.broadcasted_iota(jnp.int32, lt.shape, 0).astype(F32)
    top_vals, top_idx = [], []
    for k in range(TOP_K):
        m = jnp.max(lt, axis=0, keepdims=True)
        idx = jnp.min(jnp.where(lt == m, row, float(N_EXPERTS)), axis=0, keepdims=True)
        top_vals.append(m)
        top_idx.append(idx)
        lt = jnp.where(row == idx, -jnp.inf, lt)
    exps = [jnp.exp(v - top_vals[0]) for v in top_vals]
    den = exps[0] + exps[1] + exps[2] + exps[3]
    slot = lax.broadcasted_iota(jnp.int32, (2 * TOP_K, tokens), 0)
    packed = jnp.zeros((2 * TOP_K, tokens), F32)
    for k in range(TOP_K):
        packed = jnp.where(slot == k, top_idx[k], packed)
        packed = jnp.where(slot == TOP_K + k, exps[k] / den, packed)
    packed = jnp.concatenate([packed, jnp.zeros((LANES - 2 * TOP_K, tokens), F32)], axis=0)
    tr_ref[...] = packed.T


def _mix(x_p, x_s, cact_p, cact_s, oact_p, oact_s, siga, sigb, w_conv_out, w_delta_out, w_merge_out, norm2_g,
         router_w, router_b, tm):
    (n_p, d), n_s = x_p.shape, x_s.shape[0]
    n = n_p + n_s
    row = lambda i: (i, 0)
    const = lambda i: (0, 0)
    return pl.pallas_call(
        functools.partial(_mix_kernel, n_prompt_tiles=n_p // tm),
        grid=(n // tm,),
        in_specs=[
            *_two_source_specs(tm, d, n_p // tm),
            *_two_source_specs(tm, D_CONV, n_p // tm),
            *_two_source_specs(tm, DN_WIDTH, n_p // tm),
            pl.BlockSpec((tm, d), row),
            pl.BlockSpec((tm, d), row),
            pl.BlockSpec((D_CONV, d), const),
            pl.BlockSpec((DN_WIDTH, d), const),
            pl.BlockSpec((d, d), const),
            pl.BlockSpec((1, d), const),
            pl.BlockSpec((d, 2 * LANES), const),
            pl.BlockSpec((1, LANES), const),
        ],
        out_specs=[
            pl.BlockSpec((tm, d), row),
            pl.BlockSpec((tm, LANES), row),
        ],
        out_shape=[
            jax.ShapeDtypeStruct((n, d), F32),
            jax.ShapeDtypeStruct((n, LANES), F32),
        ],
        compiler_params=_params(1),
    )(x_p, x_s, cact_p, cact_s, oact_p, oact_s, siga, sigb, w_conv_out, w_delta_out, w_merge_out, norm2_g,
      router_w, router_b)


def _fill_rows_per_step(n_fill, n_steps):
    per_step = SUBLANES
    while per_step * n_steps < n_fill:
        per_step *= 2
    assert n_fill % per_step == 0
    return per_step


def _route(top_idx, tm, n_blocks, n_steps):
    n = top_idx.shape[1]
    n_fill = n_blocks * tm - n * TOP_K
    assert n_fill == N_EXPERTS * tm
    experts = jnp.arange(N_EXPERTS, dtype=jnp.int32)
    chosen = top_idx[:, :, None] == experts[None, None, :]
    per_token = jnp.sum(chosen, axis=0, dtype=jnp.int32)
    csum = jnp.cumsum(per_token, axis=0)
    counts = csum[-1]
    padded = (counts + tm - 1) // tm * tm
    pad_end = jnp.cumsum(padded)
    pad_start = pad_end - padded
    first_free = (pad_start[None, :] + csum - per_token)[None, :, :]
    dest = jnp.sum(jnp.where(chosen, first_free, 0), axis=2)
    dest = dest.reshape(TOP_K, n_steps, -1).transpose(1, 0, 2).reshape(n_steps, -1)
    nvalid = (pad_end[-1] // tm).astype(jnp.int32)
    blk = jnp.arange(n_blocks, dtype=jnp.int32)
    owner = jnp.sum((pad_end[None, :] <= (blk * tm)[:, None]).astype(jnp.int32), axis=1)
    block_e = jnp.minimum(owner, N_EXPERTS - 1)
    block_e = jnp.where(blk < nvalid, block_e, jnp.sum(jnp.where(blk == nvalid - 1, block_e, 0)))
    present = counts > 0
    later = present[None, :] & (experts[None, :] > experts[:, None])
    next_present = jnp.min(jnp.where(later, experts[None, :], N_EXPERTS), axis=1)
    next_present = jnp.where(next_present == N_EXPERTS, -1, next_present)
    parity = (jnp.cumsum(present.astype(jnp.int32)) - 1) % 2
    of_block = (block_e[:, None] == experts[None, :]).astype(jnp.int32)
    sched = jnp.stack([block_e, jnp.sum(of_block * next_present[None, :], axis=1),
                       jnp.sum(of_block * parity[None, :], axis=1)])
    n_pad = padded - counts
    spill = tm - n_pad
    spill_start = pad_end[-1] + jnp.cumsum(spill) - spill
    j = jnp.arange(tm, dtype=jnp.int32)[None, :]
    fill = jnp.where(j < n_pad[:, None], (pad_start + counts)[:, None] + j, (spill_start - n_pad)[:, None] + j)
    fill_step = _fill_rows_per_step(n_fill, n_steps)
    fill = jnp.pad(fill.reshape(-1, fill_step), ((0, n_steps - n_fill // fill_step), (0, 0)))
    table = jnp.concatenate([dest, fill], axis=1)
    return table, sched, nvalid.reshape(1)


def _to_token_tiles(x, ref, row0=0):
    t = x.shape[0]
    for s in range(SUBLANES):
        ref[pl.ds(row0 + s, t, stride=SUBLANES), :] = x[:, s * LANES:(s + 1) * LANES]


def _from_token_tiles(ref, t, row0=0):
    return jnp.concatenate([ref[pl.ds(row0 + s, t, stride=SUBLANES), :] for s in range(SUBLANES)], axis=1)


def _tile(ref, row):
    return ref.at[pl.ds(pl.multiple_of(row * SUBLANES, SUBLANES), SUBLANES)]


def _dispatch_kernel(tab_ref, x2_ref, g2_ref, xs_ref, buf, sems, *, tokens, fill_step, n_fill_steps):
    i = pl.program_id(0)
    last = pl.num_programs(0) - 1
    slot = i % 2
    base = slot * tokens
    fill_sem = 2

    x2 = x2_ref[...]
    h2 = x2 * lax.rsqrt(jnp.mean(x2 * x2, axis=-1, keepdims=True) + EPS) * g2_ref[...]
    _to_token_tiles(h2, buf, pl.multiple_of(base * SUBLANES, SUBLANES))

    def wait_tiles(sem_idx, count):
        while count > 0:
            rows = min(count, tokens) * SUBLANES
            pltpu.make_async_copy(buf.at[pl.ds(0, rows)], xs_ref.at[pl.ds(0, rows)], sems.at[sem_idx]).wait()
            count -= min(count, tokens)

    for t in range(tokens):
        for k in range(TOP_K):
            pltpu.make_async_copy(_tile(buf, base + t), _tile(xs_ref, tab_ref[i, k * tokens + t]),
                                  sems.at[slot]).start(priority=k % DMA_PRIORITIES)

    @pl.when(i < n_fill_steps)
    def _():
        for p in range(fill_step):
            pltpu.make_async_copy(_tile(buf, base), _tile(xs_ref, tab_ref[i, tokens * TOP_K + p]),
                                  sems.at[fill_sem]).start(priority=p % DMA_PRIORITIES)
        wait_tiles(fill_sem, fill_step)

    @pl.when(i > 0)
    def _():
        wait_tiles(1 - slot, tokens * TOP_K)

    @pl.when(i == last)
    def _():
        wait_tiles(slot, tokens * TOP_K)


def _dispatch(x2, norm_g, table, n_rows, tokens):
    n, d = x2.shape
    assert d == SUBLANES * LANES
    fill_step = table.shape[1] - tokens * TOP_K
    n_fill_steps = (n_rows - n * TOP_K) // fill_step
    return pl.pallas_call(
        functools.partial(_dispatch_kernel, tokens=tokens, fill_step=fill_step, n_fill_steps=n_fill_steps),
        grid_spec=pltpu.PrefetchScalarGridSpec(
            num_scalar_prefetch=1,
            grid=(n // tokens,),
            in_specs=[pl.BlockSpec((tokens, d), lambda i, tab: (i, 0)),
                      pl.BlockSpec((1, d), lambda i, tab: (0, 0))],
            out_specs=pl.BlockSpec(memory_space=pl.ANY),
            scratch_shapes=[pltpu.VMEM((2 * tokens * SUBLANES, LANES), F32), pltpu.SemaphoreType.DMA((3,))],
        ),
        out_shape=jax.ShapeDtypeStruct((n_rows * SUBLANES, LANES), F32),
        compiler_params=_params(1),
    )(table, x2, norm_g)


def _moe_ffn_kernel(sched_ref, nv_ref, xs_ref, wgu_hbm, bgu_ref, wd_hbm, bd_ref, out_ref,
                    wgu_f32, wd_f32, wgu_bf, wd_bf, sems, *, blocks_per_step):
    tm = xs_ref.shape[0] // SUBLANES // blocks_per_step
    f = wd_bf.shape[0]

    def weight_copies(e, h):
        return (pltpu.make_async_copy(wgu_hbm.at[e], wgu_f32.at[h], sems.at[h]),
                pltpu.make_async_copy(wd_hbm.at[e], wd_f32.at[h], sems.at[h]))

    def one_block(i, r0):
        expert, next_expert, half = sched_ref[0, i], sched_ref[1, i], sched_ref[2, i]

        @pl.when(i == 0)
        def _():
            for c in weight_copies(expert, half):
                c.start()

        @pl.when(jnp.logical_or(i == 0, expert != sched_ref[0, jnp.maximum(i - 1, 0)]))
        def _():
            for c in weight_copies(expert, half):
                c.wait()
            wgu_bf[...] = wgu_f32[half].astype(BF16)
            wd_bf[...] = wd_f32[half].astype(BF16)

            @pl.when(next_expert >= 0)
            def _():
                for c in weight_copies(next_expert, 1 - half):
                    c.start()

        @pl.when(i < nv_ref[0])
        def _():
            x = _from_token_tiles(xs_ref, tm, r0).astype(BF16)
            gu = jnp.dot(x, wgu_bf[...], preferred_element_type=F32) + bgu_ref[expert]
            gt = jnp.minimum(gu[:, :f], SWIGLU_LIMIT)
            up = jnp.clip(gu[:, f:], -SWIGLU_LIMIT, SWIGLU_LIMIT)
            act = (up + 1.0) * (gt * _sigmoid(SWIGLU_ALPHA * gt))
            y = jnp.dot(act.astype(BF16), wd_bf[...], preferred_element_type=F32) + bd_ref[expert]
            _to_token_tiles(y, out_ref, r0)

        @pl.when(i >= nv_ref[0])
        def _():
            out_ref[r0:r0 + tm * SUBLANES, :] = jnp.zeros((tm * SUBLANES, LANES), out_ref.dtype)

    for h in range(blocks_per_step):
        one_block(pl.program_id(0) * blocks_per_step + h, h * tm * SUBLANES)


def _moe_ffn(xs, sched, nvalid, w_gate_up, b_gate_up, w_down, b_down, tm, n_blocks):
    ne, d, f2 = w_gate_up.shape
    f = f2 // 2
    per_step = FFN_BLOCKS_PER_STEP if n_blocks % FFN_BLOCKS_PER_STEP == 0 else 1
    rows = per_step * tm * SUBLANES
    used = lambda p, sc, nv: (jnp.minimum(p, (nv[0] - 1) // per_step), 0)
    whole = lambda p, sc, nv: (0, 0, 0)
    return pl.pallas_call(
        functools.partial(_moe_ffn_kernel, blocks_per_step=per_step),
        grid_spec=pltpu.PrefetchScalarGridSpec(
            num_scalar_prefetch=2,
            grid=(n_blocks // per_step,),
            in_specs=[
                pl.BlockSpec((rows, LANES), used),
                pl.BlockSpec(memory_space=pl.ANY),
                pl.BlockSpec((ne, 1, f2), whole),
                pl.BlockSpec(memory_space=pl.ANY),
                pl.BlockSpec((ne, 1, d), whole),
            ],
            out_specs=pl.BlockSpec((rows, LANES), lambda p, sc, nv: (p, 0)),
            scratch_shapes=[pltpu.VMEM((2, d, f2), F32), pltpu.VMEM((2, f, d), F32),
                            pltpu.VMEM((d, f2), BF16), pltpu.VMEM((f, d), BF16), pltpu.SemaphoreType.DMA((2,))],
        ),
        out_shape=jax.ShapeDtypeStruct((n_blocks * tm * SUBLANES, LANES), F32),
        compiler_params=_params(1),
    )(sched, nvalid, xs, w_gate_up, b_gate_up.reshape(ne, 1, f2), w_down, b_down.reshape(ne, 1, d))


def _combine_kernel(slot_ref, yb_ref, x2_ref, tg_ref, fg_ref, yp_ref, ys_ref, buf, sems, *, tc, n_prompt_tiles):
    i = pl.program_id(0)
    n_steps = pl.num_programs(0)

    def region(slot, k):
        return (slot * TOP_K + k) * tc

    def fetch(step, slot):
        for t in range(tc):
            for k in range(TOP_K):
                pltpu.make_async_copy(
                    _tile(yb_ref, slot_ref[step, k * tc + t]), _tile(buf, region(slot, k) + t),
                    sems.at[slot]).start(priority=k % DMA_PRIORITIES)

    @pl.when(i == 0)
    def _():
        fetch(0, 0)

    @pl.when(i + 1 < n_steps)
    def _():
        fetch(i + 1, (i + 1) % 2)

    slot = i % 2
    for k in range(TOP_K):
        pltpu.make_async_copy(yb_ref.at[pl.ds(0, tc * SUBLANES)], buf.at[pl.ds(0, tc * SUBLANES)],
                              sems.at[slot]).wait()
    tg = tg_ref[...]
    y = x2_ref[...]
    for k in range(TOP_K):
        rows = _from_token_tiles(buf, tc, pl.multiple_of(region(slot, k) * SUBLANES, SUBLANES))
        y = y + tg[:, TOP_K + k:TOP_K + k + 1] * rows
    out = y * lax.rsqrt(jnp.mean(y * y, axis=-1, keepdims=True) + EPS) * fg_ref[...]

    @pl.when(i < n_prompt_tiles)
    def _():
        yp_ref[...] = out

    @pl.when(i >= n_prompt_tiles)
    def _():
        ys_ref[...] = out


def _combine(slot2d, yb, x2, tg, final_g, n_p, tc):
    n, d = x2.shape
    n_s = n - n_p
    assert n_p % tc == 0 and n_s % tc == 0
    npt = n_p // tc
    kern = functools.partial(_combine_kernel, tc=tc, n_prompt_tiles=npt)
    out_p, out_s = _two_source_specs(tc, d, npt)
    return pl.pallas_call(
        kern,
        grid_spec=pltpu.PrefetchScalarGridSpec(
            num_scalar_prefetch=1,
            grid=(n // tc,),
            in_specs=[
                pl.BlockSpec(memory_space=pl.ANY),
                pl.BlockSpec((tc, d), lambda i, s: (i, 0)),
                pl.BlockSpec((tc, LANES), lambda i, s: (i, 0)),
                pl.BlockSpec((1, d), lambda i, s: (0, 0)),
            ],
            out_specs=[out_p, out_s],
            scratch_shapes=[pltpu.VMEM((2 * TOP_K * tc * SUBLANES, LANES), F32), pltpu.SemaphoreType.DMA((2,))],
        ),
        out_shape=[jax.ShapeDtypeStruct((n_p, d), F32), jax.ShapeDtypeStruct((n_s, d), F32)],
        compiler_params=_params(1),
    )(slot2d, yb, x2, tg, final_g)


def _pad_lanes(v, width=LANES):
    v = v.reshape(1, -1)
    return jnp.pad(v, ((0, 0), (0, width - v.shape[1])))


def kernel(x_prompt, x_sample, state_conv, state_short_conv, state_delta, norm1_g, w_in, conv_dw_w,
           conv_dw_b, conv_ln_g, conv_ln_b, w_conv_out, short_conv_w, a_log, dt_bias, delta_norm_g,
           w_delta_out, w_merge_out, norm2_g, router_w, router_b, w_gate_up, b_gate_up, w_down, b_down,
           final_norm_g):
    depth = w_in.shape[0]
    assert depth == 1
    bp, tp, d = x_prompt.shape
    bs, ts, _ = x_sample.shape
    n_p, n_s = bp * tp, bs * ts
    n = n_p + n_s
    l = 0
    x_p = x_prompt.reshape(n_p, d)
    x_s = x_sample.reshape(n_s, d)

    o_ab = 2 * D_CONV + 4 * DN_WIDTH
    w = w_in[l]
    wt = jnp.transpose(w)
    wt_bf = wt.astype(BF16)
    w_main = wt_bf[:o_ab]
    w_gates = wt_bf[o_ab + 2 * DN_HEADS:]
    w_ab = _split_bf16(jnp.pad(jnp.transpose(wt[o_ab:o_ab + 2 * DN_HEADS]), ((0, 0), (0, LANES - 2 * DN_HEADS))))

    glu, qkv_pre, z, gb, siga, sigb = _inproj(
        x_p, x_s, norm1_g[l].reshape(1, d), w_main, w_gates, w_ab, _pad_lanes(a_log[l]), _pad_lanes(dt_bias[l]),
        MIX_TILE if n_p % MIX_TILE == 0 and n_s % MIX_TILE == 0 else TOKEN_TILE)

    dw = (conv_dw_w[l], conv_dw_b[l].reshape(1, -1), conv_ln_g[l].reshape(1, -1), conv_ln_b[l].reshape(1, -1))
    st_c_p = jnp.zeros((depth, bp, CONV_WIDTH - 1, D_CONV), F32)
    seq_tile = SEQ_TILE if tp % SEQ_TILE == 0 else TOKEN_TILE
    cact_p, conv_p = _conv_branch(glu, st_c_p, *dw, row0=0, bsz=bp, t_len=tp, bb=1, tt=seq_tile)
    cact_s, conv_s = _conv_branch(glu, state_conv, *dw, row0=n_p, bsz=bs, t_len=ts, bb=8, tt=ts)

    st_s_p = jnp.zeros((depth, bp, SHORT_WIDTH - 1, 3 * DN_WIDTH), F32)
    s0_p = jnp.zeros((depth, bp, DN_HEADS, DN_HEAD_DIM, DN_HEAD_DIM), F32)
    ng = delta_norm_g[l].reshape(1, -1)
    oact_p, short_p, s_p = _delta_prompt(qkv_pre, z, gb, st_s_p, s0_p, short_conv_w[l], ng,
                                         bsz=bp, t_len=tp, tt=seq_tile)
    oact_s, short_s, s_s = _delta_sample(qkv_pre, z, gb, state_short_conv, state_delta, short_conv_w[l], ng,
                                         row0=n_p, bsz=bs, seq_len=ts)

    rw = _split_bf16(jnp.pad(router_w[l], ((0, 0), (0, LANES - N_EXPERTS))))
    x2, tg = _mix(x_p, x_s, cact_p, cact_s, oact_p, oact_s, siga, sigb, w_conv_out[l].astype(BF16),
                  w_delta_out[l].astype(BF16), w_merge_out[l].astype(BF16), norm2_g[l].reshape(1, d),
                  rw, _pad_lanes(router_b[l]),
                  MIX_TILE if n_p % MIX_TILE == 0 and n_s % MIX_TILE == 0 else TOKEN_TILE)

    n_blocks = -(-(n * TOP_K) // MOE_TILE) + N_EXPERTS
    route_tile = ROUTE_TILE if n_p % ROUTE_TILE == 0 and n_s % ROUTE_TILE == 0 else TOKEN_TILE
    n_steps = n // route_tile
    table, sched, nvalid = _route(tg[:, :TOP_K].T.astype(jnp.int32), MOE_TILE, n_blocks, n_steps)
    xs = _dispatch(x2, norm2_g[l].reshape(1, d), table, n_blocks * MOE_TILE, route_tile)
    yb = _moe_ffn(xs, sched, nvalid, w_gate_up[l], b_gate_up[l], w_down[l], b_down[l], MOE_TILE, n_blocks)
    y_p, y_s = _combine(table, yb, x2, tg, final_norm_g.reshape(1, d), n_p, route_tile)

    return (y_p.reshape(bp, tp, d), y_s.reshape(bs, ts, d), conv_p, short_p, s_p, conv_s, short_s, s_s)
```
